```python
import jax, jax.numpy as jnp
from jax import lax
import numpy as np

D_MODEL = 1024
BATCH = 8
SEQ = 4096
DEPTH = 1

CTX_LEN = 256
GRID_W = 64
D_HGRN = 512
HGRN_HEADS = 4
HGRN_HEAD_DIM = D_HGRN // HGRN_HEADS
HGRN_CHUNK = 64
D_CONV = 512
CONV_WIDTH = 31
D_MIX = D_HGRN + D_CONV
SPLITS = [D_HGRN, 2 * D_HGRN, 3 * D_HGRN, 4 * D_HGRN, 5 * D_HGRN,
          5 * D_HGRN + D_CONV, 5 * D_HGRN + 2 * D_CONV]
D_IN = 5 * D_HGRN + 3 * D_CONV
EPS = 1e-6

kernel_name = "hymba_hgrn2_conformer_dit_block"


def rmsnorm(x, g):
    xf = x.astype(jnp.float32)
    y = xf * lax.rsqrt(jnp.mean(xf * xf, axis=-1, keepdims=True) + EPS)
    return (y * g.astype(jnp.float32)).astype(x.dtype)


def layernorm(x, g, b):
    xf = x.astype(jnp.float32)
    mu = jnp.mean(xf, axis=-1, keepdims=True)
    var = jnp.mean(jnp.square(xf - mu), axis=-1, keepdims=True)
    return ((xf - mu) * lax.rsqrt(var + EPS) * g.astype(jnp.float32) + b.astype(jnp.float32)).astype(x.dtype)


def split_heads(a):
    bsz, t, _ = a.shape
    return a.reshape(bsz, t, HGRN_HEADS, HGRN_HEAD_DIM).transpose(0, 2, 1, 3)


def merge_heads(a):
    bsz, _, t, _ = a.shape
    return a.transpose(0, 2, 1, 3).reshape(bsz, t, D_HGRN)


def forget_gate(z, lb):
    f = lb + (1.0 - lb) * jax.nn.sigmoid(z.astype(jnp.float32))
    return jnp.log(f), 1.0 - f


def hgrn2_scan(q, k, v, log_f, s0):
    bsz, h, t, dk = q.shape
    dv = v.shape[-1]
    n = t // HGRN_CHUNK

    def to_chunks(a):
        return jnp.moveaxis(a.astype(jnp.float32).reshape(bsz, h, n, HGRN_CHUNK, a.shape[-1]), 2, 0)

    pos = jnp.arange(HGRN_CHUNK)
    lower_tri = pos[:, None] >= pos[None, :]

    def step(s, inp):
        qc, kc, vc, gc = inp
        b = jnp.cumsum(gc, axis=-2)
        b_last = b[..., -1:, :]
        inter = jnp.einsum('bhtd,bhde->bhte', qc * jnp.exp(b), s)
        diff = b[..., :, None, :] - b[..., None, :, :]
        decay = jnp.where(lower_tri[:, :, None], jnp.exp(jnp.minimum(diff, 0.0)), 0.0)
        scores = jnp.einsum('bhtd,bhsd,bhtsd->bhts', qc, kc, decay)
        intra = jnp.einsum('bhts,bhse->bhte', scores, vc)
        s_new = jnp.exp(b_last[..., 0, :])[..., None] * s + jnp.einsum(
            'bhsd,bhse->bhde', kc * jnp.exp(b_last - b), vc)
        return s_new, intra + inter

    s_fin, o = lax.scan(step, s0, (to_chunks(q), to_chunks(k), to_chunks(v), to_chunks(log_f)))
    o = jnp.moveaxis(o, 0, 2).reshape(bsz, h, t, dv)
    return o, s_fin


def hgrn2_bidir(q, v, z_fwd, z_bwd, lb_fwd, lb_bwd, s0_fwd, s0_bwd):
    g_f, k_f = forget_gate(z_fwd, lb_fwd)
    g_b, k_b = forget_gate(z_bwd, lb_bwd)
    qh, vh = split_heads(q), split_heads(v)
    o_f, s_f = hgrn2_scan(qh, split_heads(k_f), vh, split_heads(g_f), s0_fwd)
    flip = lambda a: jnp.flip(a, axis=2)
    o_b, s_b = hgrn2_scan(flip(qh), flip(split_heads(k_b)), flip(vh), flip(split_heads(g_b)), s0_bwd)
    o = merge_heads(o_f + flip(o_b)).astype(q.dtype)
    return o, s_f, s_b


def head_rmsnorm(o, g):
    bsz, t, _ = o.shape
    oh = o.reshape(bsz, t, HGRN_HEADS, HGRN_HEAD_DIM)
    of = oh.astype(jnp.float32)
    of = of * lax.rsqrt(jnp.mean(of * of, axis=-1, keepdims=True) + EPS)
    return (of.reshape(bsz, t, D_HGRN) * g.astype(jnp.float32)).astype(o.dtype)


def depthwise_conv2d(x, k):
    return lax.conv_general_dilated(x, k.astype(x.dtype), window_strides=(1, 1), padding='SAME',
                                    dimension_numbers=('NHWC', 'HWIO', 'NHWC'),
                                    feature_group_count=x.shape[-1])


def conv_tail(y, conv_b_l, ln_g_l, ln_b_l):
    return jax.nn.silu(layernorm(y + conv_b_l, ln_g_l, ln_b_l))


def conformer_conv_latent(u, rows, conv_w_l, conv_b_l, ln_g_l, ln_b_l):
    bsz, t, ch = u.shape
    half = ch // 2
    grid = u.reshape(bsz, rows, GRID_W, ch)
    k_row = conv_w_l[:, :half].reshape(1, CONV_WIDTH, 1, half)
    k_col = conv_w_l[:, half:].reshape(CONV_WIDTH, 1, 1, half)
    y = jnp.concatenate([depthwise_conv2d(grid[..., :half], k_row),
                         depthwise_conv2d(grid[..., half:], k_col)], axis=-1).reshape(bsz, t, ch)
    return conv_tail(y, conv_b_l, ln_g_l, ln_b_l)


def conformer_conv_context(u, conv_w_l, conv_b_l, ln_g_l, ln_b_l):
    bsz, t, ch = u.shape
    y = depthwise_conv2d(u[:, None], conv_w_l.reshape(1, CONV_WIDTH, 1, ch)).reshape(bsz, t, ch)
    return conv_tail(y, conv_b_l, ln_g_l, ln_b_l)


def merge_branches(o_hgrn, g_a, conv_y, g_b, hgrn_norm_g_l, w_out_l):
    branch_a = head_rmsnorm(o_hgrn, hgrn_norm_g_l) * jax.nn.silu(g_a)
    branch_b = conv_y * jax.nn.silu(g_b)
    return jnp.concatenate([branch_a, branch_b], axis=-1) @ w_out_l


def setup_inputs(seed: int = 0) -> dict:
    key = jax.random.key(seed)
    ks = jax.random.split(key, 20)
    nrm = jax.random.normal
    f32 = jnp.float32
    return {
        "x": nrm(ks[0], (BATCH, SEQ, D_MODEL), f32),
        "c": nrm(ks[1], (BATCH, D_MODEL), f32),
        "ctx": nrm(ks[2], (BATCH, CTX_LEN, D_MODEL), f32),
        "c_ctx": nrm(ks[3], (D_MODEL,), f32),
        "norm_g": 1.0 + 0.02 * nrm(ks[4], (DEPTH, D_MODEL), f32),
        "w_mod": 0.5 * D_MODEL ** -0.5 * nrm(ks[5], (DEPTH, D_MODEL, 3 * D_MODEL), f32),
        "b_mod": 0.02 * nrm(ks[6], (DEPTH, 3 * D_MODEL), f32),
        "w_in": D_MODEL ** -0.5 * nrm(ks[7], (DEPTH, D_MODEL, D_IN), f32),
        "lb_logits": 0.5 * nrm(ks[8], (DEPTH + 1, 2, D_HGRN), f32),
        "hgrn_norm_g": 1.0 + 0.02 * nrm(ks[9], (DEPTH, D_HGRN), f32),
        "conv_w": CONV_WIDTH ** -0.5 * nrm(ks[10], (DEPTH, CONV_WIDTH, D_CONV), f32),
        "conv_b": 0.02 * nrm(ks[11], (DEPTH, D_CONV), f32),
        "conv_ln_g": 1.0 + 0.02 * nrm(ks[12], (DEPTH, D_CONV), f32),
        "conv_ln_b": 0.02 * nrm(ks[13], (DEPTH, D_CONV), f32),
        "w_out": D_MIX ** -0.5 * nrm(ks[14], (DEPTH, D_MIX, D_MODEL), f32),
        "final_norm_g": 1.0 + 0.02 * nrm(ks[15], (D_MODEL,), f32),
    }


def reference(x, c, ctx, c_ctx, norm_g, w_mod, b_mod, w_in, lb_logits, hgrn_norm_g,
              conv_w, conv_b, conv_ln_g, conv_ln_b, w_out, final_norm_g):
    bsz, seq_len, _ = x.shape
    rows = seq_len // GRID_W
    lower_bounds = jnp.cumsum(jax.nn.softmax(lb_logits.astype(jnp.float32), axis=0), axis=0)
    zero_state = jnp.zeros((bsz, HGRN_HEADS, HGRN_HEAD_DIM, HGRN_HEAD_DIM), jnp.float32)
    h_lat, h_ctx = x, ctx
    for l in range(DEPTH):
        mod_lat = jax.nn.silu(c) @ w_mod[l] + b_mod[l]
        mod_ctx = jax.nn.silu(c_ctx) @ w_mod[l] + b_mod[l]
        sh_l, sc_l, gt_l = jnp.split(mod_lat[:, None, :], 3, axis=-1)
        sh_c, sc_c, gt_c = jnp.split(mod_ctx, 3, axis=-1)
        a_lat = rmsnorm(h_lat, norm_g[l]) * (1.0 + sc_l) + sh_l
        a_ctx = rmsnorm(h_ctx, norm_g[l]) * (1.0 + sc_c) + sh_c
        q_l, zf_l, zb_l, v_l, ga_l, u_l, ug_l, gb_l = jnp.split(a_lat @ w_in[l], SPLITS, axis=-1)
        q_c, zf_c, zb_c, v_c, ga_c, u_c, ug_c, gb_c = jnp.split(a_ctx @ w_in[l], SPLITS, axis=-1)
        lb_f, lb_b = lower_bounds[l, 0], lower_bounds[l, 1]
        o_c, s_f, s_b = hgrn2_bidir(q_c, v_c, zf_c, zb_c, lb_f, lb_b, zero_state, zero_state)
        o_l, _, _ = hgrn2_bidir(q_l, v_l, zf_l, zb_l, lb_f, lb_b, s_f, s_b)
        y_l = conformer_conv_latent(u_l * jax.nn.sigmoid(ug_l), rows, conv_w[l], conv_b[l],
                                    conv_ln_g[l], conv_ln_b[l])
        out_lat = merge_branches(o_l, ga_l, y_l, gb_l, hgrn_norm_g[l], w_out[l])
        if l < DEPTH - 1:
            y_c = conformer_conv_context(u_c * jax.nn.sigmoid(ug_c), conv_w[l], conv_b[l],
                                         conv_ln_g[l], conv_ln_b[l])
            h_ctx = h_ctx + gt_c * merge_branches(o_c, ga_c, y_c, gb_c, hgrn_norm_g[l], w_out[l])
        h_lat = h_lat + gt_l * out_lat
    return rmsnorm(h_lat, final_norm_g)
```

```python
import functools

import numpy as np
import jax
import jax.numpy as jnp
from jax import lax
from jax.experimental import pallas as pl
from jax.experimental.pallas import tpu as pltpu

D_MODEL = 1024
CTX_LEN = 256
GRID_W = 64
D_HGRN = 512
HGRN_HEADS = 4
HEAD_DIM = D_HGRN // HGRN_HEADS
D_CONV = 512
CONV_WIDTH = 31
CONV_HALF = CONV_WIDTH // 2
D_MIX = D_HGRN + D_CONV
D_IN = 5 * D_HGRN + 3 * D_CONV
EPS = 1e-6

LANES = 128
SCAN_CHUNK = 64
SCAN_LEVELS = SCAN_CHUNK.bit_length() - 1
PROJ_ROWS = 256
OUT_ROWS = 512
CONV_ROWS = 64
VMEM_LIMIT = 56 * 1024 * 1024

COL_Q, COL_ZF, COL_ZB, COL_V, COL_GA = (i * HGRN_HEADS for i in range(5))
COL_U, COL_UG, COL_GB = (5 * HGRN_HEADS + i * (D_CONV // LANES) for i in range(3))


def _silu(x):
    return x * jax.nn.sigmoid(x)


def _mod_kernel(c_ref, w_ref, b_ref, o_ref):
    a = _silu(c_ref[...])
    o_ref[...] = jnp.dot(a, w_ref[...], preferred_element_type=jnp.float32,
                         precision=lax.Precision.HIGHEST) + b_ref[...]


def _modulation(cc, w_mod, b_mod):
    rows = cc.shape[0]
    n = w_mod.shape[1]
    return pl.pallas_call(
        _mod_kernel,
        grid=(n // D_MODEL,),
        in_specs=[pl.BlockSpec((rows, D_MODEL), lambda j: (0, 0)),
                  pl.BlockSpec((D_MODEL, D_MODEL), lambda j: (0, j)),
                  pl.BlockSpec((1, D_MODEL), lambda j: (0, j))],
        out_specs=pl.BlockSpec((rows, D_MODEL), lambda j: (0, j)),
        out_shape=jax.ShapeDtypeStruct((rows, n), jnp.float32),
        compiler_params=pltpu.CompilerParams(dimension_semantics=("arbitrary",),
                                             vmem_limit_bytes=VMEM_LIMIT),
        name="modulation",
    )(cc, w_mod, b_mod)


def _proj_kernel(x_ref, g_ref, sh_ref, sc_ref, w_ref, o_ref):
    x = x_ref[0]
    y = x * lax.rsqrt(jnp.mean(x * x, axis=-1, keepdims=True) + EPS) * g_ref[...]
    a = y * (1.0 + sc_ref[0]) + sh_ref[0]
    o_ref[0] = jnp.dot(a.astype(jnp.bfloat16), w_ref[...], preferred_element_type=jnp.float32)


def _projection(x, norm_g, shift, scale, w_in_bf16):
    bsz, t, _ = x.shape
    rows = min(PROJ_ROWS, t)
    return pl.pallas_call(
        _proj_kernel,
        grid=(bsz, t // rows),
        in_specs=[pl.BlockSpec((1, rows, D_MODEL), lambda b, i: (b, i, 0)),
                  pl.BlockSpec((1, D_MODEL), lambda b, i: (0, 0)),
                  pl.BlockSpec((1, 1, D_MODEL), lambda b, i: (b, 0, 0)),
                  pl.BlockSpec((1, 1, D_MODEL), lambda b, i: (b, 0, 0)),
                  pl.BlockSpec((D_MODEL, D_IN), lambda b, i: (0, 0))],
        out_specs=pl.BlockSpec((1, rows, D_IN), lambda b, i: (b, i, 0)),
        out_shape=jax.ShapeDtypeStruct((bsz, t, D_IN), jnp.float32),
        compiler_params=pltpu.CompilerParams(dimension_semantics=("arbitrary", "arbitrary"),
                                             vmem_limit_bytes=VMEM_LIMIT),
        name="in_projection",
    )(x, norm_g, shift, scale, w_in_bf16)


def _scan_constants():
    c = SCAN_CHUNK
    idx = np.arange(c)
    t, u = idx[:, None], idx[None, :]
    blocks = [(u <= t), (u > t)]
    masks = [t == u]
    for lvl in range(SCAN_LEVELS):
        h = c >> (lvl + 1)
        r = (t // (2 * h)) * (2 * h) + h - 1
        right = (t % (2 * h)) >= h
        blocks.append(np.where(right, (u > r) & (u <= t), (u > t) & (u <= r)))
        s = u
        masks.append(((t // (2 * h)) == (s // (2 * h))) & right & ((s % (2 * h)) < h))
    w_f = np.concatenate([b.astype(np.float32) for b in blocks], axis=0)
    m_f = np.stack([m.astype(np.float32) for m in masks])
    w_b = np.concatenate([b[::-1, ::-1].astype(np.float32) for b in blocks], axis=0)
    m_b = m_f[:, ::-1, ::-1].copy()
    w_f3 = np.concatenate([w_f] * 3, axis=1)
    w_b3 = np.concatenate([w_b] * 3, axis=1)
    return w_f3, w_b3, m_f, m_b


def _dot_nt(a, b):
    return lax.dot_general(a, b, (((1,), (1,)), ((), ())), preferred_element_type=jnp.float32)


def _dot_tn(a, b):
    return lax.dot_general(a, b, (((0,), (0,)), ((), ())), preferred_element_type=jnp.float32)


def _chunk_step(q, z, v, st_ref, lb, w_ref, m_ref, end_row):
    c = SCAN_CHUNK
    bf = jnp.bfloat16
    f = lb + (1.0 - lb) * jax.nn.sigmoid(z)
    g = jnp.log(f)
    k = 1.0 - f
    g_hi = g.astype(bf)
    r1 = g - g_hi.astype(jnp.float32)
    g_mid = r1.astype(bf)
    g_lo = (r1 - g_mid.astype(jnp.float32)).astype(bf)
    g3 = jnp.concatenate([g_hi, g_mid, g_lo], axis=0)
    decay = jnp.exp(jnp.dot(w_ref[...], g3, preferred_element_type=jnp.float32))
    d_read = decay[0:c]
    d_state = decay[c:2 * c]
    d_end = d_read[end_row:end_row + 1]
    a = m_ref[0] * _dot_nt(q.astype(bf), k.astype(bf))
    for lvl in range(SCAN_LEVELS):
        d = decay[(2 + lvl) * c:(3 + lvl) * c]
        a = a + m_ref[lvl + 1] * _dot_nt((q * d).astype(bf), (k * d).astype(bf))
    st = st_ref[...]
    vb = v.astype(bf)
    o = jnp.dot(a.astype(bf), vb, preferred_element_type=jnp.float32)
    o = o + _dot_nt((q * d_read).astype(bf), st.astype(bf))
    st_ref[...] = st * d_end + _dot_tn(vb, (k * d_state).astype(bf))
    return o


def _scan_kernel(q_ref, zf_ref, zb_ref, v_ref, ga_ref, qc_ref, zfc_ref, zbc_ref, vc_ref,
                 lbl_ref, gn_ref, wf_ref, wb_ref, mf_ref, mb_ref, o_ref,
                 stf_ref, stb_ref, of_ref, ob_ref):
    c = SCAN_CHUNK
    t_lat = q_ref.shape[1]
    t_ctx = qc_ref.shape[1]
    l0, l1 = lbl_ref[0], lbl_ref[1]
    mx = jnp.maximum(l0, l1)
    e0, e1 = jnp.exp(l0 - mx), jnp.exp(l1 - mx)
    lb = e0 / (e0 + e1)
    lb_f, lb_b = lb[0:1], lb[1:2]

    stf_ref[...] = jnp.zeros_like(stf_ref)
    stb_ref[...] = jnp.zeros_like(stb_ref)

    def both_directions(i, n, q_r, zf_r, zb_r, v_r, store):
        lo = pl.multiple_of(i * c, c)
        hi = pl.multiple_of((n - 1 - i) * c, c)
        o_f = _chunk_step(q_r[0, pl.ds(lo, c), :], zf_r[0, pl.ds(lo, c), :], v_r[0, pl.ds(lo, c), :],
                          stf_ref, lb_f, wf_ref, mf_ref, c - 1)
        o_b = _chunk_step(q_r[0, pl.ds(hi, c), :], zb_r[0, pl.ds(hi, c), :], v_r[0, pl.ds(hi, c), :],
                          stb_ref, lb_b, wb_ref, mb_ref, 0)
        if store:
            of_ref[pl.ds(lo, c), :] = o_f
            ob_ref[pl.ds(hi, c), :] = o_b

    n_ctx = t_ctx // c

    def ctx_body(i, carry):
        both_directions(i, n_ctx, qc_ref, zfc_ref, zbc_ref, vc_ref, False)
        return carry

    lax.fori_loop(0, n_ctx, ctx_body, 0)

    n_lat = t_lat // c

    def lat_body(i, carry):
        both_directions(i, n_lat, q_ref, zf_ref, zb_ref, v_ref, True)
        return carry

    lax.fori_loop(0, n_lat, lat_body, 0)

    rows = 256
    gn = gn_ref[...]

    def fin_body(i, carry):
        lo = pl.multiple_of(i * rows, rows)
        o = of_ref[pl.ds(lo, rows), :] + ob_ref[pl.ds(lo, rows), :]
        o = o * lax.rsqrt(jnp.mean(o * o, axis=-1, keepdims=True) + EPS) * gn
        o_ref[0, pl.ds(lo, rows), :] = (o * _silu(ga_ref[0, pl.ds(lo, rows), :])).astype(o_ref.dtype)
        return carry

    lax.fori_loop(0, t_lat // rows, fin_body, 0)


def _hgrn_scan(p_lat, p_ctx, lb_logits, hgrn_norm_g):
    bsz, t_lat, _ = p_lat.shape
    t_ctx = p_ctx.shape[1]
    w_f3, w_b3, m_f, m_b = _scan_constants()
    wf = jnp.asarray(w_f3, jnp.bfloat16)
    wb = jnp.asarray(w_b3, jnp.bfloat16)
    mf = jnp.asarray(m_f, jnp.float32)
    mb = jnp.asarray(m_b, jnp.float32)

    def col(t, base):
        return pl.BlockSpec((1, t, LANES), lambda b, h, base=base: (b, 0, base + h))

    def whole(a):
        return pl.BlockSpec(a.shape, lambda b, h, nd=a.ndim: (0,) * nd)

    return pl.pallas_call(
        _scan_kernel,
        grid=(bsz, HGRN_HEADS),
        in_specs=[col(t_lat, COL_Q), col(t_lat, COL_ZF), col(t_lat, COL_ZB), col(t_lat, COL_V),
                  col(t_lat, COL_GA),
                  col(t_ctx, COL_Q), col(t_ctx, COL_ZF), col(t_ctx, COL_ZB), col(t_ctx, COL_V),
                  pl.BlockSpec((2, 2, LANES), lambda b, h: (0, 0, h)),
                  pl.BlockSpec((1, LANES), lambda b, h: (0, h)),
                  whole(wf), whole(wb), whole(mf), whole(mb)],
        out_specs=pl.BlockSpec((1, t_lat, LANES), lambda b, h: (b, 0, h)),
        out_shape=jax.ShapeDtypeStruct((bsz, t_lat, D_HGRN), jnp.bfloat16),
        scratch_shapes=[pltpu.VMEM((HEAD_DIM, HEAD_DIM), jnp.float32),
                        pltpu.VMEM((HEAD_DIM, HEAD_DIM), jnp.float32),
                        pltpu.VMEM((t_lat, HEAD_DIM), jnp.float32),
                        pltpu.VMEM((t_lat, HEAD_DIM), jnp.float32)],
        compiler_params=pltpu.CompilerParams(dimension_semantics=("arbitrary", "arbitrary"),
                                             vmem_limit_bytes=VMEM_LIMIT),
        name="hgrn_scan",
    )(p_lat, p_lat, p_lat, p_lat, p_lat, p_ctx, p_ctx, p_ctx, p_ctx,
      lb_logits, hgrn_norm_g, wf, wb, mf, mb)


def _conv_kernel(u_ref, ug_ref, w_ref, b_ref, o_ref, pad_ref):
    t = u_ref.shape[1]
    n_rows = t // GRID_W
    cg = pl.program_id(1)
    bias = b_ref[...]

    @pl.when(cg < (D_CONV // 2) // LANES)
    def _along_rows():
        padw = GRID_W + 32

        def fill(r, carry):
            src = pl.multiple_of(r * GRID_W, GRID_W)
            dst = pl.multiple_of(r * padw, 32)
            glu = u_ref[0, pl.ds(src, GRID_W), :] * jax.nn.sigmoid(ug_ref[0, pl.ds(src, GRID_W), :])
            pad_ref[pl.ds(dst, 16), :] = jnp.zeros((16, LANES), jnp.float32)
            pad_ref[pl.ds(dst + 16, GRID_W), :] = glu
            pad_ref[pl.ds(dst + 16 + GRID_W, 16), :] = jnp.zeros((16, LANES), jnp.float32)
            return carry

        lax.fori_loop(0, n_rows, fill, 0)

        def conv(r, carry):
            dst = pl.multiple_of(r * GRID_W, GRID_W)
            base = r * padw + (16 - CONV_HALF)
            acc = jnp.zeros((GRID_W, LANES), jnp.float32)
            for k in range(CONV_WIDTH):
                acc = acc + w_ref[k:k + 1, :] * pad_ref[pl.ds(base + k, GRID_W), :]
            o_ref[0, pl.ds(dst, GRID_W), :] = acc + bias
            return carry

        lax.fori_loop(0, n_rows, conv, 0)

    @pl.when(cg >= (D_CONV // 2) // LANES)
    def _along_cols():
        halo = CONV_HALF * GRID_W
        pad_ref[pl.ds(0, halo), :] = jnp.zeros((halo, LANES), jnp.float32)
        pad_ref[pl.ds(halo + t, halo), :] = jnp.zeros((halo, LANES), jnp.float32)

        def fill(r, carry):
            src = pl.multiple_of(r * GRID_W, GRID_W)
            glu = u_ref[0, pl.ds(src, GRID_W), :] * jax.nn.sigmoid(ug_ref[0, pl.ds(src, GRID_W), :])
            pad_ref[pl.ds(halo + src, GRID_W), :] = glu
            return carry

        lax.fori_loop(0, n_rows, fill, 0)

        def conv(r, carry):
            dst = pl.multiple_of(r * GRID_W, GRID_W)
            acc = jnp.zeros((GRID_W, LANES), jnp.float32)
            for k in range(CONV_WIDTH):
                acc = acc + w_ref[k:k + 1, :] * pad_ref[pl.ds(dst + k * GRID_W, GRID_W), :]
            o_ref[0, pl.ds(dst, GRID_W), :] = acc + bias
            return carry

        lax.fori_loop(0, n_rows, conv, 0)


def _axial_conv(p_lat, conv_w, conv_b):
    bsz, t, _ = p_lat.shape
    n_rows = t // GRID_W
    pad_rows = max(n_rows * (GRID_W + 32), t + 2 * CONV_HALF * GRID_W)
    w_pad = jnp.zeros((32, D_CONV), jnp.float32).at[:CONV_WIDTH].set(conv_w)
    return pl.pallas_call(
        _conv_kernel,
        grid=(bsz, D_CONV // LANES),
        in_specs=[pl.BlockSpec((1, t, LANES), lambda b, g: (b, 0, COL_U + g)),
                  pl.BlockSpec((1, t, LANES), lambda b, g: (b, 0, COL_UG + g)),
                  pl.BlockSpec((32, LANES), lambda b, g: (0, g)),
                  pl.BlockSpec((1, LANES), lambda b, g: (0, g))],
        out_specs=pl.BlockSpec((1, t, LANES), lambda b, g: (b, 0, g)),
        out_shape=jax.ShapeDtypeStruct((bsz, t, D_CONV), jnp.float32),
        scratch_shapes=[pltpu.VMEM((pad_rows, LANES), jnp.float32)],
        compiler_params=pltpu.CompilerParams(dimension_semantics=("arbitrary", "arbitrary"),
                                             vmem_limit_bytes=VMEM_LIMIT),
        name="axial_conv",
    )(p_lat, p_lat, w_pad, conv_b)


def _out_kernel(x_ref, ba_ref, y_ref, gb_ref, gt_ref, lng_ref, lnb_ref, wa_ref, wb_ref, fg_ref, o_ref):
    y = y_ref[0]
    mu = jnp.mean(y, axis=-1, keepdims=True)
    yc = y - mu
    var = jnp.mean(yc * yc, axis=-1, keepdims=True)
    yn = yc * lax.rsqrt(var + EPS) * lng_ref[...] + lnb_ref[...]
    branch_b = _silu(yn) * _silu(gb_ref[0])
    mix = jnp.dot(ba_ref[0], wa_ref[...], preferred_element_type=jnp.float32)
    mix = mix + jnp.dot(branch_b.astype(jnp.bfloat16), wb_ref[...], preferred_element_type=jnp.float32)
    h = x_ref[0] + gt_ref[0] * mix
    o_ref[0] = h * lax.rsqrt(jnp.mean(h * h, axis=-1, keepdims=True) + EPS) * fg_ref[...]


def _output(x, branch_a, y_conv, p_lat, gate, ln_g, ln_b, w_out_bf16, final_g):
    bsz, t, _ = x.shape
    rows = OUT_ROWS
    w_a, w_b = w_out_bf16[:D_HGRN], w_out_bf16[D_HGRN:]
    gb_block = (5 * D_HGRN + 2 * D_CONV) // D_CONV
    return pl.pallas_call(
        _out_kernel,
        grid=(bsz, t // rows),
        in_specs=[pl.BlockSpec((1, rows, D_MODEL), lambda b, i: (b, i, 0)),
                  pl.BlockSpec((1, rows, D_HGRN), lambda b, i: (b, i, 0)),
                  pl.BlockSpec((1, rows, D_CONV), lambda b, i: (b, i, 0)),
                  pl.BlockSpec((1, rows, D_CONV), lambda b, i: (b, i, gb_block)),
                  pl.BlockSpec((1, 1, D_MODEL), lambda b, i: (b, 0, 0)),
                  pl.BlockSpec((1, D_CONV), lambda b, i: (0, 0)),
                  pl.BlockSpec((1, D_CONV), lambda b, i: (0, 0)),
                  pl.BlockSpec((D_HGRN, D_MODEL), lambda b, i: (0, 0)),
                  pl.BlockSpec((D_CONV, D_MODEL), lambda b, i: (0, 0)),
                  pl.BlockSpec((1, D_MODEL), lambda b, i: (0, 0))],
        out_specs=pl.BlockSpec((1, rows, D_MODEL), lambda b, i: (b, i, 0)),
        out_shape=jax.ShapeDtypeStruct((bsz, t, D_MODEL), jnp.float32),
        compiler_params=pltpu.CompilerParams(dimension_semantics=("arbitrary", "arbitrary"),
                                             vmem_limit_bytes=VMEM_LIMIT),
        name="out_projection",
    )(x, branch_a, y_conv, p_lat, gate, ln_g, ln_b, w_a, w_b, final_g)


def kernel(x, c, ctx, c_ctx, norm_g, w_mod, b_mod, w_in, lb_logits, hgrn_norm_g, conv_w, conv_b,
           conv_ln_g, conv_ln_b, w_out, final_norm_g):
    bsz, seq_len, _ = x.shape
    assert norm_g.shape[0] == 1, "single-layer block"
    assert seq_len % GRID_W == 0 and seq_len % SCAN_CHUNK == 0 and ctx.shape[1] % SCAN_CHUNK == 0

    pad = (-(bsz + 1)) % 8
    cc = jnp.concatenate([c, c_ctx[None, :], jnp.zeros((pad, D_MODEL), c.dtype)], axis=0)
    mod = _modulation(cc, w_mod[0], b_mod)
    shift, scale, gate = (mod[:, i * D_MODEL:(i + 1) * D_MODEL] for i in range(3))
    shift_lat, scale_lat, gate_lat = (m[:bsz, None, :] for m in (shift, scale, gate))
    shift_ctx, scale_ctx = (jnp.broadcast_to(m[bsz][None, None, :], (bsz, 1, D_MODEL)) for m in (shift, scale))

    w_in_bf16 = w_in[0].astype(jnp.bfloat16)
    p_lat = _projection(x, norm_g, shift_lat, scale_lat, w_in_bf16)
    p_ctx = _projection(ctx, norm_g, shift_ctx, scale_ctx, w_in_bf16)

    branch_a = _hgrn_scan(p_lat, p_ctx, lb_logits, hgrn_norm_g)
    y_conv = _axial_conv(p_lat, conv_w[0], conv_b)
    return _output(x, branch_a, y_conv, p_lat, gate_lat, conv_ln_g, conv_ln_b,
                   w_out[0].astype(jnp.bfloat16), final_norm_g[None, :])
```

```python
import functools

import numpy as np
import jax
import jax.numpy as jnp
from jax import lax
from jax.experimental import pallas as pl
from jax.experimental.pallas import tpu as pltpu

D_MODEL = 1024
CTX_LEN = 256
GRID_W = 64
D_HGRN = 512
HGRN_HEADS = 4
HEAD_DIM = D_HGRN // HGRN_HEADS
D_CONV = 512
CONV_WIDTH = 31
CONV_HALF = CONV_WIDTH // 2
D_MIX = D_HGRN + D_CONV
D_IN = 5 * D_HGRN + 3 * D_CONV
EPS = 1e-6

LANES = 128
SCAN_CHUNK = 64
SCAN_LEVELS = SCAN_CHUNK.bit_length() - 1
SCAN_GROUP = 4
PROJ_ROWS = 256
OUT_ROWS = 512
CONV_ROWS = 64
VMEM_LIMIT = 56 * 1024 * 1024

COL_Q, COL_ZF, COL_ZB, COL_V, COL_GA = (i * HGRN_HEADS for i in range(5))
COL_U, COL_UG, COL_GB = (5 * HGRN_HEADS + i * (D_CONV // LANES) for i in range(3))


def _silu(x):
    return x * jax.nn.sigmoid(x)


def _mod_kernel(c_ref, w_ref, b_ref, o_ref):
    a = _silu(c_ref[...])
    o_ref[...] = jnp.dot(a, w_ref[...], preferred_element_type=jnp.float32,
                         precision=lax.Precision.HIGHEST) + b_ref[...]


def _modulation(cc, w_mod, b_mod):
    rows = cc.shape[0]
    n = w_mod.shape[1]
    return pl.pallas_call(
        _mod_kernel,
        grid=(n // D_MODEL,),
        in_specs=[pl.BlockSpec((rows, D_MODEL), lambda j: (0, 0)),
                  pl.BlockSpec((D_MODEL, D_MODEL), lambda j: (0, j)),
                  pl.BlockSpec((1, D_MODEL), lambda j: (0, j))],
        out_specs=pl.BlockSpec((rows, D_MODEL), lambda j: (0, j)),
        out_shape=jax.ShapeDtypeStruct((rows, n), jnp.float32),
        compiler_params=pltpu.CompilerParams(dimension_semantics=("arbitrary",),
                                             vmem_limit_bytes=VMEM_LIMIT),
        name="modulation",
    )(cc, w_mod, b_mod)


def _proj_kernel(x_ref, g_ref, sh_ref, sc_ref, w_ref, o_ref):
    x = x_ref[0]
    y = x * lax.rsqrt(jnp.mean(x * x, axis=-1, keepdims=True) + EPS) * g_ref[...]
    a = y * (1.0 + sc_ref[0]) + sh_ref[0]
    o_ref[0] = jnp.dot(a.astype(jnp.bfloat16), w_ref[...], preferred_element_type=jnp.float32)


def _projection(x, norm_g, shift, scale, w_in_bf16):
    bsz, t, _ = x.shape
    rows = min(PROJ_ROWS, t)
    return pl.pallas_call(
        _proj_kernel,
        grid=(bsz, t // rows),
        in_specs=[pl.BlockSpec((1, rows, D_MODEL), lambda b, i: (b, i, 0)),
                  pl.BlockSpec((1, D_MODEL), lambda b, i: (0, 0)),
                  pl.BlockSpec((1, 1, D_MODEL), lambda b, i: (b, 0, 0)),
                  pl.BlockSpec((1, 1, D_MODEL), lambda b, i: (b, 0, 0)),
                  pl.BlockSpec((D_MODEL, D_IN), lambda b, i: (0, 0))],
        out_specs=pl.BlockSpec((1, rows, D_IN), lambda b, i: (b, i, 0)),
        out_shape=jax.ShapeDtypeStruct((bsz, t, D_IN), jnp.float32),
        compiler_params=pltpu.CompilerParams(dimension_semantics=("arbitrary", "arbitrary"),
                                             vmem_limit_bytes=VMEM_LIMIT),
        name="in_projection",
    )(x, norm_g, shift, scale, w_in_bf16)


def _scan_constants():
    c = SCAN_CHUNK
    idx = np.arange(c)
    t, u = idx[:, None], idx[None, :]
    blocks = [(u <= t), (u > t)]
    masks = [t == u]
    for lvl in range(SCAN_LEVELS):
        h = c >> (lvl + 1)
        r = (t // (2 * h)) * (2 * h) + h - 1
        right = (t % (2 * h)) >= h
        blocks.append(np.where(right, (u > r) & (u <= t), (u > t) & (u <= r)))
        s = u
        masks.append(((t // (2 * h)) == (s // (2 * h))) & right & ((s % (2 * h)) < h))
    w_f = np.concatenate([b.astype(np.float32) for b in blocks], axis=0)
    m_f = np.stack([m.astype(np.float32) for m in masks])
    w_b = np.concatenate([b[::-1, ::-1].astype(np.float32) for b in blocks], axis=0)
    m_b = m_f[:, ::-1, ::-1].copy()
    w_f3 = np.concatenate([w_f] * 3, axis=1)
    w_b3 = np.concatenate([w_b] * 3, axis=1)
    return w_f3, w_b3, m_f, m_b


def _dot_nt(a, b):
    return lax.dot_general(a, b, (((1,), (1,)), ((), ())), preferred_element_type=jnp.float32)


def _dot_tn(a, b):
    return lax.dot_general(a, b, (((0,), (0,)), ((), ())), preferred_element_type=jnp.float32)


def _mix_rows(q, k, half, reverse):
    parts = []
    for lo in range(0, SCAN_CHUNK, 2 * half):
        first, second = (q, k) if reverse else (k, q)
        parts += [first[lo:lo + half], second[lo + half:lo + 2 * half]]
    return jnp.concatenate(parts, axis=0)


def _chunk_local(q, k, v, decay, m_ref, end_row, reverse):
    c = SCAN_CHUNK
    bf = jnp.bfloat16
    d_read = decay[0:c]
    d_state = decay[c:2 * c]
    a = m_ref[0] * _dot_nt(q.astype(bf), k.astype(bf))
    for lvl in range(SCAN_LEVELS):
        half = c >> (lvl + 1)
        d = decay[(2 + lvl) * c:(3 + lvl) * c]
        if half >= 8:
            x = (_mix_rows(q, k, half, reverse) * d).astype(bf)
            p = _dot_nt(x, x)
        else:
            p = _dot_nt((q * d).astype(bf), (k * d).astype(bf))
        a = a + m_ref[lvl + 1] * p
    vb = v.astype(bf)
    o_intra = jnp.dot(a.astype(bf), vb, preferred_element_type=jnp.float32)
    kv = _dot_tn(vb, (k * d_state).astype(bf))
    return o_intra, kv, (q * d_read).astype(bf), d_read[end_row:end_row + 1]


def _direction_group(q_r, z_r, v_r, starts, st_ref, lb, w_ref, m_ref, reverse, o_r):
    c = SCAN_CHUNK
    bf = jnp.bfloat16
    end_row = 0 if reverse else c - 1
    qs = [q_r[0, pl.ds(s, c), :] for s in starts]
    vs = [v_r[0, pl.ds(s, c), :] for s in starts]
    ks, g3s = [], []
    for s in starts:
        f = lb + (1.0 - lb) * jax.nn.sigmoid(z_r[0, pl.ds(s, c), :])
        g = jnp.log2(f)
        ks.append(1.0 - f)
        g_hi = g.astype(bf)
        r1 = g - g_hi.astype(jnp.float32)
        g_mid = r1.astype(bf)
        g_lo = (r1 - g_mid.astype(jnp.float32)).astype(bf)
        g3s.append(jnp.concatenate([g_hi, g_mid, g_lo], axis=0))
    local = []
    for j in range(0, len(starts), 2):
        pair = jnp.concatenate(g3s[j:j + 2], axis=1)
        decay = jnp.exp2(jnp.dot(w_ref[...], pair, preferred_element_type=jnp.float32))
        for jj in range(2):
            local.append(_chunk_local(qs[j + jj], ks[j + jj], vs[j + jj],
                                      decay[:, jj * LANES:(jj + 1) * LANES], m_ref, end_row, reverse))
    st = st_ref[...]
    for s, (o_intra, kv, q_read, d_end) in zip(starts, local):
        if o_r is not None:
            o_r[pl.ds(s, c), :] = o_intra + _dot_nt(q_read, st.astype(bf))
        st = st * d_end + kv
    st_ref[...] = st


def _scan_kernel(q_ref, zf_ref, zb_ref, v_ref, ga_ref, qc_ref, zfc_ref, zbc_ref, vc_ref,
                 lbl_ref, gn_ref, wf_ref, wb_ref, mf_ref, mb_ref, o_ref,
                 stf_ref, stb_ref, of_ref, ob_ref):
    c = SCAN_CHUNK
    span = SCAN_GROUP * c
    t_lat = q_ref.shape[1]
    t_ctx = qc_ref.shape[1]
    l0, l1 = lbl_ref[0], lbl_ref[1]
    mx = jnp.maximum(l0, l1)
    e0, e1 = jnp.exp(l0 - mx), jnp.exp(l1 - mx)
    lb = e0 / (e0 + e1)
    lb_f, lb_b = lb[0:1], lb[1:2]

    stf_ref[...] = jnp.zeros_like(stf_ref)
    stb_ref[...] = jnp.zeros_like(stb_ref)

    def both_directions(i, t, q_r, zf_r, zb_r, v_r, of_r, ob_r):
        lo = pl.multiple_of(i * span, span)
        hi = pl.multiple_of(t - span - i * span, span)
        _direction_group(q_r, zf_r, v_r, [pl.multiple_of(lo + j * c, c) for j in range(SCAN_GROUP)],
                         stf_ref, lb_f, wf_ref, mf_ref, False, of_r)
        _direction_group(q_r, zb_r, v_r, [pl.multiple_of(hi + j * c, c) for j in reversed(range(SCAN_GROUP))],
                         stb_ref, lb_b, wb_ref, mb_ref, True, ob_r)

    def ctx_body(i, carry):
        both_directions(i, t_ctx, qc_ref, zfc_ref, zbc_ref, vc_ref, None, None)
        return carry

    lax.fori_loop(0, t_ctx // span, ctx_body, 0)

    def lat_body(i, carry):
        both_directions(i, t_lat, q_ref, zf_ref, zb_ref, v_ref, of_ref, ob_ref)
        return carry

    lax.fori_loop(0, t_lat // span, lat_body, 0)

    rows = 256
    gn = gn_ref[...]

    def fin_body(i, carry):
        lo = pl.multiple_of(i * rows, rows)
        o = of_ref[pl.ds(lo, rows), :] + ob_ref[pl.ds(lo, rows), :]
        o = o * lax.rsqrt(jnp.mean(o * o, axis=-1, keepdims=True) + EPS) * gn
        o_ref[0, pl.ds(lo, rows), :] = (o * _silu(ga_ref[0, pl.ds(lo, rows), :])).astype(o_ref.dtype)
        return carry

    lax.fori_loop(0, t_lat // rows, fin_body, 0)


def _hgrn_scan(p_lat, p_ctx, lb_logits, hgrn_norm_g):
    bsz, t_lat, _ = p_lat.shape
    t_ctx = p_ctx.shape[1]
    w_f3, w_b3, m_f, m_b = _scan_constants()
    wf = jnp.asarray(w_f3, jnp.bfloat16)
    wb = jnp.asarray(w_b3, jnp.bfloat16)
    mf = jnp.asarray(m_f, jnp.float32)
    mb = jnp.asarray(m_b, jnp.float32)

    def col(t, base):
        return pl.BlockSpec((1, t, LANES), lambda b, h, base=base: (b, 0, base + h))

    def whole(a):
        return pl.BlockSpec(a.shape, lambda b, h, nd=a.ndim: (0,) * nd)

    return pl.pallas_call(
        _scan_kernel,
        grid=(bsz, HGRN_HEADS),
        in_specs=[col(t_lat, COL_Q), col(t_lat, COL_ZF), col(t_lat, COL_ZB), col(t_lat, COL_V),
                  col(t_lat, COL_GA),
                  col(t_ctx, COL_Q), col(t_ctx, COL_ZF), col(t_ctx, COL_ZB), col(t_ctx, COL_V),
                  pl.BlockSpec((2, 2, LANES), lambda b, h: (0, 0, h)),
                  pl.BlockSpec((1, LANES), lambda b, h: (0, h)),
                  whole(wf), whole(wb), whole(mf), whole(mb)],
        out_specs=pl.BlockSpec((1, t_lat, LANES), lambda b, h: (b, 0, h)),
        out_shape=jax.ShapeDtypeStruct((bsz, t_lat, D_HGRN), jnp.bfloat16),
        scratch_shapes=[pltpu.VMEM((HEAD_DIM, HEAD_DIM), jnp.float32),
                        pltpu.VMEM((HEAD_DIM, HEAD_DIM), jnp.float32),
                        pltpu.VMEM((t_lat, HEAD_DIM), jnp.float32),
                        pltpu.VMEM((t_lat, HEAD_DIM), jnp.float32)],
        compiler_params=pltpu.CompilerParams(dimension_semantics=("arbitrary", "arbitrary"),
                                             vmem_limit_bytes=VMEM_LIMIT),
        name="hgrn_scan",
    )(p_lat, p_lat, p_lat, p_lat, p_lat, p_ctx, p_ctx, p_ctx, p_ctx,
      lb_logits, hgrn_norm_g, wf, wb, mf, mb)


def _conv_kernel(u_ref, ug_ref, w_ref, b_ref, o_ref, pad_ref):
    t = u_ref.shape[1]
    n_rows = t // GRID_W
    cg = pl.program_id(1)
    bias = b_ref[...]

    @pl.when(cg < (D_CONV // 2) // LANES)
    def _along_rows():
        padw = GRID_W + 32

        def fill(r, carry):
            src = pl.multiple_of(r * GRID_W, GRID_W)
            dst = pl.multiple_of(r * padw, 32)
            glu = u_ref[0, pl.ds(src, GRID_W), :] * jax.nn.sigmoid(ug_ref[0, pl.ds(src, GRID_W), :])
            pad_ref[pl.ds(dst, 16), :] = jnp.zeros((16, LANES), jnp.float32)
            pad_ref[pl.ds(dst + 16, GRID_W), :] = glu
            pad_ref[pl.ds(dst + 16 + GRID_W, 16), :] = jnp.zeros((16, LANES), jnp.float32)
            return carry

        lax.fori_loop(0, n_rows, fill, 0)

        def conv(r, carry):
            dst = pl.multiple_of(r * GRID_W, GRID_W)
            base = r * padw + (16 - CONV_HALF)
            acc = jnp.zeros((GRID_W, LANES), jnp.float32)
            for k in range(CONV_WIDTH):
                acc = acc + w_ref[k:k + 1, :] * pad_ref[pl.ds(base + k, GRID_W), :]
            o_ref[0, pl.ds(dst, GRID_W), :] = acc + bias
            return carry

        lax.fori_loop(0, n_rows, conv, 0)

    @pl.when(cg >= (D_CONV // 2) // LANES)
    def _along_cols():
        halo = CONV_HALF * GRID_W
        pad_ref[pl.ds(0, halo), :] = jnp.zeros((halo, LANES), jnp.float32)
        pad_ref[pl.ds(halo + t, halo), :] = jnp.zeros((halo, LANES), jnp.float32)

        def fill(r, carry):
            src = pl.multiple_of(r * GRID_W, GRID_W)
            glu = u_ref[0, pl.ds(src, GRID_W), :] * jax.nn.sigmoid(ug_ref[0, pl.ds(src, GRID_W), :])
            pad_ref[pl.ds(halo + src, GRID_W), :] = glu
            return carry

        lax.fori_loop(0, n_rows, fill, 0)

        def conv(r, carry):
            dst = pl.multiple_of(r * GRID_W, GRID_W)
            acc = jnp.zeros((GRID_W, LANES), jnp.float32)
            for k in range(CONV_WIDTH):
                acc = acc + w_ref[k:k + 1, :] * pad_ref[pl.ds(dst + k * GRID_W, GRID_W), :]
            o_ref[0, pl.ds(dst, GRID_W), :] = acc + bias
            return carry

        lax.fori_loop(0, n_rows, conv, 0)


def _axial_conv(p_lat, conv_w, conv_b):
    bsz, t, _ = p_lat.shape
    n_rows = t // GRID_W
    pad_rows = max(n_rows * (GRID_W + 32), t + 2 * CONV_HALF * GRID_W)
    w_pad = jnp.zeros((32, D_CONV), jnp.float32).at[:CONV_WIDTH].set(conv_w)
    return pl.pallas_call(
        _conv_kernel,
        grid=(bsz, D_CONV // LANES),
        in_specs=[pl.BlockSpec((1, t, LANES), lambda b, g: (b, 0, COL_U + g)),
                  pl.BlockSpec((1, t, LANES), lambda b, g: (b, 0, COL_UG + g)),
                  pl.BlockSpec((32, LANES), lambda b, g: (0, g)),
                  pl.BlockSpec((1, LANES), lambda b, g: (0, g))],
        out_specs=pl.BlockSpec((1, t, LANES), lambda b, g: (b, 0, g)),
        out_shape=jax.ShapeDtypeStruct((bsz, t, D_CONV), jnp.float32),
        scratch_shapes=[pltpu.VMEM((pad_rows, LANES), jnp.float32)],
        compiler_params=pltpu.CompilerParams(dimension_semantics=("arbitrary", "arbitrary"),
                                             vmem_limit_bytes=VMEM_LIMIT),
        name="axial_conv",
    )(p_lat, p_lat, w_pad, conv_b)


def _out_kernel(x_ref, ba_ref, y_ref, gb_ref, gt_ref, lng_ref, lnb_ref, wa_ref, wb_ref, fg_ref, o_ref):
    y = y_ref[0]
    mu = jnp.mean(y, axis=-1, keepdims=True)
    yc = y - mu
    var = jnp.mean(yc * yc, axis=-1, keepdims=True)
    yn = yc * lax.rsqrt(var + EPS) * lng_ref[...] + lnb_ref[...]
    branch_b = _silu(yn) * _silu(gb_ref[0])
    mix = jnp.dot(ba_ref[0], wa_ref[...], preferred_element_type=jnp.float32)
    mix = mix + jnp.dot(branch_b.astype(jnp.bfloat16), wb_ref[...], preferred_element_type=jnp.float32)
    h = x_ref[0] + gt_ref[0] * mix
    o_ref[0] = h * lax.rsqrt(jnp.mean(h * h, axis=-1, keepdims=True) + EPS) * fg_ref[...]


def _output(x, branch_a, y_conv, p_lat, gate, ln_g, ln_b, w_out_bf16, final_g):
    bsz, t, _ = x.shape
    rows = OUT_ROWS
    w_a, w_b = w_out_bf16[:D_HGRN], w_out_bf16[D_HGRN:]
    gb_block = (5 * D_HGRN + 2 * D_CONV) // D_CONV
    return pl.pallas_call(
        _out_kernel,
        grid=(bsz, t // rows),
        in_specs=[pl.BlockSpec((1, rows, D_MODEL), lambda b, i: (b, i, 0)),
                  pl.BlockSpec((1, rows, D_HGRN), lambda b, i: (b, i, 0)),
                  pl.BlockSpec((1, rows, D_CONV), lambda b, i: (b, i, 0)),
                  pl.BlockSpec((1, rows, D_CONV), lambda b, i: (b, i, gb_block)),
                  pl.BlockSpec((1, 1, D_MODEL), lambda b, i: (b, 0, 0)),
                  pl.BlockSpec((1, D_CONV), lambda b, i: (0, 0)),
                  pl.BlockSpec((1, D_CONV), lambda b, i: (0, 0)),
                  pl.BlockSpec((D_HGRN, D_MODEL), lambda b, i: (0, 0)),
                  pl.BlockSpec((D_CONV, D_MODEL), lambda b, i: (0, 0)),
                  pl.BlockSpec((1, D_MODEL), lambda b, i: (0, 0))],
        out_specs=pl.BlockSpec((1, rows, D_MODEL), lambda b, i: (b, i, 0)),
        out_shape=jax.ShapeDtypeStruct((bsz, t, D_MODEL), jnp.float32),
        compiler_params=pltpu.CompilerParams(dimension_semantics=("arbitrary", "arbitrary"),
                                             vmem_limit_bytes=VMEM_LIMIT),
        name="out_projection",
    )(x, branch_a, y_conv, p_lat, gate, ln_g, ln_b, w_a, w_b, final_g)


def kernel(x, c, ctx, c_ctx, norm_g, w_mod, b_mod, w_in, lb_logits, hgrn_norm_g, conv_w, conv_b,
           conv_ln_g, conv_ln_b, w_out, final_norm_g):
    bsz, seq_len, _ = x.shape
    assert norm_g.shape[0] == 1, "single-layer block"
    assert seq_len % GRID_W == 0 and seq_len % SCAN_CHUNK == 0 and ctx.shape[1] % SCAN_CHUNK == 0

    pad = (-(bsz + 1)) % 8
    cc = jnp.concatenate([c, c_ctx[None, :], jnp.zeros((pad, D_MODEL), c.dtype)], axis=0)
    mod = _modulation(cc, w_mod[0], b_mod)
    shift, scale, gate = (mod[:, i * D_MODEL:(i + 1) * D_MODEL] for i in range(3))
    shift_lat, scale_lat, gate_lat = (m[:bsz, None, :] for m in (shift, scale, gate))
    shift_ctx, scale_ctx = (jnp.broadcast_to(m[bsz][None, None, :], (bsz, 1, D_MODEL)) for m in (shift, scale))

    w_in_bf16 = w_in[0].astype(jnp.bfloat16)
    p_lat = _projection(x, norm_g, shift_lat, scale_lat, w_in_bf16)
    p_ctx = _projection(ctx, norm_g, shift_ctx, scale_ctx, w_in_bf16)

    branch_a = _hgrn_scan(p_lat, p_ctx, lb_logits, hgrn_norm_g)
    y_conv = _axial_conv(p_lat, conv_w[0], conv_b)
    return _output(x, branch_a, y_conv, p_lat, gate_lat, conv_ln_g, conv_ln_b,
                   w_out[0].astype(jnp.bfloat16), final_norm_g[None, :])
```

```python
import numpy as np
import jax
import jax.numpy as jnp
from jax import lax
from jax.experimental import pallas as pl
from jax.experimental.pallas import tpu as pltpu

D_MODEL = 1024
CTX_LEN = 256
GRID_W = 64
D_HGRN = 512
HGRN_HEADS = 4
HEAD_DIM = D_HGRN // HGRN_HEADS
D_CONV = 512
CONV_WIDTH = 31
CONV_HALF = CONV_WIDTH // 2
D_MIX = D_HGRN + D_CONV
D_IN = 5 * D_HGRN + 3 * D_CONV
EPS = 1e-6

LANES = 128
SUBLANES = 8
SCAN_CHUNK = 64
SCAN_LEVELS = SCAN_CHUNK.bit_length() - 1
SCAN_GROUP = 4
PROJ_ROWS = 256
OUT_ROWS = 512
VMEM_LIMIT = 56 * 1024 * 1024

COL_Q, COL_ZF, COL_ZB, COL_V, COL_GA = (i * HGRN_HEADS for i in range(5))
COL_U, COL_UG, COL_GB = (5 * HGRN_HEADS + i * (D_CONV // LANES) for i in range(3))


def _silu(x):
    return x * jax.nn.sigmoid(x)


def _mod_kernel(c_ref, w_ref, b_ref, o_ref):
    a = _silu(c_ref[...])
    o_ref[...] = jnp.dot(a, w_ref[...], preferred_element_type=jnp.float32,
                         precision=lax.Precision.HIGHEST) + b_ref[...]


def _modulation(cc, w_mod, b_mod):
    rows = cc.shape[0]
    n = w_mod.shape[1]
    return pl.pallas_call(
        _mod_kernel,
        grid=(n // D_MODEL,),
        in_specs=[pl.BlockSpec((rows, D_MODEL), lambda j: (0, 0)),
                  pl.BlockSpec((D_MODEL, D_MODEL), lambda j: (0, j)),
                  pl.BlockSpec((1, D_MODEL), lambda j: (0, j))],
        out_specs=pl.BlockSpec((rows, D_MODEL), lambda j: (0, j)),
        out_shape=jax.ShapeDtypeStruct((rows, n), jnp.float32),
        compiler_params=pltpu.CompilerParams(dimension_semantics=("arbitrary",),
                                             vmem_limit_bytes=VMEM_LIMIT),
        name="modulation",
    )(cc, w_mod, b_mod)


def _proj_kernel(x_ref, g_ref, sh_ref, sc_ref, w_ref, o_ref):
    x = x_ref[0]
    y = x * lax.rsqrt(jnp.mean(x * x, axis=-1, keepdims=True) + EPS) * g_ref[...]
    a = y * (1.0 + sc_ref[0]) + sh_ref[0]
    o_ref[0] = jnp.dot(a.astype(jnp.bfloat16), w_ref[...], preferred_element_type=jnp.float32)


def _projection(x, norm_g, shift, scale, w_in_bf16):
    bsz, t, _ = x.shape
    rows = min(PROJ_ROWS, t)
    return pl.pallas_call(
        _proj_kernel,
        grid=(bsz, t // rows),
        in_specs=[pl.BlockSpec((1, rows, D_MODEL), lambda b, i: (b, i, 0)),
                  pl.BlockSpec((1, D_MODEL), lambda b, i: (0, 0)),
                  pl.BlockSpec((1, 1, D_MODEL), lambda b, i: (b, 0, 0)),
                  pl.BlockSpec((1, 1, D_MODEL), lambda b, i: (b, 0, 0)),
                  pl.BlockSpec((D_MODEL, D_IN), lambda b, i: (0, 0))],
        out_specs=pl.BlockSpec((1, rows, D_IN), lambda b, i: (b, i, 0)),
        out_shape=jax.ShapeDtypeStruct((bsz, t, D_IN), jnp.float32),
        compiler_params=pltpu.CompilerParams(dimension_semantics=("arbitrary", "arbitrary"),
                                             vmem_limit_bytes=VMEM_LIMIT),
        name="in_projection",
    )(x, norm_g, shift, scale, w_in_bf16)


def _scan_constants():
    c = SCAN_CHUNK
    idx = np.arange(c)
    t, s = idx[:, None], idx[None, :]
    masks = [t == s]
    for lvl in range(SCAN_LEVELS):
        h = c >> (lvl + 1)
        masks.append(((t // (2 * h)) == (s // (2 * h))) & ((t % (2 * h)) >= h) & ((s % (2 * h)) < h))
    m_f = np.stack([m.astype(np.float32) for m in masks])
    m_b = m_f[:, ::-1, ::-1].copy()
    tri_f = (s <= t).astype(np.float32)
    tri_b = (s >= t).astype(np.float32)
    return np.concatenate([tri_f] * 3, axis=1), np.concatenate([tri_b] * 3, axis=1), m_f, m_b


def _dot_nt(a, b):
    return lax.dot_general(a, b, (((1,), (1,)), ((), ())), preferred_element_type=jnp.float32)


def _dot_tn(a, b):
    return lax.dot_general(a, b, (((0,), (0,)), ((), ())), preferred_element_type=jnp.float32)


def _mix_rows(q, k, half, reverse):
    parts = []
    for lo in range(0, SCAN_CHUNK, 2 * half):
        first, second = (q, k) if reverse else (k, q)
        parts += [first[lo:lo + half], second[lo + half:lo + 2 * half]]
    return jnp.concatenate(parts, axis=0)


def _level_decay(g_cum, f, half, reverse):
    c = SCAN_CHUNK
    if half >= SUBLANES:
        parts = []
        for lo in range(0, c, 2 * half):
            mid = lo + half
            if reverse:
                parts += [g_cum[lo:mid] - g_cum[mid:mid + 1], g_cum[mid:mid + 1] - g_cum[mid:mid + half]]
            else:
                parts += [g_cum[mid - 1:mid] - g_cum[lo:mid], g_cum[mid:mid + half] - g_cum[mid - 1:mid]]
        return jnp.exp2(jnp.concatenate(parts, axis=0))
    if half == SUBLANES // 2:
        g3 = g_cum.reshape(c // SUBLANES, SUBLANES, LANES)
        r = half if reverse else half - 1
        later = lax.broadcasted_iota(jnp.int32, g3.shape, 1) >= half
        sign = jnp.where(later != reverse, 1.0, -1.0)
        return jnp.exp2((g3 - g3[:, r:r + 1, :]) * sign).reshape(c, LANES)
    f3 = f.reshape(c // SUBLANES, SUBLANES, LANES)
    row = lax.broadcasted_iota(jnp.int32, f3.shape, 1)
    if half == 1:
        on_query_side = (row % 2 == 0) if reverse else (row % 2 == 1)
        return jnp.where(on_query_side, f3, 1.0).reshape(c, LANES)
    prev = pltpu.roll(f3, 1, 1)
    nxt = pltpu.roll(f3, SUBLANES - 1, 1)
    m4 = row % 4
    if reverse:
        d = jnp.where(m4 == 0, f3 * nxt, jnp.where(m4 == 1, f3, jnp.where(m4 == 2, 1.0, prev)))
    else:
        d = jnp.where(m4 == 0, nxt, jnp.where(m4 == 1, 1.0, jnp.where(m4 == 2, f3, f3 * prev)))
    return d.reshape(c, LANES)


def _chunk_local(q, k, v, f, g_cum, m_ref, reverse):
    c = SCAN_CHUNK
    bf = jnp.bfloat16
    end_row = 0 if reverse else c - 1
    d_read = jnp.exp2(g_cum)
    d_state = jnp.exp2(g_cum[end_row:end_row + 1] - g_cum)
    kb = k.astype(bf)
    a = m_ref[0] * _dot_nt(q.astype(bf), kb)
    for lvl in range(SCAN_LEVELS):
        half = c >> (lvl + 1)
        d = _level_decay(g_cum, f, half, reverse)
        if half >= SUBLANES:
            x = (_mix_rows(q, k, half, reverse) * d).astype(bf)
            p = _dot_nt(x, x)
        elif half == 1:
            p = _dot_nt((q * d).astype(bf), kb)
        else:
            p = _dot_nt((q * d).astype(bf), (k * d).astype(bf))
        a = a + m_ref[lvl + 1] * p
    vb = v.astype(bf)
    o_intra = jnp.dot(a.astype(bf), vb, preferred_element_type=jnp.float32)
    kv = _dot_tn(vb, (k * d_state).astype(bf))
    return o_intra, kv, (q * d_read).astype(bf), d_read[end_row:end_row + 1]


def _direction_group(q_r, z_r, v_r, starts, st_ref, lb, w_ref, m_ref, reverse, o_r):
    c = SCAN_CHUNK
    bf = jnp.bfloat16
    qs = [q_r[0, pl.ds(s, c), :] for s in starts]
    vs = [v_r[0, pl.ds(s, c), :] for s in starts]
    fs, g3s = [], []
    for s in starts:
        f = lb + (1.0 - lb) * jax.nn.sigmoid(z_r[0, pl.ds(s, c), :])
        g = jnp.log2(f)
        fs.append(f)
        g_hi = g.astype(bf)
        r1 = g - g_hi.astype(jnp.float32)
        g_mid = r1.astype(bf)
        g_lo = (r1 - g_mid.astype(jnp.float32)).astype(bf)
        g3s.append(jnp.concatenate([g_hi, g_mid, g_lo], axis=0))
    local = []
    for j in range(0, len(starts), 2):
        pair = jnp.concatenate(g3s[j:j + 2], axis=1)
        g_cum = jnp.dot(w_ref[...], pair, preferred_element_type=jnp.float32)
        for jj in range(2):
            local.append(_chunk_local(qs[j + jj], 1.0 - fs[j + jj], vs[j + jj], fs[j + jj],
                                      g_cum[:, jj * LANES:(jj + 1) * LANES], m_ref, reverse))
    st = st_ref[...]
    for s, (o_intra, kv, q_read, d_end) in zip(starts, local):
        if o_r is not None:
            o_r[pl.ds(s, c), :] = o_intra + _dot_nt(q_read, st.astype(bf))
        st = st * d_end + kv
    st_ref[...] = st


def _scan_kernel(q_ref, zf_ref, zb_ref, v_ref, ga_ref, qc_ref, zfc_ref, zbc_ref, vc_ref,
                 lbl_ref, gn_ref, wf_ref, wb_ref, mf_ref, mb_ref, o_ref,
                 stf_ref, stb_ref, of_ref, ob_ref):
    c = SCAN_CHUNK
    span = SCAN_GROUP * c
    t_lat = q_ref.shape[1]
    t_ctx = qc_ref.shape[1]
    l0, l1 = lbl_ref[0], lbl_ref[1]
    mx = jnp.maximum(l0, l1)
    e0, e1 = jnp.exp(l0 - mx), jnp.exp(l1 - mx)
    lb = e0 / (e0 + e1)
    lb_f, lb_b = lb[0:1], lb[1:2]

    stf_ref[...] = jnp.zeros_like(stf_ref)
    stb_ref[...] = jnp.zeros_like(stb_ref)

    def both_directions(i, t, q_r, zf_r, zb_r, v_r, of_r, ob_r):
        lo = pl.multiple_of(i * span, span)
        hi = pl.multiple_of(t - span - i * span, span)
        _direction_group(q_r, zf_r, v_r, [pl.multiple_of(lo + j * c, c) for j in range(SCAN_GROUP)],
                         stf_ref, lb_f, wf_ref, mf_ref, False, of_r)
        _direction_group(q_r, zb_r, v_r, [pl.multiple_of(hi + j * c, c) for j in reversed(range(SCAN_GROUP))],
                         stb_ref, lb_b, wb_ref, mb_ref, True, ob_r)

    def ctx_body(i, carry):
        both_directions(i, t_ctx, qc_ref, zfc_ref, zbc_ref, vc_ref, None, None)
        return carry

    lax.fori_loop(0, t_ctx // span, ctx_body, 0)

    def lat_body(i, carry):
        both_directions(i, t_lat, q_ref, zf_ref, zb_ref, v_ref, of_ref, ob_ref)
        return carry

    lax.fori_loop(0, t_lat // span, lat_body, 0)

    rows = 256
    gn = gn_ref[...]

    def fin_body(i, carry):
        lo = pl.multiple_of(i * rows, rows)
        o = of_ref[pl.ds(lo, rows), :] + ob_ref[pl.ds(lo, rows), :]
        o = o * lax.rsqrt(jnp.mean(o * o, axis=-1, keepdims=True) + EPS) * gn
        o_ref[0, pl.ds(lo, rows), :] = (o * _silu(ga_ref[0, pl.ds(lo, rows), :])).astype(o_ref.dtype)
        return carry

    lax.fori_loop(0, t_lat // rows, fin_body, 0)


def _hgrn_scan(p_lat, p_ctx, lb_logits, hgrn_norm_g):
    bsz, t_lat, _ = p_lat.shape
    t_ctx = p_ctx.shape[1]
    w_f3, w_b3, m_f, m_b = _scan_constants()
    wf = jnp.asarray(w_f3, jnp.bfloat16)
    wb = jnp.asarray(w_b3, jnp.bfloat16)
    mf = jnp.asarray(m_f, jnp.float32)
    mb = jnp.asarray(m_b, jnp.float32)

    def col(t, base):
        return pl.BlockSpec((1, t, LANES), lambda b, h, base=base: (b, 0, base + h))

    def whole(a):
        return pl.BlockSpec(a.shape, lambda b, h, nd=a.ndim: (0,) * nd)

    return pl.pallas_call(
        _scan_kernel,
        grid=(bsz, HGRN_HEADS),
        in_specs=[col(t_lat, COL_Q), col(t_lat, COL_ZF), col(t_lat, COL_ZB), col(t_lat, COL_V),
                  col(t_lat, COL_GA),
                  col(t_ctx, COL_Q), col(t_ctx, COL_ZF), col(t_ctx, COL_ZB), col(t_ctx, COL_V),
                  pl.BlockSpec((2, 2, LANES), lambda b, h: (0, 0, h)),
                  pl.BlockSpec((1, LANES), lambda b, h: (0, h)),
                  whole(wf), whole(wb), whole(mf), whole(mb)],
        out_specs=pl.BlockSpec((1, t_lat, LANES), lambda b, h: (b, 0, h)),
        out_shape=jax.ShapeDtypeStruct((bsz, t_lat, D_HGRN), jnp.bfloat16),
        scratch_shapes=[pltpu.VMEM((HEAD_DIM, HEAD_DIM), jnp.float32),
                        pltpu.VMEM((HEAD_DIM, HEAD_DIM), jnp.float32),
                        pltpu.VMEM((t_lat, HEAD_DIM), jnp.float32),
                        pltpu.VMEM((t_lat, HEAD_DIM), jnp.float32)],
        compiler_params=pltpu.CompilerParams(dimension_semantics=("arbitrary", "arbitrary"),
                                             vmem_limit_bytes=VMEM_LIMIT),
        name="hgrn_scan",
    )(p_lat, p_lat, p_lat, p_lat, p_lat, p_ctx, p_ctx, p_ctx, p_ctx,
      lb_logits, hgrn_norm_g, wf, wb, mf, mb)


def _conv_kernel(u_ref, ug_ref, w_ref, b_ref, o_ref, pad_ref):
    t = u_ref.shape[1]
    n_rows = t // GRID_W
    cg = pl.program_id(1)
    bias = b_ref[...]

    @pl.when(cg < (D_CONV // 2) // LANES)
    def _along_rows():
        padw = GRID_W + 32

        def fill(r, carry):
            src = pl.multiple_of(r * GRID_W, GRID_W)
            dst = pl.multiple_of(r * padw, 32)
            glu = u_ref[0, pl.ds(src, GRID_W), :] * jax.nn.sigmoid(ug_ref[0, pl.ds(src, GRID_W), :])
            pad_ref[pl.ds(dst, 16), :] = jnp.zeros((16, LANES), jnp.float32)
            pad_ref[pl.ds(dst + 16, GRID_W), :] = glu
            pad_ref[pl.ds(dst + 16 + GRID_W, 16), :] = jnp.zeros((16, LANES), jnp.float32)
            return carry

        lax.fori_loop(0, n_rows, fill, 0)

        def conv(r, carry):
            dst = pl.multiple_of(r * GRID_W, GRID_W)
            base = r * padw + (16 - CONV_HALF)
            acc = jnp.zeros((GRID_W, LANES), jnp.float32)
            for k in range(CONV_WIDTH):
                acc = acc + w_ref[k:k + 1, :] * pad_ref[pl.ds(base + k, GRID_W), :]
            o_ref[0, pl.ds(dst, GRID_W), :] = acc + bias
            return carry

        lax.fori_loop(0, n_rows, conv, 0)

    @pl.when(cg >= (D_CONV // 2) // LANES)
    def _along_cols():
        halo = CONV_HALF * GRID_W
        pad_ref[pl.ds(0, halo), :] = jnp.zeros((halo, LANES), jnp.float32)
        pad_ref[pl.ds(halo + t, halo), :] = jnp.zeros((halo, LANES), jnp.float32)

        def fill(r, carry):
            src = pl.multiple_of(r * GRID_W, GRID_W)
            glu = u_ref[0, pl.ds(src, GRID_W), :] * jax.nn.sigmoid(ug_ref[0, pl.ds(src, GRID_W), :])
            pad_ref[pl.ds(halo + src, GRID_W), :] = glu
            return carry

        lax.fori_loop(0, n_rows, fill, 0)

        def conv(r, carry):
            dst = pl.multiple_of(r * GRID_W, GRID_W)
            acc = jnp.zeros((GRID_W, LANES), jnp.float32)
            for k in range(CONV_WIDTH):
                acc = acc + w_ref[k:k + 1, :] * pad_ref[pl.ds(dst + k * GRID_W, GRID_W), :]
            o_ref[0, pl.ds(dst, GRID_W), :] = acc + bias
            return carry

        lax.fori_loop(0, n_rows, conv, 0)


def _axial_conv(p_lat, conv_w, conv_b):
    bsz, t, _ = p_lat.shape
    n_rows = t // GRID_W
    pad_rows = max(n_rows * (GRID_W + 32), t + 2 * CONV_HALF * GRID_W)
    w_pad = jnp.zeros((32, D_CONV), jnp.float32).at[:CONV_WIDTH].set(conv_w)
    return pl.pallas_call(
        _conv_kernel,
        grid=(bsz, D_CONV // LANES),
        in_specs=[pl.BlockSpec((1, t, LANES), lambda b, g: (b, 0, COL_U + g)),
                  pl.BlockSpec((1, t, LANES), lambda b, g: (b, 0, COL_UG + g)),
                  pl.BlockSpec((32, LANES), lambda b, g: (0, g)),
                  pl.BlockSpec((1, LANES), lambda b, g: (0, g))],
        out_specs=pl.BlockSpec((1, t, LANES), lambda b, g: (b, 0, g)),
        out_shape=jax.ShapeDtypeStruct((bsz, t, D_CONV), jnp.float32),
        scratch_shapes=[pltpu.VMEM((pad_rows, LANES), jnp.float32)],
        compiler_params=pltpu.CompilerParams(dimension_semantics=("arbitrary", "arbitrary"),
                                             vmem_limit_bytes=VMEM_LIMIT),
        name="axial_conv",
    )(p_lat, p_lat, w_pad, conv_b)


def _out_kernel(x_ref, ba_ref, y_ref, gb_ref, gt_ref, lng_ref, lnb_ref, wa_ref, wb_ref, fg_ref, o_ref):
    y = y_ref[0]
    mu = jnp.mean(y, axis=-1, keepdims=True)
    yc = y - mu
    var = jnp.mean(yc * yc, axis=-1, keepdims=True)
    yn = yc * lax.rsqrt(var + EPS) * lng_ref[...] + lnb_ref[...]
    branch_b = _silu(yn) * _silu(gb_ref[0])
    mix = jnp.dot(ba_ref[0], wa_ref[...], preferred_element_type=jnp.float32)
    mix = mix + jnp.dot(branch_b.astype(jnp.bfloat16), wb_ref[...], preferred_element_type=jnp.float32)
    h = x_ref[0] + gt_ref[0] * mix
    o_ref[0] = h * lax.rsqrt(jnp.mean(h * h, axis=-1, keepdims=True) + EPS) * fg_ref[...]


def _output(x, branch_a, y_conv, p_lat, gate, ln_g, ln_b, w_out_bf16, final_g):
    bsz, t, _ = x.shape
    rows = OUT_ROWS
    w_a, w_b = w_out_bf16[:D_HGRN], w_out_bf16[D_HGRN:]
    gb_block = (5 * D_HGRN + 2 * D_CONV) // D_CONV
    return pl.pallas_call(
        _out_kernel,
        grid=(bsz, t // rows),
        in_specs=[pl.BlockSpec((1, rows, D_MODEL), lambda b, i: (b, i, 0)),
                  pl.BlockSpec((1, rows, D_HGRN), lambda b, i: (b, i, 0)),
                  pl.BlockSpec((1, rows, D_CONV), lambda b, i: (b, i, 0)),
                  pl.BlockSpec((1, rows, D_CONV), lambda b, i: (b, i, gb_block)),
                  pl.BlockSpec((1, 1, D_MODEL), lambda b, i: (b, 0, 0)),
                  pl.BlockSpec((1, D_CONV), lambda b, i: (0, 0)),
                  pl.BlockSpec((1, D_CONV), lambda b, i: (0, 0)),
                  pl.BlockSpec((D_HGRN, D_MODEL), lambda b, i: (0, 0)),
                  pl.BlockSpec((D_CONV, D_MODEL), lambda b, i: (0, 0)),
                  pl.BlockSpec((1, D_MODEL), lambda b, i: (0, 0))],
        out_specs=pl.BlockSpec((1, rows, D_MODEL), lambda b, i: (b, i, 0)),
        out_shape=jax.ShapeDtypeStruct((bsz, t, D_MODEL), jnp.float32),
        compiler_params=pltpu.CompilerParams(dimension_semantics=("arbitrary", "arbitrary"),
                                             vmem_limit_bytes=VMEM_LIMIT),
        name="out_projection",
    )(x, branch_a, y_conv, p_lat, gate, ln_g, ln_b, w_a, w_b, final_g)


def kernel(x, c, ctx, c_ctx, norm_g, w_mod, b_mod, w_in, lb_logits, hgrn_norm_g, conv_w, conv_b,
           conv_ln_g, conv_ln_b, w_out, final_norm_g):
    bsz, seq_len, _ = x.shape
    span = SCAN_GROUP * SCAN_CHUNK
    assert norm_g.shape[0] == 1, "single-layer block"
    assert seq_len % GRID_W == 0 and seq_len % span == 0 and ctx.shape[1] % span == 0

    pad = (-(bsz + 1)) % SUBLANES
    cc = jnp.concatenate([c, c_ctx[None, :], jnp.zeros((pad, D_MODEL), c.dtype)], axis=0)
    mod = _modulation(cc, w_mod[0], b_mod)
    shift, scale, gate = (mod[:, i * D_MODEL:(i + 1) * D_MODEL] for i in range(3))
    shift_lat, scale_lat, gate_lat = (m[:bsz, None, :] for m in (shift, scale, gate))
    shift_ctx, scale_ctx = (jnp.broadcast_to(m[bsz][None, None, :], (bsz, 1, D_MODEL)) for m in (shift, scale))

    w_in_bf16 = w_in[0].astype(jnp.bfloat16)
    p_lat = _projection(x, norm_g, shift_lat, scale_lat, w_in_bf16)
    p_ctx = _projection(ctx, norm_g, shift_ctx, scale_ctx, w_in_bf16)

    branch_a = _hgrn_scan(p_lat, p_ctx, lb_logits, hgrn_norm_g)
    y_conv = _axial_conv(p_lat, conv_w[0], conv_b)
    return _output(x, branch_a, y_conv, p_lat, gate_lat, conv_ln_g, conv_ln_b,
                   w_out[0].astype(jnp.bfloat16), final_norm_g[None, :])
```

```python
import numpy as np
import jax
import jax.numpy as jnp
from jax import lax
from jax.experimental import pallas as pl
from jax.experimental.pallas import tpu as pltpu

D_MODEL = 1024
CTX_LEN = 256
GRID_W = 64
D_HGRN = 512
HGRN_HEADS = 4
HEAD_DIM = D_HGRN // HGRN_HEADS
D_CONV = 512
CONV_WIDTH = 31
CONV_HALF = CONV_WIDTH // 2
D_MIX = D_HGRN + D_CONV
D_IN = 5 * D_HGRN + 3 * D_CONV
EPS = 1e-6

LANES = 128
SUBLANES = 8
PACKED_ROWS = 16
SCAN_CHUNK = 64
SCAN_LEVELS = SCAN_CHUNK.bit_length() - 1
SCAN_GROUP = 4
PROJ_ROWS = 256
OUT_ROWS = 512
VMEM_LIMIT = 56 * 1024 * 1024

COL_Q, COL_ZF, COL_ZB, COL_V, COL_GA = (i * HGRN_HEADS for i in range(5))
COL_U, COL_UG, COL_GB = (5 * HGRN_HEADS + i * (D_CONV // LANES) for i in range(3))


def _silu(x):
    return x * jax.nn.sigmoid(x)


def _mod_kernel(c_ref, w_ref, b_ref, o_ref):
    a = _silu(c_ref[...])
    o_ref[...] = jnp.dot(a, w_ref[...], preferred_element_type=jnp.float32,
                         precision=lax.Precision.HIGHEST) + b_ref[...]


def _modulation(cc, w_mod, b_mod):
    rows = cc.shape[0]
    n = w_mod.shape[1]
    return pl.pallas_call(
        _mod_kernel,
        grid=(n // D_MODEL,),
        in_specs=[pl.BlockSpec((rows, D_MODEL), lambda j: (0, 0)),
                  pl.BlockSpec((D_MODEL, D_MODEL), lambda j: (0, j)),
                  pl.BlockSpec((1, D_MODEL), lambda j: (0, j))],
        out_specs=pl.BlockSpec((rows, D_MODEL), lambda j: (0, j)),
        out_shape=jax.ShapeDtypeStruct((rows, n), jnp.float32),
        compiler_params=pltpu.CompilerParams(dimension_semantics=("arbitrary",),
                                             vmem_limit_bytes=VMEM_LIMIT),
        name="modulation",
    )(cc, w_mod, b_mod)


def _proj_kernel(x_ref, g_ref, sh_ref, sc_ref, w_ref, o_ref):
    x = x_ref[0]
    y = x * lax.rsqrt(jnp.mean(x * x, axis=-1, keepdims=True) + EPS) * g_ref[...]
    a = y * (1.0 + sc_ref[0]) + sh_ref[0]
    o_ref[0] = jnp.dot(a.astype(jnp.bfloat16), w_ref[...], preferred_element_type=jnp.float32)


def _projection(x, norm_g, shift, scale, w_in_bf16):
    bsz, t, _ = x.shape
    rows = min(PROJ_ROWS, t)
    return pl.pallas_call(
        _proj_kernel,
        grid=(bsz, t // rows),
        in_specs=[pl.BlockSpec((1, rows, D_MODEL), lambda b, i: (b, i, 0)),
                  pl.BlockSpec((1, D_MODEL), lambda b, i: (0, 0)),
                  pl.BlockSpec((1, 1, D_MODEL), lambda b, i: (b, 0, 0)),
                  pl.BlockSpec((1, 1, D_MODEL), lambda b, i: (b, 0, 0)),
                  pl.BlockSpec((D_MODEL, D_IN), lambda b, i: (0, 0))],
        out_specs=pl.BlockSpec((1, rows, D_IN), lambda b, i: (b, i, 0)),
        out_shape=jax.ShapeDtypeStruct((bsz, t, D_IN), jnp.float32),
        compiler_params=pltpu.CompilerParams(dimension_semantics=("arbitrary", "arbitrary"),
                                             vmem_limit_bytes=VMEM_LIMIT),
        name="in_projection",
    )(x, norm_g, shift, scale, w_in_bf16)


(OP_Q, OP_X32, OP_Q1, OP_X16, OP_X8, OP_Q4, OP_Q2, OP_QREAD, OP_KSTATE, OP_V, OP_W1, _, OP_W2, _, OP_W3, _) = range(16)
N_OPERANDS = 16
SCORE_BLOCKS = (("diag", 0), (32, 1), (1, 0), (16, 0), (8, 1), (4, 0), (2, 1))


def _scan_constants():
    c = SCAN_CHUNK
    idx = np.arange(c)
    t, s = idx[:, None], idx[None, :]
    m_f, m_b = [], []
    for h, lane_half in SCORE_BLOCKS:
        if h == "diag":
            m = t == s
        else:
            m = ((t // (2 * h)) == (s // (2 * h))) & ((t % (2 * h)) >= h) & ((s % (2 * h)) < h)
        for out, mm in ((m_f, m), (m_b, m[::-1, ::-1])):
            wide = np.zeros((c, 2 * c), np.float32)
            wide[:, lane_half * c:(lane_half + 1) * c] = mm
            out.append(wide)
    m_f, m_b = np.stack(m_f), np.stack(m_b)
    tri_f = (s <= t).astype(np.float32)
    tri_b = (s >= t).astype(np.float32)
    return np.concatenate([tri_f] * 3, axis=1), np.concatenate([tri_b] * 3, axis=1), m_f, m_b


def _dot_nt(a, b):
    return lax.dot_general(a, b, (((1,), (1,)), ((), ())), preferred_element_type=jnp.float32)


def _dot_tn(a, b):
    return lax.dot_general(a, b, (((0,), (0,)), ((), ())), preferred_element_type=jnp.float32)


def _gate_pass(z_refs, lbs, tri_refs, f_refs, g_refs, t):
    c = SCAN_CHUNK
    bf = jnp.bfloat16

    def body(i, carry):
        lo = pl.multiple_of(i * 2 * c, 2 * c)
        for z_r, lb, tri_r, f_r, g_r in zip(z_refs, lbs, tri_refs, f_refs, g_refs):
            f = lb + (1.0 - lb) * jax.nn.sigmoid(z_r[0, pl.ds(lo, 2 * c), :])
            f_r[pl.ds(lo, 2 * c), :] = f
            g = jnp.log2(f)
            g_hi = g.astype(bf)
            r1 = g - g_hi.astype(jnp.float32)
            g_mid = r1.astype(bf)
            g_lo = (r1 - g_mid.astype(jnp.float32)).astype(bf)
            pair = jnp.concatenate(
                [jnp.concatenate([g_hi[j * c:(j + 1) * c], g_mid[j * c:(j + 1) * c], g_lo[j * c:(j + 1) * c]],
                                 axis=0) for j in range(2)], axis=1)
            g_cum = jnp.dot(tri_r[...], pair, preferred_element_type=jnp.float32)
            g_r[pl.ds(lo, c), :] = g_cum[:, :LANES]
            g_r[pl.ds(lo + c, c), :] = g_cum[:, LANES:]
        return carry

    lax.fori_loop(0, t // (2 * c), body, 0, unroll=2)


def _chunk_row(ref, row0, r):
    blk = ref[pl.ds(pl.multiple_of(row0 + (r // SUBLANES) * SUBLANES, SUBLANES), SUBLANES), :]
    return blk[r % SUBLANES:r % SUBLANES + 1]


def _build_operands(q_r, v_r, f_r, g_r, row0, ops_r, slot, reverse):
    c = SCAN_CHUNK
    bf = jnp.bfloat16
    g_end = _chunk_row(g_r, row0, 0 if reverse else c - 1)
    key_side = {name: [] for name in ("k", "x32", "x16", "x8", "k4", "k2")}
    for r0 in range(0, c, PACKED_ROWS):
        rows = pl.ds(pl.multiple_of(row0 + r0, PACKED_ROWS), PACKED_ROWS)
        q = q_r[0, rows, :]
        f = f_r[rows, :]
        g = g_r[rows, :]
        k = 1.0 - f

        def put(op, val, key_name=None):
            val = val.astype(bf)
            if op is not None:
                ops_r[slot, op, r0:r0 + PACKED_ROWS, :] = val
            if key_name is not None:
                key_side[key_name].append(val)

        put(OP_Q, q)
        put(None, k, "k")
        put(OP_V, v_r[0, rows, :])
        put(OP_QREAD, q * jnp.exp2(g))
        put(OP_KSTATE, k * jnp.exp2(g_end - g))
        for op, half in ((OP_X32, 32), (OP_X16, 16)):
            mid = (r0 // (2 * half)) * 2 * half + half
            later = r0 >= mid
            g_split = _chunk_row(g_r, row0, mid if reverse else mid - 1)
            if later != reverse:
                put(op, q * jnp.exp2(g - g_split), "x%d" % half)
            else:
                put(op, k * jnp.exp2(g_split - g), "x%d" % half)
        h8 = SUBLANES
        if reverse:
            x8 = jnp.concatenate([q[:h8] * jnp.exp2(g[:h8] - g[h8:h8 + 1]),
                                  k[h8:] * jnp.exp2(g[h8:h8 + 1] - g[h8:])], axis=0)
        else:
            x8 = jnp.concatenate([k[:h8] * jnp.exp2(g[h8 - 1:h8] - g[:h8]),
                                  q[h8:] * jnp.exp2(g[h8:] - g[h8 - 1:h8])], axis=0)
        put(OP_X8, x8, "x8")
        shape3 = (PACKED_ROWS // SUBLANES, SUBLANES, LANES)
        g3 = g.reshape(shape3)
        row = lax.broadcasted_iota(jnp.int32, shape3, 1)
        r4 = SUBLANES // 2 if reverse else SUBLANES // 2 - 1
        sign = jnp.where((row >= SUBLANES // 2) != reverse, 1.0, -1.0)
        d4 = jnp.exp2((g3 - g3[:, r4:r4 + 1, :]) * sign).reshape(PACKED_ROWS, LANES)
        put(OP_Q4, q * d4)
        put(None, k * d4, "k4")
        f3 = f.reshape(shape3)
        prev = pltpu.roll(f3, 1, 1)
        nxt = pltpu.roll(f3, SUBLANES - 1, 1)
        m4 = row % 4
        if reverse:
            d2 = jnp.where(m4 == 0, f3 * nxt, jnp.where(m4 == 1, f3, jnp.where(m4 == 2, 1.0, prev)))
            d1 = jnp.where(row % 2 == 0, f3, 1.0)
        else:
            d2 = jnp.where(m4 == 0, nxt, jnp.where(m4 == 1, 1.0, jnp.where(m4 == 2, f3, f3 * prev)))
            d1 = jnp.where(row % 2 == 1, f3, 1.0)
        d2 = d2.reshape(PACKED_ROWS, LANES)
        put(OP_Q2, q * d2)
        put(None, k * d2, "k2")
        put(OP_Q1, q * d1.reshape(PACKED_ROWS, LANES))
        yield
    for op, names in ((OP_W1, ("k", "x32")), (OP_W2, ("x16", "x8")), (OP_W3, ("k4", "k2"))):
        stacked = jnp.concatenate(key_side[names[0]] + key_side[names[1]], axis=0)
        ops_r[slot, op:op + 2] = stacked.T.reshape(2, c, LANES)
        yield


def _consume_operands(ops_r, slot, m_ref, st, d_end, result):
    c = SCAN_CHUNK
    bf = jnp.bfloat16

    def ld(op, n=1):
        return ops_r[slot, op:op + n].reshape(n * c, LANES)

    a = None
    block = 0
    for lhs_op, n, w_op in ((OP_Q, 3, OP_W1), (OP_X16, 2, OP_W2), (OP_Q4, 2, OP_W3)):
        p = jnp.dot(ld(lhs_op, n), ld(w_op, 2), preferred_element_type=jnp.float32)
        for i in range(n):
            term = m_ref[block] * p[i * c:(i + 1) * c]
            a = term if a is None else a + term
            block += 1
    yield
    vb = ld(OP_V)
    o = jnp.dot(a.astype(bf), jnp.concatenate([vb, vb], axis=0), preferred_element_type=jnp.float32)
    result["o"] = o + jnp.dot(ld(OP_QREAD), st.T.astype(bf), preferred_element_type=jnp.float32)
    yield
    result["st"] = st * d_end + _dot_tn(vb, ld(OP_KSTATE))
    yield


def _interleave(*generators):
    pending = list(generators)
    while pending:
        for gen in list(pending):
            try:
                next(gen)
            except StopIteration:
                pending.remove(gen)


def _scan_sequence(q_r, v_r, f_refs, g_refs, t, o_refs, ops_refs, st_refs, m_refs):
    c = SCAN_CHUNK
    span = SCAN_GROUP * c
    n_groups = t // span

    def starts(g):
        g = jnp.asarray(g, jnp.int32)
        lo = pl.multiple_of(g * span, span)
        hi = pl.multiple_of(t - span - g * span, span)
        return ([pl.multiple_of(lo + j * c, c) for j in range(SCAN_GROUP)],
                [pl.multiple_of(hi + j * c, c) for j in reversed(range(SCAN_GROUP))])

    def builders(g, ops_r):
        return [_build_operands(q_r, v_r, f_refs[d], g_refs[d], row0, ops_r, d * SCAN_GROUP + j, d == 1)
                for d, rows in enumerate(starts(g)) for j, row0 in enumerate(rows)]

    def consume_direction(d, rows, ops_r):
        st = st_refs[d][...]
        for j, row0 in enumerate(rows):
            d_end = jnp.exp2(_chunk_row(g_refs[d], row0, 0 if d == 1 else c - 1))
            result = {}
            yield from _consume_operands(ops_r, d * SCAN_GROUP + j, m_refs[d], st, d_end, result)
            st = result["st"]
            if o_refs is not None:
                o_refs[d][pl.ds(row0, c), :] = result["o"]
        st_refs[d][...] = st

    def consumers(g, ops_r):
        return [consume_direction(d, rows, ops_r) for d, rows in enumerate(starts(g))]

    _interleave(*builders(0, ops_refs[0]))

    def body(i, carry):
        g0 = 2 * i
        _interleave(*consumers(g0, ops_refs[0]), *builders(g0 + 1, ops_refs[1]))
        _interleave(*consumers(g0 + 1, ops_refs[1]),
                    *builders(jnp.minimum(g0 + 2, n_groups - 1), ops_refs[0]))
        return carry

    lax.fori_loop(0, n_groups // 2, body, 0)
    if n_groups % 2:
        _interleave(*consumers(n_groups - 1, ops_refs[0]))


def _scan_kernel(q_ref, zf_ref, zb_ref, v_ref, ga_ref, qc_ref, zfc_ref, zbc_ref, vc_ref,
                 lbl_ref, gn_ref, wf_ref, wb_ref, mf_ref, mb_ref, o_ref,
                 stf_ref, stb_ref, of_ref, ob_ref, ff_ref, fb_ref, gf_ref, gb_ref,
                 ffc_ref, fbc_ref, gfc_ref, gbc_ref, ops0_ref, ops1_ref):
    t_lat = q_ref.shape[1]
    t_ctx = qc_ref.shape[1]
    l0, l1 = lbl_ref[0], lbl_ref[1]
    mx = jnp.maximum(l0, l1)
    e0, e1 = jnp.exp(l0 - mx), jnp.exp(l1 - mx)
    lb = e0 / (e0 + e1)
    lbs = (lb[0:1], lb[1:2])

    stf_ref[...] = jnp.zeros_like(stf_ref)
    stb_ref[...] = jnp.zeros_like(stb_ref)
    st_refs, m_refs, tri_refs, ops_refs = (stf_ref, stb_ref), (mf_ref, mb_ref), (wf_ref, wb_ref), (ops0_ref, ops1_ref)

    _gate_pass((zfc_ref, zbc_ref), lbs, tri_refs, (ffc_ref, fbc_ref), (gfc_ref, gbc_ref), t_ctx)
    _scan_sequence(qc_ref, vc_ref, (ffc_ref, fbc_ref), (gfc_ref, gbc_ref), t_ctx, None, ops_refs, st_refs, m_refs)
    _gate_pass((zf_ref, zb_ref), lbs, tri_refs, (ff_ref, fb_ref), (gf_ref, gb_ref), t_lat)
    _scan_sequence(q_ref, v_ref, (ff_ref, fb_ref), (gf_ref, gb_ref), t_lat, (of_ref, ob_ref),
                   ops_refs, st_refs, m_refs)

    rows = 256
    gn = gn_ref[...]

    def fin_body(i, carry):
        lo = pl.multiple_of(i * rows, rows)
        o = of_ref[pl.ds(lo, rows), :] + ob_ref[pl.ds(lo, rows), :]
        o = o * lax.rsqrt(jnp.mean(o * o, axis=-1, keepdims=True) + EPS) * gn
        o_ref[0, pl.ds(lo, rows), :] = (o * _silu(ga_ref[0, pl.ds(lo, rows), :])).astype(o_ref.dtype)
        return carry

    lax.fori_loop(0, t_lat // rows, fin_body, 0)


def _hgrn_scan(p_lat, p_ctx, lb_logits, hgrn_norm_g):
    bsz, t_lat, _ = p_lat.shape
    t_ctx = p_ctx.shape[1]
    w_f3, w_b3, m_f, m_b = _scan_constants()
    wf = jnp.asarray(w_f3, jnp.bfloat16)
    wb = jnp.asarray(w_b3, jnp.bfloat16)
    mf = jnp.asarray(m_f, jnp.float32)
    mb = jnp.asarray(m_b, jnp.float32)

    def col(t, base):
        return pl.BlockSpec((1, t, LANES), lambda b, h, base=base: (b, 0, base + h))

    def whole(a):
        return pl.BlockSpec(a.shape, lambda b, h, nd=a.ndim: (0,) * nd)

    def f32(rows):
        return pltpu.VMEM((rows, HEAD_DIM), jnp.float32)

    operands = pltpu.VMEM((2 * SCAN_GROUP, N_OPERANDS, SCAN_CHUNK, HEAD_DIM), jnp.bfloat16)
    return pl.pallas_call(
        _scan_kernel,
        grid=(bsz, HGRN_HEADS),
        in_specs=[col(t_lat, COL_Q), col(t_lat, COL_ZF), col(t_lat, COL_ZB), col(t_lat, COL_V),
                  col(t_lat, COL_GA),
                  col(t_ctx, COL_Q), col(t_ctx, COL_ZF), col(t_ctx, COL_ZB), col(t_ctx, COL_V),
                  pl.BlockSpec((2, 2, LANES), lambda b, h: (0, 0, h)),
                  pl.BlockSpec((1, LANES), lambda b, h: (0, h)),
                  whole(wf), whole(wb), whole(mf), whole(mb)],
        out_specs=pl.BlockSpec((1, t_lat, LANES), lambda b, h: (b, 0, h)),
        out_shape=jax.ShapeDtypeStruct((bsz, t_lat, D_HGRN), jnp.bfloat16),
        scratch_shapes=[f32(HEAD_DIM), f32(HEAD_DIM),
                        f32(t_lat), f32(t_lat),
                        f32(t_lat), f32(t_lat), f32(t_lat), f32(t_lat),
                        f32(t_ctx), f32(t_ctx), f32(t_ctx), f32(t_ctx),
                        operands, operands],
        compiler_params=pltpu.CompilerParams(dimension_semantics=("arbitrary", "arbitrary"),
                                             vmem_limit_bytes=VMEM_LIMIT),
        name="hgrn_scan",
    )(p_lat, p_lat, p_lat, p_lat, p_lat, p_ctx, p_ctx, p_ctx, p_ctx,
      lb_logits, hgrn_norm_g, wf, wb, mf, mb)


def _conv_kernel(u_ref, ug_ref, w_ref, b_ref, o_ref, pad_ref):
    t = u_ref.shape[1]
    n_rows = t // GRID_W
    cg = pl.program_id(1)
    bias = b_ref[...]

    @pl.when(cg < (D_CONV // 2) // LANES)
    def _along_rows():
        padw = GRID_W + 32

        def fill(r, carry):
            src = pl.multiple_of(r * GRID_W, GRID_W)
            dst = pl.multiple_of(r * padw, 32)
            glu = u_ref[0, pl.ds(src, GRID_W), :] * jax.nn.sigmoid(ug_ref[0, pl.ds(src, GRID_W), :])
            pad_ref[pl.ds(dst, 16), :] = jnp.zeros((16, LANES), jnp.float32)
            pad_ref[pl.ds(dst + 16, GRID_W), :] = glu
            pad_ref[pl.ds(dst + 16 + GRID_W, 16), :] = jnp.zeros((16, LANES), jnp.float32)
            return carry

        lax.fori_loop(0, n_rows, fill, 0)

        def conv(r, carry):
            dst = pl.multiple_of(r * GRID_W, GRID_W)
            base = r * padw + (16 - CONV_HALF)
            acc = jnp.zeros((GRID_W, LANES), jnp.float32)
            for k in range(CONV_WIDTH):
                acc = acc + w_ref[k:k + 1, :] * pad_ref[pl.ds(base + k, GRID_W), :]
            o_ref[0, pl.ds(dst, GRID_W), :] = acc + bias
            return carry

        lax.fori_loop(0, n_rows, conv, 0)

    @pl.when(cg >= (D_CONV // 2) // LANES)
    def _along_cols():
        halo = CONV_HALF * GRID_W
        pad_ref[pl.ds(0, halo), :] = jnp.zeros((halo, LANES), jnp.float32)
        pad_ref[pl.ds(halo + t, halo), :] = jnp.zeros((halo, LANES), jnp.float32)

        def fill(r, carry):
            src = pl.multiple_of(r * GRID_W, GRID_W)
            glu = u_ref[0, pl.ds(src, GRID_W), :] * jax.nn.sigmoid(ug_ref[0, pl.ds(src, GRID_W), :])
            pad_ref[pl.ds(halo + src, GRID_W), :] = glu
            return carry

        lax.fori_loop(0, n_rows, fill, 0)

        def conv(r, carry):
            dst = pl.multiple_of(r * GRID_W, GRID_W)
            acc = jnp.zeros((GRID_W, LANES), jnp.float32)
            for k in range(CONV_WIDTH):
                acc = acc + w_ref[k:k + 1, :] * pad_ref[pl.ds(dst + k * GRID_W, GRID_W), :]
            o_ref[0, pl.ds(dst, GRID_W), :] = acc + bias
            return carry

        lax.fori_loop(0, n_rows, conv, 0)


def _axial_conv(p_lat, conv_w, conv_b):
    bsz, t, _ = p_lat.shape
    n_rows = t // GRID_W
    pad_rows = max(n_rows * (GRID_W + 32), t + 2 * CONV_HALF * GRID_W)
    w_pad = jnp.zeros((32, D_CONV), jnp.float32).at[:CONV_WIDTH].set(conv_w)
    return pl.pallas_call(
        _conv_kernel,
        grid=(bsz, D_CONV // LANES),
        in_specs=[pl.BlockSpec((1, t, LANES), lambda b, g: (b, 0, COL_U + g)),
                  pl.BlockSpec((1, t, LANES), lambda b, g: (b, 0, COL_UG + g)),
                  pl.BlockSpec((32, LANES), lambda b, g: (0, g)),
                  pl.BlockSpec((1, LANES), lambda b, g: (0, g))],
        out_specs=pl.BlockSpec((1, t, LANES), lambda b, g: (b, 0, g)),
        out_shape=jax.ShapeDtypeStruct((bsz, t, D_CONV), jnp.float32),
        scratch_shapes=[pltpu.VMEM((pad_rows, LANES), jnp.float32)],
        compiler_params=pltpu.CompilerParams(dimension_semantics=("arbitrary", "arbitrary"),
                                             vmem_limit_bytes=VMEM_LIMIT),
        name="axial_conv",
    )(p_lat, p_lat, w_pad, conv_b)


def _out_kernel(x_ref, ba_ref, y_ref, gb_ref, gt_ref, lng_ref, lnb_ref, wa_ref, wb_ref, fg_ref, o_ref):
    y = y_ref[0]
    mu = jnp.mean(y, axis=-1, keepdims=True)
    yc = y - mu
    var = jnp.mean(yc * yc, axis=-1, keepdims=True)
    yn = yc * lax.rsqrt(var + EPS) * lng_ref[...] + lnb_ref[...]
    branch_b = _silu(yn) * _silu(gb_ref[0])
    mix = jnp.dot(ba_ref[0], wa_ref[...], preferred_element_type=jnp.float32)
    mix = mix + jnp.dot(branch_b.astype(jnp.bfloat16), wb_ref[...], preferred_element_type=jnp.float32)
    h = x_ref[0] + gt_ref[0] * mix
    o_ref[0] = h * lax.rsqrt(jnp.mean(h * h, axis=-1, keepdims=True) + EPS) * fg_ref[...]


def _output(x, branch_a, y_conv, p_lat, gate, ln_g, ln_b, w_out_bf16, final_g):
    bsz, t, _ = x.shape
    rows = OUT_ROWS
    w_a, w_b = w_out_bf16[:D_HGRN], w_out_bf16[D_HGRN:]
    gb_block = (5 * D_HGRN + 2 * D_CONV) // D_CONV
    return pl.pallas_call(
        _out_kernel,
        grid=(bsz, t // rows),
        in_specs=[pl.BlockSpec((1, rows, D_MODEL), lambda b, i: (b, i, 0)),
                  pl.BlockSpec((1, rows, D_HGRN), lambda b, i: (b, i, 0)),
                  pl.BlockSpec((1, rows, D_CONV), lambda b, i: (b, i, 0)),
                  pl.BlockSpec((1, rows, D_CONV), lambda b, i: (b, i, gb_block)),
                  pl.BlockSpec((1, 1, D_MODEL), lambda b, i: (b, 0, 0)),
                  pl.BlockSpec((1, D_CONV), lambda b, i: (0, 0)),
                  pl.BlockSpec((1, D_CONV), lambda b, i: (0, 0)),
                  pl.BlockSpec((D_HGRN, D_MODEL), lambda b, i: (0, 0)),
                  pl.BlockSpec((D_CONV, D_MODEL), lambda b, i: (0, 0)),
                  pl.BlockSpec((1, D_MODEL), lambda b, i: (0, 0))],
        out_specs=pl.BlockSpec((1, rows, D_MODEL), lambda b, i: (b, i, 0)),
        out_shape=jax.ShapeDtypeStruct((bsz, t, D_MODEL), jnp.float32),
        compiler_params=pltpu.CompilerParams(dimension_semantics=("arbitrary", "arbitrary"),
                                             vmem_limit_bytes=VMEM_LIMIT),
        name="out_projection",
    )(x, branch_a, y_conv, p_lat, gate, ln_g, ln_b, w_a, w_b, final_g)


def kernel(x, c, ctx, c_ctx, norm_g, w_mod, b_mod, w_in, lb_logits, hgrn_norm_g, conv_w, conv_b,
           conv_ln_g, conv_ln_b, w_out, final_norm_g):
    bsz, seq_len, _ = x.shape
    span = SCAN_GROUP * SCAN_CHUNK
    assert norm_g.shape[0] == 1, "single-layer block"
    assert SCAN_CHUNK == 64, "operand slots are laid out for 64-token chunks"
    assert seq_len % GRID_W == 0 and seq_len % span == 0 and ctx.shape[1] % span == 0

    pad = (-(bsz + 1)) % SUBLANES
    cc = jnp.concatenate([c, c_ctx[None, :], jnp.zeros((pad, D_MODEL), c.dtype)], axis=0)
    mod = _modulation(cc, w_mod[0], b_mod)
    shift, scale, gate = (mod[:, i * D_MODEL:(i + 1) * D_MODEL] for i in range(3))
    shift_lat, scale_lat, gate_lat = (m[:bsz, None, :] for m in (shift, scale, gate))
    shift_ctx, scale_ctx = (jnp.broadcast_to(m[bsz][None, None, :], (bsz, 1, D_MODEL)) for m in (shift, scale))

    w_in_bf16 = w_in[0].astype(jnp.bfloat16)
    p_lat = _projection(x, norm_g, shift_lat, scale_lat, w_in_bf16)
    p_ctx = _projection(ctx, norm_g, shift_ctx, scale_ctx, w_in_bf16)

    branch_a = _hgrn_scan(p_lat, p_ctx, lb_logits, hgrn_norm_g)
    y_conv = _axial_conv(p_lat, conv_w[0], conv_b)
    return _output(x, branch_a, y_conv, p_lat, gate_lat, conv_ln_g, conv_ln_b,
                   w_out[0].astype(jnp.bfloat16), final_norm_g[None, :])
```

```python
import numpy as np
import jax
import jax.numpy as jnp
from jax import lax
from jax.experimental import pallas as pl
from jax.experimental.pallas import tpu as pltpu

D_MODEL = 1024
CTX_LEN = 256
GRID_W = 64
D_HGRN = 512
HGRN_HEADS = 4
HEAD_DIM = D_HGRN // HGRN_HEADS
D_CONV = 512
CONV_WIDTH = 31
CONV_HALF = CONV_WIDTH // 2
D_MIX = D_HGRN + D_CONV
D_IN = 5 * D_HGRN + 3 * D_CONV
EPS = 1e-6

LANES = 128
SUBLANES = 8
SCAN_CHUNK = 64
SCAN_LEVELS = SCAN_CHUNK.bit_length() - 1
SCAN_GROUP = 4
PROJ_ROWS = 256
OUT_ROWS = 512
VMEM_LIMIT = 56 * 1024 * 1024

COL_Q, COL_ZF, COL_ZB, COL_V, COL_GA = (i * HGRN_HEADS for i in range(5))
COL_U, COL_UG, COL_GB = (5 * HGRN_HEADS + i * (D_CONV // LANES) for i in range(3))


def _silu(x):
    return x * jax.nn.sigmoid(x)


def _mod_kernel(c_ref, w_ref, b_ref, o_ref):
    a = _silu(c_ref[...])
    o_ref[...] = jnp.dot(a, w_ref[...], preferred_element_type=jnp.float32,
                         precision=lax.Precision.HIGHEST) + b_ref[...]


def _modulation(cc, w_mod, b_mod):
    rows = cc.shape[0]
    n = w_mod.shape[1]
    return pl.pallas_call(
        _mod_kernel,
        grid=(n // D_MODEL,),
        in_specs=[pl.BlockSpec((rows, D_MODEL), lambda j: (0, 0)),
                  pl.BlockSpec((D_MODEL, D_MODEL), lambda j: (0, j)),
                  pl.BlockSpec((1, D_MODEL), lambda j: (0, j))],
        out_specs=pl.BlockSpec((rows, D_MODEL), lambda j: (0, j)),
        out_shape=jax.ShapeDtypeStruct((rows, n), jnp.float32),
        compiler_params=pltpu.CompilerParams(dimension_semantics=("arbitrary",),
                                             vmem_limit_bytes=VMEM_LIMIT),
        name="modulation",
    )(cc, w_mod, b_mod)


def _proj_kernel(x_ref, g_ref, sh_ref, sc_ref, w_ref, o_ref):
    x = x_ref[0]
    y = x * lax.rsqrt(jnp.mean(x * x, axis=-1, keepdims=True) + EPS) * g_ref[...]
    a = y * (1.0 + sc_ref[0]) + sh_ref[0]
    o_ref[0] = jnp.dot(a.astype(jnp.bfloat16), w_ref[...], preferred_element_type=jnp.float32)


def _projection(x, norm_g, shift, scale, w_in_bf16):
    bsz, t, _ = x.shape
    rows = min(PROJ_ROWS, t)
    return pl.pallas_call(
        _proj_kernel,
        grid=(bsz, t // rows),
        in_specs=[pl.BlockSpec((1, rows, D_MODEL), lambda b, i: (b, i, 0)),
                  pl.BlockSpec((1, D_MODEL), lambda b, i: (0, 0)),
                  pl.BlockSpec((1, 1, D_MODEL), lambda b, i: (b, 0, 0)),
                  pl.BlockSpec((1, 1, D_MODEL), lambda b, i: (b, 0, 0)),
                  pl.BlockSpec((D_MODEL, D_IN), lambda b, i: (0, 0))],
        out_specs=pl.BlockSpec((1, rows, D_IN), lambda b, i: (b, i, 0)),
        out_shape=jax.ShapeDtypeStruct((bsz, t, D_IN), jnp.float32),
        compiler_params=pltpu.CompilerParams(dimension_semantics=("arbitrary", "arbitrary"),
                                             vmem_limit_bytes=VMEM_LIMIT),
        name="in_projection",
    )(x, norm_g, shift, scale, w_in_bf16)


def _scan_constants():
    c = SCAN_CHUNK
    idx = np.arange(c)
    t, s = idx[:, None], idx[None, :]
    masks = [t == s]
    for lvl in range(SCAN_LEVELS):
        h = c >> (lvl + 1)
        masks.append(((t // (2 * h)) == (s // (2 * h))) & ((t % (2 * h)) >= h) & ((s % (2 * h)) < h))
    m_f = np.stack([m.astype(np.float32) for m in masks])
    m_b = m_f[:, ::-1, ::-1].copy()
    tri_f = (s <= t).astype(np.float32)
    tri_b = (s >= t).astype(np.float32)
    return np.concatenate([tri_f] * 3, axis=1), np.concatenate([tri_b] * 3, axis=1), m_f, m_b


def _dot_nt(a, b):
    return lax.dot_general(a, b, (((1,), (1,)), ((), ())), preferred_element_type=jnp.float32)


def _dot_tn(a, b):
    return lax.dot_general(a, b, (((0,), (0,)), ((), ())), preferred_element_type=jnp.float32)


def _interleave(*generators):
    pending = list(generators)
    while pending:
        for gen in list(pending):
            try:
                next(gen)
            except StopIteration:
                pending.remove(gen)


def _mix_rows(q, k, half, reverse):
    parts = []
    for lo in range(0, SCAN_CHUNK, 2 * half):
        first, second = (q, k) if reverse else (k, q)
        parts += [first[lo:lo + half], second[lo + half:lo + 2 * half]]
    return jnp.concatenate(parts, axis=0)


def _level_decay(g_cum, f, half, reverse):
    c = SCAN_CHUNK
    if half >= SUBLANES:
        parts = []
        for lo in range(0, c, 2 * half):
            mid = lo + half
            if reverse:
                parts += [g_cum[lo:mid] - g_cum[mid:mid + 1], g_cum[mid:mid + 1] - g_cum[mid:mid + half]]
            else:
                parts += [g_cum[mid - 1:mid] - g_cum[lo:mid], g_cum[mid:mid + half] - g_cum[mid - 1:mid]]
        return jnp.exp2(jnp.concatenate(parts, axis=0))
    if half == SUBLANES // 2:
        g3 = g_cum.reshape(c // SUBLANES, SUBLANES, LANES)
        r = half if reverse else half - 1
        later = lax.broadcasted_iota(jnp.int32, g3.shape, 1) >= half
        sign = jnp.where(later != reverse, 1.0, -1.0)
        return jnp.exp2((g3 - g3[:, r:r + 1, :]) * sign).reshape(c, LANES)
    f3 = f.reshape(c // SUBLANES, SUBLANES, LANES)
    row = lax.broadcasted_iota(jnp.int32, f3.shape, 1)
    if half == 1:
        on_query_side = (row % 2 == 0) if reverse else (row % 2 == 1)
        return jnp.where(on_query_side, f3, 1.0).reshape(c, LANES)
    prev = pltpu.roll(f3, 1, 1)
    nxt = pltpu.roll(f3, SUBLANES - 1, 1)
    m4 = row % 4
    if reverse:
        d = jnp.where(m4 == 0, f3 * nxt, jnp.where(m4 == 1, f3, jnp.where(m4 == 2, 1.0, prev)))
    else:
        d = jnp.where(m4 == 0, nxt, jnp.where(m4 == 1, 1.0, jnp.where(m4 == 2, f3, f3 * prev)))
    return d.reshape(c, LANES)


def _chunk_local(q, k, v, f, g_cum, m_ref, reverse, result):
    c = SCAN_CHUNK
    bf = jnp.bfloat16
    end_row = 0 if reverse else c - 1
    kb = k.astype(bf)
    a = m_ref[0] * _dot_nt(q.astype(bf), kb)
    yield
    for lvl in range(SCAN_LEVELS):
        half = c >> (lvl + 1)
        d = _level_decay(g_cum, f, half, reverse)
        if half >= SUBLANES:
            x = (_mix_rows(q, k, half, reverse) * d).astype(bf)
            p = _dot_nt(x, x)
        elif half == 1:
            p = _dot_nt((q * d).astype(bf), kb)
        else:
            p = _dot_nt((q * d).astype(bf), (k * d).astype(bf))
        a = a + m_ref[lvl + 1] * p
        yield
    d_read = jnp.exp2(g_cum)
    d_state = jnp.exp2(g_cum[end_row:end_row + 1] - g_cum)
    vb = v.astype(bf)
    o_intra = jnp.dot(a.astype(bf), vb, preferred_element_type=jnp.float32)
    yield
    kv = _dot_tn(vb, (k * d_state).astype(bf))
    result.extend([o_intra, kv, (q * d_read).astype(bf), d_read[end_row:end_row + 1]])
    yield


def _direction_chunks(q_r, z_r, v_r, starts, lb, w_ref, m_ref, reverse):
    c = SCAN_CHUNK
    bf = jnp.bfloat16
    qs = [q_r[0, pl.ds(s, c), :] for s in starts]
    vs = [v_r[0, pl.ds(s, c), :] for s in starts]
    fs, g3s = [], []
    for s in starts:
        f = lb + (1.0 - lb) * jax.nn.sigmoid(z_r[0, pl.ds(s, c), :])
        g = jnp.log2(f)
        fs.append(f)
        g_hi = g.astype(bf)
        r1 = g - g_hi.astype(jnp.float32)
        g_mid = r1.astype(bf)
        g_lo = (r1 - g_mid.astype(jnp.float32)).astype(bf)
        g3s.append(jnp.concatenate([g_hi, g_mid, g_lo], axis=0))
    gens, results = [], []
    for j in range(0, len(starts), 2):
        pair = jnp.concatenate(g3s[j:j + 2], axis=1)
        g_cum = jnp.dot(w_ref[...], pair, preferred_element_type=jnp.float32)
        for jj in range(2):
            results.append([])
            gens.append(_chunk_local(qs[j + jj], 1.0 - fs[j + jj], vs[j + jj], fs[j + jj],
                                     g_cum[:, jj * LANES:(jj + 1) * LANES], m_ref, reverse, results[-1]))
    return gens, results


def _direction_states(starts, results, st_ref, o_r):
    c = SCAN_CHUNK
    st = st_ref[...]
    for s, (o_intra, kv, q_read, d_end) in zip(starts, results):
        if o_r is not None:
            o_r[pl.ds(s, c), :] = o_intra + _dot_nt(q_read, st.astype(jnp.bfloat16))
        st = st * d_end + kv
    st_ref[...] = st


def _scan_kernel(q_ref, zf_ref, zb_ref, v_ref, ga_ref, qc_ref, zfc_ref, zbc_ref, vc_ref,
                 lbl_ref, gn_ref, wf_ref, wb_ref, mf_ref, mb_ref, o_ref,
                 stf_ref, stb_ref, of_ref, ob_ref):
    c = SCAN_CHUNK
    span = SCAN_GROUP * c
    t_lat = q_ref.shape[1]
    t_ctx = qc_ref.shape[1]
    l0, l1 = lbl_ref[0], lbl_ref[1]
    mx = jnp.maximum(l0, l1)
    e0, e1 = jnp.exp(l0 - mx), jnp.exp(l1 - mx)
    lb = e0 / (e0 + e1)
    lb_f, lb_b = lb[0:1], lb[1:2]

    stf_ref[...] = jnp.zeros_like(stf_ref)
    stb_ref[...] = jnp.zeros_like(stb_ref)

    def both_directions(i, t, q_r, zf_r, zb_r, v_r, of_r, ob_r):
        lo = pl.multiple_of(i * span, span)
        hi = pl.multiple_of(t - span - i * span, span)
        starts_f = [pl.multiple_of(lo + j * c, c) for j in range(SCAN_GROUP)]
        starts_b = [pl.multiple_of(hi + j * c, c) for j in reversed(range(SCAN_GROUP))]
        gens_f, res_f = _direction_chunks(q_r, zf_r, v_r, starts_f, lb_f, wf_ref, mf_ref, False)
        gens_b, res_b = _direction_chunks(q_r, zb_r, v_r, starts_b, lb_b, wb_ref, mb_ref, True)
        _interleave(*gens_f, *gens_b)
        _direction_states(starts_f, res_f, stf_ref, of_r)
        _direction_states(starts_b, res_b, stb_ref, ob_r)

    def ctx_body(i, carry):
        both_directions(i, t_ctx, qc_ref, zfc_ref, zbc_ref, vc_ref, None, None)
        return carry

    lax.fori_loop(0, t_ctx // span, ctx_body, 0)

    def lat_body(i, carry):
        both_directions(i, t_lat, q_ref, zf_ref, zb_ref, v_ref, of_ref, ob_ref)
        return carry

    lax.fori_loop(0, t_lat // span, lat_body, 0)

    rows = 256
    gn = gn_ref[...]

    def fin_body(i, carry):
        lo = pl.multiple_of(i * rows, rows)
        o = of_ref[pl.ds(lo, rows), :] + ob_ref[pl.ds(lo, rows), :]
        o = o * lax.rsqrt(jnp.mean(o * o, axis=-1, keepdims=True) + EPS) * gn
        o_ref[0, pl.ds(lo, rows), :] = (o * _silu(ga_ref[0, pl.ds(lo, rows), :])).astype(o_ref.dtype)
        return carry

    lax.fori_loop(0, t_lat // rows, fin_body, 0)


def _hgrn_scan(p_lat, p_ctx, lb_logits, hgrn_norm_g):
    bsz, t_lat, _ = p_lat.shape
    t_ctx = p_ctx.shape[1]
    w_f3, w_b3, m_f, m_b = _scan_constants()
    wf = jnp.asarray(w_f3, jnp.bfloat16)
    wb = jnp.asarray(w_b3, jnp.bfloat16)
    mf = jnp.asarray(m_f, jnp.float32)
    mb = jnp.asarray(m_b, jnp.float32)

    def col(t, base):
        return pl.BlockSpec((1, t, LANES), lambda b, h, base=base: (b, 0, base + h))

    def whole(a):
        return pl.BlockSpec(a.shape, lambda b, h, nd=a.ndim: (0,) * nd)

    return pl.pallas_call(
        _scan_kernel,
        grid=(bsz, HGRN_HEADS),
        in_specs=[col(t_lat, COL_Q), col(t_lat, COL_ZF), col(t_lat, COL_ZB), col(t_lat, COL_V),
                  col(t_lat, COL_GA),
                  col(t_ctx, COL_Q), col(t_ctx, COL_ZF), col(t_ctx, COL_ZB), col(t_ctx, COL_V),
                  pl.BlockSpec((2, 2, LANES), lambda b, h: (0, 0, h)),
                  pl.BlockSpec((1, LANES), lambda b, h: (0, h)),
                  whole(wf), whole(wb), whole(mf), whole(mb)],
        out_specs=pl.BlockSpec((1, t_lat, LANES), lambda b, h: (b, 0, h)),
        out_shape=jax.ShapeDtypeStruct((bsz, t_lat, D_HGRN), jnp.bfloat16),
        scratch_shapes=[pltpu.VMEM((HEAD_DIM, HEAD_DIM), jnp.float32),
                        pltpu.VMEM((HEAD_DIM, HEAD_DIM), jnp.float32),
                        pltpu.VMEM((t_lat, HEAD_DIM), jnp.float32),
                        pltpu.VMEM((t_lat, HEAD_DIM), jnp.float32)],
        compiler_params=pltpu.CompilerParams(dimension_semantics=("arbitrary", "arbitrary"),
                                             vmem_limit_bytes=VMEM_LIMIT),
        name="hgrn_scan",
    )(p_lat, p_lat, p_lat, p_lat, p_lat, p_ctx, p_ctx, p_ctx, p_ctx,
      lb_logits, hgrn_norm_g, wf, wb, mf, mb)


def _conv_kernel(u_ref, ug_ref, w_ref, b_ref, o_ref, pad_ref):
    t = u_ref.shape[1]
    n_rows = t // GRID_W
    cg = pl.program_id(1)
    bias = b_ref[...]

    @pl.when(cg < (D_CONV // 2) // LANES)
    def _along_rows():
        padw = GRID_W + 32

        def fill(r, carry):
            src = pl.multiple_of(r * GRID_W, GRID_W)
            dst = pl.multiple_of(r * padw, 32)
            glu = u_ref[0, pl.ds(src, GRID_W), :] * jax.nn.sigmoid(ug_ref[0, pl.ds(src, GRID_W), :])
            pad_ref[pl.ds(dst, 16), :] = jnp.zeros((16, LANES), jnp.float32)
            pad_ref[pl.ds(dst + 16, GRID_W), :] = glu
            pad_ref[pl.ds(dst + 16 + GRID_W, 16), :] = jnp.zeros((16, LANES), jnp.float32)
            return carry

        lax.fori_loop(0, n_rows, fill, 0)

        def conv(r, carry):
            dst = pl.multiple_of(r * GRID_W, GRID_W)
            base = r * padw + (16 - CONV_HALF)
            acc = jnp.zeros((GRID_W, LANES), jnp.float32)
            for k in range(CONV_WIDTH):
                acc = acc + w_ref[k:k + 1, :] * pad_ref[pl.ds(base + k, GRID_W), :]
            o_ref[0, pl.ds(dst, GRID_W), :] = acc + bias
            return carry

        lax.fori_loop(0, n_rows, conv, 0)

    @pl.when(cg >= (D_CONV // 2) // LANES)
    def _along_cols():
        halo = CONV_HALF * GRID_W
        pad_ref[pl.ds(0, halo), :] = jnp.zeros((halo, LANES), jnp.float32)
        pad_ref[pl.ds(halo + t, halo), :] = jnp.zeros((halo, LANES), jnp.float32)

        def fill(r, carry):
            src = pl.multiple_of(r * GRID_W, GRID_W)
            glu = u_ref[0, pl.ds(src, GRID_W), :] * jax.nn.sigmoid(ug_ref[0, pl.ds(src, GRID_W), :])
            pad_ref[pl.ds(halo + src, GRID_W), :] = glu
            return carry

        lax.fori_loop(0, n_rows, fill, 0)

        def conv(r, carry):
            dst = pl.multiple_of(r * GRID_W, GRID_W)
            acc = jnp.zeros((GRID_W, LANES), jnp.float32)
            for k in range(CONV_WIDTH):
                acc = acc + w_ref[k:k + 1, :] * pad_ref[pl.ds(dst + k * GRID_W, GRID_W), :]
            o_ref[0, pl.ds(dst, GRID_W), :] = acc + bias
            return carry

        lax.fori_loop(0, n_rows, conv, 0)


def _axial_conv(p_lat, conv_w, conv_b):
    bsz, t, _ = p_lat.shape
    n_rows = t // GRID_W
    pad_rows = max(n_rows * (GRID_W + 32), t + 2 * CONV_HALF * GRID_W)
    w_pad = jnp.zeros((32, D_CONV), jnp.float32).at[:CONV_WIDTH].set(conv_w)
    return pl.pallas_call(
        _conv_kernel,
        grid=(bsz, D_CONV // LANES),
        in_specs=[pl.BlockSpec((1, t, LANES), lambda b, g: (b, 0, COL_U + g)),
                  pl.BlockSpec((1, t, LANES), lambda b, g: (b, 0, COL_UG + g)),
                  pl.BlockSpec((32, LANES), lambda b, g: (0, g)),
                  pl.BlockSpec((1, LANES), lambda b, g: (0, g))],
        out_specs=pl.BlockSpec((1, t, LANES), lambda b, g: (b, 0, g)),
        out_shape=jax.ShapeDtypeStruct((bsz, t, D_CONV), jnp.float32),
        scratch_shapes=[pltpu.VMEM((pad_rows, LANES), jnp.float32)],
        compiler_params=pltpu.CompilerParams(dimension_semantics=("arbitrary", "arbitrary"),
                                             vmem_limit_bytes=VMEM_LIMIT),
        name="axial_conv",
    )(p_lat, p_lat, w_pad, conv_b)


def _out_kernel(x_ref, ba_ref, y_ref, gb_ref, gt_ref, lng_ref, lnb_ref, wa_ref, wb_ref, fg_ref, o_ref):
    y = y_ref[0]
    mu = jnp.mean(y, axis=-1, keepdims=True)
    yc = y - mu
    var = jnp.mean(yc * yc, axis=-1, keepdims=True)
    yn = yc * lax.rsqrt(var + EPS) * lng_ref[...] + lnb_ref[...]
    branch_b = _silu(yn) * _silu(gb_ref[0])
    mix = jnp.dot(ba_ref[0], wa_ref[...], preferred_element_type=jnp.float32)
    mix = mix + jnp.dot(branch_b.astype(jnp.bfloat16), wb_ref[...], preferred_element_type=jnp.float32)
    h = x_ref[0] + gt_ref[0] * mix
    o_ref[0] = h * lax.rsqrt(jnp.mean(h * h, axis=-1, keepdims=True) + EPS) * fg_ref[...]


def _output(x, branch_a, y_conv, p_lat, gate, ln_g, ln_b, w_out_bf16, final_g):
    bsz, t, _ = x.shape
    rows = OUT_ROWS
    w_a, w_b = w_out_bf16[:D_HGRN], w_out_bf16[D_HGRN:]
    gb_block = (5 * D_HGRN + 2 * D_CONV) // D_CONV
    return pl.pallas_call(
        _out_kernel,
        grid=(bsz, t // rows),
        in_specs=[pl.BlockSpec((1, rows, D_MODEL), lambda b, i: (b, i, 0)),
                  pl.BlockSpec((1, rows, D_HGRN), lambda b, i: (b, i, 0)),
                  pl.BlockSpec((1, rows, D_CONV), lambda b, i: (b, i, 0)),
                  pl.BlockSpec((1, rows, D_CONV), lambda b, i: (b, i, gb_block)),
                  pl.BlockSpec((1, 1, D_MODEL), lambda b, i: (b, 0, 0)),
                  pl.BlockSpec((1, D_CONV), lambda b, i: (0, 0)),
                  pl.BlockSpec((1, D_CONV), lambda b, i: (0, 0)),
                  pl.BlockSpec((D_HGRN, D_MODEL), lambda b, i: (0, 0)),
                  pl.BlockSpec((D_CONV, D_MODEL), lambda b, i: (0, 0)),
                  pl.BlockSpec((1, D_MODEL), lambda b, i: (0, 0))],
        out_specs=pl.BlockSpec((1, rows, D_MODEL), lambda b, i: (b, i, 0)),
        out_shape=jax.ShapeDtypeStruct((bsz, t, D_MODEL), jnp.float32),
        compiler_params=pltpu.CompilerParams(dimension_semantics=("arbitrary", "arbitrary"),
                                             vmem_limit_bytes=VMEM_LIMIT),
        name="out_projection",
    )(x, branch_a, y_conv, p_lat, gate, ln_g, ln_b, w_a, w_b, final_g)


def kernel(x, c, ctx, c_ctx, norm_g, w_mod, b_mod, w_in, lb_logits, hgrn_norm_g, conv_w, conv_b,
           conv_ln_g, conv_ln_b, w_out, final_norm_g):
    bsz, seq_len, _ = x.shape
    span = SCAN_GROUP * SCAN_CHUNK
    assert norm_g.shape[0] == 1, "single-layer block"
    assert seq_len % GRID_W == 0 and seq_len % span == 0 and ctx.shape[1] % span == 0

    pad = (-(bsz + 1)) % SUBLANES
    cc = jnp.concatenate([c, c_ctx[None, :], jnp.zeros((pad, D_MODEL), c.dtype)], axis=0)
    mod = _modulation(cc, w_mod[0], b_mod)
    shift, scale, gate = (mod[:, i * D_MODEL:(i + 1) * D_MODEL] for i in range(3))
    shift_lat, scale_lat, gate_lat = (m[:bsz, None, :] for m in (shift, scale, gate))
    shift_ctx, scale_ctx = (jnp.broadcast_to(m[bsz][None, None, :], (bsz, 1, D_MODEL)) for m in (shift, scale))

    w_in_bf16 = w_in[0].astype(jnp.bfloat16)
    p_lat = _projection(x, norm_g, shift_lat, scale_lat, w_in_bf16)
    p_ctx = _projection(ctx, norm_g, shift_ctx, scale_ctx, w_in_bf16)

    branch_a = _hgrn_scan(p_lat, p_ctx, lb_logits, hgrn_norm_g)
    y_conv = _axial_conv(p_lat, conv_w[0], conv_b)
    return _output(x, branch_a, y_conv, p_lat, gate_lat, conv_ln_g, conv_ln_b,
                   w_out[0].astype(jnp.bfloat16), final_norm_g[None, :])
```

```python
import numpy as np
import jax
import jax.numpy as jnp
from jax import lax
from jax.experimental import pallas as pl
from jax.experimental.pallas import tpu as pltpu

D_MODEL = 1024
CTX_LEN = 256
GRID_W = 64
D_HGRN = 512
HGRN_HEADS = 4
HEAD_DIM = D_HGRN // HGRN_HEADS
D_CONV = 512
CONV_WIDTH = 31
CONV_HALF = CONV_WIDTH // 2
D_MIX = D_HGRN + D_CONV
D_IN = 5 * D_HGRN + 3 * D_CONV
EPS = 1e-6

LANES = 128
SUBLANES = 8
SCAN_CHUNK = 64
SCAN_LEVELS = SCAN_CHUNK.bit_length() - 1
SCAN_GROUP = 4
PROJ_ROWS = 512
OUT_ROWS = 1024
VMEM_LIMIT = 56 * 1024 * 1024

D_SCAN_IN = 4 * D_HGRN
COL_Q, COL_ZF, COL_ZB, COL_V = (i * HGRN_HEADS for i in range(4))
COL_GA, COL_U, COL_UG, COL_GB = (i * (D_HGRN // LANES) for i in range(4))


def _silu(x):
    return x * jax.nn.sigmoid(x)


def _mod_kernel(c_ref, w_ref, b_ref, o_ref):
    a = _silu(c_ref[...])
    o_ref[...] = jnp.dot(a, w_ref[...], preferred_element_type=jnp.float32,
                         precision=lax.Precision.HIGHEST) + b_ref[...]


def _modulation(cc, w_mod, b_mod):
    rows = cc.shape[0]
    n = w_mod.shape[1]
    return pl.pallas_call(
        _mod_kernel,
        grid=(n // D_MODEL,),
        in_specs=[pl.BlockSpec((rows, D_MODEL), lambda j: (0, 0)),
                  pl.BlockSpec((D_MODEL, D_MODEL), lambda j: (0, j)),
                  pl.BlockSpec((1, D_MODEL), lambda j: (0, j))],
        out_specs=pl.BlockSpec((rows, D_MODEL), lambda j: (0, j)),
        out_shape=jax.ShapeDtypeStruct((rows, n), jnp.float32),
        compiler_params=pltpu.CompilerParams(dimension_semantics=("arbitrary",),
                                             vmem_limit_bytes=VMEM_LIMIT),
        name="modulation",
    )(cc, w_mod, b_mod)


def _proj_kernel(x_ref, g_ref, sh_ref, sc_ref, *refs):
    x = x_ref[0]
    y = x * lax.rsqrt(jnp.mean(x * x, axis=-1, keepdims=True) + EPS) * g_ref[...]
    a = (y * (1.0 + sc_ref[0]) + sh_ref[0]).astype(jnp.bfloat16)
    n = len(refs) // 2
    for w_ref, o_ref in zip(refs[:n], refs[n:]):
        o_ref[0] = jnp.dot(a, w_ref[...], preferred_element_type=jnp.float32).astype(o_ref.dtype)


def _projection(x, norm_g, shift, scale, weights, dtypes):
    bsz, t, _ = x.shape
    rows = min(PROJ_ROWS, t)
    return pl.pallas_call(
        _proj_kernel,
        grid=(bsz, t // rows),
        in_specs=[pl.BlockSpec((1, rows, D_MODEL), lambda b, i: (b, i, 0)),
                  pl.BlockSpec((1, D_MODEL), lambda b, i: (0, 0)),
                  pl.BlockSpec((1, 1, D_MODEL), lambda b, i: (b, 0, 0)),
                  pl.BlockSpec((1, 1, D_MODEL), lambda b, i: (b, 0, 0))]
        + [pl.BlockSpec(w.shape, lambda b, i: (0, 0)) for w in weights],
        out_specs=[pl.BlockSpec((1, rows, w.shape[1]), lambda b, i: (b, i, 0)) for w in weights],
        out_shape=[jax.ShapeDtypeStruct((bsz, t, w.shape[1]), dt) for w, dt in zip(weights, dtypes)],
        compiler_params=pltpu.CompilerParams(dimension_semantics=("arbitrary", "arbitrary"),
                                             vmem_limit_bytes=VMEM_LIMIT),
        name="in_projection",
    )(x, norm_g, shift, scale, *weights)


def _scan_constants():
    c = SCAN_CHUNK
    idx = np.arange(c)
    t, s = idx[:, None], idx[None, :]
    masks = [t == s]
    for lvl in range(SCAN_LEVELS):
        h = c >> (lvl + 1)
        masks.append(((t // (2 * h)) == (s // (2 * h))) & ((t % (2 * h)) >= h) & ((s % (2 * h)) < h))
    m_f = np.stack([m.astype(np.float32) for m in masks])
    m_b = m_f[:, ::-1, ::-1].copy()
    tri_f = (s <= t).astype(np.float32)
    tri_b = (s >= t).astype(np.float32)
    return np.concatenate([tri_f] * 3, axis=1), np.concatenate([tri_b] * 3, axis=1), m_f, m_b


def _dot_nt(a, b):
    return lax.dot_general(a, b, (((1,), (1,)), ((), ())), preferred_element_type=jnp.float32)


def _dot_tn(a, b):
    return lax.dot_general(a, b, (((0,), (0,)), ((), ())), preferred_element_type=jnp.float32)


def _interleave(*generators):
    pending = list(generators)
    while pending:
        for gen in list(pending):
            try:
                next(gen)
            except StopIteration:
                pending.remove(gen)


def _mix_rows(q, k, half, reverse):
    parts = []
    for lo in range(0, SCAN_CHUNK, 2 * half):
        first, second = (q, k) if reverse else (k, q)
        parts += [first[lo:lo + half], second[lo + half:lo + 2 * half]]
    return jnp.concatenate(parts, axis=0)


def _level_decay(g_cum, f, half, reverse):
    c = SCAN_CHUNK
    if half >= SUBLANES:
        parts = []
        for lo in range(0, c, 2 * half):
            mid = lo + half
            if reverse:
                parts += [g_cum[lo:mid] - g_cum[mid:mid + 1], g_cum[mid:mid + 1] - g_cum[mid:mid + half]]
            else:
                parts += [g_cum[mid - 1:mid] - g_cum[lo:mid], g_cum[mid:mid + half] - g_cum[mid - 1:mid]]
        return jnp.exp2(jnp.concatenate(parts, axis=0))
    if half == SUBLANES // 2:
        g3 = g_cum.reshape(c // SUBLANES, SUBLANES, LANES)
        r = half if reverse else half - 1
        later = lax.broadcasted_iota(jnp.int32, g3.shape, 1) >= half
        sign = jnp.where(later != reverse, 1.0, -1.0)
        return jnp.exp2((g3 - g3[:, r:r + 1, :]) * sign).reshape(c, LANES)
    f3 = f.reshape(c // SUBLANES, SUBLANES, LANES)
    row = lax.broadcasted_iota(jnp.int32, f3.shape, 1)
    if half == 1:
        on_query_side = (row % 2 == 0) if reverse else (row % 2 == 1)
        return jnp.where(on_query_side, f3, 1.0).reshape(c, LANES)
    prev = pltpu.roll(f3, 1, 1)
    nxt = pltpu.roll(f3, SUBLANES - 1, 1)
    m4 = row % 4
    if reverse:
        d = jnp.where(m4 == 0, f3 * nxt, jnp.where(m4 == 1, f3, jnp.where(m4 == 2, 1.0, prev)))
    else:
        d = jnp.where(m4 == 0, nxt, jnp.where(m4 == 1, 1.0, jnp.where(m4 == 2, f3, f3 * prev)))
    return d.reshape(c, LANES)


def _chunk_local(q, k, v, f, g_cum, m_ref, reverse, result):
    c = SCAN_CHUNK
    bf = jnp.bfloat16
    end_row = 0 if reverse else c - 1

    kb = k.astype(bf)
    a = m_ref[0] * _dot_nt(q.astype(bf), kb)
    yield
    for lvl in range(SCAN_LEVELS):
        half = c >> (lvl + 1)
        d = _level_decay(g_cum, f, half, reverse)
        if half >= SUBLANES:
            x = (_mix_rows(q, k, half, reverse) * d).astype(bf)
            p = _dot_nt(x, x)
        elif half == 1:
            p = _dot_nt((q * d).astype(bf), kb)
        else:
            p = _dot_nt((q * d).astype(bf), (k * d).astype(bf))
        a = a + m_ref[lvl + 1] * p
        yield
    d_read = jnp.exp2(g_cum)
    d_state = jnp.exp2(g_cum[end_row:end_row + 1] - g_cum)
    vb = v.astype(bf)
    o_intra = jnp.dot(a.astype(bf), vb, preferred_element_type=jnp.float32)
    yield
    kv = _dot_tn(vb, (k * d_state).astype(bf))
    result.extend([o_intra, kv, (q * d_read).astype(bf), d_read[end_row:end_row + 1]])
    yield


def _chunk_pair(q_r, z_r, v_r, starts, lb, w_ref, m_ref, reverse, results):
    c = SCAN_CHUNK
    bf = jnp.bfloat16
    fs, g3s = [], []
    for s in starts:
        f = lb + (1.0 - lb) * jax.nn.sigmoid(z_r[0, pl.ds(s, c), :])
        g = jnp.log2(f)
        fs.append(f)
        g_hi = g.astype(bf)
        r1 = g - g_hi.astype(jnp.float32)
        g_mid = r1.astype(bf)
        g_lo = (r1 - g_mid.astype(jnp.float32)).astype(bf)
        g3s.append(jnp.concatenate([g_hi, g_mid, g_lo], axis=0))
        yield
    g_cum = jnp.dot(w_ref[...], jnp.concatenate(g3s, axis=1), preferred_element_type=jnp.float32)
    yield
    chains = [_chunk_local(q_r[0, pl.ds(s, c), :], 1.0 - fs[j], v_r[0, pl.ds(s, c), :], fs[j],
                           g_cum[:, j * LANES:(j + 1) * LANES], m_ref, reverse, results[j])
              for j, s in enumerate(starts)]
    while chains:
        for chain in list(chains):
            try:
                next(chain)
            except StopIteration:
                chains.remove(chain)
        yield


def _direction_chunks(q_r, z_r, v_r, starts, lb, w_ref, m_ref, reverse):
    results = [[] for _ in starts]
    gens = [_chunk_pair(q_r, z_r, v_r, starts[j:j + 2], lb, w_ref, m_ref, reverse, results[j:j + 2])
            for j in range(0, len(starts), 2)]
    return gens, results


def _direction_states(starts, results, st_ref, o_r):
    c = SCAN_CHUNK
    st = st_ref[...]
    for s, (o_intra, kv, q_read, d_end) in zip(starts, results):
        if o_r is not None:
            o_r[pl.ds(s, c), :] = o_intra + _dot_nt(q_read, st.astype(jnp.bfloat16))
        st = st * d_end + kv
        yield
    st_ref[...] = st


def _scan_kernel(q_ref, zf_ref, zb_ref, v_ref, ga_ref, qc_ref, zfc_ref, zbc_ref, vc_ref,
                 lbl_ref, gn_ref, wf_ref, wb_ref, mf_ref, mb_ref, o_ref,
                 stf_ref, stb_ref, of_ref, ob_ref):
    c = SCAN_CHUNK
    t_lat = q_ref.shape[1]
    t_ctx = qc_ref.shape[1]
    l0, l1 = lbl_ref[0], lbl_ref[1]
    mx = jnp.maximum(l0, l1)
    e0, e1 = jnp.exp(l0 - mx), jnp.exp(l1 - mx)
    lb = e0 / (e0 + e1)
    lb_f, lb_b = lb[0:1], lb[1:2]

    stf_ref[...] = jnp.zeros_like(stf_ref)
    stb_ref[...] = jnp.zeros_like(stb_ref)

    def both_directions(i, t, group, q_r, zf_r, zb_r, v_r, of_r, ob_r):
        span = group * c
        lo = pl.multiple_of(i * span, span)
        hi = pl.multiple_of(t - span - i * span, span)
        starts_f = [pl.multiple_of(lo + j * c, c) for j in range(group)]
        starts_b = [pl.multiple_of(hi + j * c, c) for j in reversed(range(group))]
        gens_f, res_f = _direction_chunks(q_r, zf_r, v_r, starts_f, lb_f, wf_ref, mf_ref, False)
        gens_b, res_b = _direction_chunks(q_r, zb_r, v_r, starts_b, lb_b, wb_ref, mb_ref, True)
        _interleave(*gens_f, *gens_b)
        _interleave(_direction_states(starts_f, res_f, stf_ref, of_r),
                    _direction_states(starts_b, res_b, stb_ref, ob_r))

    ctx_group = min(SCAN_GROUP, t_ctx // c)

    def ctx_body(i, carry):
        both_directions(i, t_ctx, ctx_group, qc_ref, zfc_ref, zbc_ref, vc_ref, None, None)
        return carry

    lax.fori_loop(0, t_ctx // (ctx_group * c), ctx_body, 0)

    def lat_body(i, carry):
        both_directions(i, t_lat, SCAN_GROUP, q_ref, zf_ref, zb_ref, v_ref, of_ref, ob_ref)
        return carry

    lax.fori_loop(0, t_lat // (SCAN_GROUP * c), lat_body, 0)

    rows = 256
    gn = gn_ref[...]

    def fin_body(i, carry):
        lo = pl.multiple_of(i * rows, rows)
        o = of_ref[pl.ds(lo, rows), :] + ob_ref[pl.ds(lo, rows), :]
        o = o * lax.rsqrt(jnp.mean(o * o, axis=-1, keepdims=True) + EPS) * gn
        gate = _silu(ga_ref[0, pl.ds(lo, rows), :].astype(jnp.float32))
        o_ref[0, pl.ds(lo, rows), :] = (o * gate).astype(o_ref.dtype)
        return carry

    lax.fori_loop(0, t_lat // rows, fin_body, 0)


def _hgrn_scan(p_lat, aux_lat, p_ctx, lb_logits, hgrn_norm_g):
    bsz, t_lat, _ = p_lat.shape
    t_ctx = p_ctx.shape[1]
    w_f3, w_b3, m_f, m_b = _scan_constants()
    wf = jnp.asarray(w_f3, jnp.bfloat16)
    wb = jnp.asarray(w_b3, jnp.bfloat16)
    mf = jnp.asarray(m_f, jnp.float32)
    mb = jnp.asarray(m_b, jnp.float32)

    def col(t, base):
        return pl.BlockSpec((1, t, LANES), lambda b, h, base=base: (b, 0, base + h))

    def whole(a):
        return pl.BlockSpec(a.shape, lambda b, h, nd=a.ndim: (0,) * nd)

    return pl.pallas_call(
        _scan_kernel,
        grid=(bsz, HGRN_HEADS),
        in_specs=[col(t_lat, COL_Q), col(t_lat, COL_ZF), col(t_lat, COL_ZB), col(t_lat, COL_V),
                  col(t_lat, COL_GA),
                  col(t_ctx, COL_Q), col(t_ctx, COL_ZF), col(t_ctx, COL_ZB), col(t_ctx, COL_V),
                  pl.BlockSpec((2, 2, LANES), lambda b, h: (0, 0, h)),
                  pl.BlockSpec((1, LANES), lambda b, h: (0, h)),
                  whole(wf), whole(wb), whole(mf), whole(mb)],
        out_specs=pl.BlockSpec((1, t_lat, LANES), lambda b, h: (b, 0, h)),
        out_shape=jax.ShapeDtypeStruct((bsz, t_lat, D_HGRN), jnp.bfloat16),
        scratch_shapes=[pltpu.VMEM((HEAD_DIM, HEAD_DIM), jnp.float32),
                        pltpu.VMEM((HEAD_DIM, HEAD_DIM), jnp.float32),
                        pltpu.VMEM((t_lat, HEAD_DIM), jnp.float32),
                        pltpu.VMEM((t_lat, HEAD_DIM), jnp.float32)],
        compiler_params=pltpu.CompilerParams(dimension_semantics=("arbitrary", "arbitrary"),
                                             vmem_limit_bytes=VMEM_LIMIT),
        name="hgrn_scan",
    )(p_lat, p_lat, p_lat, p_lat, aux_lat, p_ctx, p_ctx, p_ctx, p_ctx,
      lb_logits, hgrn_norm_g, wf, wb, mf, mb)


def _conv_kernel(u_ref, ug_ref, w_ref, b_ref, o_ref, pad_ref):
    t = u_ref.shape[1]
    n_rows = t // GRID_W
    cg = pl.program_id(1)
    bias = b_ref[...]

    @pl.when(cg < (D_CONV // 2) // LANES)
    def _along_rows():
        padw = GRID_W + 32

        def fill(r, carry):
            src = pl.multiple_of(r * GRID_W, GRID_W)
            dst = pl.multiple_of(r * padw, 32)
            glu = (u_ref[0, pl.ds(src, GRID_W), :].astype(jnp.float32)
                   * jax.nn.sigmoid(ug_ref[0, pl.ds(src, GRID_W), :].astype(jnp.float32)))
            pad_ref[pl.ds(dst, 16), :] = jnp.zeros((16, LANES), jnp.float32)
            pad_ref[pl.ds(dst + 16, GRID_W), :] = glu
            pad_ref[pl.ds(dst + 16 + GRID_W, 16), :] = jnp.zeros((16, LANES), jnp.float32)
            return carry

        lax.fori_loop(0, n_rows, fill, 0)

        def conv(r, carry):
            dst = pl.multiple_of(r * GRID_W, GRID_W)
            base = r * padw + (16 - CONV_HALF)
            acc = jnp.zeros((GRID_W, LANES), jnp.float32)
            for k in range(CONV_WIDTH):
                acc = acc + w_ref[k:k + 1, :] * pad_ref[pl.ds(base + k, GRID_W), :]
            o_ref[0, pl.ds(dst, GRID_W), :] = (acc + bias).astype(o_ref.dtype)
            return carry

        lax.fori_loop(0, n_rows, conv, 0)

    @pl.when(cg >= (D_CONV // 2) // LANES)
    def _along_cols():
        halo = CONV_HALF * GRID_W
        pad_ref[pl.ds(0, halo), :] = jnp.zeros((halo, LANES), jnp.float32)
        pad_ref[pl.ds(halo + t, halo), :] = jnp.zeros((halo, LANES), jnp.float32)

        def fill(r, carry):
            src = pl.multiple_of(r * GRID_W, GRID_W)
            glu = (u_ref[0, pl.ds(src, GRID_W), :].astype(jnp.float32)
                   * jax.nn.sigmoid(ug_ref[0, pl.ds(src, GRID_W), :].astype(jnp.float32)))
            pad_ref[pl.ds(halo + src, GRID_W), :] = glu
            return carry

        lax.fori_loop(0, n_rows, fill, 0)

        def conv(r, carry):
            dst = pl.multiple_of(r * GRID_W, GRID_W)
            acc = jnp.zeros((GRID_W, LANES), jnp.float32)
            for k in range(CONV_WIDTH):
                acc = acc + w_ref[k:k + 1, :] * pad_ref[pl.ds(dst + k * GRID_W, GRID_W), :]
            o_ref[0, pl.ds(dst, GRID_W), :] = (acc + bias).astype(o_ref.dtype)
            return carry

        lax.fori_loop(0, n_rows, conv, 0)


def _axial_conv(aux_lat, conv_w, conv_b):
    bsz, t, _ = aux_lat.shape
    n_rows = t // GRID_W
    pad_rows = max(n_rows * (GRID_W + 32), t + 2 * CONV_HALF * GRID_W)
    w_pad = jnp.zeros((32, D_CONV), jnp.float32).at[:CONV_WIDTH].set(conv_w)
    return pl.pallas_call(
        _conv_kernel,
        grid=(bsz, D_CONV // LANES),
        in_specs=[pl.BlockSpec((1, t, LANES), lambda b, g: (b, 0, COL_U + g)),
                  pl.BlockSpec((1, t, LANES), lambda b, g: (b, 0, COL_UG + g)),
                  pl.BlockSpec((32, LANES), lambda b, g: (0, g)),
                  pl.BlockSpec((1, LANES), lambda b, g: (0, g))],
        out_specs=pl.BlockSpec((1, t, LANES), lambda b, g: (b, 0, g)),
        out_shape=jax.ShapeDtypeStruct((bsz, t, D_CONV), jnp.bfloat16),
        scratch_shapes=[pltpu.VMEM((pad_rows, LANES), jnp.float32)],
        compiler_params=pltpu.CompilerParams(dimension_semantics=("arbitrary", "arbitrary"),
                                             vmem_limit_bytes=VMEM_LIMIT),
        name="axial_conv",
    )(aux_lat, aux_lat, w_pad, conv_b)


def _out_kernel(x_ref, ba_ref, y_ref, gb_ref, gt_ref, lng_ref, lnb_ref, wa_ref, wb_ref, fg_ref, o_ref):
    y = y_ref[0].astype(jnp.float32)
    mu = jnp.mean(y, axis=-1, keepdims=True)
    yc = y - mu
    var = jnp.mean(yc * yc, axis=-1, keepdims=True)
    yn = yc * lax.rsqrt(var + EPS) * lng_ref[...] + lnb_ref[...]
    branch_b = _silu(yn) * _silu(gb_ref[0].astype(jnp.float32))
    mix = jnp.dot(ba_ref[0], wa_ref[...], preferred_element_type=jnp.float32)
    mix = mix + jnp.dot(branch_b.astype(jnp.bfloat16), wb_ref[...], preferred_element_type=jnp.float32)
    h = x_ref[0] + gt_ref[0] * mix
    o_ref[0] = h * lax.rsqrt(jnp.mean(h * h, axis=-1, keepdims=True) + EPS) * fg_ref[...]


def _output(x, branch_a, y_conv, aux_lat, gate, ln_g, ln_b, w_out_bf16, final_g):
    bsz, t, _ = x.shape
    rows = OUT_ROWS
    w_a, w_b = w_out_bf16[:D_HGRN], w_out_bf16[D_HGRN:]
    gb_block = COL_GB * LANES // D_CONV
    return pl.pallas_call(
        _out_kernel,
        grid=(bsz, t // rows),
        in_specs=[pl.BlockSpec((1, rows, D_MODEL), lambda b, i: (b, i, 0)),
                  pl.BlockSpec((1, rows, D_HGRN), lambda b, i: (b, i, 0)),
                  pl.BlockSpec((1, rows, D_CONV), lambda b, i: (b, i, 0)),
                  pl.BlockSpec((1, rows, D_CONV), lambda b, i: (b, i, gb_block)),
                  pl.BlockSpec((1, 1, D_MODEL), lambda b, i: (b, 0, 0)),
                  pl.BlockSpec((1, D_CONV), lambda b, i: (0, 0)),
                  pl.BlockSpec((1, D_CONV), lambda b, i: (0, 0)),
                  pl.BlockSpec((D_HGRN, D_MODEL), lambda b, i: (0, 0)),
                  pl.BlockSpec((D_CONV, D_MODEL), lambda b, i: (0, 0)),
                  pl.BlockSpec((1, D_MODEL), lambda b, i: (0, 0))],
        out_specs=pl.BlockSpec((1, rows, D_MODEL), lambda b, i: (b, i, 0)),
        out_shape=jax.ShapeDtypeStruct((bsz, t, D_MODEL), jnp.float32),
        compiler_params=pltpu.CompilerParams(dimension_semantics=("arbitrary", "arbitrary"),
                                             vmem_limit_bytes=VMEM_LIMIT),
        name="out_projection",
    )(x, branch_a, y_conv, aux_lat, gate, ln_g, ln_b, w_a, w_b, final_g)


def kernel(x, c, ctx, c_ctx, norm_g, w_mod, b_mod, w_in, lb_logits, hgrn_norm_g, conv_w, conv_b,
           conv_ln_g, conv_ln_b, w_out, final_norm_g):
    bsz, seq_len, _ = x.shape
    span = SCAN_GROUP * SCAN_CHUNK
    assert norm_g.shape[0] == 1, "single-layer block"
    assert seq_len % GRID_W == 0 and seq_len % span == 0
    assert ctx.shape[1] % (2 * SCAN_CHUNK) == 0 and (ctx.shape[1] % span == 0 or ctx.shape[1] < span)

    pad = (-(bsz + 1)) % SUBLANES
    cc = jnp.concatenate([c, c_ctx[None, :], jnp.zeros((pad, D_MODEL), c.dtype)], axis=0)
    mod = _modulation(cc, w_mod[0], b_mod)
    shift, scale, gate = (mod[:, i * D_MODEL:(i + 1) * D_MODEL] for i in range(3))
    shift_lat, scale_lat, gate_lat = (m[:bsz, None, :] for m in (shift, scale, gate))
    shift_ctx, scale_ctx = (jnp.broadcast_to(m[bsz][None, None, :], (bsz, 1, D_MODEL)) for m in (shift, scale))

    w_in_bf16 = w_in[0].astype(jnp.bfloat16)
    w_scan, w_aux = w_in_bf16[:, :D_SCAN_IN], w_in_bf16[:, D_SCAN_IN:]
    p_lat, aux_lat = _projection(x, norm_g, shift_lat, scale_lat, (w_scan, w_aux), (jnp.float32, jnp.bfloat16))
    p_ctx, = _projection(ctx, norm_g, shift_ctx, scale_ctx, (w_scan,), (jnp.float32,))

    branch_a = _hgrn_scan(p_lat, aux_lat, p_ctx, lb_logits, hgrn_norm_g)
    y_conv = _axial_conv(aux_lat, conv_w[0], conv_b)
    return _output(x, branch_a, y_conv, aux_lat, gate_lat, conv_ln_g, conv_ln_b,
                   w_out[0].astype(jnp.bfloat16), final_norm_g[None, :])
```

```python
import numpy as np
import jax
import jax.numpy as jnp
from jax import lax
from jax.experimental import pallas as pl
from jax.experimental.pallas import tpu as pltpu

D_MODEL = 1024
CTX_LEN = 256
GRID_W = 64
D_HGRN = 512
HGRN_HEADS = 4
HEAD_DIM = D_HGRN // HGRN_HEADS
D_CONV = 512
CONV_WIDTH = 31
CONV_HALF = CONV_WIDTH // 2
D_MIX = D_HGRN + D_CONV
D_IN = 5 * D_HGRN + 3 * D_CONV
EPS = 1e-6

LANES = 128
SUBLANES = 8
SCAN_CHUNK = 64
SCAN_LEVELS = SCAN_CHUNK.bit_length() - 1
SCAN_GROUP = 4
PROJ_ROWS = 512
OUT_ROWS = 1024
VMEM_LIMIT = 56 * 1024 * 1024

D_SCAN_IN = 4 * D_HGRN
COL_Q, COL_ZF, COL_ZB, COL_V = (i * HGRN_HEADS for i in range(4))
COL_GA, COL_U, COL_UG, COL_GB = (i * (D_HGRN // LANES) for i in range(4))


def _silu(x):
    return x * jax.nn.sigmoid(x)


def _mod_kernel(c_ref, w_ref, b_ref, o_ref):
    a = _silu(c_ref[...])
    o_ref[...] = jnp.dot(a, w_ref[...], preferred_element_type=jnp.float32,
                         precision=lax.Precision.HIGHEST) + b_ref[...]


def _modulation(cc, w_mod, b_mod):
    rows = cc.shape[0]
    n = w_mod.shape[1]
    return pl.pallas_call(
        _mod_kernel,
        grid=(n // D_MODEL,),
        in_specs=[pl.BlockSpec((rows, D_MODEL), lambda j: (0, 0)),
                  pl.BlockSpec((D_MODEL, D_MODEL), lambda j: (0, j)),
                  pl.BlockSpec((1, D_MODEL), lambda j: (0, j))],
        out_specs=pl.BlockSpec((rows, D_MODEL), lambda j: (0, j)),
        out_shape=jax.ShapeDtypeStruct((rows, n), jnp.float32),
        compiler_params=pltpu.CompilerParams(dimension_semantics=("arbitrary",),
                                             vmem_limit_bytes=VMEM_LIMIT),
        name="modulation",
    )(cc, w_mod, b_mod)


def _proj_kernel(x_ref, g_ref, sh_ref, sc_ref, *refs):
    x = x_ref[0]
    y = x * lax.rsqrt(jnp.mean(x * x, axis=-1, keepdims=True) + EPS) * g_ref[...]
    a = (y * (1.0 + sc_ref[0]) + sh_ref[0]).astype(jnp.bfloat16)
    n = len(refs) // 2
    for w_ref, o_ref in zip(refs[:n], refs[n:]):
        o_ref[0] = jnp.dot(a, w_ref[...], preferred_element_type=jnp.float32).astype(o_ref.dtype)


def _projection(x, norm_g, shift, scale, weights, dtypes):
    bsz, t, _ = x.shape
    rows = min(PROJ_ROWS, t)
    return pl.pallas_call(
        _proj_kernel,
        grid=(bsz, t // rows),
        in_specs=[pl.BlockSpec((1, rows, D_MODEL), lambda b, i: (b, i, 0)),
                  pl.BlockSpec((1, D_MODEL), lambda b, i: (0, 0)),
                  pl.BlockSpec((1, 1, D_MODEL), lambda b, i: (b, 0, 0)),
                  pl.BlockSpec((1, 1, D_MODEL), lambda b, i: (b, 0, 0))]
        + [pl.BlockSpec(w.shape, lambda b, i: (0, 0)) for w in weights],
        out_specs=[pl.BlockSpec((1, rows, w.shape[1]), lambda b, i: (b, i, 0)) for w in weights],
        out_shape=[jax.ShapeDtypeStruct((bsz, t, w.shape[1]), dt) for w, dt in zip(weights, dtypes)],
        compiler_params=pltpu.CompilerParams(dimension_semantics=("arbitrary", "arbitrary"),
                                             vmem_limit_bytes=VMEM_LIMIT),
        name="in_projection",
    )(x, norm_g, shift, scale, *weights)


def _scan_constants():
    c = SCAN_CHUNK
    idx = np.arange(c)
    t, s = idx[:, None], idx[None, :]
    masks = [t == s]
    for lvl in range(SCAN_LEVELS):
        h = c >> (lvl + 1)
        masks.append(((t // (2 * h)) == (s // (2 * h))) & ((t % (2 * h)) >= h) & ((s % (2 * h)) < h))
    m_f = np.stack([m.astype(np.float32) for m in masks])
    m_b = m_f[:, ::-1, ::-1].copy()
    tri_f = (s <= t).astype(np.float32)
    tri_b = (s >= t).astype(np.float32)
    return np.concatenate([tri_f] * 3, axis=1), np.concatenate([tri_b] * 3, axis=1), m_f, m_b


def _dot_nt(a, b):
    return lax.dot_general(a, b, (((1,), (1,)), ((), ())), preferred_element_type=jnp.float32)


def _dot_tn(a, b):
    return lax.dot_general(a, b, (((0,), (0,)), ((), ())), preferred_element_type=jnp.float32)


def _interleave(*generators):
    pending = list(generators)
    while pending:
        for gen in list(pending):
            try:
                next(gen)
            except StopIteration:
                pending.remove(gen)


def _mix_rows(q, k, half, reverse):
    parts = []
    for lo in range(0, SCAN_CHUNK, 2 * half):
        first, second = (q, k) if reverse else (k, q)
        parts += [first[lo:lo + half], second[lo + half:lo + 2 * half]]
    return jnp.concatenate(parts, axis=0)


def _level_decay(g_cum, f, half, reverse):
    c = SCAN_CHUNK
    if half >= SUBLANES:
        parts = []
        for lo in range(0, c, 2 * half):
            mid = lo + half
            if reverse:
                parts += [g_cum[lo:mid] - g_cum[mid:mid + 1], g_cum[mid:mid + 1] - g_cum[mid:mid + half]]
            else:
                parts += [g_cum[mid - 1:mid] - g_cum[lo:mid], g_cum[mid:mid + half] - g_cum[mid - 1:mid]]
        return jnp.exp2(jnp.concatenate(parts, axis=0))
    if half == SUBLANES // 2:
        g3 = g_cum.reshape(c // SUBLANES, SUBLANES, LANES)
        r = half if reverse else half - 1
        later = lax.broadcasted_iota(jnp.int32, g3.shape, 1) >= half
        sign = jnp.where(later != reverse, 1.0, -1.0)
        return jnp.exp2((g3 - g3[:, r:r + 1, :]) * sign).reshape(c, LANES)
    f3 = f.reshape(c // SUBLANES, SUBLANES, LANES)
    row = lax.broadcasted_iota(jnp.int32, f3.shape, 1)
    if half == 1:
        on_query_side = (row % 2 == 0) if reverse else (row % 2 == 1)
        return jnp.where(on_query_side, f3, 1.0).reshape(c, LANES)
    prev = pltpu.roll(f3, 1, 1)
    nxt = pltpu.roll(f3, SUBLANES - 1, 1)
    m4 = row % 4
    if reverse:
        d = jnp.where(m4 == 0, f3 * nxt, jnp.where(m4 == 1, f3, jnp.where(m4 == 2, 1.0, prev)))
    else:
        d = jnp.where(m4 == 0, nxt, jnp.where(m4 == 1, 1.0, jnp.where(m4 == 2, f3, f3 * prev)))
    return d.reshape(c, LANES)


def _chunk_local(q, k, v, f, g_cum, m_ref, reverse, result):
    c = SCAN_CHUNK
    bf = jnp.bfloat16
    end_row = 0 if reverse else c - 1

    kb = k.astype(bf)
    a = m_ref[0] * _dot_nt(q.astype(bf), kb)
    yield
    for lvl in range(SCAN_LEVELS):
        half = c >> (lvl + 1)
        d = _level_decay(g_cum, f, half, reverse)
        if half >= SUBLANES:
            x = (_mix_rows(q, k, half, reverse) * d).astype(bf)
            p = _dot_nt(x, x)
        elif half == 1:
            p = _dot_nt((q * d).astype(bf), kb)
        else:
            p = _dot_nt((q * d).astype(bf), (k * d).astype(bf))
        a = a + m_ref[lvl + 1] * p
        yield
    d_read = jnp.exp2(g_cum)
    d_state = jnp.exp2(g_cum[end_row:end_row + 1] - g_cum)
    vb = v.astype(bf)
    o_intra = jnp.dot(a.astype(bf), vb, preferred_element_type=jnp.float32)
    yield
    kv = _dot_tn(vb, (k * d_state).astype(bf))
    result.extend([o_intra, kv, (q * d_read).astype(bf), d_read[end_row:end_row + 1]])
    yield


def _chunk_pair(q_r, z_r, v_r, starts, lb, w_ref, m_ref, reverse, results):
    c = SCAN_CHUNK
    bf = jnp.bfloat16
    fs, g3s = [], []
    for s in starts:
        f = lb + (1.0 - lb) * jax.nn.sigmoid(z_r[0, pl.ds(s, c), :])
        g = jnp.log2(f)
        fs.append(f)
        g_hi = g.astype(bf)
        r1 = g - g_hi.astype(jnp.float32)
        g_mid = r1.astype(bf)
        g_lo = (r1 - g_mid.astype(jnp.float32)).astype(bf)
        g3s.append(jnp.concatenate([g_hi, g_mid, g_lo], axis=0))
        yield
    g_cum = jnp.dot(w_ref[...], jnp.concatenate(g3s, axis=1), preferred_element_type=jnp.float32)
    yield
    chains = [_chunk_local(q_r[0, pl.ds(s, c), :], 1.0 - fs[j], v_r[0, pl.ds(s, c), :], fs[j],
                           g_cum[:, j * LANES:(j + 1) * LANES], m_ref, reverse, results[j])
              for j, s in enumerate(starts)]
    while chains:
        for chain in list(chains):
            try:
                next(chain)
            except StopIteration:
                chains.remove(chain)
        yield


def _direction_chunks(q_r, z_r, v_r, starts, lb, w_ref, m_ref, reverse):
    results = [[] for _ in starts]
    gens = [_chunk_pair(q_r, z_r, v_r, starts[j:j + 2], lb, w_ref, m_ref, reverse, results[j:j + 2])
            for j in range(0, len(starts), 2)]
    return gens, results


def _direction_states(starts, results, st_ref, o_r):
    c = SCAN_CHUNK
    st = st_ref[...]
    for s, (o_intra, kv, q_read, d_end) in zip(starts, results):
        if o_r is not None:
            o_r[pl.ds(s, c), :] = o_intra + _dot_nt(q_read, st.astype(jnp.bfloat16))
        st = st * d_end + kv
        yield
    st_ref[...] = st


def _scan_kernel(q_ref, zf_ref, zb_ref, v_ref, ga_ref, qc_ref, zfc_ref, zbc_ref, vc_ref,
                 lbl_ref, gn_ref, wf_ref, wb_ref, mf_ref, mb_ref, o_ref,
                 stf_ref, stb_ref, of_ref, ob_ref):
    c = SCAN_CHUNK
    t_lat = q_ref.shape[1]
    t_ctx = qc_ref.shape[1]
    l0, l1 = lbl_ref[0], lbl_ref[1]
    mx = jnp.maximum(l0, l1)
    e0, e1 = jnp.exp(l0 - mx), jnp.exp(l1 - mx)
    lb = e0 / (e0 + e1)
    lb_f, lb_b = lb[0:1], lb[1:2]

    stf_ref[...] = jnp.zeros_like(stf_ref)
    stb_ref[...] = jnp.zeros_like(stb_ref)

    def both_directions(i, t, group, q_r, zf_r, zb_r, v_r, of_r, ob_r):
        span = group * c
        lo = pl.multiple_of(i * span, span)
        hi = pl.multiple_of(t - span - i * span, span)
        starts_f = [pl.multiple_of(lo + j * c, c) for j in range(group)]
        starts_b = [pl.multiple_of(hi + j * c, c) for j in reversed(range(group))]
        gens_f, res_f = _direction_chunks(q_r, zf_r, v_r, starts_f, lb_f, wf_ref, mf_ref, False)
        gens_b, res_b = _direction_chunks(q_r, zb_r, v_r, starts_b, lb_b, wb_ref, mb_ref, True)
        _interleave(*gens_f, *gens_b)
        _interleave(_direction_states(starts_f, res_f, stf_ref, of_r),
                    _direction_states(starts_b, res_b, stb_ref, ob_r))

    ctx_group = min(SCAN_GROUP, t_ctx // c)

    def ctx_body(i, carry):
        both_directions(i, t_ctx, ctx_group, qc_ref, zfc_ref, zbc_ref, vc_ref, None, None)
        return carry

    lax.fori_loop(0, t_ctx // (ctx_group * c), ctx_body, 0)

    def lat_body(i, carry):
        both_directions(i, t_lat, SCAN_GROUP, q_ref, zf_ref, zb_ref, v_ref, of_ref, ob_ref)
        return carry

    lax.fori_loop(0, t_lat // (SCAN_GROUP * c), lat_body, 0)

    rows = 256
    gn = gn_ref[...]

    def fin_body(i, carry):
        lo = pl.multiple_of(i * rows, rows)
        o = of_ref[pl.ds(lo, rows), :] + ob_ref[pl.ds(lo, rows), :]
        o = o * lax.rsqrt(jnp.mean(o * o, axis=-1, keepdims=True) + EPS) * gn
        gate = _silu(ga_ref[0, pl.ds(lo, rows), :].astype(jnp.float32))
        o_ref[0, pl.ds(lo, rows), :] = (o * gate).astype(o_ref.dtype)
        return carry

    lax.fori_loop(0, t_lat // rows, fin_body, 0, unroll=2)


def _hgrn_scan(p_lat, aux_lat, p_ctx, lb_logits, hgrn_norm_g):
    bsz, t_lat, _ = p_lat.shape
    t_ctx = p_ctx.shape[1]
    w_f3, w_b3, m_f, m_b = _scan_constants()
    wf = jnp.asarray(w_f3, jnp.bfloat16)
    wb = jnp.asarray(w_b3, jnp.bfloat16)
    mf = jnp.asarray(m_f, jnp.float32)
    mb = jnp.asarray(m_b, jnp.float32)

    def col(t, base):
        return pl.BlockSpec((1, t, LANES), lambda b, h, base=base: (b, 0, base + h))

    def whole(a):
        return pl.BlockSpec(a.shape, lambda b, h, nd=a.ndim: (0,) * nd)

    return pl.pallas_call(
        _scan_kernel,
        grid=(bsz, HGRN_HEADS),
        in_specs=[col(t_lat, COL_Q), col(t_lat, COL_ZF), col(t_lat, COL_ZB), col(t_lat, COL_V),
                  col(t_lat, COL_GA),
                  col(t_ctx, COL_Q), col(t_ctx, COL_ZF), col(t_ctx, COL_ZB), col(t_ctx, COL_V),
                  pl.BlockSpec((2, 2, LANES), lambda b, h: (0, 0, h)),
                  pl.BlockSpec((1, LANES), lambda b, h: (0, h)),
                  whole(wf), whole(wb), whole(mf), whole(mb)],
        out_specs=pl.BlockSpec((1, t_lat, LANES), lambda b, h: (b, 0, h)),
        out_shape=jax.ShapeDtypeStruct((bsz, t_lat, D_HGRN), jnp.bfloat16),
        scratch_shapes=[pltpu.VMEM((HEAD_DIM, HEAD_DIM), jnp.float32),
                        pltpu.VMEM((HEAD_DIM, HEAD_DIM), jnp.float32),
                        pltpu.VMEM((t_lat, HEAD_DIM), jnp.float32),
                        pltpu.VMEM((t_lat, HEAD_DIM), jnp.float32)],
        compiler_params=pltpu.CompilerParams(dimension_semantics=("arbitrary", "arbitrary"),
                                             vmem_limit_bytes=VMEM_LIMIT),
        name="hgrn_scan",
    )(p_lat, p_lat, p_lat, p_lat, aux_lat, p_ctx, p_ctx, p_ctx, p_ctx,
      lb_logits, hgrn_norm_g, wf, wb, mf, mb)


def _conv_kernel(u_ref, ug_ref, w_ref, b_ref, o_ref, pad_ref):
    t = u_ref.shape[1]
    n_rows = t // GRID_W
    cg = pl.program_id(1)
    bias = b_ref[...]

    @pl.when(cg < (D_CONV // 2) // LANES)
    def _along_rows():
        padw = GRID_W + 32

        def fill(r, carry):
            src = pl.multiple_of(r * GRID_W, GRID_W)
            dst = pl.multiple_of(r * padw, 32)
            glu = (u_ref[0, pl.ds(src, GRID_W), :].astype(jnp.float32)
                   * jax.nn.sigmoid(ug_ref[0, pl.ds(src, GRID_W), :].astype(jnp.float32)))
            pad_ref[pl.ds(dst, 16), :] = jnp.zeros((16, LANES), jnp.float32)
            pad_ref[pl.ds(dst + 16, GRID_W), :] = glu
            pad_ref[pl.ds(dst + 16 + GRID_W, 16), :] = jnp.zeros((16, LANES), jnp.float32)
            return carry

        lax.fori_loop(0, n_rows, fill, 0, unroll=4)

        def conv(r, carry):
            dst = pl.multiple_of(r * GRID_W, GRID_W)
            base = r * padw + (16 - CONV_HALF)
            acc = jnp.zeros((GRID_W, LANES), jnp.float32)
            for k in range(CONV_WIDTH):
                acc = acc + w_ref[k:k + 1, :] * pad_ref[pl.ds(base + k, GRID_W), :]
            o_ref[0, pl.ds(dst, GRID_W), :] = (acc + bias).astype(o_ref.dtype)
            return carry

        lax.fori_loop(0, n_rows, conv, 0, unroll=4)

    @pl.when(cg >= (D_CONV // 2) // LANES)
    def _along_cols():
        halo = CONV_HALF * GRID_W
        pad_ref[pl.ds(0, halo), :] = jnp.zeros((halo, LANES), jnp.float32)
        pad_ref[pl.ds(halo + t, halo), :] = jnp.zeros((halo, LANES), jnp.float32)

        def fill(r, carry):
            src = pl.multiple_of(r * GRID_W, GRID_W)
            glu = (u_ref[0, pl.ds(src, GRID_W), :].astype(jnp.float32)
                   * jax.nn.sigmoid(ug_ref[0, pl.ds(src, GRID_W), :].astype(jnp.float32)))
            pad_ref[pl.ds(halo + src, GRID_W), :] = glu
            return carry

        lax.fori_loop(0, n_rows, fill, 0, unroll=4)

        def conv(r, carry):
            dst = pl.multiple_of(r * GRID_W, GRID_W)
            acc = jnp.zeros((GRID_W, LANES), jnp.float32)
            for k in range(CONV_WIDTH):
                acc = acc + w_ref[k:k + 1, :] * pad_ref[pl.ds(dst + k * GRID_W, GRID_W), :]
            o_ref[0, pl.ds(dst, GRID_W), :] = (acc + bias).astype(o_ref.dtype)
            return carry

        lax.fori_loop(0, n_rows, conv, 0, unroll=4)


def _axial_conv(aux_lat, conv_w, conv_b):
    bsz, t, _ = aux_lat.shape
    n_rows = t // GRID_W
    pad_rows = max(n_rows * (GRID_W + 32), t + 2 * CONV_HALF * GRID_W)
    w_pad = jnp.zeros((32, D_CONV), jnp.float32).at[:CONV_WIDTH].set(conv_w)
    return pl.pallas_call(
        _conv_kernel,
        grid=(bsz, D_CONV // LANES),
        in_specs=[pl.BlockSpec((1, t, LANES), lambda b, g: (b, 0, COL_U + g)),
                  pl.BlockSpec((1, t, LANES), lambda b, g: (b, 0, COL_UG + g)),
                  pl.BlockSpec((32, LANES), lambda b, g: (0, g)),
                  pl.BlockSpec((1, LANES), lambda b, g: (0, g))],
        out_specs=pl.BlockSpec((1, t, LANES), lambda b, g: (b, 0, g)),
        out_shape=jax.ShapeDtypeStruct((bsz, t, D_CONV), jnp.bfloat16),
        scratch_shapes=[pltpu.VMEM((pad_rows, LANES), jnp.float32)],
        compiler_params=pltpu.CompilerParams(dimension_semantics=("arbitrary", "arbitrary"),
                                             vmem_limit_bytes=VMEM_LIMIT),
        name="axial_conv",
    )(aux_lat, aux_lat, w_pad, conv_b)


def _out_kernel(x_ref, ba_ref, y_ref, gb_ref, gt_ref, lng_ref, lnb_ref, wa_ref, wb_ref, fg_ref, o_ref):
    y = y_ref[0].astype(jnp.float32)
    mu = jnp.mean(y, axis=-1, keepdims=True)
    yc = y - mu
    var = jnp.mean(yc * yc, axis=-1, keepdims=True)
    yn = yc * lax.rsqrt(var + EPS) * lng_ref[...] + lnb_ref[...]
    branch_b = _silu(yn) * _silu(gb_ref[0].astype(jnp.float32))
    mix = jnp.dot(ba_ref[0], wa_ref[...], preferred_element_type=jnp.float32)
    mix = mix + jnp.dot(branch_b.astype(jnp.bfloat16), wb_ref[...], preferred_element_type=jnp.float32)
    h = x_ref[0] + gt_ref[0] * mix
    o_ref[0] = h * lax.rsqrt(jnp.mean(h * h, axis=-1, keepdims=True) + EPS) * fg_ref[...]


def _output(x, branch_a, y_conv, aux_lat, gate, ln_g, ln_b, w_out_bf16, final_g):
    bsz, t, _ = x.shape
    rows = OUT_ROWS
    w_a, w_b = w_out_bf16[:D_HGRN], w_out_bf16[D_HGRN:]
    gb_block = COL_GB * LANES // D_CONV
    return pl.pallas_call(
        _out_kernel,
        grid=(bsz, t // rows),
        in_specs=[pl.BlockSpec((1, rows, D_MODEL), lambda b, i: (b, i, 0)),
                  pl.BlockSpec((1, rows, D_HGRN), lambda b, i: (b, i, 0)),
                  pl.BlockSpec((1, rows, D_CONV), lambda b, i: (b, i, 0)),
                  pl.BlockSpec((1, rows, D_CONV), lambda b, i: (b, i, gb_block)),
                  pl.BlockSpec((1, 1, D_MODEL), lambda b, i: (b, 0, 0)),
                  pl.BlockSpec((1, D_CONV), lambda b, i: (0, 0)),
                  pl.BlockSpec((1, D_CONV), lambda b, i: (0, 0)),
                  pl.BlockSpec((D_HGRN, D_MODEL), lambda b, i: (0, 0)),
                  pl.BlockSpec((D_CONV, D_MODEL), lambda b, i: (0, 0)),
                  pl.BlockSpec((1, D_MODEL), lambda b, i: (0, 0))],
        out_specs=pl.BlockSpec((1, rows, D_MODEL), lambda b, i: (b, i, 0)),
        out_shape=jax.ShapeDtypeStruct((bsz, t, D_MODEL), jnp.float32),
        compiler_params=pltpu.CompilerParams(dimension_semantics=("arbitrary", "arbitrary"),
                                             vmem_limit_bytes=VMEM_LIMIT),
        name="out_projection",
    )(x, branch_a, y_conv, aux_lat, gate, ln_g, ln_b, w_a, w_b, final_g)


def kernel(x, c, ctx, c_ctx, norm_g, w_mod, b_mod, w_in, lb_logits, hgrn_norm_g, conv_w, conv_b,
           conv_ln_g, conv_ln_b, w_out, final_norm_g):
    bsz, seq_len, _ = x.shape
    span = SCAN_GROUP * SCAN_CHUNK
    assert norm_g.shape[0] == 1, "single-layer block"
    assert seq_len % GRID_W == 0 and seq_len % span == 0
    assert ctx.shape[1] % (2 * SCAN_CHUNK) == 0 and (ctx.shape[1] % span == 0 or ctx.shape[1] < span)

    pad = (-(bsz + 1)) % SUBLANES
    cc = jnp.concatenate([c, c_ctx[None, :], jnp.zeros((pad, D_MODEL), c.dtype)], axis=0)
    mod = _modulation(cc, w_mod[0], b_mod)
    shift, scale, gate = (mod[:, i * D_MODEL:(i + 1) * D_MODEL] for i in range(3))
    shift_lat, scale_lat, gate_lat = (m[:bsz, None, :] for m in (shift, scale, gate))
    shift_ctx, scale_ctx = (jnp.broadcast_to(m[bsz][None, None, :], (bsz, 1, D_MODEL)) for m in (shift, scale))

    w_in_bf16 = w_in[0].astype(jnp.bfloat16)
    w_scan, w_aux = w_in_bf16[:, :D_SCAN_IN], w_in_bf16[:, D_SCAN_IN:]
    p_lat, aux_lat = _projection(x, norm_g, shift_lat, scale_lat, (w_scan, w_aux), (jnp.float32, jnp.bfloat16))
    p_ctx, = _projection(ctx, norm_g, shift_ctx, scale_ctx, (w_scan,), (jnp.float32,))

    branch_a = _hgrn_scan(p_lat, aux_lat, p_ctx, lb_logits, hgrn_norm_g)
    y_conv = _axial_conv(aux_lat, conv_w[0], conv_b)
    return _output(x, branch_a, y_conv, aux_lat, gate_lat, conv_ln_g, conv_ln_b,
                   w_out[0].astype(jnp.bfloat16), final_norm_g[None, :])
```

```python
import numpy as np
import jax
import jax.numpy as jnp
from jax import lax
from jax.experimental import pallas as pl
from jax.experimental.pallas import tpu as pltpu

D_MODEL = 1024
CTX_LEN = 256
GRID_W = 64
D_HGRN = 512
HGRN_HEADS = 4
HEAD_DIM = D_HGRN // HGRN_HEADS
D_CONV = 512
CONV_WIDTH = 31
CONV_HALF = CONV_WIDTH // 2
D_MIX = D_HGRN + D_CONV
D_IN = 5 * D_HGRN + 3 * D_CONV
EPS = 1e-6

LANES = 128
SUBLANES = 8
SCAN_CHUNK = 64
SCAN_LEVELS = SCAN_CHUNK.bit_length() - 1
SCAN_GROUP = 8
PROJ_ROWS = 512
OUT_ROWS = 1024
OUT_SLAB = 256
VMEM_LIMIT = 56 * 1024 * 1024

D_SCAN_IN = 4 * D_HGRN
COL_Q, COL_ZF, COL_ZB, COL_V = (i * HGRN_HEADS for i in range(4))
COL_GA, COL_U, COL_UG, COL_GB = (i * (D_HGRN // LANES) for i in range(4))


def _silu(x):
    return x * jax.nn.sigmoid(x)


def _mod_kernel(c_ref, w_ref, b_ref, o_ref):
    a = _silu(c_ref[...])
    o_ref[...] = jnp.dot(a, w_ref[...], preferred_element_type=jnp.float32,
                         precision=lax.Precision.HIGHEST) + b_ref[...]


def _modulation(cc, w_mod, b_mod):
    rows = cc.shape[0]
    n = w_mod.shape[1]
    return pl.pallas_call(
        _mod_kernel,
        grid=(n // D_MODEL,),
        in_specs=[pl.BlockSpec((rows, D_MODEL), lambda j: (0, 0)),
                  pl.BlockSpec((D_MODEL, D_MODEL), lambda j: (0, j)),
                  pl.BlockSpec((1, D_MODEL), lambda j: (0, j))],
        out_specs=pl.BlockSpec((rows, D_MODEL), lambda j: (0, j)),
        out_shape=jax.ShapeDtypeStruct((rows, n), jnp.float32),
        compiler_params=pltpu.CompilerParams(dimension_semantics=("arbitrary",),
                                             vmem_limit_bytes=VMEM_LIMIT),
        name="modulation",
    )(cc, w_mod, b_mod)


def _proj_kernel(x_ref, g_ref, sh_ref, sc_ref, *refs):
    x = x_ref[0]
    y = x * lax.rsqrt(jnp.mean(x * x, axis=-1, keepdims=True) + EPS) * g_ref[...]
    a = (y * (1.0 + sc_ref[0]) + sh_ref[0]).astype(jnp.bfloat16)
    n = len(refs) // 2
    for w_ref, o_ref in zip(refs[:n], refs[n:]):
        o_ref[0] = jnp.dot(a, w_ref[...], preferred_element_type=jnp.float32).astype(o_ref.dtype)


def _projection(x, norm_g, shift, scale, weights, dtypes):
    bsz, t, _ = x.shape
    rows = min(PROJ_ROWS, t)
    return pl.pallas_call(
        _proj_kernel,
        grid=(bsz, t // rows),
        in_specs=[pl.BlockSpec((1, rows, D_MODEL), lambda b, i: (b, i, 0)),
                  pl.BlockSpec((1, D_MODEL), lambda b, i: (0, 0)),
                  pl.BlockSpec((1, 1, D_MODEL), lambda b, i: (b, 0, 0)),
                  pl.BlockSpec((1, 1, D_MODEL), lambda b, i: (b, 0, 0))]
        + [pl.BlockSpec(w.shape, lambda b, i: (0, 0)) for w in weights],
        out_specs=[pl.BlockSpec((1, rows, w.shape[1]), lambda b, i: (b, i, 0)) for w in weights],
        out_shape=[jax.ShapeDtypeStruct((bsz, t, w.shape[1]), dt) for w, dt in zip(weights, dtypes)],
        compiler_params=pltpu.CompilerParams(dimension_semantics=("arbitrary", "arbitrary"),
                                             vmem_limit_bytes=VMEM_LIMIT),
        name="in_projection",
    )(x, norm_g, shift, scale, *weights)


def _scan_constants():
    c = SCAN_CHUNK
    idx = np.arange(c)
    t, s = idx[:, None], idx[None, :]
    masks = [t == s]
    for lvl in range(SCAN_LEVELS):
        h = c >> (lvl + 1)
        masks.append(((t // (2 * h)) == (s // (2 * h))) & ((t % (2 * h)) >= h) & ((s % (2 * h)) < h))
    m_f = np.stack([m.astype(np.float32) for m in masks])
    m_b = m_f[:, ::-1, ::-1].copy()
    tri_f = (s <= t).astype(np.float32)
    tri_b = (s >= t).astype(np.float32)
    return np.concatenate([tri_f] * 3, axis=1), np.concatenate([tri_b] * 3, axis=1), m_f, m_b


def _dot_nt(a, b):
    return lax.dot_general(a, b, (((1,), (1,)), ((), ())), preferred_element_type=jnp.float32)


def _dot_tn(a, b):
    return lax.dot_general(a, b, (((0,), (0,)), ((), ())), preferred_element_type=jnp.float32)


def _interleave(*generators):
    pending = list(generators)
    while pending:
        for gen in list(pending):
            try:
                next(gen)
            except StopIteration:
                pending.remove(gen)


def _mix_rows(q, k, half, reverse):
    parts = []
    for lo in range(0, SCAN_CHUNK, 2 * half):
        first, second = (q, k) if reverse else (k, q)
        parts += [first[lo:lo + half], second[lo + half:lo + 2 * half]]
    return jnp.concatenate(parts, axis=0)


def _level_decay(g_cum, f, half, reverse):
    c = SCAN_CHUNK
    if half >= SUBLANES:
        parts = []
        for lo in range(0, c, 2 * half):
            mid = lo + half
            if reverse:
                parts += [g_cum[lo:mid] - g_cum[mid:mid + 1], g_cum[mid:mid + 1] - g_cum[mid:mid + half]]
            else:
                parts += [g_cum[mid - 1:mid] - g_cum[lo:mid], g_cum[mid:mid + half] - g_cum[mid - 1:mid]]
        return jnp.exp2(jnp.concatenate(parts, axis=0))
    if half == SUBLANES // 2:
        g3 = g_cum.reshape(c // SUBLANES, SUBLANES, LANES)
        r = half if reverse else half - 1
        later = lax.broadcasted_iota(jnp.int32, g3.shape, 1) >= half
        sign = jnp.where(later != reverse, 1.0, -1.0)
        return jnp.exp2((g3 - g3[:, r:r + 1, :]) * sign).reshape(c, LANES)
    f3 = f.reshape(c // SUBLANES, SUBLANES, LANES)
    row = lax.broadcasted_iota(jnp.int32, f3.shape, 1)
    if half == 1:
        on_query_side = (row % 2 == 0) if reverse else (row % 2 == 1)
        return jnp.where(on_query_side, f3, 1.0).reshape(c, LANES)
    prev = pltpu.roll(f3, 1, 1)
    nxt = pltpu.roll(f3, SUBLANES - 1, 1)
    m4 = row % 4
    if reverse:
        d = jnp.where(m4 == 0, f3 * nxt, jnp.where(m4 == 1, f3, jnp.where(m4 == 2, 1.0, prev)))
    else:
        d = jnp.where(m4 == 0, nxt, jnp.where(m4 == 1, 1.0, jnp.where(m4 == 2, f3, f3 * prev)))
    return d.reshape(c, LANES)


def _chunk_local(q, k, v, f, g_cum, m_ref, reverse, result):
    c = SCAN_CHUNK
    bf = jnp.bfloat16
    end_row = 0 if reverse else c - 1

    kb = k.astype(bf)
    a = m_ref[0] * _dot_nt(q.astype(bf), kb)
    yield
    for lvl in range(SCAN_LEVELS):
        half = c >> (lvl + 1)
        d = _level_decay(g_cum, f, half, reverse)
        if half >= SUBLANES:
            x = (_mix_rows(q, k, half, reverse) * d).astype(bf)
            p = _dot_nt(x, x)
        elif half == 1:
            p = _dot_nt((q * d).astype(bf), kb)
        else:
            p = _dot_nt((q * d).astype(bf), (k * d).astype(bf))
        a = a + m_ref[lvl + 1] * p
        yield
    d_read = jnp.exp2(g_cum)
    d_state = jnp.exp2(g_cum[end_row:end_row + 1] - g_cum)
    vb = v.astype(bf)
    o_intra = jnp.dot(a.astype(bf), vb, preferred_element_type=jnp.float32)
    yield
    kv = _dot_tn(vb, (k * d_state).astype(bf))
    result.extend([o_intra, kv, (q * d_read).astype(bf), d_read[end_row:end_row + 1]])
    yield


def _chunk_pair(q_r, z_r, v_r, starts, lb, w_ref, m_ref, reverse, results):
    c = SCAN_CHUNK
    bf = jnp.bfloat16
    fs, g3s = [], []
    for s in starts:
        f = lb + (1.0 - lb) * jax.nn.sigmoid(z_r[0, pl.ds(s, c), :])
        g = jnp.log2(f)
        fs.append(f)
        g_hi = g.astype(bf)
        r1 = g - g_hi.astype(jnp.float32)
        g_mid = r1.astype(bf)
        g_lo = (r1 - g_mid.astype(jnp.float32)).astype(bf)
        g3s.append(jnp.concatenate([g_hi, g_mid, g_lo], axis=0))
        yield
    g_cum = jnp.dot(w_ref[...], jnp.concatenate(g3s, axis=1), preferred_element_type=jnp.float32)
    yield
    chains = [_chunk_local(q_r[0, pl.ds(s, c), :], 1.0 - fs[j], v_r[0, pl.ds(s, c), :], fs[j],
                           g_cum[:, j * LANES:(j + 1) * LANES], m_ref, reverse, results[j])
              for j, s in enumerate(starts)]
    while chains:
        for chain in list(chains):
            try:
                next(chain)
            except StopIteration:
                chains.remove(chain)
        yield


def _direction_chunks(q_r, z_r, v_r, starts, lb, w_ref, m_ref, reverse):
    results = [[] for _ in starts]
    gens = [_chunk_pair(q_r, z_r, v_r, starts[j:j + 2], lb, w_ref, m_ref, reverse, results[j:j + 2])
            for j in range(0, len(starts), 2)]
    return gens, results


def _direction_states(starts, results, st_ref, o_r):
    c = SCAN_CHUNK
    st = st_ref[...]
    for s, (o_intra, kv, q_read, d_end) in zip(starts, results):
        if o_r is not None:
            o_r[pl.ds(s, c), :] = o_intra + _dot_nt(q_read, st.astype(jnp.bfloat16))
        st = st * d_end + kv
        yield
    st_ref[...] = st


def _scan_kernel(q_ref, zf_ref, zb_ref, v_ref, ga_ref, qc_ref, zfc_ref, zbc_ref, vc_ref,
                 lbl_ref, gn_ref, wf_ref, wb_ref, mf_ref, mb_ref, o_ref,
                 stf_ref, stb_ref, of_ref, ob_ref):
    c = SCAN_CHUNK
    t_lat = q_ref.shape[1]
    t_ctx = qc_ref.shape[1]
    l0, l1 = lbl_ref[0], lbl_ref[1]
    mx = jnp.maximum(l0, l1)
    e0, e1 = jnp.exp(l0 - mx), jnp.exp(l1 - mx)
    lb = e0 / (e0 + e1)
    lb_f, lb_b = lb[0:1], lb[1:2]

    stf_ref[...] = jnp.zeros_like(stf_ref)
    stb_ref[...] = jnp.zeros_like(stb_ref)

    def both_directions(i, t, group, q_r, zf_r, zb_r, v_r, of_r, ob_r):
        span = group * c
        lo = pl.multiple_of(i * span, span)
        hi = pl.multiple_of(t - span - i * span, span)
        starts_f = [pl.multiple_of(lo + j * c, c) for j in range(group)]
        starts_b = [pl.multiple_of(hi + j * c, c) for j in reversed(range(group))]
        gens_f, res_f = _direction_chunks(q_r, zf_r, v_r, starts_f, lb_f, wf_ref, mf_ref, False)
        gens_b, res_b = _direction_chunks(q_r, zb_r, v_r, starts_b, lb_b, wb_ref, mb_ref, True)
        _interleave(*gens_f, *gens_b)
        _interleave(_direction_states(starts_f, res_f, stf_ref, of_r),
                    _direction_states(starts_b, res_b, stb_ref, ob_r))

    ctx_group = min(SCAN_GROUP, t_ctx // c)

    def ctx_body(i, carry):
        both_directions(i, t_ctx, ctx_group, qc_ref, zfc_ref, zbc_ref, vc_ref, None, None)
        return carry

    lax.fori_loop(0, t_ctx // (ctx_group * c), ctx_body, 0)

    def lat_body(i, carry):
        both_directions(i, t_lat, SCAN_GROUP, q_ref, zf_ref, zb_ref, v_ref, of_ref, ob_ref)
        return carry

    lax.fori_loop(0, t_lat // (SCAN_GROUP * c), lat_body, 0)

    rows = 256
    gn = gn_ref[...]

    def fin_body(i, carry):
        lo = pl.multiple_of(i * rows, rows)
        o = of_ref[pl.ds(lo, rows), :] + ob_ref[pl.ds(lo, rows), :]
        o = o * lax.rsqrt(jnp.mean(o * o, axis=-1, keepdims=True) + EPS) * gn
        gate = _silu(ga_ref[0, pl.ds(lo, rows), :].astype(jnp.float32))
        o_ref[0, pl.ds(lo, rows), :] = (o * gate).astype(o_ref.dtype)
        return carry

    lax.fori_loop(0, t_lat // rows, fin_body, 0, unroll=2)


def _hgrn_scan(p_lat, aux_lat, p_ctx, lb_logits, hgrn_norm_g):
    bsz, t_lat, _ = p_lat.shape
    t_ctx = p_ctx.shape[1]
    w_f3, w_b3, m_f, m_b = _scan_constants()
    wf = jnp.asarray(w_f3, jnp.bfloat16)
    wb = jnp.asarray(w_b3, jnp.bfloat16)
    mf = jnp.asarray(m_f, jnp.float32)
    mb = jnp.asarray(m_b, jnp.float32)

    def col(t, base):
        return pl.BlockSpec((1, t, LANES), lambda b, h, base=base: (b, 0, base + h))

    def whole(a):
        return pl.BlockSpec(a.shape, lambda b, h, nd=a.ndim: (0,) * nd)

    return pl.pallas_call(
        _scan_kernel,
        grid=(bsz, HGRN_HEADS),
        in_specs=[col(t_lat, COL_Q), col(t_lat, COL_ZF), col(t_lat, COL_ZB), col(t_lat, COL_V),
                  col(t_lat, COL_GA),
                  col(t_ctx, COL_Q), col(t_ctx, COL_ZF), col(t_ctx, COL_ZB), col(t_ctx, COL_V),
                  pl.BlockSpec((2, 2, LANES), lambda b, h: (0, 0, h)),
                  pl.BlockSpec((1, LANES), lambda b, h: (0, h)),
                  whole(wf), whole(wb), whole(mf), whole(mb)],
        out_specs=pl.BlockSpec((1, t_lat, LANES), lambda b, h: (b, 0, h)),
        out_shape=jax.ShapeDtypeStruct((bsz, t_lat, D_HGRN), jnp.bfloat16),
        scratch_shapes=[pltpu.VMEM((HEAD_DIM, HEAD_DIM), jnp.float32),
                        pltpu.VMEM((HEAD_DIM, HEAD_DIM), jnp.float32),
                        pltpu.VMEM((t_lat, HEAD_DIM), jnp.float32),
                        pltpu.VMEM((t_lat, HEAD_DIM), jnp.float32)],
        compiler_params=pltpu.CompilerParams(dimension_semantics=("arbitrary", "arbitrary"),
                                             vmem_limit_bytes=VMEM_LIMIT),
        name="hgrn_scan",
    )(p_lat, p_lat, p_lat, p_lat, aux_lat, p_ctx, p_ctx, p_ctx, p_ctx,
      lb_logits, hgrn_norm_g, wf, wb, mf, mb)


CONV_PADW = GRID_W + 32


def _conv_fill(u_ref, ug_ref, pad_ref, along_rows):
    t = u_ref.shape[1]
    n_rows = t // GRID_W

    @pl.when(along_rows)
    def _along_rows():
        padw = CONV_PADW

        def fill(r, carry):
            src = pl.multiple_of(r * GRID_W, GRID_W)
            dst = pl.multiple_of(r * padw, 32)
            glu = (u_ref[0, pl.ds(src, GRID_W), :].astype(jnp.float32)
                   * jax.nn.sigmoid(ug_ref[0, pl.ds(src, GRID_W), :].astype(jnp.float32)))
            pad_ref[pl.ds(dst, 16), :] = jnp.zeros((16, LANES), jnp.float32)
            pad_ref[pl.ds(dst + 16, GRID_W), :] = glu
            pad_ref[pl.ds(dst + 16 + GRID_W, 16), :] = jnp.zeros((16, LANES), jnp.float32)
            return carry

        lax.fori_loop(0, n_rows, fill, 0, unroll=4)

    @pl.when(jnp.logical_not(along_rows))
    def _along_cols():
        halo = CONV_HALF * GRID_W
        pad_ref[pl.ds(0, halo), :] = jnp.zeros((halo, LANES), jnp.float32)
        pad_ref[pl.ds(halo + t, halo), :] = jnp.zeros((halo, LANES), jnp.float32)

        def fill(r, carry):
            src = pl.multiple_of(r * GRID_W, GRID_W)
            glu = (u_ref[0, pl.ds(src, GRID_W), :].astype(jnp.float32)
                   * jax.nn.sigmoid(ug_ref[0, pl.ds(src, GRID_W), :].astype(jnp.float32)))
            pad_ref[pl.ds(halo + src, GRID_W), :] = glu
            return carry

        lax.fori_loop(0, n_rows, fill, 0, unroll=4)


def _conv_kernel(u_ref, ug_ref, w_ref, b_ref, o_ref, pad_ref):
    n_rows = u_ref.shape[1] // GRID_W
    along_rows = pl.program_id(1) < (D_CONV // 2) // LANES
    _conv_fill(u_ref, ug_ref, pad_ref, along_rows)
    bias = b_ref[...]

    def taps(first_tap, stride):
        def conv(r, carry):
            dst = pl.multiple_of(r * GRID_W, GRID_W)
            base = first_tap(r)
            acc = jnp.zeros((GRID_W, LANES), jnp.float32)
            for k in range(CONV_WIDTH):
                acc = acc + w_ref[k:k + 1, :] * pad_ref[pl.ds(base + k * stride, GRID_W), :]
            o_ref[0, pl.ds(dst, GRID_W), :] = (acc + bias).astype(o_ref.dtype)
            return carry

        lax.fori_loop(0, n_rows, conv, 0, unroll=4)

    @pl.when(along_rows)
    def _():
        taps(lambda r: r * CONV_PADW + (16 - CONV_HALF), 1)

    @pl.when(jnp.logical_not(along_rows))
    def _():
        taps(lambda r: pl.multiple_of(r * GRID_W, GRID_W), GRID_W)


def _axial_conv(aux_lat, conv_w, conv_b):
    bsz, t, _ = aux_lat.shape
    n_rows = t // GRID_W
    pad_rows = max(n_rows * CONV_PADW, t + 2 * CONV_HALF * GRID_W)
    w_pad = jnp.zeros((32, D_CONV), jnp.float32).at[:CONV_WIDTH].set(conv_w)
    return pl.pallas_call(
        _conv_kernel,
        grid=(bsz, D_CONV // LANES),
        in_specs=[pl.BlockSpec((1, t, LANES), lambda b, g: (b, 0, COL_U + g)),
                  pl.BlockSpec((1, t, LANES), lambda b, g: (b, 0, COL_UG + g)),
                  pl.BlockSpec((32, LANES), lambda b, g: (0, g)),
                  pl.BlockSpec((1, LANES), lambda b, g: (0, g))],
        out_specs=pl.BlockSpec((1, t, LANES), lambda b, g: (b, 0, g)),
        out_shape=jax.ShapeDtypeStruct((bsz, t, D_CONV), jnp.bfloat16),
        scratch_shapes=[pltpu.VMEM((pad_rows, LANES), jnp.float32)],
        compiler_params=pltpu.CompilerParams(dimension_semantics=("arbitrary", "arbitrary"),
                                             vmem_limit_bytes=VMEM_LIMIT),
        name="axial_conv",
    )(aux_lat, aux_lat, w_pad, conv_b)


def _out_kernel(x_ref, ba_ref, y_ref, gb_ref, gt_ref, lng_ref, lnb_ref, wa_ref, wb_ref, fg_ref, o_ref):
    def slab(lo):
        rows = slice(lo, lo + OUT_SLAB)
        y = y_ref[0, rows, :].astype(jnp.float32)
        mu = jnp.mean(y, axis=-1, keepdims=True)
        yc = y - mu
        var = jnp.mean(yc * yc, axis=-1, keepdims=True)
        yn = yc * lax.rsqrt(var + EPS) * lng_ref[...] + lnb_ref[...]
        branch_b = (_silu(yn) * _silu(gb_ref[0, rows, :].astype(jnp.float32))).astype(jnp.bfloat16)
        yield
        mix = jnp.dot(ba_ref[0, rows, :], wa_ref[...], preferred_element_type=jnp.float32)
        mix = mix + jnp.dot(branch_b, wb_ref[...], preferred_element_type=jnp.float32)
        yield
        h = x_ref[0, rows, :] + gt_ref[0] * mix
        o_ref[0, rows, :] = h * lax.rsqrt(jnp.mean(h * h, axis=-1, keepdims=True) + EPS) * fg_ref[...]
        yield

    slabs = [slab(lo) for lo in range(0, x_ref.shape[1], OUT_SLAB)]
    active = []
    while slabs or active:
        if slabs:
            active.append(slabs.pop(0))
        for gen in list(active):
            try:
                next(gen)
            except StopIteration:
                active.remove(gen)


def _output(x, branch_a, y_conv, aux_lat, gate, ln_g, ln_b, w_out_bf16, final_g):
    bsz, t, _ = x.shape
    rows = OUT_ROWS
    w_a, w_b = w_out_bf16[:D_HGRN], w_out_bf16[D_HGRN:]
    gb_block = COL_GB * LANES // D_CONV
    return pl.pallas_call(
        _out_kernel,
        grid=(bsz, t // rows),
        in_specs=[pl.BlockSpec((1, rows, D_MODEL), lambda b, i: (b, i, 0)),
                  pl.BlockSpec((1, rows, D_HGRN), lambda b, i: (b, i, 0)),
                  pl.BlockSpec((1, rows, D_CONV), lambda b, i: (b, i, 0)),
                  pl.BlockSpec((1, rows, D_CONV), lambda b, i: (b, i, gb_block)),
                  pl.BlockSpec((1, 1, D_MODEL), lambda b, i: (b, 0, 0)),
                  pl.BlockSpec((1, D_CONV), lambda b, i: (0, 0)),
                  pl.BlockSpec((1, D_CONV), lambda b, i: (0, 0)),
                  pl.BlockSpec((D_HGRN, D_MODEL), lambda b, i: (0, 0)),
                  pl.BlockSpec((D_CONV, D_MODEL), lambda b, i: (0, 0)),
                  pl.BlockSpec((1, D_MODEL), lambda b, i: (0, 0))],
        out_specs=pl.BlockSpec((1, rows, D_MODEL), lambda b, i: (b, i, 0)),
        out_shape=jax.ShapeDtypeStruct((bsz, t, D_MODEL), jnp.float32),
        compiler_params=pltpu.CompilerParams(dimension_semantics=("arbitrary", "arbitrary"),
                                             vmem_limit_bytes=VMEM_LIMIT),
        name="out_projection",
    )(x, branch_a, y_conv, aux_lat, gate, ln_g, ln_b, w_a, w_b, final_g)


def kernel(x, c, ctx, c_ctx, norm_g, w_mod, b_mod, w_in, lb_logits, hgrn_norm_g, conv_w, conv_b,
           conv_ln_g, conv_ln_b, w_out, final_norm_g):
    bsz, seq_len, _ = x.shape
    span = SCAN_GROUP * SCAN_CHUNK
    assert norm_g.shape[0] == 1, "single-layer block"
    assert seq_len % GRID_W == 0 and seq_len % span == 0
    assert ctx.shape[1] % (2 * SCAN_CHUNK) == 0 and (ctx.shape[1] % span == 0 or ctx.shape[1] < span)

    pad = (-(bsz + 1)) % SUBLANES
    cc = jnp.concatenate([c, c_ctx[None, :], jnp.zeros((pad, D_MODEL), c.dtype)], axis=0)
    mod = _modulation(cc, w_mod[0], b_mod)
    shift, scale, gate = (mod[:, i * D_MODEL:(i + 1) * D_MODEL] for i in range(3))
    shift_lat, scale_lat, gate_lat = (m[:bsz, None, :] for m in (shift, scale, gate))
    shift_ctx, scale_ctx = (jnp.broadcast_to(m[bsz][None, None, :], (bsz, 1, D_MODEL)) for m in (shift, scale))

    w_in_bf16 = w_in[0].astype(jnp.bfloat16)
    w_scan, w_aux = w_in_bf16[:, :D_SCAN_IN], w_in_bf16[:, D_SCAN_IN:]
    p_lat, aux_lat = _projection(x, norm_g, shift_lat, scale_lat, (w_scan, w_aux), (jnp.float32, jnp.bfloat16))
    p_ctx, = _projection(ctx, norm_g, shift_ctx, scale_ctx, (w_scan,), (jnp.float32,))

    branch_a = _hgrn_scan(p_lat, aux_lat, p_ctx, lb_logits, hgrn_norm_g)
    y_conv = _axial_conv(aux_lat, conv_w[0], conv_b)
    return _output(x, branch_a, y_conv, aux_lat, gate_lat, conv_ln_g, conv_ln_b,
                   w_out[0].astype(jnp.bfloat16), final_norm_g[None, :])
```

```python
import numpy as np
import jax
import jax.numpy as jnp
from jax import lax
from jax.experimental import pallas as pl
from jax.experimental.pallas import tpu as pltpu

D_MODEL = 1024
CTX_LEN = 256
GRID_W = 64
D_HGRN = 512
HGRN_HEADS = 4
HEAD_DIM = D_HGRN // HGRN_HEADS
D_CONV = 512
CONV_WIDTH = 31
CONV_HALF = CONV_WIDTH // 2
D_MIX = D_HGRN + D_CONV
D_IN = 5 * D_HGRN + 3 * D_CONV
EPS = 1e-6

LANES = 128
SUBLANES = 8
SCAN_CHUNK = 64
SCAN_LEVELS = SCAN_CHUNK.bit_length() - 1
SCAN_GROUP = 8
SCAN_STAGGER = 1
PROJ_ROWS = 512
OUT_ROWS = 1024
VMEM_LIMIT = 56 * 1024 * 1024

D_SCAN_IN = 4 * D_HGRN
COL_Q, COL_ZF, COL_ZB, COL_V = (i * HGRN_HEADS for i in range(4))
COL_GA, COL_U, COL_UG, COL_GB = (i * (D_HGRN // LANES) for i in range(4))


def _silu(x):
    return x * jax.nn.sigmoid(x)


def _mod_kernel(c_ref, w_ref, b_ref, o_ref):
    a = _silu(c_ref[...])
    o_ref[...] = jnp.dot(a, w_ref[...], preferred_element_type=jnp.float32,
                         precision=lax.Precision.HIGHEST) + b_ref[...]


def _modulation(cc, w_mod, b_mod):
    rows = cc.shape[0]
    n = w_mod.shape[1]
    return pl.pallas_call(
        _mod_kernel,
        grid=(n // D_MODEL,),
        in_specs=[pl.BlockSpec((rows, D_MODEL), lambda j: (0, 0)),
                  pl.BlockSpec((D_MODEL, D_MODEL), lambda j: (0, j)),
                  pl.BlockSpec((1, D_MODEL), lambda j: (0, j))],
        out_specs=pl.BlockSpec((rows, D_MODEL), lambda j: (0, j)),
        out_shape=jax.ShapeDtypeStruct((rows, n), jnp.float32),
        compiler_params=pltpu.CompilerParams(dimension_semantics=("arbitrary",),
                                             vmem_limit_bytes=VMEM_LIMIT),
        name="modulation",
    )(cc, w_mod, b_mod)


def _proj_kernel(x_ref, g_ref, sh_ref, sc_ref, *refs):
    x = x_ref[0]
    y = x * lax.rsqrt(jnp.mean(x * x, axis=-1, keepdims=True) + EPS) * g_ref[...]
    a = (y * (1.0 + sc_ref[0]) + sh_ref[0]).astype(jnp.bfloat16)
    n = len(refs) // 2
    for w_ref, o_ref in zip(refs[:n], refs[n:]):
        o_ref[0] = jnp.dot(a, w_ref[...], preferred_element_type=jnp.float32).astype(o_ref.dtype)


def _projection(x, norm_g, shift, scale, weights, dtypes):
    bsz, t, _ = x.shape
    rows = min(PROJ_ROWS, t)
    return pl.pallas_call(
        _proj_kernel,
        grid=(bsz, t // rows),
        in_specs=[pl.BlockSpec((1, rows, D_MODEL), lambda b, i: (b, i, 0)),
                  pl.BlockSpec((1, D_MODEL), lambda b, i: (0, 0)),
                  pl.BlockSpec((1, 1, D_MODEL), lambda b, i: (b, 0, 0)),
                  pl.BlockSpec((1, 1, D_MODEL), lambda b, i: (b, 0, 0))]
        + [pl.BlockSpec(w.shape, lambda b, i: (0, 0)) for w in weights],
        out_specs=[pl.BlockSpec((1, rows, w.shape[1]), lambda b, i: (b, i, 0)) for w in weights],
        out_shape=[jax.ShapeDtypeStruct((bsz, t, w.shape[1]), dt) for w, dt in zip(weights, dtypes)],
        compiler_params=pltpu.CompilerParams(dimension_semantics=("arbitrary", "arbitrary"),
                                             vmem_limit_bytes=VMEM_LIMIT),
        name="in_projection",
    )(x, norm_g, shift, scale, *weights)


def _scan_constants():
    c = SCAN_CHUNK
    idx = np.arange(c)
    t, s = idx[:, None], idx[None, :]
    masks = [t == s]
    for lvl in range(SCAN_LEVELS):
        h = c >> (lvl + 1)
        masks.append(((t // (2 * h)) == (s // (2 * h))) & ((t % (2 * h)) >= h) & ((s % (2 * h)) < h))
    m_f = np.stack([m.astype(np.float32) for m in masks])
    m_b = m_f[:, ::-1, ::-1].copy()
    tri_f = (s <= t).astype(np.float32)
    tri_b = (s >= t).astype(np.float32)
    return np.concatenate([tri_f] * 3, axis=1), np.concatenate([tri_b] * 3, axis=1), m_f, m_b


def _dot_nt(a, b):
    return lax.dot_general(a, b, (((1,), (1,)), ((), ())), preferred_element_type=jnp.float32)


def _dot_tn(a, b):
    return lax.dot_general(a, b, (((0,), (0,)), ((), ())), preferred_element_type=jnp.float32)


def _interleave(*generators):
    pending = list(generators)
    while pending:
        for gen in list(pending):
            try:
                next(gen)
            except StopIteration:
                pending.remove(gen)


def _mix_rows(q, k, half, reverse):
    parts = []
    for lo in range(0, SCAN_CHUNK, 2 * half):
        first, second = (q, k) if reverse else (k, q)
        parts += [first[lo:lo + half], second[lo + half:lo + 2 * half]]
    return jnp.concatenate(parts, axis=0)


def _level_decay(g_cum, f, half, reverse):
    c = SCAN_CHUNK
    if half >= SUBLANES:
        parts = []
        for lo in range(0, c, 2 * half):
            mid = lo + half
            if reverse:
                parts += [g_cum[lo:mid] - g_cum[mid:mid + 1], g_cum[mid:mid + 1] - g_cum[mid:mid + half]]
            else:
                parts += [g_cum[mid - 1:mid] - g_cum[lo:mid], g_cum[mid:mid + half] - g_cum[mid - 1:mid]]
        return jnp.exp2(jnp.concatenate(parts, axis=0))
    if half == SUBLANES // 2:
        g3 = g_cum.reshape(c // SUBLANES, SUBLANES, LANES)
        r = half if reverse else half - 1
        later = lax.broadcasted_iota(jnp.int32, g3.shape, 1) >= half
        sign = jnp.where(later != reverse, 1.0, -1.0)
        return jnp.exp2((g3 - g3[:, r:r + 1, :]) * sign).reshape(c, LANES)
    f3 = f.reshape(c // SUBLANES, SUBLANES, LANES)
    row = lax.broadcasted_iota(jnp.int32, f3.shape, 1)
    if half == 1:
        on_query_side = (row % 2 == 0) if reverse else (row % 2 == 1)
        return jnp.where(on_query_side, f3, 1.0).reshape(c, LANES)
    prev = pltpu.roll(f3, 1, 1)
    nxt = pltpu.roll(f3, SUBLANES - 1, 1)
    m4 = row % 4
    if reverse:
        d = jnp.where(m4 == 0, f3 * nxt, jnp.where(m4 == 1, f3, jnp.where(m4 == 2, 1.0, prev)))
    else:
        d = jnp.where(m4 == 0, nxt, jnp.where(m4 == 1, 1.0, jnp.where(m4 == 2, f3, f3 * prev)))
    return d.reshape(c, LANES)


def _chunk_local(q, k, v, f, g_cum, m_ref, reverse, result):
    c = SCAN_CHUNK
    bf = jnp.bfloat16
    end_row = 0 if reverse else c - 1

    kb = k.astype(bf)
    a = m_ref[0] * _dot_nt(q.astype(bf), kb)
    yield
    for lvl in range(SCAN_LEVELS):
        half = c >> (lvl + 1)
        d = _level_decay(g_cum, f, half, reverse)
        if half >= SUBLANES:
            x = (_mix_rows(q, k, half, reverse) * d).astype(bf)
            p = _dot_nt(x, x)
        elif half == 1:
            p = _dot_nt((q * d).astype(bf), kb)
        else:
            p = _dot_nt((q * d).astype(bf), (k * d).astype(bf))
        a = a + m_ref[lvl + 1] * p
        yield
    d_read = jnp.exp2(g_cum)
    d_state = jnp.exp2(g_cum[end_row:end_row + 1] - g_cum)
    vb = v.astype(bf)
    o_intra = jnp.dot(a.astype(bf), vb, preferred_element_type=jnp.float32)
    yield
    kv = _dot_tn(vb, (k * d_state).astype(bf))
    result.extend([o_intra, kv, (q * d_read).astype(bf), d_read[end_row:end_row + 1]])
    yield


def _chunk_pair(q_r, z_r, v_r, starts, lb, w_ref, m_ref, reverse, results):
    c = SCAN_CHUNK
    bf = jnp.bfloat16
    fs, g3s = [], []
    for s in starts:
        f = lb + (1.0 - lb) * jax.nn.sigmoid(z_r[0, pl.ds(s, c), :])
        g = jnp.log2(f)
        fs.append(f)
        g_hi = g.astype(bf)
        r1 = g - g_hi.astype(jnp.float32)
        g_mid = r1.astype(bf)
        g_lo = (r1 - g_mid.astype(jnp.float32)).astype(bf)
        g3s.append(jnp.concatenate([g_hi, g_mid, g_lo], axis=0))
        yield
    g_cum = jnp.dot(w_ref[...], jnp.concatenate(g3s, axis=1), preferred_element_type=jnp.float32)
    yield
    chains = [_chunk_local(q_r[0, pl.ds(s, c), :], 1.0 - fs[j], v_r[0, pl.ds(s, c), :], fs[j],
                           g_cum[:, j * LANES:(j + 1) * LANES], m_ref, reverse, results[j])
              for j, s in enumerate(starts)]
    while chains:
        for chain in list(chains):
            try:
                next(chain)
            except StopIteration:
                chains.remove(chain)
        yield


def _direction_chunks(q_r, z_r, v_r, starts, lb, w_ref, m_ref, reverse):
    results = [[] for _ in starts]
    gens = [_chunk_pair(q_r, z_r, v_r, starts[j:j + 2], lb, w_ref, m_ref, reverse, results[j:j + 2])
            for j in range(0, len(starts), 2)]
    return gens, results


def _direction_states(starts, results, st_ref, o_r):
    c = SCAN_CHUNK
    st = st_ref[...]
    for s, result in zip(starts, results):
        while not result:
            yield
        o_intra, kv, q_read, d_end = result
        if o_r is not None:
            o_r[pl.ds(s, c), :] = o_intra + _dot_nt(q_read, st.astype(jnp.bfloat16))
        st = st * d_end + kv
        yield
    st_ref[...] = st


def _delayed(generator, rounds):
    for _ in range(rounds):
        yield
    yield from generator


def _scan_kernel(q_ref, zf_ref, zb_ref, v_ref, ga_ref, qc_ref, zfc_ref, zbc_ref, vc_ref,
                 lbl_ref, gn_ref, wf_ref, wb_ref, mf_ref, mb_ref, o_ref,
                 stf_ref, stb_ref, of_ref, ob_ref):
    c = SCAN_CHUNK
    t_lat = q_ref.shape[1]
    t_ctx = qc_ref.shape[1]
    l0, l1 = lbl_ref[0], lbl_ref[1]
    mx = jnp.maximum(l0, l1)
    e0, e1 = jnp.exp(l0 - mx), jnp.exp(l1 - mx)
    lb = e0 / (e0 + e1)
    lb_f, lb_b = lb[0:1], lb[1:2]

    stf_ref[...] = jnp.zeros_like(stf_ref)
    stb_ref[...] = jnp.zeros_like(stb_ref)

    def both_directions(i, t, group, q_r, zf_r, zb_r, v_r, of_r, ob_r):
        span = group * c
        lo = pl.multiple_of(i * span, span)
        hi = pl.multiple_of(t - span - i * span, span)
        starts_f = [pl.multiple_of(lo + j * c, c) for j in range(group)]
        starts_b = [pl.multiple_of(hi + j * c, c) for j in reversed(range(group))]
        gens_f, res_f = _direction_chunks(q_r, zf_r, v_r, starts_f, lb_f, wf_ref, mf_ref, False)
        gens_b, res_b = _direction_chunks(q_r, zb_r, v_r, starts_b, lb_b, wb_ref, mb_ref, True)
        staggered = [_delayed(g, SCAN_STAGGER * j) for j, pair in enumerate(zip(gens_f, gens_b)) for g in pair]
        _interleave(*staggered,
                    _direction_states(starts_f, res_f, stf_ref, of_r),
                    _direction_states(starts_b, res_b, stb_ref, ob_r))

    ctx_group = min(SCAN_GROUP, t_ctx // c)

    def ctx_body(i, carry):
        both_directions(i, t_ctx, ctx_group, qc_ref, zfc_ref, zbc_ref, vc_ref, None, None)
        return carry

    lax.fori_loop(0, t_ctx // (ctx_group * c), ctx_body, 0)

    def lat_body(i, carry):
        both_directions(i, t_lat, SCAN_GROUP, q_ref, zf_ref, zb_ref, v_ref, of_ref, ob_ref)
        return carry

    lax.fori_loop(0, t_lat // (SCAN_GROUP * c), lat_body, 0)

    rows = 256
    gn = gn_ref[...]

    def fin_body(i, carry):
        lo = pl.multiple_of(i * rows, rows)
        o = of_ref[pl.ds(lo, rows), :] + ob_ref[pl.ds(lo, rows), :]
        o = o * lax.rsqrt(jnp.mean(o * o, axis=-1, keepdims=True) + EPS) * gn
        gate = _silu(ga_ref[0, pl.ds(lo, rows), :].astype(jnp.float32))
        o_ref[0, pl.ds(lo, rows), :] = (o * gate).astype(o_ref.dtype)
        return carry

    lax.fori_loop(0, t_lat // rows, fin_body, 0, unroll=2)


def _hgrn_scan(p_lat, aux_lat, p_ctx, lb_logits, hgrn_norm_g):
    bsz, t_lat, _ = p_lat.shape
    t_ctx = p_ctx.shape[1]
    w_f3, w_b3, m_f, m_b = _scan_constants()
    wf = jnp.asarray(w_f3, jnp.bfloat16)
    wb = jnp.asarray(w_b3, jnp.bfloat16)
    mf = jnp.asarray(m_f, jnp.float32)
    mb = jnp.asarray(m_b, jnp.float32)

    def col(t, base):
        return pl.BlockSpec((1, t, LANES), lambda b, h, base=base: (b, 0, base + h))

    def whole(a):
        return pl.BlockSpec(a.shape, lambda b, h, nd=a.ndim: (0,) * nd)

    return pl.pallas_call(
        _scan_kernel,
        grid=(bsz, HGRN_HEADS),
        in_specs=[col(t_lat, COL_Q), col(t_lat, COL_ZF), col(t_lat, COL_ZB), col(t_lat, COL_V),
                  col(t_lat, COL_GA),
                  col(t_ctx, COL_Q), col(t_ctx, COL_ZF), col(t_ctx, COL_ZB), col(t_ctx, COL_V),
                  pl.BlockSpec((2, 2, LANES), lambda b, h: (0, 0, h)),
                  pl.BlockSpec((1, LANES), lambda b, h: (0, h)),
                  whole(wf), whole(wb), whole(mf), whole(mb)],
        out_specs=pl.BlockSpec((1, t_lat, LANES), lambda b, h: (b, 0, h)),
        out_shape=jax.ShapeDtypeStruct((bsz, t_lat, D_HGRN), jnp.bfloat16),
        scratch_shapes=[pltpu.VMEM((HEAD_DIM, HEAD_DIM), jnp.float32),
                        pltpu.VMEM((HEAD_DIM, HEAD_DIM), jnp.float32),
                        pltpu.VMEM((t_lat, HEAD_DIM), jnp.float32),
                        pltpu.VMEM((t_lat, HEAD_DIM), jnp.float32)],
        compiler_params=pltpu.CompilerParams(dimension_semantics=("arbitrary", "arbitrary"),
                                             vmem_limit_bytes=VMEM_LIMIT),
        name="hgrn_scan",
    )(p_lat, p_lat, p_lat, p_lat, aux_lat, p_ctx, p_ctx, p_ctx, p_ctx,
      lb_logits, hgrn_norm_g, wf, wb, mf, mb)


CONV_PADW = GRID_W + 32


def _conv_fill(u_ref, ug_ref, pad_ref, along_rows):
    t = u_ref.shape[1]
    n_rows = t // GRID_W

    @pl.when(along_rows)
    def _along_rows():
        padw = CONV_PADW

        def fill(r, carry):
            src = pl.multiple_of(r * GRID_W, GRID_W)
            dst = pl.multiple_of(r * padw, 32)
            glu = (u_ref[0, pl.ds(src, GRID_W), :].astype(jnp.float32)
                   * jax.nn.sigmoid(ug_ref[0, pl.ds(src, GRID_W), :].astype(jnp.float32)))
            pad_ref[pl.ds(dst, 16), :] = jnp.zeros((16, LANES), jnp.float32)
            pad_ref[pl.ds(dst + 16, GRID_W), :] = glu
            pad_ref[pl.ds(dst + 16 + GRID_W, 16), :] = jnp.zeros((16, LANES), jnp.float32)
            return carry

        lax.fori_loop(0, n_rows, fill, 0, unroll=4)

    @pl.when(jnp.logical_not(along_rows))
    def _along_cols():
        halo = CONV_HALF * GRID_W
        pad_ref[pl.ds(0, halo), :] = jnp.zeros((halo, LANES), jnp.float32)
        pad_ref[pl.ds(halo + t, halo), :] = jnp.zeros((halo, LANES), jnp.float32)

        def fill(r, carry):
            src = pl.multiple_of(r * GRID_W, GRID_W)
            glu = (u_ref[0, pl.ds(src, GRID_W), :].astype(jnp.float32)
                   * jax.nn.sigmoid(ug_ref[0, pl.ds(src, GRID_W), :].astype(jnp.float32)))
            pad_ref[pl.ds(halo + src, GRID_W), :] = glu
            return carry

        lax.fori_loop(0, n_rows, fill, 0, unroll=4)


def _conv_kernel(u_ref, ug_ref, w_ref, b_ref, o_ref, pad_ref):
    n_rows = u_ref.shape[1] // GRID_W
    along_rows = pl.program_id(1) < (D_CONV // 2) // LANES
    _conv_fill(u_ref, ug_ref, pad_ref, along_rows)
    bias = b_ref[...]

    def taps(first_tap, stride):
        def conv(r, carry):
            dst = pl.multiple_of(r * GRID_W, GRID_W)
            base = first_tap(r)
            acc = jnp.zeros((GRID_W, LANES), jnp.float32)
            for k in range(CONV_WIDTH):
                acc = acc + w_ref[k:k + 1, :] * pad_ref[pl.ds(base + k * stride, GRID_W), :]
            o_ref[0, pl.ds(dst, GRID_W), :] = (acc + bias).astype(o_ref.dtype)
            return carry

        lax.fori_loop(0, n_rows, conv, 0, unroll=4)

    @pl.when(along_rows)
    def _():
        taps(lambda r: r * CONV_PADW + (16 - CONV_HALF), 1)

    @pl.when(jnp.logical_not(along_rows))
    def _():
        taps(lambda r: pl.multiple_of(r * GRID_W, GRID_W), GRID_W)


def _axial_conv(aux_lat, conv_w, conv_b):
    bsz, t, _ = aux_lat.shape
    n_rows = t // GRID_W
    pad_rows = max(n_rows * CONV_PADW, t + 2 * CONV_HALF * GRID_W)
    w_pad = jnp.zeros((32, D_CONV), jnp.float32).at[:CONV_WIDTH].set(conv_w)
    return pl.pallas_call(
        _conv_kernel,
        grid=(bsz, D_CONV // LANES),
        in_specs=[pl.BlockSpec((1, t, LANES), lambda b, g: (b, 0, COL_U + g)),
                  pl.BlockSpec((1, t, LANES), lambda b, g: (b, 0, COL_UG + g)),
                  pl.BlockSpec((32, LANES), lambda b, g: (0, g)),
                  pl.BlockSpec((1, LANES), lambda b, g: (0, g))],
        out_specs=pl.BlockSpec((1, t, LANES), lambda b, g: (b, 0, g)),
        out_shape=jax.ShapeDtypeStruct((bsz, t, D_CONV), jnp.bfloat16),
        scratch_shapes=[pltpu.VMEM((pad_rows, LANES), jnp.float32)],
        compiler_params=pltpu.CompilerParams(dimension_semantics=("arbitrary", "arbitrary"),
                                             vmem_limit_bytes=VMEM_LIMIT),
        name="axial_conv",
    )(aux_lat, aux_lat, w_pad, conv_b)


def _out_kernel(x_ref, ba_ref, y_ref, gb_ref, gt_ref, lng_ref, lnb_ref, wa_ref, wb_ref, fg_ref, o_ref):
    y = y_ref[0].astype(jnp.float32)
    mu = jnp.mean(y, axis=-1, keepdims=True)
    yc = y - mu
    var = jnp.mean(yc * yc, axis=-1, keepdims=True)
    yn = yc * lax.rsqrt(var + EPS) * lng_ref[...] + lnb_ref[...]
    branch_b = _silu(yn) * _silu(gb_ref[0].astype(jnp.float32))
    mix = jnp.dot(ba_ref[0], wa_ref[...], preferred_element_type=jnp.float32)
    mix = mix + jnp.dot(branch_b.astype(jnp.bfloat16), wb_ref[...], preferred_element_type=jnp.float32)
    h = x_ref[0] + gt_ref[0] * mix
    o_ref[0] = h * lax.rsqrt(jnp.mean(h * h, axis=-1, keepdims=True) + EPS) * fg_ref[...]


def _output(x, branch_a, y_conv, aux_lat, gate, ln_g, ln_b, w_out_bf16, final_g):
    bsz, t, _ = x.shape
    rows = OUT_ROWS
    w_a, w_b = w_out_bf16[:D_HGRN], w_out_bf16[D_HGRN:]
    gb_block = COL_GB * LANES // D_CONV
    return pl.pallas_call(
        _out_kernel,
        grid=(bsz, t // rows),
        in_specs=[pl.BlockSpec((1, rows, D_MODEL), lambda b, i: (b, i, 0)),
                  pl.BlockSpec((1, rows, D_HGRN), lambda b, i: (b, i, 0)),
                  pl.BlockSpec((1, rows, D_CONV), lambda b, i: (b, i, 0)),
                  pl.BlockSpec((1, rows, D_CONV), lambda b, i: (b, i, gb_block)),
                  pl.BlockSpec((1, 1, D_MODEL), lambda b, i: (b, 0, 0)),
                  pl.BlockSpec((1, D_CONV), lambda b, i: (0, 0)),
                  pl.BlockSpec((1, D_CONV), lambda b, i: (0, 0)),
                  pl.BlockSpec((D_HGRN, D_MODEL), lambda b, i: (0, 0)),
                  pl.BlockSpec((D_CONV, D_MODEL), lambda b, i: (0, 0)),
                  pl.BlockSpec((1, D_MODEL), lambda b, i: (0, 0))],
        out_specs=pl.BlockSpec((1, rows, D_MODEL), lambda b, i: (b, i, 0)),
        out_shape=jax.ShapeDtypeStruct((bsz, t, D_MODEL), jnp.float32),
        compiler_params=pltpu.CompilerParams(dimension_semantics=("arbitrary", "arbitrary"),
                                             vmem_limit_bytes=VMEM_LIMIT),
        name="out_projection",
    )(x, branch_a, y_conv, aux_lat, gate, ln_g, ln_b, w_a, w_b, final_g)


def kernel(x, c, ctx, c_ctx, norm_g, w_mod, b_mod, w_in, lb_logits, hgrn_norm_g, conv_w, conv_b,
           conv_ln_g, conv_ln_b, w_out, final_norm_g):
    bsz, seq_len, _ = x.shape
    span = SCAN_GROUP * SCAN_CHUNK
    assert norm_g.shape[0] == 1, "single-layer block"
    assert seq_len % GRID_W == 0 and seq_len % span == 0
    assert ctx.shape[1] % (2 * SCAN_CHUNK) == 0 and (ctx.shape[1] % span == 0 or ctx.shape[1] < span)

    pad = (-(bsz + 1)) % SUBLANES
    cc = jnp.concatenate([c, c_ctx[None, :], jnp.zeros((pad, D_MODEL), c.dtype)], axis=0)
    mod = _modulation(cc, w_mod[0], b_mod)
    shift, scale, gate = (mod[:, i * D_MODEL:(i + 1) * D_MODEL] for i in range(3))
    shift_lat, scale_lat, gate_lat = (m[:bsz, None, :] for m in (shift, scale, gate))
    shift_ctx, scale_ctx = (jnp.broadcast_to(m[bsz][None, None, :], (bsz, 1, D_MODEL)) for m in (shift, scale))

    w_in_bf16 = w_in[0].astype(jnp.bfloat16)
    w_scan, w_aux = w_in_bf16[:, :D_SCAN_IN], w_in_bf16[:, D_SCAN_IN:]
    p_lat, aux_lat = _projection(x, norm_g, shift_lat, scale_lat, (w_scan, w_aux), (jnp.float32, jnp.bfloat16))
    p_ctx, = _projection(ctx, norm_g, shift_ctx, scale_ctx, (w_scan,), (jnp.float32,))

    branch_a = _hgrn_scan(p_lat, aux_lat, p_ctx, lb_logits, hgrn_norm_g)
    y_conv = _axial_conv(aux_lat, conv_w[0], conv_b)
    return _output(x, branch_a, y_conv, aux_lat, gate_lat, conv_ln_g, conv_ln_b,
                   w_out[0].astype(jnp.bfloat16), final_norm_g[None, :])
```

```python
import numpy as np
import jax
import jax.numpy as jnp
from jax import lax
from jax.experimental import pallas as pl
from jax.experimental.pallas import tpu as pltpu

D_MODEL = 1024
GRID_W = 64
D_HGRN = 512
HGRN_HEADS = 4
HEAD_DIM = D_HGRN // HGRN_HEADS
D_CONV = 512
CONV_WIDTH = 31
CONV_HALF = CONV_WIDTH // 2
EPS = 1e-6

LANES = 128
SUBLANES = 8
SCAN_CHUNK = 64
SCAN_LEVELS = SCAN_CHUNK.bit_length() - 1
SCAN_GROUP = 8
SCAN_OUT_ROWS = 256
SCAN_STAGGER = 1
PROJ_ROWS = 512
OUT_ROWS = 1024
VMEM_LIMIT = 56 * 1024 * 1024

D_SCAN_IN = 4 * D_HGRN
COL_Q, COL_ZF, COL_ZB, COL_V = (i * HGRN_HEADS for i in range(4))
COL_GA, COL_U, COL_UG, COL_GB = (i * (D_HGRN // LANES) for i in range(4))


def _silu(x):
    return x * jax.nn.sigmoid(x)


def _mod_kernel(c_ref, w_ref, b_ref, o_ref):
    a = _silu(c_ref[...])
    o_ref[...] = jnp.dot(a, w_ref[...], preferred_element_type=jnp.float32,
                         precision=lax.Precision.HIGHEST) + b_ref[...]


def _modulation(cc, w_mod, b_mod):
    rows = cc.shape[0]
    n = w_mod.shape[1]
    return pl.pallas_call(
        _mod_kernel,
        grid=(n // D_MODEL,),
        in_specs=[pl.BlockSpec((rows, D_MODEL), lambda j: (0, 0)),
                  pl.BlockSpec((D_MODEL, D_MODEL), lambda j: (0, j)),
                  pl.BlockSpec((1, D_MODEL), lambda j: (0, j))],
        out_specs=pl.BlockSpec((rows, D_MODEL), lambda j: (0, j)),
        out_shape=jax.ShapeDtypeStruct((rows, n), jnp.float32),
        compiler_params=pltpu.CompilerParams(dimension_semantics=("arbitrary",),
                                             vmem_limit_bytes=VMEM_LIMIT),
        name="modulation",
    )(cc, w_mod, b_mod)


def _proj_kernel(x_ref, g_ref, sh_ref, sc_ref, *refs):
    x = x_ref[0]
    y = x * lax.rsqrt(jnp.mean(x * x, axis=-1, keepdims=True) + EPS) * g_ref[...]
    a = (y * (1.0 + sc_ref[0]) + sh_ref[0]).astype(jnp.bfloat16)
    n = len(refs) // 2
    for w_ref, o_ref in zip(refs[:n], refs[n:]):
        o_ref[0] = jnp.dot(a, w_ref[...], preferred_element_type=jnp.float32).astype(o_ref.dtype)


def _projection(x, norm_g, shift, scale, weights, dtypes):
    bsz, t, _ = x.shape
    rows = min(PROJ_ROWS, t)
    return pl.pallas_call(
        _proj_kernel,
        grid=(bsz, t // rows),
        in_specs=[pl.BlockSpec((1, rows, D_MODEL), lambda b, i: (b, i, 0)),
                  pl.BlockSpec((1, D_MODEL), lambda b, i: (0, 0)),
                  pl.BlockSpec((1, 1, D_MODEL), lambda b, i: (b, 0, 0)),
                  pl.BlockSpec((1, 1, D_MODEL), lambda b, i: (b, 0, 0))]
        + [pl.BlockSpec(w.shape, lambda b, i: (0, 0)) for w in weights],
        out_specs=[pl.BlockSpec((1, rows, w.shape[1]), lambda b, i: (b, i, 0)) for w in weights],
        out_shape=[jax.ShapeDtypeStruct((bsz, t, w.shape[1]), dt) for w, dt in zip(weights, dtypes)],
        compiler_params=pltpu.CompilerParams(dimension_semantics=("arbitrary", "arbitrary"),
                                             vmem_limit_bytes=VMEM_LIMIT),
        name="in_projection",
    )(x, norm_g, shift, scale, *weights)


def _scan_constants():
    c = SCAN_CHUNK
    idx = np.arange(c)
    t, s = idx[:, None], idx[None, :]
    masks = [t == s]
    for lvl in range(SCAN_LEVELS):
        h = c >> (lvl + 1)
        masks.append(((t // (2 * h)) == (s // (2 * h))) & ((t % (2 * h)) >= h) & ((s % (2 * h)) < h))
    m_f = np.stack([m.astype(np.float32) for m in masks])
    m_b = m_f[:, ::-1, ::-1].copy()
    tri_f = (s <= t).astype(np.float32)
    tri_b = (s >= t).astype(np.float32)
    return np.concatenate([tri_f] * 3, axis=1), np.concatenate([tri_b] * 3, axis=1), m_f, m_b


def _dot_nt(a, b):
    return lax.dot_general(a, b, (((1,), (1,)), ((), ())), preferred_element_type=jnp.float32)


def _dot_tn(a, b):
    return lax.dot_general(a, b, (((0,), (0,)), ((), ())), preferred_element_type=jnp.float32)


def _interleave(*generators):
    pending = list(generators)
    while pending:
        for gen in list(pending):
            try:
                next(gen)
            except StopIteration:
                pending.remove(gen)


def _mix_rows(q, k, half, reverse):
    parts = []
    for lo in range(0, SCAN_CHUNK, 2 * half):
        first, second = (q, k) if reverse else (k, q)
        parts += [first[lo:lo + half], second[lo + half:lo + 2 * half]]
    return jnp.concatenate(parts, axis=0)


def _level_decay(g_cum, f, half, reverse):
    c = SCAN_CHUNK
    if half >= SUBLANES:
        parts = []
        for lo in range(0, c, 2 * half):
            mid = lo + half
            if reverse:
                parts += [g_cum[lo:mid] - g_cum[mid:mid + 1], g_cum[mid:mid + 1] - g_cum[mid:mid + half]]
            else:
                parts += [g_cum[mid - 1:mid] - g_cum[lo:mid], g_cum[mid:mid + half] - g_cum[mid - 1:mid]]
        return jnp.exp2(jnp.concatenate(parts, axis=0))
    if half == SUBLANES // 2:
        g3 = g_cum.reshape(c // SUBLANES, SUBLANES, LANES)
        r = half if reverse else half - 1
        later = lax.broadcasted_iota(jnp.int32, g3.shape, 1) >= half
        sign = jnp.where(later != reverse, 1.0, -1.0)
        return jnp.exp2((g3 - g3[:, r:r + 1, :]) * sign).reshape(c, LANES)
    f3 = f.reshape(c // SUBLANES, SUBLANES, LANES)
    row = lax.broadcasted_iota(jnp.int32, f3.shape, 1)
    if half == 1:
        on_query_side = (row % 2 == 0) if reverse else (row % 2 == 1)
        return jnp.where(on_query_side, f3, 1.0).reshape(c, LANES)
    prev = pltpu.roll(f3, 1, 1)
    nxt = pltpu.roll(f3, SUBLANES - 1, 1)
    m4 = row % 4
    if reverse:
        d = jnp.where(m4 == 0, f3 * nxt, jnp.where(m4 == 1, f3, jnp.where(m4 == 2, 1.0, prev)))
    else:
        d = jnp.where(m4 == 0, nxt, jnp.where(m4 == 1, 1.0, jnp.where(m4 == 2, f3, f3 * prev)))
    return d.reshape(c, LANES)


def _chunk_local(q, k, v, f, g_cum, m_ref, reverse, result):
    c = SCAN_CHUNK
    bf = jnp.bfloat16
    end_row = 0 if reverse else c - 1

    kb = k.astype(bf)
    a = m_ref[0] * _dot_nt(q.astype(bf), kb)
    yield
    for lvl in range(SCAN_LEVELS):
        half = c >> (lvl + 1)
        d = _level_decay(g_cum, f, half, reverse)
        if half >= SUBLANES:
            x = (_mix_rows(q, k, half, reverse) * d).astype(bf)
            p = _dot_nt(x, x)
        elif half == 1:
            p = _dot_nt((q * d).astype(bf), kb)
        else:
            p = _dot_nt((q * d).astype(bf), (k * d).astype(bf))
        a = a + m_ref[lvl + 1] * p
        yield
    d_read = jnp.exp2(g_cum)
    d_state = jnp.exp2(g_cum[end_row:end_row + 1] - g_cum)
    vb = v.astype(bf)
    o_intra = jnp.dot(a.astype(bf), vb, preferred_element_type=jnp.float32)
    yield
    kv = _dot_tn(vb, (k * d_state).astype(bf))
    result.extend([o_intra, kv, (q * d_read).astype(bf), d_read[end_row:end_row + 1]])
    yield


def _chunk_pair(q_r, z_r, v_r, starts, lb, w_ref, m_ref, reverse, results):
    c = SCAN_CHUNK
    bf = jnp.bfloat16
    fs, g3s = [], []
    for s in starts:
        f = lb + (1.0 - lb) * jax.nn.sigmoid(z_r[0, pl.ds(s, c), :])
        g = jnp.log2(f)
        fs.append(f)
        g_hi = g.astype(bf)
        r1 = g - g_hi.astype(jnp.float32)
        g_mid = r1.astype(bf)
        g_lo = (r1 - g_mid.astype(jnp.float32)).astype(bf)
        g3s.append(jnp.concatenate([g_hi, g_mid, g_lo], axis=0))
        yield
    g_cum = jnp.dot(w_ref[...], jnp.concatenate(g3s, axis=1), preferred_element_type=jnp.float32)
    yield
    chains = [_chunk_local(q_r[0, pl.ds(s, c), :], 1.0 - fs[j], v_r[0, pl.ds(s, c), :], fs[j],
                           g_cum[:, j * LANES:(j + 1) * LANES], m_ref, reverse, results[j])
              for j, s in enumerate(starts)]
    while chains:
        for chain in list(chains):
            try:
                next(chain)
            except StopIteration:
                chains.remove(chain)
        yield


def _direction_chunks(q_r, z_r, v_r, starts, lb, w_ref, m_ref, reverse):
    results = [[] for _ in starts]
    gens = [_chunk_pair(q_r, z_r, v_r, starts[j:j + 2], lb, w_ref, m_ref, reverse, results[j:j + 2])
            for j in range(0, len(starts), 2)]
    return gens, results


def _direction_states(starts, results, st_ref, o_r):
    c = SCAN_CHUNK
    st = st_ref[...]
    for s, result in zip(starts, results):
        while not result:
            yield
        o_intra, kv, q_read, d_end = result
        if o_r is not None:
            o_r[pl.ds(s, c), :] = o_intra + _dot_nt(q_read, st.astype(jnp.bfloat16))
        st = st * d_end + kv
        yield
    st_ref[...] = st


def _delayed(generator, rounds):
    for _ in range(rounds):
        yield
    yield from generator


def _scan_kernel(q_ref, zf_ref, zb_ref, v_ref, ga_ref, qc_ref, zfc_ref, zbc_ref, vc_ref,
                 lbl_ref, gn_ref, wf_ref, wb_ref, mf_ref, mb_ref, o_ref,
                 stf_ref, stb_ref, of_ref, ob_ref):
    c = SCAN_CHUNK
    t_lat = q_ref.shape[1]
    t_ctx = qc_ref.shape[1]
    l0, l1 = lbl_ref[0], lbl_ref[1]
    mx = jnp.maximum(l0, l1)
    e0, e1 = jnp.exp(l0 - mx), jnp.exp(l1 - mx)
    lb = e0 / (e0 + e1)
    lb_f, lb_b = lb[0:1], lb[1:2]

    stf_ref[...] = jnp.zeros_like(stf_ref)
    stb_ref[...] = jnp.zeros_like(stb_ref)

    def both_directions(i, t, group, q_r, zf_r, zb_r, v_r, of_r, ob_r):
        span = group * c
        lo = pl.multiple_of(i * span, span)
        hi = pl.multiple_of(t - span - i * span, span)
        starts_f = [pl.multiple_of(lo + j * c, c) for j in range(group)]
        starts_b = [pl.multiple_of(hi + j * c, c) for j in reversed(range(group))]
        gens_f, res_f = _direction_chunks(q_r, zf_r, v_r, starts_f, lb_f, wf_ref, mf_ref, False)
        gens_b, res_b = _direction_chunks(q_r, zb_r, v_r, starts_b, lb_b, wb_ref, mb_ref, True)
        staggered = [_delayed(g, SCAN_STAGGER * j) for j, pair in enumerate(zip(gens_f, gens_b)) for g in pair]
        _interleave(*staggered,
                    _direction_states(starts_f, res_f, stf_ref, of_r),
                    _direction_states(starts_b, res_b, stb_ref, ob_r))

    ctx_group = min(SCAN_GROUP, t_ctx // c)

    def ctx_body(i, carry):
        both_directions(i, t_ctx, ctx_group, qc_ref, zfc_ref, zbc_ref, vc_ref, None, None)
        return carry

    lax.fori_loop(0, t_ctx // (ctx_group * c), ctx_body, 0)

    def lat_body(i, carry):
        both_directions(i, t_lat, SCAN_GROUP, q_ref, zf_ref, zb_ref, v_ref, of_ref, ob_ref)
        return carry

    lax.fori_loop(0, t_lat // (SCAN_GROUP * c), lat_body, 0)

    rows = SCAN_OUT_ROWS
    gn = gn_ref[...]

    def fin_body(i, carry):
        lo = pl.multiple_of(i * rows, rows)
        o = of_ref[pl.ds(lo, rows), :] + ob_ref[pl.ds(lo, rows), :]
        o = o * lax.rsqrt(jnp.mean(o * o, axis=-1, keepdims=True) + EPS) * gn
        gate = _silu(ga_ref[0, pl.ds(lo, rows), :].astype(jnp.float32))
        o_ref[0, pl.ds(lo, rows), :] = (o * gate).astype(o_ref.dtype)
        return carry

    lax.fori_loop(0, t_lat // rows, fin_body, 0, unroll=4)


def _hgrn_scan(p_lat, aux_lat, p_ctx, lb_logits, hgrn_norm_g):
    bsz, t_lat, _ = p_lat.shape
    t_ctx = p_ctx.shape[1]
    w_f3, w_b3, m_f, m_b = _scan_constants()
    wf = jnp.asarray(w_f3, jnp.bfloat16)
    wb = jnp.asarray(w_b3, jnp.bfloat16)
    mf = jnp.asarray(m_f, jnp.float32)
    mb = jnp.asarray(m_b, jnp.float32)

    def col(t, base):
        return pl.BlockSpec((1, t, LANES), lambda b, h, base=base: (b, 0, base + h))

    def whole(a):
        return pl.BlockSpec(a.shape, lambda b, h, nd=a.ndim: (0,) * nd)

    return pl.pallas_call(
        _scan_kernel,
        grid=(bsz, HGRN_HEADS),
        in_specs=[col(t_lat, COL_Q), col(t_lat, COL_ZF), col(t_lat, COL_ZB), col(t_lat, COL_V),
                  col(t_lat, COL_GA),
                  col(t_ctx, COL_Q), col(t_ctx, COL_ZF), col(t_ctx, COL_ZB), col(t_ctx, COL_V),
                  pl.BlockSpec((2, 2, LANES), lambda b, h: (0, 0, h)),
                  pl.BlockSpec((1, LANES), lambda b, h: (0, h)),
                  whole(wf), whole(wb), whole(mf), whole(mb)],
        out_specs=pl.BlockSpec((1, t_lat, LANES), lambda b, h: (b, 0, h)),
        out_shape=jax.ShapeDtypeStruct((bsz, t_lat, D_HGRN), jnp.bfloat16),
        scratch_shapes=[pltpu.VMEM((HEAD_DIM, HEAD_DIM), jnp.float32),
                        pltpu.VMEM((HEAD_DIM, HEAD_DIM), jnp.float32),
                        pltpu.VMEM((t_lat, HEAD_DIM), jnp.float32),
                        pltpu.VMEM((t_lat, HEAD_DIM), jnp.float32)],
        compiler_params=pltpu.CompilerParams(dimension_semantics=("arbitrary", "arbitrary"),
                                             vmem_limit_bytes=VMEM_LIMIT),
        name="hgrn_scan",
    )(p_lat, p_lat, p_lat, p_lat, aux_lat, p_ctx, p_ctx, p_ctx, p_ctx,
      lb_logits, hgrn_norm_g, wf, wb, mf, mb)


CONV_PADW = GRID_W + 32


def _conv_fill(u_ref, ug_ref, pad_ref, along_rows):
    t = u_ref.shape[1]
    n_rows = t // GRID_W

    @pl.when(along_rows)
    def _along_rows():
        padw = CONV_PADW

        def fill(r, carry):
            src = pl.multiple_of(r * GRID_W, GRID_W)
            dst = pl.multiple_of(r * padw, 32)
            glu = (u_ref[0, pl.ds(src, GRID_W), :].astype(jnp.float32)
                   * jax.nn.sigmoid(ug_ref[0, pl.ds(src, GRID_W), :].astype(jnp.float32)))
            pad_ref[pl.ds(dst, 16), :] = jnp.zeros((16, LANES), jnp.float32)
            pad_ref[pl.ds(dst + 16, GRID_W), :] = glu
            pad_ref[pl.ds(dst + 16 + GRID_W, 16), :] = jnp.zeros((16, LANES), jnp.float32)
            return carry

        lax.fori_loop(0, n_rows, fill, 0, unroll=4)

    @pl.when(jnp.logical_not(along_rows))
    def _along_cols():
        halo = CONV_HALF * GRID_W
        pad_ref[pl.ds(0, halo), :] = jnp.zeros((halo, LANES), jnp.float32)
        pad_ref[pl.ds(halo + t, halo), :] = jnp.zeros((halo, LANES), jnp.float32)

        def fill(r, carry):
            src = pl.multiple_of(r * GRID_W, GRID_W)
            glu = (u_ref[0, pl.ds(src, GRID_W), :].astype(jnp.float32)
                   * jax.nn.sigmoid(ug_ref[0, pl.ds(src, GRID_W), :].astype(jnp.float32)))
            pad_ref[pl.ds(halo + src, GRID_W), :] = glu
            return carry

        lax.fori_loop(0, n_rows, fill, 0, unroll=4)


def _conv_kernel(u_ref, ug_ref, w_ref, b_ref, o_ref, pad_ref):
    n_rows = u_ref.shape[1] // GRID_W
    along_rows = pl.program_id(1) < (D_CONV // 2) // LANES
    _conv_fill(u_ref, ug_ref, pad_ref, along_rows)
    bias = b_ref[...]

    def taps(first_tap, stride):
        def conv(r, carry):
            dst = pl.multiple_of(r * GRID_W, GRID_W)
            base = first_tap(r)
            acc = jnp.zeros((GRID_W, LANES), jnp.float32)
            for k in range(CONV_WIDTH):
                acc = acc + w_ref[k:k + 1, :] * pad_ref[pl.ds(base + k * stride, GRID_W), :]
            o_ref[0, pl.ds(dst, GRID_W), :] = (acc + bias).astype(o_ref.dtype)
            return carry

        lax.fori_loop(0, n_rows, conv, 0, unroll=8)

    @pl.when(along_rows)
    def _():
        taps(lambda r: r * CONV_PADW + (16 - CONV_HALF), 1)

    @pl.when(jnp.logical_not(along_rows))
    def _():
        taps(lambda r: pl.multiple_of(r * GRID_W, GRID_W), GRID_W)


def _axial_conv(aux_lat, conv_w, conv_b):
    bsz, t, _ = aux_lat.shape
    n_rows = t // GRID_W
    pad_rows = max(n_rows * CONV_PADW, t + 2 * CONV_HALF * GRID_W)
    w_pad = jnp.zeros((32, D_CONV), jnp.float32).at[:CONV_WIDTH].set(conv_w)
    return pl.pallas_call(
        _conv_kernel,
        grid=(bsz, D_CONV // LANES),
        in_specs=[pl.BlockSpec((1, t, LANES), lambda b, g: (b, 0, COL_U + g)),
                  pl.BlockSpec((1, t, LANES), lambda b, g: (b, 0, COL_UG + g)),
                  pl.BlockSpec((32, LANES), lambda b, g: (0, g)),
                  pl.BlockSpec((1, LANES), lambda b, g: (0, g))],
        out_specs=pl.BlockSpec((1, t, LANES), lambda b, g: (b, 0, g)),
        out_shape=jax.ShapeDtypeStruct((bsz, t, D_CONV), jnp.bfloat16),
        scratch_shapes=[pltpu.VMEM((pad_rows, LANES), jnp.float32)],
        compiler_params=pltpu.CompilerParams(dimension_semantics=("arbitrary", "arbitrary"),
                                             vmem_limit_bytes=VMEM_LIMIT),
        name="axial_conv",
    )(aux_lat, aux_lat, w_pad, conv_b)


def _out_kernel(x_ref, ba_ref, y_ref, gb_ref, gt_ref, lng_ref, lnb_ref, wa_ref, wb_ref, fg_ref, o_ref):
    y = y_ref[0].astype(jnp.float32)
    mu = jnp.mean(y, axis=-1, keepdims=True)
    yc = y - mu
    var = jnp.mean(yc * yc, axis=-1, keepdims=True)
    yn = yc * lax.rsqrt(var + EPS) * lng_ref[...] + lnb_ref[...]
    branch_b = _silu(yn) * _silu(gb_ref[0].astype(jnp.float32))
    mix = jnp.dot(ba_ref[0], wa_ref[...], preferred_element_type=jnp.float32)
    mix = mix + jnp.dot(branch_b.astype(jnp.bfloat16), wb_ref[...], preferred_element_type=jnp.float32)
    h = x_ref[0] + gt_ref[0] * mix
    o_ref[0] = h * lax.rsqrt(jnp.mean(h * h, axis=-1, keepdims=True) + EPS) * fg_ref[...]


def _output(x, branch_a, y_conv, aux_lat, gate, ln_g, ln_b, w_out_bf16, final_g):
    bsz, t, _ = x.shape
    rows = OUT_ROWS
    w_a, w_b = w_out_bf16[:D_HGRN], w_out_bf16[D_HGRN:]
    gb_block = COL_GB * LANES // D_CONV
    return pl.pallas_call(
        _out_kernel,
        grid=(bsz, t // rows),
        in_specs=[pl.BlockSpec((1, rows, D_MODEL), lambda b, i: (b, i, 0)),
                  pl.BlockSpec((1, rows, D_HGRN), lambda b, i: (b, i, 0)),
                  pl.BlockSpec((1, rows, D_CONV), lambda b, i: (b, i, 0)),
                  pl.BlockSpec((1, rows, D_CONV), lambda b, i: (b, i, gb_block)),
                  pl.BlockSpec((1, 1, D_MODEL), lambda b, i: (b, 0, 0)),
                  pl.BlockSpec((1, D_CONV), lambda b, i: (0, 0)),
                  pl.BlockSpec((1, D_CONV), lambda b, i: (0, 0)),
                  pl.BlockSpec((D_HGRN, D_MODEL), lambda b, i: (0, 0)),
                  pl.BlockSpec((D_CONV, D_MODEL), lambda b, i: (0, 0)),
                  pl.BlockSpec((1, D_MODEL), lambda b, i: (0, 0))],
        out_specs=pl.BlockSpec((1, rows, D_MODEL), lambda b, i: (b, i, 0)),
        out_shape=jax.ShapeDtypeStruct((bsz, t, D_MODEL), jnp.float32),
        compiler_params=pltpu.CompilerParams(dimension_semantics=("arbitrary", "arbitrary"),
                                             vmem_limit_bytes=VMEM_LIMIT),
        name="out_projection",
    )(x, branch_a, y_conv, aux_lat, gate, ln_g, ln_b, w_a, w_b, final_g)


def kernel(x, c, ctx, c_ctx, norm_g, w_mod, b_mod, w_in, lb_logits, hgrn_norm_g, conv_w, conv_b,
           conv_ln_g, conv_ln_b, w_out, final_norm_g):
    bsz, seq_len, _ = x.shape
    span = SCAN_GROUP * SCAN_CHUNK
    assert norm_g.shape[0] == 1, "single-layer block"
    assert seq_len % GRID_W == 0 and seq_len % span == 0
    assert ctx.shape[1] % (2 * SCAN_CHUNK) == 0 and (ctx.shape[1] % span == 0 or ctx.shape[1] < span)

    pad = (-(bsz + 1)) % SUBLANES
    cc = jnp.concatenate([c, c_ctx[None, :], jnp.zeros((pad, D_MODEL), c.dtype)], axis=0)
    mod = _modulation(cc, w_mod[0], b_mod)
    shift, scale, gate = (mod[:, i * D_MODEL:(i + 1) * D_MODEL] for i in range(3))
    shift_lat, scale_lat, gate_lat = (m[:bsz, None, :] for m in (shift, scale, gate))
    shift_ctx, scale_ctx = (jnp.broadcast_to(m[bsz][None, None, :], (bsz, 1, D_MODEL)) for m in (shift, scale))

    w_in_bf16 = w_in[0].astype(jnp.bfloat16)
    w_scan, w_aux = w_in_bf16[:, :D_SCAN_IN], w_in_bf16[:, D_SCAN_IN:]
    p_lat, aux_lat = _projection(x, norm_g, shift_lat, scale_lat, (w_scan, w_aux), (jnp.float32, jnp.bfloat16))
    p_ctx, = _projection(ctx, norm_g, shift_ctx, scale_ctx, (w_scan,), (jnp.float32,))

    branch_a = _hgrn_scan(p_lat, aux_lat, p_ctx, lb_logits, hgrn_norm_g)
    y_conv = _axial_conv(aux_lat, conv_w[0], conv_b)
    return _output(x, branch_a, y_conv, aux_lat, gate_lat, conv_ln_g, conv_ln_b,
                   w_out[0].astype(jnp.bfloat16), final_norm_g[None, :])
```

```python
import numpy as np
import jax
import jax.numpy as jnp
from jax import lax
from jax.experimental import pallas as pl
from jax.experimental.pallas import tpu as pltpu

D_MODEL = 1024
GRID_W = 64
D_HGRN = 512
HGRN_HEADS = 4
HEAD_DIM = D_HGRN // HGRN_HEADS
D_CONV = 512
CONV_WIDTH = 31
CONV_HALF = CONV_WIDTH // 2
EPS = 1e-6

LANES = 128
SUBLANES = 8
SCAN_CHUNK = 64
SCAN_LEVELS = SCAN_CHUNK.bit_length() - 1
SCAN_GROUP = 8
SCAN_OUT_ROWS = 256
SCAN_STAGGER = 1
PROJ_ROWS = 512
OUT_ROWS = 1024
VMEM_LIMIT = 56 * 1024 * 1024

D_SCAN_IN = 4 * D_HGRN
COL_Q, COL_ZF, COL_ZB, COL_V = (i * HGRN_HEADS for i in range(4))
COL_GA, COL_U, COL_UG, COL_GB = (i * (D_HGRN // LANES) for i in range(4))


def _silu(x):
    return x * jax.nn.sigmoid(x)


def _mod_kernel(c_ref, w_ref, b_ref, o_ref):
    a = _silu(c_ref[...])
    o_ref[...] = jnp.dot(a, w_ref[...], preferred_element_type=jnp.float32,
                         precision=lax.Precision.HIGHEST) + b_ref[...]


def _modulation(cc, w_mod, b_mod):
    rows = cc.shape[0]
    n = w_mod.shape[1]
    return pl.pallas_call(
        _mod_kernel,
        grid=(n // D_MODEL,),
        in_specs=[pl.BlockSpec((rows, D_MODEL), lambda j: (0, 0)),
                  pl.BlockSpec((D_MODEL, D_MODEL), lambda j: (0, j)),
                  pl.BlockSpec((1, D_MODEL), lambda j: (0, j))],
        out_specs=pl.BlockSpec((rows, D_MODEL), lambda j: (0, j)),
        out_shape=jax.ShapeDtypeStruct((rows, n), jnp.float32),
        compiler_params=pltpu.CompilerParams(dimension_semantics=("arbitrary",),
                                             vmem_limit_bytes=VMEM_LIMIT),
        name="modulation",
    )(cc, w_mod, b_mod)


def _proj_kernel(x_ref, g_ref, sh_ref, sc_ref, *refs):
    x = x_ref[0]
    y = x * lax.rsqrt(jnp.mean(x * x, axis=-1, keepdims=True) + EPS) * g_ref[...]
    a = (y * (1.0 + sc_ref[0]) + sh_ref[0]).astype(jnp.bfloat16)
    n = len(refs) // 2
    for w_ref, o_ref in zip(refs[:n], refs[n:]):
        o_ref[0] = jnp.dot(a, w_ref[...], preferred_element_type=jnp.float32).astype(o_ref.dtype)


def _projection(x, norm_g, shift, scale, weights, dtypes):
    bsz, t, _ = x.shape
    rows = min(PROJ_ROWS, t)
    return pl.pallas_call(
        _proj_kernel,
        grid=(bsz, t // rows),
        in_specs=[pl.BlockSpec((1, rows, D_MODEL), lambda b, i: (b, i, 0)),
                  pl.BlockSpec((1, D_MODEL), lambda b, i: (0, 0)),
                  pl.BlockSpec((1, 1, D_MODEL), lambda b, i: (b, 0, 0)),
                  pl.BlockSpec((1, 1, D_MODEL), lambda b, i: (b, 0, 0))]
        + [pl.BlockSpec(w.shape, lambda b, i: (0, 0)) for w in weights],
        out_specs=[pl.BlockSpec((1, rows, w.shape[1]), lambda b, i: (b, i, 0)) for w in weights],
        out_shape=[jax.ShapeDtypeStruct((bsz, t, w.shape[1]), dt) for w, dt in zip(weights, dtypes)],
        compiler_params=pltpu.CompilerParams(dimension_semantics=("arbitrary", "arbitrary"),
                                             vmem_limit_bytes=VMEM_LIMIT),
        name="in_projection",
    )(x, norm_g, shift, scale, *weights)


def _scan_constants():
    c = SCAN_CHUNK
    idx = np.arange(c)
    t, s = idx[:, None], idx[None, :]
    masks = [t == s]
    for lvl in range(SCAN_LEVELS):
        h = c >> (lvl + 1)
        masks.append(((t // (2 * h)) == (s // (2 * h))) & ((t % (2 * h)) >= h) & ((s % (2 * h)) < h))
    m_f = np.stack([m.astype(np.float32) for m in masks])
    m_b = m_f[:, ::-1, ::-1].copy()
    tri_f = (s <= t).astype(np.float32)
    tri_b = (s >= t).astype(np.float32)
    return np.concatenate([tri_f] * 3, axis=1), np.concatenate([tri_b] * 3, axis=1), m_f, m_b


def _dot_nt(a, b):
    return lax.dot_general(a, b, (((1,), (1,)), ((), ())), preferred_element_type=jnp.float32)


def _dot_tn(a, b):
    return lax.dot_general(a, b, (((0,), (0,)), ((), ())), preferred_element_type=jnp.float32)


def _interleave(*generators):
    pending = list(generators)
    while pending:
        for gen in list(pending):
            try:
                next(gen)
            except StopIteration:
                pending.remove(gen)


def _mix_rows(q, k, half, reverse):
    parts = []
    for lo in range(0, SCAN_CHUNK, 2 * half):
        first, second = (q, k) if reverse else (k, q)
        parts += [first[lo:lo + half], second[lo + half:lo + 2 * half]]
    return jnp.concatenate(parts, axis=0)


def _level_decay(g_cum, f, half, reverse):
    c = SCAN_CHUNK
    if half >= SUBLANES:
        parts = []
        for lo in range(0, c, 2 * half):
            mid = lo + half
            if reverse:
                parts += [g_cum[lo:mid] - g_cum[mid:mid + 1], g_cum[mid:mid + 1] - g_cum[mid:mid + half]]
            else:
                parts += [g_cum[mid - 1:mid] - g_cum[lo:mid], g_cum[mid:mid + half] - g_cum[mid - 1:mid]]
        return jnp.exp2(jnp.concatenate(parts, axis=0))
    if half == SUBLANES // 2:
        g3 = g_cum.reshape(c // SUBLANES, SUBLANES, LANES)
        r = half if reverse else half - 1
        later = lax.broadcasted_iota(jnp.int32, g3.shape, 1) >= half
        sign = jnp.where(later != reverse, 1.0, -1.0)
        return jnp.exp2((g3 - g3[:, r:r + 1, :]) * sign).reshape(c, LANES)
    f3 = f.reshape(c // SUBLANES, SUBLANES, LANES)
    row = lax.broadcasted_iota(jnp.int32, f3.shape, 1)
    if half == 1:
        on_query_side = (row % 2 == 0) if reverse else (row % 2 == 1)
        return jnp.where(on_query_side, f3, 1.0).reshape(c, LANES)
    prev = pltpu.roll(f3, 1, 1)
    nxt = pltpu.roll(f3, SUBLANES - 1, 1)
    m4 = row % 4
    if reverse:
        d = jnp.where(m4 == 0, f3 * nxt, jnp.where(m4 == 1, f3, jnp.where(m4 == 2, 1.0, prev)))
    else:
        d = jnp.where(m4 == 0, nxt, jnp.where(m4 == 1, 1.0, jnp.where(m4 == 2, f3, f3 * prev)))
    return d.reshape(c, LANES)


def _chunk_local(q, k, v, f, g_cum, m_ref, reverse, result):
    c = SCAN_CHUNK
    bf = jnp.bfloat16
    end_row = 0 if reverse else c - 1

    q_fine = (q * _level_decay(g_cum, f, 1, reverse)).astype(bf)
    p = _dot_nt(jnp.concatenate([q.astype(bf), q_fine], axis=0), k.astype(bf))
    a = m_ref[0] * p[:c] + m_ref[SCAN_LEVELS] * p[c:]
    yield
    for lvl in range(SCAN_LEVELS - 1):
        half = c >> (lvl + 1)
        d = _level_decay(g_cum, f, half, reverse)
        if half >= SUBLANES:
            x = (_mix_rows(q, k, half, reverse) * d).astype(bf)
            p = _dot_nt(x, x)
        else:
            p = _dot_nt((q * d).astype(bf), (k * d).astype(bf))
        a = a + m_ref[lvl + 1] * p
        yield
    d_read = jnp.exp2(g_cum)
    d_state = jnp.exp2(g_cum[end_row:end_row + 1] - g_cum)
    vb = v.astype(bf)
    o_intra = jnp.dot(a.astype(bf), vb, preferred_element_type=jnp.float32)
    yield
    kv = _dot_tn(vb, (k * d_state).astype(bf))
    result.extend([o_intra, kv, (q * d_read).astype(bf), d_read[end_row:end_row + 1]])
    yield


def _chunk_pair(q_r, z_r, v_r, starts, lb, w_ref, m_ref, reverse, results):
    c = SCAN_CHUNK
    bf = jnp.bfloat16
    fs, g3s = [], []
    for s in starts:
        f = lb + (1.0 - lb) * jax.nn.sigmoid(z_r[0, pl.ds(s, c), :])
        g = jnp.log2(f)
        fs.append(f)
        g_hi = g.astype(bf)
        r1 = g - g_hi.astype(jnp.float32)
        g_mid = r1.astype(bf)
        g_lo = (r1 - g_mid.astype(jnp.float32)).astype(bf)
        g3s.append(jnp.concatenate([g_hi, g_mid, g_lo], axis=0))
        yield
    g_cum = jnp.dot(w_ref[...], jnp.concatenate(g3s, axis=1), preferred_element_type=jnp.float32)
    yield
    chains = [_chunk_local(q_r[0, pl.ds(s, c), :], 1.0 - fs[j], v_r[0, pl.ds(s, c), :], fs[j],
                           g_cum[:, j * LANES:(j + 1) * LANES], m_ref, reverse, results[j])
              for j, s in enumerate(starts)]
    while chains:
        for chain in list(chains):
            try:
                next(chain)
            except StopIteration:
                chains.remove(chain)
        yield


def _direction_chunks(q_r, z_r, v_r, starts, lb, w_ref, m_ref, reverse):
    results = [[] for _ in starts]
    gens = [_chunk_pair(q_r, z_r, v_r, starts[j:j + 2], lb, w_ref, m_ref, reverse, results[j:j + 2])
            for j in range(0, len(starts), 2)]
    return gens, results


def _direction_states(starts, results, st_ref, o_r):
    c = SCAN_CHUNK
    st = st_ref[...]
    for s, result in zip(starts, results):
        while not result:
            yield
        o_intra, kv, q_read, d_end = result
        if o_r is not None:
            o_r[pl.ds(s, c), :] = o_intra + _dot_nt(q_read, st.astype(jnp.bfloat16))
        st = st * d_end + kv
        yield
    st_ref[...] = st


def _delayed(generator, rounds):
    for _ in range(rounds):
        yield
    yield from generator


def _scan_kernel(q_ref, zf_ref, zb_ref, v_ref, ga_ref, qc_ref, zfc_ref, zbc_ref, vc_ref,
                 lbl_ref, gn_ref, wf_ref, wb_ref, mf_ref, mb_ref, o_ref,
                 stf_ref, stb_ref, of_ref, ob_ref):
    c = SCAN_CHUNK
    t_lat = q_ref.shape[1]
    t_ctx = qc_ref.shape[1]
    l0, l1 = lbl_ref[0], lbl_ref[1]
    mx = jnp.maximum(l0, l1)
    e0, e1 = jnp.exp(l0 - mx), jnp.exp(l1 - mx)
    lb = e0 / (e0 + e1)
    lb_f, lb_b = lb[0:1], lb[1:2]

    stf_ref[...] = jnp.zeros_like(stf_ref)
    stb_ref[...] = jnp.zeros_like(stb_ref)

    def both_directions(i, t, group, q_r, zf_r, zb_r, v_r, of_r, ob_r):
        span = group * c
        lo = pl.multiple_of(i * span, span)
        hi = pl.multiple_of(t - span - i * span, span)
        starts_f = [pl.multiple_of(lo + j * c, c) for j in range(group)]
        starts_b = [pl.multiple_of(hi + j * c, c) for j in reversed(range(group))]
        gens_f, res_f = _direction_chunks(q_r, zf_r, v_r, starts_f, lb_f, wf_ref, mf_ref, False)
        gens_b, res_b = _direction_chunks(q_r, zb_r, v_r, starts_b, lb_b, wb_ref, mb_ref, True)
        staggered = [_delayed(g, SCAN_STAGGER * j) for j, pair in enumerate(zip(gens_f, gens_b)) for g in pair]
        _interleave(*staggered,
                    _direction_states(starts_f, res_f, stf_ref, of_r),
                    _direction_states(starts_b, res_b, stb_ref, ob_r))

    ctx_group = min(SCAN_GROUP, t_ctx // c)

    def ctx_body(i, carry):
        both_directions(i, t_ctx, ctx_group, qc_ref, zfc_ref, zbc_ref, vc_ref, None, None)
        return carry

    lax.fori_loop(0, t_ctx // (ctx_group * c), ctx_body, 0)

    def lat_body(i, carry):
        both_directions(i, t_lat, SCAN_GROUP, q_ref, zf_ref, zb_ref, v_ref, of_ref, ob_ref)
        return carry

    lax.fori_loop(0, t_lat // (SCAN_GROUP * c), lat_body, 0)

    rows = SCAN_OUT_ROWS
    gn = gn_ref[...]

    def fin_body(i, carry):
        lo = pl.multiple_of(i * rows, rows)
        o = of_ref[pl.ds(lo, rows), :] + ob_ref[pl.ds(lo, rows), :]
        o = o * lax.rsqrt(jnp.mean(o * o, axis=-1, keepdims=True) + EPS) * gn
        gate = _silu(ga_ref[0, pl.ds(lo, rows), :].astype(jnp.float32))
        o_ref[0, pl.ds(lo, rows), :] = (o * gate).astype(o_ref.dtype)
        return carry

    lax.fori_loop(0, t_lat // rows, fin_body, 0, unroll=4)


def _hgrn_scan(p_lat, aux_lat, p_ctx, lb_logits, hgrn_norm_g):
    bsz, t_lat, _ = p_lat.shape
    t_ctx = p_ctx.shape[1]
    w_f3, w_b3, m_f, m_b = _scan_constants()
    wf = jnp.asarray(w_f3, jnp.bfloat16)
    wb = jnp.asarray(w_b3, jnp.bfloat16)
    mf = jnp.asarray(m_f, jnp.float32)
    mb = jnp.asarray(m_b, jnp.float32)

    def col(t, base):
        return pl.BlockSpec((1, t, LANES), lambda b, h, base=base: (b, 0, base + h))

    def whole(a):
        return pl.BlockSpec(a.shape, lambda b, h, nd=a.ndim: (0,) * nd)

    return pl.pallas_call(
        _scan_kernel,
        grid=(bsz, HGRN_HEADS),
        in_specs=[col(t_lat, COL_Q), col(t_lat, COL_ZF), col(t_lat, COL_ZB), col(t_lat, COL_V),
                  col(t_lat, COL_GA),
                  col(t_ctx, COL_Q), col(t_ctx, COL_ZF), col(t_ctx, COL_ZB), col(t_ctx, COL_V),
                  pl.BlockSpec((2, 2, LANES), lambda b, h: (0, 0, h)),
                  pl.BlockSpec((1, LANES), lambda b, h: (0, h)),
                  whole(wf), whole(wb), whole(mf), whole(mb)],
        out_specs=pl.BlockSpec((1, t_lat, LANES), lambda b, h: (b, 0, h)),
        out_shape=jax.ShapeDtypeStruct((bsz, t_lat, D_HGRN), jnp.bfloat16),
        scratch_shapes=[pltpu.VMEM((HEAD_DIM, HEAD_DIM), jnp.float32),
                        pltpu.VMEM((HEAD_DIM, HEAD_DIM), jnp.float32),
                        pltpu.VMEM((t_lat, HEAD_DIM), jnp.float32),
                        pltpu.VMEM((t_lat, HEAD_DIM), jnp.float32)],
        compiler_params=pltpu.CompilerParams(dimension_semantics=("arbitrary", "arbitrary"),
                                             vmem_limit_bytes=VMEM_LIMIT),
        name="hgrn_scan",
    )(p_lat, p_lat, p_lat, p_lat, aux_lat, p_ctx, p_ctx, p_ctx, p_ctx,
      lb_logits, hgrn_norm_g, wf, wb, mf, mb)


CONV_PADW = GRID_W + 32


def _conv_fill(u_ref, ug_ref, pad_ref, along_rows):
    t = u_ref.shape[1]
    n_rows = t // GRID_W

    @pl.when(along_rows)
    def _along_rows():
        padw = CONV_PADW

        def fill(r, carry):
            src = pl.multiple_of(r * GRID_W, GRID_W)
            dst = pl.multiple_of(r * padw, 32)
            glu = (u_ref[0, pl.ds(src, GRID_W), :].astype(jnp.float32)
                   * jax.nn.sigmoid(ug_ref[0, pl.ds(src, GRID_W), :].astype(jnp.float32)))
            pad_ref[pl.ds(dst, 16), :] = jnp.zeros((16, LANES), jnp.float32)
            pad_ref[pl.ds(dst + 16, GRID_W), :] = glu
            pad_ref[pl.ds(dst + 16 + GRID_W, 16), :] = jnp.zeros((16, LANES), jnp.float32)
            return carry

        lax.fori_loop(0, n_rows, fill, 0, unroll=4)

    @pl.when(jnp.logical_not(along_rows))
    def _along_cols():
        halo = CONV_HALF * GRID_W
        pad_ref[pl.ds(0, halo), :] = jnp.zeros((halo, LANES), jnp.float32)
        pad_ref[pl.ds(halo + t, halo), :] = jnp.zeros((halo, LANES), jnp.float32)

        def fill(r, carry):
            src = pl.multiple_of(r * GRID_W, GRID_W)
            glu = (u_ref[0, pl.ds(src, GRID_W), :].astype(jnp.float32)
                   * jax.nn.sigmoid(ug_ref[0, pl.ds(src, GRID_W), :].astype(jnp.float32)))
            pad_ref[pl.ds(halo + src, GRID_W), :] = glu
            return carry

        lax.fori_loop(0, n_rows, fill, 0, unroll=4)


def _conv_kernel(u_ref, ug_ref, w_ref, b_ref, o_ref, pad_ref):
    n_rows = u_ref.shape[1] // GRID_W
    along_rows = pl.program_id(1) < (D_CONV // 2) // LANES
    _conv_fill(u_ref, ug_ref, pad_ref, along_rows)
    bias = b_ref[...]

    def taps(first_tap, stride):
        def conv(r, carry):
            dst = pl.multiple_of(r * GRID_W, GRID_W)
            base = first_tap(r)
            acc = jnp.zeros((GRID_W, LANES), jnp.float32)
            for k in range(CONV_WIDTH):
                acc = acc + w_ref[k:k + 1, :] * pad_ref[pl.ds(base + k * stride, GRID_W), :]
            o_ref[0, pl.ds(dst, GRID_W), :] = (acc + bias).astype(o_ref.dtype)
            return carry

        lax.fori_loop(0, n_rows, conv, 0, unroll=8)

    @pl.when(along_rows)
    def _():
        taps(lambda r: r * CONV_PADW + (16 - CONV_HALF), 1)

    @pl.when(jnp.logical_not(along_rows))
    def _():
        taps(lambda r: pl.multiple_of(r * GRID_W, GRID_W), GRID_W)


def _axial_conv(aux_lat, conv_w, conv_b):
    bsz, t, _ = aux_lat.shape
    n_rows = t // GRID_W
    pad_rows = max(n_rows * CONV_PADW, t + 2 * CONV_HALF * GRID_W)
    w_pad = jnp.zeros((32, D_CONV), jnp.float32).at[:CONV_WIDTH].set(conv_w)
    return pl.pallas_call(
        _conv_kernel,
        grid=(bsz, D_CONV // LANES),
        in_specs=[pl.BlockSpec((1, t, LANES), lambda b, g: (b, 0, COL_U + g)),
                  pl.BlockSpec((1, t, LANES), lambda b, g: (b, 0, COL_UG + g)),
                  pl.BlockSpec((32, LANES), lambda b, g: (0, g)),
                  pl.BlockSpec((1, LANES), lambda b, g: (0, g))],
        out_specs=pl.BlockSpec((1, t, LANES), lambda b, g: (b, 0, g)),
        out_shape=jax.ShapeDtypeStruct((bsz, t, D_CONV), jnp.bfloat16),
        scratch_shapes=[pltpu.VMEM((pad_rows, LANES), jnp.float32)],
        compiler_params=pltpu.CompilerParams(dimension_semantics=("arbitrary", "arbitrary"),
                                             vmem_limit_bytes=VMEM_LIMIT),
        name="axial_conv",
    )(aux_lat, aux_lat, w_pad, conv_b)


def _out_kernel(x_ref, ba_ref, y_ref, gb_ref, gt_ref, lng_ref, lnb_ref, wa_ref, wb_ref, fg_ref, o_ref):
    y = y_ref[0].astype(jnp.float32)
    mu = jnp.mean(y, axis=-1, keepdims=True)
    yc = y - mu
    var = jnp.mean(yc * yc, axis=-1, keepdims=True)
    yn = yc * lax.rsqrt(var + EPS) * lng_ref[...] + lnb_ref[...]
    branch_b = _silu(yn) * _silu(gb_ref[0].astype(jnp.float32))
    mix = jnp.dot(ba_ref[0], wa_ref[...], preferred_element_type=jnp.float32)
    mix = mix + jnp.dot(branch_b.astype(jnp.bfloat16), wb_ref[...], preferred_element_type=jnp.float32)
    h = x_ref[0] + gt_ref[0] * mix
    o_ref[0] = h * lax.rsqrt(jnp.mean(h * h, axis=-1, keepdims=True) + EPS) * fg_ref[...]


def _output(x, branch_a, y_conv, aux_lat, gate, ln_g, ln_b, w_out_bf16, final_g):
    bsz, t, _ = x.shape
    rows = OUT_ROWS
    w_a, w_b = w_out_bf16[:D_HGRN], w_out_bf16[D_HGRN:]
    gb_block = COL_GB * LANES // D_CONV
    return pl.pallas_call(
        _out_kernel,
        grid=(bsz, t // rows),
        in_specs=[pl.BlockSpec((1, rows, D_MODEL), lambda b, i: (b, i, 0)),
                  pl.BlockSpec((1, rows, D_HGRN), lambda b, i: (b, i, 0)),
                  pl.BlockSpec((1, rows, D_CONV), lambda b, i: (b, i, 0)),
                  pl.BlockSpec((1, rows, D_CONV), lambda b, i: (b, i, gb_block)),
                  pl.BlockSpec((1, 1, D_MODEL), lambda b, i: (b, 0, 0)),
                  pl.BlockSpec((1, D_CONV), lambda b, i: (0, 0)),
                  pl.BlockSpec((1, D_CONV), lambda b, i: (0, 0)),
                  pl.BlockSpec((D_HGRN, D_MODEL), lambda b, i: (0, 0)),
                  pl.BlockSpec((D_CONV, D_MODEL), lambda b, i: (0, 0)),
                  pl.BlockSpec((1, D_MODEL), lambda b, i: (0, 0))],
        out_specs=pl.BlockSpec((1, rows, D_MODEL), lambda b, i: (b, i, 0)),
        out_shape=jax.ShapeDtypeStruct((bsz, t, D_MODEL), jnp.float32),
        compiler_params=pltpu.CompilerParams(dimension_semantics=("arbitrary", "arbitrary"),
                                             vmem_limit_bytes=VMEM_LIMIT),
        name="out_projection",
    )(x, branch_a, y_conv, aux_lat, gate, ln_g, ln_b, w_a, w_b, final_g)


def kernel(x, c, ctx, c_ctx, norm_g, w_mod, b_mod, w_in, lb_logits, hgrn_norm_g, conv_w, conv_b,
           conv_ln_g, conv_ln_b, w_out, final_norm_g):
    bsz, seq_len, _ = x.shape
    span = SCAN_GROUP * SCAN_CHUNK
    assert norm_g.shape[0] == 1, "single-layer block"
    assert seq_len % GRID_W == 0 and seq_len % span == 0
    assert ctx.shape[1] % (2 * SCAN_CHUNK) == 0 and (ctx.shape[1] % span == 0 or ctx.shape[1] < span)

    pad = (-(bsz + 1)) % SUBLANES
    cc = jnp.concatenate([c, c_ctx[None, :], jnp.zeros((pad, D_MODEL), c.dtype)], axis=0)
    mod = _modulation(cc, w_mod[0], b_mod)
    shift, scale, gate = (mod[:, i * D_MODEL:(i + 1) * D_MODEL] for i in range(3))
    shift_lat, scale_lat, gate_lat = (m[:bsz, None, :] for m in (shift, scale, gate))
    shift_ctx, scale_ctx = (jnp.broadcast_to(m[bsz][None, None, :], (bsz, 1, D_MODEL)) for m in (shift, scale))

    w_in_bf16 = w_in[0].astype(jnp.bfloat16)
    w_scan, w_aux = w_in_bf16[:, :D_SCAN_IN], w_in_bf16[:, D_SCAN_IN:]
    p_lat, aux_lat = _projection(x, norm_g, shift_lat, scale_lat, (w_scan, w_aux), (jnp.float32, jnp.bfloat16))
    p_ctx, = _projection(ctx, norm_g, shift_ctx, scale_ctx, (w_scan,), (jnp.float32,))

    branch_a = _hgrn_scan(p_lat, aux_lat, p_ctx, lb_logits, hgrn_norm_g)
    y_conv = _axial_conv(aux_lat, conv_w[0], conv_b)
    return _output(x, branch_a, y_conv, aux_lat, gate_lat, conv_ln_g, conv_ln_b,
                   w_out[0].astype(jnp.bfloat16), final_norm_g[None, :])
```

```python
import numpy as np
import jax
import jax.numpy as jnp
from jax import lax
from jax.experimental import pallas as pl
from jax.experimental.pallas import tpu as pltpu

D_MODEL = 1024
GRID_W = 64
D_HGRN = 512
HGRN_HEADS = 4
HEAD_DIM = D_HGRN // HGRN_HEADS
D_CONV = 512
CONV_WIDTH = 31
CONV_HALF = CONV_WIDTH // 2
EPS = 1e-6

LANES = 128
SUBLANES = 8
SCAN_CHUNK = 64
SCAN_LEVELS = SCAN_CHUNK.bit_length() - 1
SCAN_GROUP = 8
SCAN_OUT_ROWS = 256
SCAN_STAGGER = 1
PROJ_ROWS = 512
OUT_ROWS = 1024
VMEM_LIMIT = 56 * 1024 * 1024

D_SCAN_IN = 4 * D_HGRN
COL_Q, COL_ZF, COL_ZB, COL_V = (i * HGRN_HEADS for i in range(4))
COL_GA, COL_U, COL_UG, COL_GB = (i * (D_HGRN // LANES) for i in range(4))


def _silu(x):
    return x * jax.nn.sigmoid(x)


def _mod_kernel(c_ref, w_ref, b_ref, o_ref):
    a = _silu(c_ref[...])
    o_ref[...] = jnp.dot(a, w_ref[...], preferred_element_type=jnp.float32,
                         precision=lax.Precision.HIGHEST) + b_ref[...]


def _modulation(cc, w_mod, b_mod):
    rows = cc.shape[0]
    n = w_mod.shape[1]
    return pl.pallas_call(
        _mod_kernel,
        grid=(n // D_MODEL,),
        in_specs=[pl.BlockSpec((rows, D_MODEL), lambda j: (0, 0)),
                  pl.BlockSpec((D_MODEL, D_MODEL), lambda j: (0, j)),
                  pl.BlockSpec((1, D_MODEL), lambda j: (0, j))],
        out_specs=pl.BlockSpec((rows, D_MODEL), lambda j: (0, j)),
        out_shape=jax.ShapeDtypeStruct((rows, n), jnp.float32),
        compiler_params=pltpu.CompilerParams(dimension_semantics=("arbitrary",),
                                             vmem_limit_bytes=VMEM_LIMIT),
        name="modulation",
    )(cc, w_mod, b_mod)


def _proj_kernel(x_ref, g_ref, sh_ref, sc_ref, *refs):
    x = x_ref[0]
    y = x * lax.rsqrt(jnp.mean(x * x, axis=-1, keepdims=True) + EPS) * g_ref[...]
    a = (y * (1.0 + sc_ref[0]) + sh_ref[0]).astype(jnp.bfloat16)
    n = len(refs) // 2
    for w_ref, o_ref in zip(refs[:n], refs[n:]):
        o_ref[0] = jnp.dot(a, w_ref[...], preferred_element_type=jnp.float32).astype(o_ref.dtype)


def _projection(x, norm_g, shift, scale, weights, dtypes):
    bsz, t, _ = x.shape
    rows = min(PROJ_ROWS, t)
    return pl.pallas_call(
        _proj_kernel,
        grid=(bsz, t // rows),
        in_specs=[pl.BlockSpec((1, rows, D_MODEL), lambda b, i: (b, i, 0)),
                  pl.BlockSpec((1, D_MODEL), lambda b, i: (0, 0)),
                  pl.BlockSpec((1, 1, D_MODEL), lambda b, i: (b, 0, 0)),
                  pl.BlockSpec((1, 1, D_MODEL), lambda b, i: (b, 0, 0))]
        + [pl.BlockSpec(w.shape, lambda b, i: (0, 0)) for w in weights],
        out_specs=[pl.BlockSpec((1, rows, w.shape[1]), lambda b, i: (b, i, 0)) for w in weights],
        out_shape=[jax.ShapeDtypeStruct((bsz, t, w.shape[1]), dt) for w, dt in zip(weights, dtypes)],
        compiler_params=pltpu.CompilerParams(dimension_semantics=("arbitrary", "arbitrary"),
                                             vmem_limit_bytes=VMEM_LIMIT),
        name="in_projection",
    )(x, norm_g, shift, scale, *weights)


def _scan_constants():
    c = SCAN_CHUNK
    idx = np.arange(c)
    t, s = idx[:, None], idx[None, :]
    masks = [t == s]
    for lvl in range(SCAN_LEVELS):
        h = c >> (lvl + 1)
        masks.append(((t // (2 * h)) == (s // (2 * h))) & ((t % (2 * h)) >= h) & ((s % (2 * h)) < h))
    m_f = np.stack([m.astype(np.float32) for m in masks])
    m_b = m_f[:, ::-1, ::-1].copy()
    tri_f = (s <= t).astype(np.float32)
    tri_b = (s >= t).astype(np.float32)
    return np.concatenate([tri_f] * 3, axis=1), np.concatenate([tri_b] * 3, axis=1), m_f, m_b


def _dot_nt(a, b):
    return lax.dot_general(a, b, (((1,), (1,)), ((), ())), preferred_element_type=jnp.float32)


def _dot_tn(a, b):
    return lax.dot_general(a, b, (((0,), (0,)), ((), ())), preferred_element_type=jnp.float32)


def _interleave(*generators):
    pending = list(generators)
    while pending:
        for gen in list(pending):
            try:
                next(gen)
            except StopIteration:
                pending.remove(gen)


def _mix_rows(q, k, half, reverse):
    parts = []
    for lo in range(0, SCAN_CHUNK, 2 * half):
        first, second = (q, k) if reverse else (k, q)
        parts += [first[lo:lo + half], second[lo + half:lo + 2 * half]]
    return jnp.concatenate(parts, axis=0)


def _level_decay(g_cum, f, half, reverse):
    c = SCAN_CHUNK
    if half >= SUBLANES:
        parts = []
        for lo in range(0, c, 2 * half):
            mid = lo + half
            if reverse:
                parts += [g_cum[lo:mid] - g_cum[mid:mid + 1], g_cum[mid:mid + 1] - g_cum[mid:mid + half]]
            else:
                parts += [g_cum[mid - 1:mid] - g_cum[lo:mid], g_cum[mid:mid + half] - g_cum[mid - 1:mid]]
        return jnp.exp2(jnp.concatenate(parts, axis=0))
    if half == SUBLANES // 2:
        g3 = g_cum.reshape(c // SUBLANES, SUBLANES, LANES)
        r = half if reverse else half - 1
        later = lax.broadcasted_iota(jnp.int32, g3.shape, 1) >= half
        sign = jnp.where(later != reverse, 1.0, -1.0)
        return jnp.exp2((g3 - g3[:, r:r + 1, :]) * sign).reshape(c, LANES)
    f3 = f.reshape(c // SUBLANES, SUBLANES, LANES)
    row = lax.broadcasted_iota(jnp.int32, f3.shape, 1)
    if half == 1:
        on_query_side = (row % 2 == 0) if reverse else (row % 2 == 1)
        return jnp.where(on_query_side, f3, 1.0).reshape(c, LANES)
    prev = pltpu.roll(f3, 1, 1)
    nxt = pltpu.roll(f3, SUBLANES - 1, 1)
    m4 = row % 4
    if reverse:
        d = jnp.where(m4 == 0, f3 * nxt, jnp.where(m4 == 1, f3, jnp.where(m4 == 2, 1.0, prev)))
    else:
        d = jnp.where(m4 == 0, nxt, jnp.where(m4 == 1, 1.0, jnp.where(m4 == 2, f3, f3 * prev)))
    return d.reshape(c, LANES)


def _chunk_local(q, k, v, f, g_cum, m_ref, reverse, result):
    c = SCAN_CHUNK
    bf = jnp.bfloat16
    end_row = 0 if reverse else c - 1

    qb, kb = q.astype(bf), k.astype(bf)
    a = m_ref[0] * _dot_nt(qb, kb)
    yield
    for half in (SUBLANES, 2, 1) + tuple(h for h in (c >> (lvl + 1) for lvl in range(SCAN_LEVELS))
                                         if h not in (SUBLANES, 2, 1)):
        d = _level_decay(g_cum, f, half, reverse).astype(bf)
        if half >= 2 * SUBLANES:
            x = _mix_rows(qb, kb, half, reverse) * d
            p = _dot_nt(x, x)
        elif half == SUBLANES:
            x = (_mix_rows(q, k, half, reverse)).astype(bf) * d
            p = _dot_nt(x, x)
        elif half == 1:
            p = _dot_nt(qb * d, kb)
        else:
            p = _dot_nt(qb * d, kb * d)
        a = a + m_ref[SCAN_LEVELS - half.bit_length() + 1] * p
        yield
    d_read = jnp.exp2(g_cum)
    d_state = jnp.exp2(g_cum[end_row:end_row + 1] - g_cum)
    vb = v.astype(bf)
    o_intra = jnp.dot(a.astype(bf), vb, preferred_element_type=jnp.float32)
    yield
    kv = _dot_tn(vb, kb * d_state.astype(bf))
    result.extend([o_intra, kv, qb * d_read.astype(bf), d_read[end_row:end_row + 1]])
    yield


def _chunk_pair(q_r, z_r, v_r, starts, lb, w_ref, m_ref, reverse, results):
    c = SCAN_CHUNK
    bf = jnp.bfloat16
    fs, g3s = [], []
    for s in starts:
        f = lb + (1.0 - lb) * jax.nn.sigmoid(z_r[0, pl.ds(s, c), :])
        g = jnp.log2(f)
        fs.append(f)
        g_hi = g.astype(bf)
        r1 = g - g_hi.astype(jnp.float32)
        g_mid = r1.astype(bf)
        g_lo = (r1 - g_mid.astype(jnp.float32)).astype(bf)
        g3s.append(jnp.concatenate([g_hi, g_mid, g_lo], axis=0))
        yield
    g_cum = jnp.dot(w_ref[...], jnp.concatenate(g3s, axis=1), preferred_element_type=jnp.float32)
    yield
    chains = [_chunk_local(q_r[0, pl.ds(s, c), :], 1.0 - fs[j], v_r[0, pl.ds(s, c), :], fs[j],
                           g_cum[:, j * LANES:(j + 1) * LANES], m_ref, reverse, results[j])
              for j, s in enumerate(starts)]
    while chains:
        for chain in list(chains):
            try:
                next(chain)
            except StopIteration:
                chains.remove(chain)
        yield


def _direction_chunks(q_r, z_r, v_r, starts, lb, w_ref, m_ref, reverse):
    results = [[] for _ in starts]
    gens = [_chunk_pair(q_r, z_r, v_r, starts[j:j + 2], lb, w_ref, m_ref, reverse, results[j:j + 2])
            for j in range(0, len(starts), 2)]
    return gens, results


def _direction_states(starts, results, st_ref, o_r):
    c = SCAN_CHUNK
    st = st_ref[...]
    for s, result in zip(starts, results):
        while not result:
            yield
        o_intra, kv, q_read, d_end = result
        if o_r is not None:
            o_r[pl.ds(s, c), :] = o_intra + _dot_nt(q_read, st.astype(jnp.bfloat16))
        st = st * d_end + kv
        yield
    st_ref[...] = st


def _delayed(generator, rounds):
    for _ in range(rounds):
        yield
    yield from generator


def _scan_kernel(q_ref, zf_ref, zb_ref, v_ref, ga_ref, qc_ref, zfc_ref, zbc_ref, vc_ref,
                 lbl_ref, gn_ref, wf_ref, wb_ref, mf_ref, mb_ref, o_ref,
                 stf_ref, stb_ref, of_ref, ob_ref):
    c = SCAN_CHUNK
    t_lat = q_ref.shape[1]
    t_ctx = qc_ref.shape[1]
    l0, l1 = lbl_ref[0], lbl_ref[1]
    mx = jnp.maximum(l0, l1)
    e0, e1 = jnp.exp(l0 - mx), jnp.exp(l1 - mx)
    lb = e0 / (e0 + e1)
    lb_f, lb_b = lb[0:1], lb[1:2]

    stf_ref[...] = jnp.zeros_like(stf_ref)
    stb_ref[...] = jnp.zeros_like(stb_ref)

    def both_directions(i, t, group, q_r, zf_r, zb_r, v_r, of_r, ob_r):
        span = group * c
        lo = pl.multiple_of(i * span, span)
        hi = pl.multiple_of(t - span - i * span, span)
        starts_f = [pl.multiple_of(lo + j * c, c) for j in range(group)]
        starts_b = [pl.multiple_of(hi + j * c, c) for j in reversed(range(group))]
        gens_f, res_f = _direction_chunks(q_r, zf_r, v_r, starts_f, lb_f, wf_ref, mf_ref, False)
        gens_b, res_b = _direction_chunks(q_r, zb_r, v_r, starts_b, lb_b, wb_ref, mb_ref, True)
        staggered = [_delayed(g, SCAN_STAGGER * j) for j, pair in enumerate(zip(gens_f, gens_b)) for g in pair]
        _interleave(*staggered,
                    _direction_states(starts_f, res_f, stf_ref, of_r),
                    _direction_states(starts_b, res_b, stb_ref, ob_r))

    ctx_group = min(SCAN_GROUP, t_ctx // c)

    def ctx_body(i, carry):
        both_directions(i, t_ctx, ctx_group, qc_ref, zfc_ref, zbc_ref, vc_ref, None, None)
        return carry

    lax.fori_loop(0, t_ctx // (ctx_group * c), ctx_body, 0)

    def lat_body(i, carry):
        both_directions(i, t_lat, SCAN_GROUP, q_ref, zf_ref, zb_ref, v_ref, of_ref, ob_ref)
        return carry

    lax.fori_loop(0, t_lat // (SCAN_GROUP * c), lat_body, 0)

    rows = SCAN_OUT_ROWS
    gn = gn_ref[...]

    def fin_body(i, carry):
        lo = pl.multiple_of(i * rows, rows)
        o = of_ref[pl.ds(lo, rows), :] + ob_ref[pl.ds(lo, rows), :]
        o = o * lax.rsqrt(jnp.mean(o * o, axis=-1, keepdims=True) + EPS) * gn
        gate = _silu(ga_ref[0, pl.ds(lo, rows), :].astype(jnp.float32))
        o_ref[0, pl.ds(lo, rows), :] = (o * gate).astype(o_ref.dtype)
        return carry

    lax.fori_loop(0, t_lat // rows, fin_body, 0, unroll=4)


def _hgrn_scan(p_lat, aux_lat, p_ctx, lb_logits, hgrn_norm_g):
    bsz, t_lat, _ = p_lat.shape
    t_ctx = p_ctx.shape[1]
    w_f3, w_b3, m_f, m_b = _scan_constants()
    wf = jnp.asarray(w_f3, jnp.bfloat16)
    wb = jnp.asarray(w_b3, jnp.bfloat16)
    mf = jnp.asarray(m_f, jnp.float32)
    mb = jnp.asarray(m_b, jnp.float32)

    def col(t, base):
        return pl.BlockSpec((1, t, LANES), lambda b, h, base=base: (b, 0, base + h))

    def whole(a):
        return pl.BlockSpec(a.shape, lambda b, h, nd=a.ndim: (0,) * nd)

    return pl.pallas_call(
        _scan_kernel,
        grid=(bsz, HGRN_HEADS),
        in_specs=[col(t_lat, COL_Q), col(t_lat, COL_ZF), col(t_lat, COL_ZB), col(t_lat, COL_V),
                  col(t_lat, COL_GA),
                  col(t_ctx, COL_Q), col(t_ctx, COL_ZF), col(t_ctx, COL_ZB), col(t_ctx, COL_V),
                  pl.BlockSpec((2, 2, LANES), lambda b, h: (0, 0, h)),
                  pl.BlockSpec((1, LANES), lambda b, h: (0, h)),
                  whole(wf), whole(wb), whole(mf), whole(mb)],
        out_specs=pl.BlockSpec((1, t_lat, LANES), lambda b, h: (b, 0, h)),
        out_shape=jax.ShapeDtypeStruct((bsz, t_lat, D_HGRN), jnp.bfloat16),
        scratch_shapes=[pltpu.VMEM((HEAD_DIM, HEAD_DIM), jnp.float32),
                        pltpu.VMEM((HEAD_DIM, HEAD_DIM), jnp.float32),
                        pltpu.VMEM((t_lat, HEAD_DIM), jnp.float32),
                        pltpu.VMEM((t_lat, HEAD_DIM), jnp.float32)],
        compiler_params=pltpu.CompilerParams(dimension_semantics=("arbitrary", "arbitrary"),
                                             vmem_limit_bytes=VMEM_LIMIT),
        name="hgrn_scan",
    )(p_lat, p_lat, p_lat, p_lat, aux_lat, p_ctx, p_ctx, p_ctx, p_ctx,
      lb_logits, hgrn_norm_g, wf, wb, mf, mb)


CONV_PADW = GRID_W + 32


def _conv_fill(u_ref, ug_ref, pad_ref, along_rows):
    t = u_ref.shape[1]
    n_rows = t // GRID_W

    @pl.when(along_rows)
    def _along_rows():
        padw = CONV_PADW

        def fill(r, carry):
            src = pl.multiple_of(r * GRID_W, GRID_W)
            dst = pl.multiple_of(r * padw, 32)
            glu = (u_ref[0, pl.ds(src, GRID_W), :].astype(jnp.float32)
                   * jax.nn.sigmoid(ug_ref[0, pl.ds(src, GRID_W), :].astype(jnp.float32)))
            pad_ref[pl.ds(dst, 16), :] = jnp.zeros((16, LANES), jnp.float32)
            pad_ref[pl.ds(dst + 16, GRID_W), :] = glu
            pad_ref[pl.ds(dst + 16 + GRID_W, 16), :] = jnp.zeros((16, LANES), jnp.float32)
            return carry

        lax.fori_loop(0, n_rows, fill, 0, unroll=4)

    @pl.when(jnp.logical_not(along_rows))
    def _along_cols():
        halo = CONV_HALF * GRID_W
        pad_ref[pl.ds(0, halo), :] = jnp.zeros((halo, LANES), jnp.float32)
        pad_ref[pl.ds(halo + t, halo), :] = jnp.zeros((halo, LANES), jnp.float32)

        def fill(r, carry):
            src = pl.multiple_of(r * GRID_W, GRID_W)
            glu = (u_ref[0, pl.ds(src, GRID_W), :].astype(jnp.float32)
                   * jax.nn.sigmoid(ug_ref[0, pl.ds(src, GRID_W), :].astype(jnp.float32)))
            pad_ref[pl.ds(halo + src, GRID_W), :] = glu
            return carry

        lax.fori_loop(0, n_rows, fill, 0, unroll=4)


def _conv_kernel(u_ref, ug_ref, w_ref, b_ref, o_ref, pad_ref):
    n_rows = u_ref.shape[1] // GRID_W
    along_rows = pl.program_id(1) < (D_CONV // 2) // LANES
    _conv_fill(u_ref, ug_ref, pad_ref, along_rows)
    bias = b_ref[...]

    def taps(first_tap, stride):
        def conv(r, carry):
            dst = pl.multiple_of(r * GRID_W, GRID_W)
            base = first_tap(r)
            acc = jnp.zeros((GRID_W, LANES), jnp.float32)
            for k in range(CONV_WIDTH):
                acc = acc + w_ref[k:k + 1, :] * pad_ref[pl.ds(base + k * stride, GRID_W), :]
            o_ref[0, pl.ds(dst, GRID_W), :] = (acc + bias).astype(o_ref.dtype)
            return carry

        lax.fori_loop(0, n_rows, conv, 0, unroll=8)

    @pl.when(along_rows)
    def _():
        taps(lambda r: r * CONV_PADW + (16 - CONV_HALF), 1)

    @pl.when(jnp.logical_not(along_rows))
    def _():
        taps(lambda r: pl.multiple_of(r * GRID_W, GRID_W), GRID_W)


def _axial_conv(aux_lat, conv_w, conv_b):
    bsz, t, _ = aux_lat.shape
    n_rows = t // GRID_W
    pad_rows = max(n_rows * CONV_PADW, t + 2 * CONV_HALF * GRID_W)
    w_pad = jnp.zeros((32, D_CONV), jnp.float32).at[:CONV_WIDTH].set(conv_w)
    return pl.pallas_call(
        _conv_kernel,
        grid=(bsz, D_CONV // LANES),
        in_specs=[pl.BlockSpec((1, t, LANES), lambda b, g: (b, 0, COL_U + g)),
                  pl.BlockSpec((1, t, LANES), lambda b, g: (b, 0, COL_UG + g)),
                  pl.BlockSpec((32, LANES), lambda b, g: (0, g)),
                  pl.BlockSpec((1, LANES), lambda b, g: (0, g))],
        out_specs=pl.BlockSpec((1, t, LANES), lambda b, g: (b, 0, g)),
        out_shape=jax.ShapeDtypeStruct((bsz, t, D_CONV), jnp.bfloat16),
        scratch_shapes=[pltpu.VMEM((pad_rows, LANES), jnp.float32)],
        compiler_params=pltpu.CompilerParams(dimension_semantics=("arbitrary", "arbitrary"),
                                             vmem_limit_bytes=VMEM_LIMIT),
        name="axial_conv",
    )(aux_lat, aux_lat, w_pad, conv_b)


def _out_kernel(x_ref, ba_ref, y_ref, gb_ref, gt_ref, lng_ref, lnb_ref, wa_ref, wb_ref, fg_ref, o_ref):
    y = y_ref[0].astype(jnp.float32)
    mu = jnp.mean(y, axis=-1, keepdims=True)
    yc = y - mu
    var = jnp.mean(yc * yc, axis=-1, keepdims=True)
    yn = yc * lax.rsqrt(var + EPS) * lng_ref[...] + lnb_ref[...]
    branch_b = _silu(yn) * _silu(gb_ref[0].astype(jnp.float32))
    mix = jnp.dot(ba_ref[0], wa_ref[...], preferred_element_type=jnp.float32)
    mix = mix + jnp.dot(branch_b.astype(jnp.bfloat16), wb_ref[...], preferred_element_type=jnp.float32)
    h = x_ref[0] + gt_ref[0] * mix
    o_ref[0] = h * lax.rsqrt(jnp.mean(h * h, axis=-1, keepdims=True) + EPS) * fg_ref[...]


def _output(x, branch_a, y_conv, aux_lat, gate, ln_g, ln_b, w_out_bf16, final_g):
    bsz, t, _ = x.shape
    rows = OUT_ROWS
    w_a, w_b = w_out_bf16[:D_HGRN], w_out_bf16[D_HGRN:]
    gb_block = COL_GB * LANES // D_CONV
    return pl.pallas_call(
        _out_kernel,
        grid=(bsz, t // rows),
        in_specs=[pl.BlockSpec((1, rows, D_MODEL), lambda b, i: (b, i, 0)),
                  pl.BlockSpec((1, rows, D_HGRN), lambda b, i: (b, i, 0)),
                  pl.BlockSpec((1, rows, D_CONV), lambda b, i: (b, i, 0)),
                  pl.BlockSpec((1, rows, D_CONV), lambda b, i: (b, i, gb_block)),
                  pl.BlockSpec((1, 1, D_MODEL), lambda b, i: (b, 0, 0)),
                  pl.BlockSpec((1, D_CONV), lambda b, i: (0, 0)),
                  pl.BlockSpec((1, D_CONV), lambda b, i: (0, 0)),
                  pl.BlockSpec((D_HGRN, D_MODEL), lambda b, i: (0, 0)),
                  pl.BlockSpec((D_CONV, D_MODEL), lambda b, i: (0, 0)),
                  pl.BlockSpec((1, D_MODEL), lambda b, i: (0, 0))],
        out_specs=pl.BlockSpec((1, rows, D_MODEL), lambda b, i: (b, i, 0)),
        out_shape=jax.ShapeDtypeStruct((bsz, t, D_MODEL), jnp.float32),
        compiler_params=pltpu.CompilerParams(dimension_semantics=("arbitrary", "arbitrary"),
                                             vmem_limit_bytes=VMEM_LIMIT),
        name="out_projection",
    )(x, branch_a, y_conv, aux_lat, gate, ln_g, ln_b, w_a, w_b, final_g)


def kernel(x, c, ctx, c_ctx, norm_g, w_mod, b_mod, w_in, lb_logits, hgrn_norm_g, conv_w, conv_b,
           conv_ln_g, conv_ln_b, w_out, final_norm_g):
    bsz, seq_len, _ = x.shape
    span = SCAN_GROUP * SCAN_CHUNK
    assert norm_g.shape[0] == 1, "single-layer block"
    assert seq_len % GRID_W == 0 and seq_len % span == 0
    assert ctx.shape[1] % (2 * SCAN_CHUNK) == 0 and (ctx.shape[1] % span == 0 or ctx.shape[1] < span)

    pad = (-(bsz + 1)) % SUBLANES
    cc = jnp.concatenate([c, c_ctx[None, :], jnp.zeros((pad, D_MODEL), c.dtype)], axis=0)
    mod = _modulation(cc, w_mod[0], b_mod)
    shift, scale, gate = (mod[:, i * D_MODEL:(i + 1) * D_MODEL] for i in range(3))
    shift_lat, scale_lat, gate_lat = (m[:bsz, None, :] for m in (shift, scale, gate))
    shift_ctx, scale_ctx = (jnp.broadcast_to(m[bsz][None, None, :], (bsz, 1, D_MODEL)) for m in (shift, scale))

    w_in_bf16 = w_in[0].astype(jnp.bfloat16)
    w_scan, w_aux = w_in_bf16[:, :D_SCAN_IN], w_in_bf16[:, D_SCAN_IN:]
    p_lat, aux_lat = _projection(x, norm_g, shift_lat, scale_lat, (w_scan, w_aux), (jnp.float32, jnp.bfloat16))
    p_ctx, = _projection(ctx, norm_g, shift_ctx, scale_ctx, (w_scan,), (jnp.float32,))

    branch_a = _hgrn_scan(p_lat, aux_lat, p_ctx, lb_logits, hgrn_norm_g)
    y_conv = _axial_conv(aux_lat, conv_w[0], conv_b)
    return _output(x, branch_a, y_conv, aux_lat, gate_lat, conv_ln_g, conv_ln_b,
                   w_out[0].astype(jnp.bfloat16), final_norm_g[None, :])
```

```python
import numpy as np
import jax
import jax.numpy as jnp
from jax import lax
from jax.experimental import pallas as pl
from jax.experimental.pallas import tpu as pltpu

D_MODEL = 1024
GRID_W = 64
D_HGRN = 512
HGRN_HEADS = 4
HEAD_DIM = D_HGRN // HGRN_HEADS
D_CONV = 512
CONV_WIDTH = 31
CONV_HALF = CONV_WIDTH // 2
EPS = 1e-6

LANES = 128
SUBLANES = 8
SCAN_CHUNK = 64
SCAN_LEVELS = SCAN_CHUNK.bit_length() - 1
SCAN_GROUP = 8
SCAN_OUT_ROWS = 256
SCAN_STAGGER = 1
PROJ_ROWS = 512
OUT_ROWS = 1024
VMEM_LIMIT = 56 * 1024 * 1024

D_SCAN_IN = 4 * D_HGRN
COL_Q, COL_ZF, COL_ZB, COL_V = (i * HGRN_HEADS for i in range(4))
COL_GA, COL_U, COL_UG, COL_GB = (i * (D_HGRN // LANES) for i in range(4))


def _silu(x):
    return x * jax.nn.sigmoid(x)


def _mod_kernel(c_ref, w_ref, b_ref, o_ref):
    a = _silu(c_ref[...])
    o_ref[...] = jnp.dot(a, w_ref[...], preferred_element_type=jnp.float32,
                         precision=lax.Precision.HIGHEST) + b_ref[...]


def _modulation(cc, w_mod, b_mod):
    rows = cc.shape[0]
    n = w_mod.shape[1]
    return pl.pallas_call(
        _mod_kernel,
        grid=(n // D_MODEL,),
        in_specs=[pl.BlockSpec((rows, D_MODEL), lambda j: (0, 0)),
                  pl.BlockSpec((D_MODEL, D_MODEL), lambda j: (0, j)),
                  pl.BlockSpec((1, D_MODEL), lambda j: (0, j))],
        out_specs=pl.BlockSpec((rows, D_MODEL), lambda j: (0, j)),
        out_shape=jax.ShapeDtypeStruct((rows, n), jnp.float32),
        compiler_params=pltpu.CompilerParams(dimension_semantics=("arbitrary",),
                                             vmem_limit_bytes=VMEM_LIMIT),
        name="modulation",
    )(cc, w_mod, b_mod)


def _proj_kernel(x_ref, g_ref, sh_ref, sc_ref, *refs):
    x = x_ref[0]
    y = x * lax.rsqrt(jnp.mean(x * x, axis=-1, keepdims=True) + EPS) * g_ref[...]
    a = (y * (1.0 + sc_ref[0]) + sh_ref[0]).astype(jnp.bfloat16)
    n = len(refs) // 2
    for w_ref, o_ref in zip(refs[:n], refs[n:]):
        o_ref[0] = jnp.dot(a, w_ref[...], preferred_element_type=jnp.float32).astype(o_ref.dtype)


def _projection(x, norm_g, shift, scale, w_bf16, dtypes):
    bsz, t, _ = x.shape
    rows = min(PROJ_ROWS, t)
    width = D_SCAN_IN
    return pl.pallas_call(
        _proj_kernel,
        grid=(bsz, t // rows),
        in_specs=[pl.BlockSpec((1, rows, D_MODEL), lambda b, i: (b, i, 0)),
                  pl.BlockSpec((1, D_MODEL), lambda b, i: (0, 0)),
                  pl.BlockSpec((1, 1, D_MODEL), lambda b, i: (b, 0, 0)),
                  pl.BlockSpec((1, 1, D_MODEL), lambda b, i: (b, 0, 0))]
        + [pl.BlockSpec((D_MODEL, width), lambda b, i, j=j: (0, j)) for j in range(len(dtypes))],
        out_specs=[pl.BlockSpec((1, rows, width), lambda b, i: (b, i, 0)) for _ in dtypes],
        out_shape=[jax.ShapeDtypeStruct((bsz, t, width), dt) for dt in dtypes],
        compiler_params=pltpu.CompilerParams(dimension_semantics=("arbitrary", "arbitrary"),
                                             vmem_limit_bytes=VMEM_LIMIT),
        name="in_projection",
    )(x, norm_g, shift, scale, *([w_bf16] * len(dtypes)))


def _scan_constants():
    c = SCAN_CHUNK
    idx = np.arange(c)
    t, s = idx[:, None], idx[None, :]
    masks = [t == s]
    for lvl in range(SCAN_LEVELS):
        h = c >> (lvl + 1)
        masks.append(((t // (2 * h)) == (s // (2 * h))) & ((t % (2 * h)) >= h) & ((s % (2 * h)) < h))
    m_f = np.stack([m.astype(np.float32) for m in masks])
    m_b = m_f[:, ::-1, ::-1].copy()
    tri_f = (s <= t).astype(np.float32)
    tri_b = (s >= t).astype(np.float32)
    return np.concatenate([tri_f] * 3, axis=1), np.concatenate([tri_b] * 3, axis=1), m_f, m_b


def _dot_nt(a, b):
    return lax.dot_general(a, b, (((1,), (1,)), ((), ())), preferred_element_type=jnp.float32)


def _dot_tn(a, b):
    return lax.dot_general(a, b, (((0,), (0,)), ((), ())), preferred_element_type=jnp.float32)


def _interleave(*generators):
    pending = list(generators)
    while pending:
        for gen in list(pending):
            try:
                next(gen)
            except StopIteration:
                pending.remove(gen)


def _mix_rows(q, k, half, reverse):
    parts = []
    for lo in range(0, SCAN_CHUNK, 2 * half):
        first, second = (q, k) if reverse else (k, q)
        parts += [first[lo:lo + half], second[lo + half:lo + 2 * half]]
    return jnp.concatenate(parts, axis=0)


def _level_decay(g_cum, f, half, reverse):
    c = SCAN_CHUNK
    if half >= SUBLANES:
        parts = []
        for lo in range(0, c, 2 * half):
            mid = lo + half
            if reverse:
                parts += [g_cum[lo:mid] - g_cum[mid:mid + 1], g_cum[mid:mid + 1] - g_cum[mid:mid + half]]
            else:
                parts += [g_cum[mid - 1:mid] - g_cum[lo:mid], g_cum[mid:mid + half] - g_cum[mid - 1:mid]]
        return jnp.exp2(jnp.concatenate(parts, axis=0))
    if half == SUBLANES // 2:
        g3 = g_cum.reshape(c // SUBLANES, SUBLANES, LANES)
        r = half if reverse else half - 1
        later = lax.broadcasted_iota(jnp.int32, g3.shape, 1) >= half
        sign = jnp.where(later != reverse, 1.0, -1.0)
        return jnp.exp2((g3 - g3[:, r:r + 1, :]) * sign).reshape(c, LANES)
    f3 = f.reshape(c // SUBLANES, SUBLANES, LANES)
    row = lax.broadcasted_iota(jnp.int32, f3.shape, 1)
    if half == 1:
        on_query_side = (row % 2 == 0) if reverse else (row % 2 == 1)
        return jnp.where(on_query_side, f3, 1.0).reshape(c, LANES)
    prev = pltpu.roll(f3, 1, 1)
    nxt = pltpu.roll(f3, SUBLANES - 1, 1)
    m4 = row % 4
    if reverse:
        d = jnp.where(m4 == 0, f3 * nxt, jnp.where(m4 == 1, f3, jnp.where(m4 == 2, 1.0, prev)))
    else:
        d = jnp.where(m4 == 0, nxt, jnp.where(m4 == 1, 1.0, jnp.where(m4 == 2, f3, f3 * prev)))
    return d.reshape(c, LANES)


def _chunk_local(q, k, v, f, g_cum, m_ref, reverse, result):
    c = SCAN_CHUNK
    bf = jnp.bfloat16
    end_row = 0 if reverse else c - 1

    qb, kb = q.astype(bf), k.astype(bf)
    a = m_ref[0] * _dot_nt(qb, kb)
    yield
    for half in (SUBLANES, 2, 1) + tuple(h for h in (c >> (lvl + 1) for lvl in range(SCAN_LEVELS))
                                         if h not in (SUBLANES, 2, 1)):
        d = _level_decay(g_cum, f, half, reverse).astype(bf)
        if half >= 2 * SUBLANES:
            x = _mix_rows(qb, kb, half, reverse) * d
            p = _dot_nt(x, x)
        elif half == SUBLANES:
            x = (_mix_rows(q, k, half, reverse)).astype(bf) * d
            p = _dot_nt(x, x)
        elif half == 1:
            p = _dot_nt(qb * d, kb)
        else:
            p = _dot_nt(qb * d, kb * d)
        a = a + m_ref[SCAN_LEVELS - half.bit_length() + 1] * p
        yield
    d_read = jnp.exp2(g_cum)
    d_state = jnp.exp2(g_cum[end_row:end_row + 1] - g_cum)
    vb = v.astype(bf)
    o_intra = jnp.dot(a.astype(bf), vb, preferred_element_type=jnp.float32)
    yield
    kv = _dot_tn(vb, kb * d_state.astype(bf))
    result.extend([o_intra, kv, qb * d_read.astype(bf), d_read[end_row:end_row + 1]])
    yield


def _chunk_pair(q_r, z_r, v_r, starts, lb, w_ref, m_ref, reverse, results):
    c = SCAN_CHUNK
    bf = jnp.bfloat16
    fs, g3s = [], []
    for s in starts:
        f = lb + (1.0 - lb) * jax.nn.sigmoid(z_r[0, pl.ds(s, c), :])
        g = jnp.log2(f)
        fs.append(f)
        g_hi = g.astype(bf)
        r1 = g - g_hi.astype(jnp.float32)
        g_mid = r1.astype(bf)
        g_lo = (r1 - g_mid.astype(jnp.float32)).astype(bf)
        g3s.append(jnp.concatenate([g_hi, g_mid, g_lo], axis=0))
        yield
    g_cum = jnp.dot(w_ref[...], jnp.concatenate(g3s, axis=1), preferred_element_type=jnp.float32)
    yield
    chains = [_chunk_local(q_r[0, pl.ds(s, c), :], 1.0 - fs[j], v_r[0, pl.ds(s, c), :], fs[j],
                           g_cum[:, j * LANES:(j + 1) * LANES], m_ref, reverse, results[j])
              for j, s in enumerate(starts)]
    while chains:
        for chain in list(chains):
            try:
                next(chain)
            except StopIteration:
                chains.remove(chain)
        yield


def _direction_chunks(q_r, z_r, v_r, starts, lb, w_ref, m_ref, reverse):
    results = [[] for _ in starts]
    gens = [_chunk_pair(q_r, z_r, v_r, starts[j:j + 2], lb, w_ref, m_ref, reverse, results[j:j + 2])
            for j in range(0, len(starts), 2)]
    return gens, results


def _direction_states(starts, results, st_ref, o_r):
    c = SCAN_CHUNK
    st = st_ref[...]
    for s, result in zip(starts, results):
        while not result:
            yield
        o_intra, kv, q_read, d_end = result
        if o_r is not None:
            o_r[pl.ds(s, c), :] = o_intra + _dot_nt(q_read, st.astype(jnp.bfloat16))
        st = st * d_end + kv
        yield
    st_ref[...] = st


def _delayed(generator, rounds):
    for _ in range(rounds):
        yield
    yield from generator


def _scan_kernel(q_ref, zf_ref, zb_ref, v_ref, ga_ref, qc_ref, zfc_ref, zbc_ref, vc_ref,
                 lbl_ref, gn_ref, wf_ref, wb_ref, mf_ref, mb_ref, o_ref,
                 stf_ref, stb_ref, of_ref, ob_ref):
    c = SCAN_CHUNK
    t_lat = q_ref.shape[1]
    t_ctx = qc_ref.shape[1]
    l0, l1 = lbl_ref[0], lbl_ref[1]
    mx = jnp.maximum(l0, l1)
    e0, e1 = jnp.exp(l0 - mx), jnp.exp(l1 - mx)
    lb = e0 / (e0 + e1)
    lb_f, lb_b = lb[0:1], lb[1:2]

    stf_ref[...] = jnp.zeros_like(stf_ref)
    stb_ref[...] = jnp.zeros_like(stb_ref)

    def both_directions(i, t, group, q_r, zf_r, zb_r, v_r, of_r, ob_r):
        span = group * c
        lo = pl.multiple_of(i * span, span)
        hi = pl.multiple_of(t - span - i * span, span)
        starts_f = [pl.multiple_of(lo + j * c, c) for j in range(group)]
        starts_b = [pl.multiple_of(hi + j * c, c) for j in reversed(range(group))]
        gens_f, res_f = _direction_chunks(q_r, zf_r, v_r, starts_f, lb_f, wf_ref, mf_ref, False)
        gens_b, res_b = _direction_chunks(q_r, zb_r, v_r, starts_b, lb_b, wb_ref, mb_ref, True)
        staggered = [_delayed(g, SCAN_STAGGER * j) for j, pair in enumerate(zip(gens_f, gens_b)) for g in pair]
        _interleave(*staggered,
                    _direction_states(starts_f, res_f, stf_ref, of_r),
                    _direction_states(starts_b, res_b, stb_ref, ob_r))

    ctx_group = min(SCAN_GROUP, t_ctx // c)

    def ctx_body(i, carry):
        both_directions(i, t_ctx, ctx_group, qc_ref, zfc_ref, zbc_ref, vc_ref, None, None)
        return carry

    lax.fori_loop(0, t_ctx // (ctx_group * c), ctx_body, 0)

    def lat_body(i, carry):
        both_directions(i, t_lat, SCAN_GROUP, q_ref, zf_ref, zb_ref, v_ref, of_ref, ob_ref)
        return carry

    lax.fori_loop(0, t_lat // (SCAN_GROUP * c), lat_body, 0)

    rows = SCAN_OUT_ROWS
    gn = gn_ref[...]

    def fin_body(i, carry):
        lo = pl.multiple_of(i * rows, rows)
        o = of_ref[pl.ds(lo, rows), :] + ob_ref[pl.ds(lo, rows), :]
        o = o * lax.rsqrt(jnp.mean(o * o, axis=-1, keepdims=True) + EPS) * gn
        gate = _silu(ga_ref[0, pl.ds(lo, rows), :].astype(jnp.float32))
        o_ref[0, pl.ds(lo, rows), :] = (o * gate).astype(o_ref.dtype)
        return carry

    lax.fori_loop(0, t_lat // rows, fin_body, 0, unroll=4)


def _hgrn_scan(p_lat, aux_lat, p_ctx, lb_logits, hgrn_norm_g):
    bsz, t_lat, _ = p_lat.shape
    t_ctx = p_ctx.shape[1]
    w_f3, w_b3, m_f, m_b = _scan_constants()
    wf = jnp.asarray(w_f3, jnp.bfloat16)
    wb = jnp.asarray(w_b3, jnp.bfloat16)
    mf = jnp.asarray(m_f, jnp.float32)
    mb = jnp.asarray(m_b, jnp.float32)

    def col(t, base):
        return pl.BlockSpec((1, t, LANES), lambda b, h, base=base: (b, 0, base + h))

    def whole(a):
        return pl.BlockSpec(a.shape, lambda b, h, nd=a.ndim: (0,) * nd)

    return pl.pallas_call(
        _scan_kernel,
        grid=(bsz, HGRN_HEADS),
        in_specs=[col(t_lat, COL_Q), col(t_lat, COL_ZF), col(t_lat, COL_ZB), col(t_lat, COL_V),
                  col(t_lat, COL_GA),
                  col(t_ctx, COL_Q), col(t_ctx, COL_ZF), col(t_ctx, COL_ZB), col(t_ctx, COL_V),
                  pl.BlockSpec((2, 2, LANES), lambda b, h: (0, 0, h)),
                  pl.BlockSpec((1, LANES), lambda b, h: (0, h)),
                  whole(wf), whole(wb), whole(mf), whole(mb)],
        out_specs=pl.BlockSpec((1, t_lat, LANES), lambda b, h: (b, 0, h)),
        out_shape=jax.ShapeDtypeStruct((bsz, t_lat, D_HGRN), jnp.bfloat16),
        scratch_shapes=[pltpu.VMEM((HEAD_DIM, HEAD_DIM), jnp.float32),
                        pltpu.VMEM((HEAD_DIM, HEAD_DIM), jnp.float32),
                        pltpu.VMEM((t_lat, HEAD_DIM), jnp.float32),
                        pltpu.VMEM((t_lat, HEAD_DIM), jnp.float32)],
        compiler_params=pltpu.CompilerParams(dimension_semantics=("arbitrary", "arbitrary"),
                                             vmem_limit_bytes=VMEM_LIMIT),
        name="hgrn_scan",
    )(p_lat, p_lat, p_lat, p_lat, aux_lat, p_ctx, p_ctx, p_ctx, p_ctx,
      lb_logits, hgrn_norm_g, wf, wb, mf, mb)


CONV_PADW = GRID_W + 32


def _conv_fill(u_ref, ug_ref, pad_ref, along_rows):
    t = u_ref.shape[1]
    n_rows = t // GRID_W

    @pl.when(along_rows)
    def _along_rows():
        padw = CONV_PADW

        def fill(r, carry):
            src = pl.multiple_of(r * GRID_W, GRID_W)
            dst = pl.multiple_of(r * padw, 32)
            glu = (u_ref[0, pl.ds(src, GRID_W), :].astype(jnp.float32)
                   * jax.nn.sigmoid(ug_ref[0, pl.ds(src, GRID_W), :].astype(jnp.float32)))
            pad_ref[pl.ds(dst, 16), :] = jnp.zeros((16, LANES), jnp.float32)
            pad_ref[pl.ds(dst + 16, GRID_W), :] = glu
            pad_ref[pl.ds(dst + 16 + GRID_W, 16), :] = jnp.zeros((16, LANES), jnp.float32)
            return carry

        lax.fori_loop(0, n_rows, fill, 0, unroll=4)

    @pl.when(jnp.logical_not(along_rows))
    def _along_cols():
        halo = CONV_HALF * GRID_W
        pad_ref[pl.ds(0, halo), :] = jnp.zeros((halo, LANES), jnp.float32)
        pad_ref[pl.ds(halo + t, halo), :] = jnp.zeros((halo, LANES), jnp.float32)

        def fill(r, carry):
            src = pl.multiple_of(r * GRID_W, GRID_W)
            glu = (u_ref[0, pl.ds(src, GRID_W), :].astype(jnp.float32)
                   * jax.nn.sigmoid(ug_ref[0, pl.ds(src, GRID_W), :].astype(jnp.float32)))
            pad_ref[pl.ds(halo + src, GRID_W), :] = glu
            return carry

        lax.fori_loop(0, n_rows, fill, 0, unroll=4)


def _conv_kernel(u_ref, ug_ref, w_ref, b_ref, o_ref, pad_ref):
    n_rows = u_ref.shape[1] // GRID_W
    along_rows = pl.program_id(1) < (D_CONV // 2) // LANES
    _conv_fill(u_ref, ug_ref, pad_ref, along_rows)
    bias = b_ref[...]

    def taps(first_tap, stride):
        def conv(r, carry):
            dst = pl.multiple_of(r * GRID_W, GRID_W)
            base = first_tap(r)
            acc = jnp.zeros((GRID_W, LANES), jnp.float32)
            for k in range(CONV_WIDTH):
                acc = acc + w_ref[k:k + 1, :] * pad_ref[pl.ds(base + k * stride, GRID_W), :]
            o_ref[0, pl.ds(dst, GRID_W), :] = (acc + bias).astype(o_ref.dtype)
            return carry

        lax.fori_loop(0, n_rows, conv, 0, unroll=8)

    @pl.when(along_rows)
    def _():
        taps(lambda r: r * CONV_PADW + (16 - CONV_HALF), 1)

    @pl.when(jnp.logical_not(along_rows))
    def _():
        taps(lambda r: pl.multiple_of(r * GRID_W, GRID_W), GRID_W)


def _axial_conv(aux_lat, conv_w, conv_b):
    bsz, t, _ = aux_lat.shape
    n_rows = t // GRID_W
    pad_rows = max(n_rows * CONV_PADW, t + 2 * CONV_HALF * GRID_W)
    w_pad = jnp.zeros((32, D_CONV), jnp.float32).at[:CONV_WIDTH].set(conv_w)
    return pl.pallas_call(
        _conv_kernel,
        grid=(bsz, D_CONV // LANES),
        in_specs=[pl.BlockSpec((1, t, LANES), lambda b, g: (b, 0, COL_U + g)),
                  pl.BlockSpec((1, t, LANES), lambda b, g: (b, 0, COL_UG + g)),
                  pl.BlockSpec((32, LANES), lambda b, g: (0, g)),
                  pl.BlockSpec((1, LANES), lambda b, g: (0, g))],
        out_specs=pl.BlockSpec((1, t, LANES), lambda b, g: (b, 0, g)),
        out_shape=jax.ShapeDtypeStruct((bsz, t, D_CONV), jnp.bfloat16),
        scratch_shapes=[pltpu.VMEM((pad_rows, LANES), jnp.float32)],
        compiler_params=pltpu.CompilerParams(dimension_semantics=("arbitrary", "arbitrary"),
                                             vmem_limit_bytes=VMEM_LIMIT),
        name="axial_conv",
    )(aux_lat, aux_lat, w_pad, conv_b)


def _out_kernel(x_ref, ba_ref, y_ref, gb_ref, gt_ref, lng_ref, lnb_ref, wa_ref, wb_ref, fg_ref, o_ref):
    y = y_ref[0].astype(jnp.float32)
    mu = jnp.mean(y, axis=-1, keepdims=True)
    yc = y - mu
    var = jnp.mean(yc * yc, axis=-1, keepdims=True)
    yn = yc * lax.rsqrt(var + EPS) * lng_ref[...] + lnb_ref[...]
    branch_b = _silu(yn) * _silu(gb_ref[0].astype(jnp.float32))
    mix = jnp.dot(ba_ref[0], wa_ref[...], preferred_element_type=jnp.float32)
    mix = mix + jnp.dot(branch_b.astype(jnp.bfloat16), wb_ref[...], preferred_element_type=jnp.float32)
    h = x_ref[0] + gt_ref[0] * mix
    o_ref[0] = h * lax.rsqrt(jnp.mean(h * h, axis=-1, keepdims=True) + EPS) * fg_ref[...]


def _output(x, branch_a, y_conv, aux_lat, gate, ln_g, ln_b, w_out_bf16, final_g):
    bsz, t, _ = x.shape
    rows = OUT_ROWS
    gb_block = COL_GB * LANES // D_CONV
    return pl.pallas_call(
        _out_kernel,
        grid=(bsz, t // rows),
        in_specs=[pl.BlockSpec((1, rows, D_MODEL), lambda b, i: (b, i, 0)),
                  pl.BlockSpec((1, rows, D_HGRN), lambda b, i: (b, i, 0)),
                  pl.BlockSpec((1, rows, D_CONV), lambda b, i: (b, i, 0)),
                  pl.BlockSpec((1, rows, D_CONV), lambda b, i: (b, i, gb_block)),
                  pl.BlockSpec((1, 1, D_MODEL), lambda b, i: (b, 0, 0)),
                  pl.BlockSpec((1, D_CONV), lambda b, i: (0, 0)),
                  pl.BlockSpec((1, D_CONV), lambda b, i: (0, 0)),
                  pl.BlockSpec((D_HGRN, D_MODEL), lambda b, i: (0, 0)),
                  pl.BlockSpec((D_CONV, D_MODEL), lambda b, i: (1, 0)),
                  pl.BlockSpec((1, D_MODEL), lambda b, i: (0, 0))],
        out_specs=pl.BlockSpec((1, rows, D_MODEL), lambda b, i: (b, i, 0)),
        out_shape=jax.ShapeDtypeStruct((bsz, t, D_MODEL), jnp.float32),
        compiler_params=pltpu.CompilerParams(dimension_semantics=("arbitrary", "arbitrary"),
                                             vmem_limit_bytes=VMEM_LIMIT),
        name="out_projection",
    )(x, branch_a, y_conv, aux_lat, gate, ln_g, ln_b, w_out_bf16, w_out_bf16, final_g)


def kernel(x, c, ctx, c_ctx, norm_g, w_mod, b_mod, w_in, lb_logits, hgrn_norm_g, conv_w, conv_b,
           conv_ln_g, conv_ln_b, w_out, final_norm_g):
    bsz, seq_len, _ = x.shape
    span = SCAN_GROUP * SCAN_CHUNK
    assert norm_g.shape[0] == 1, "single-layer block"
    assert seq_len % GRID_W == 0 and seq_len % span == 0
    assert ctx.shape[1] % (2 * SCAN_CHUNK) == 0 and (ctx.shape[1] % span == 0 or ctx.shape[1] < span)

    pad = (-(bsz + 1)) % SUBLANES
    cc = jnp.concatenate([c, c_ctx[None, :], jnp.zeros((pad, D_MODEL), c.dtype)], axis=0)
    mod = _modulation(cc, w_mod[0], b_mod)
    shift, scale, gate = (mod[:, i * D_MODEL:(i + 1) * D_MODEL] for i in range(3))
    shift_lat, scale_lat, gate_lat = (m[:bsz, None, :] for m in (shift, scale, gate))
    shift_ctx, scale_ctx = (m[bsz][None, None, :] for m in (shift, scale))

    w_in_bf16 = w_in[0].astype(jnp.bfloat16)
    assert w_in_bf16.shape[1] == 2 * D_SCAN_IN
    p_lat, aux_lat = _projection(x, norm_g, shift_lat, scale_lat, w_in_bf16, (jnp.float32, jnp.bfloat16))
    p_ctx, = _projection(ctx.reshape(1, -1, D_MODEL), norm_g, shift_ctx, scale_ctx, w_in_bf16, (jnp.float32,))
    p_ctx = p_ctx.reshape(bsz, ctx.shape[1], D_SCAN_IN)

    branch_a = _hgrn_scan(p_lat, aux_lat, p_ctx, lb_logits, hgrn_norm_g)
    y_conv = _axial_conv(aux_lat, conv_w[0], conv_b)
    return _output(x, branch_a, y_conv, aux_lat, gate_lat, conv_ln_g, conv_ln_b,
                   w_out[0].astype(jnp.bfloat16), final_norm_g[None, :])
```

```python
import numpy as np
import jax
import jax.numpy as jnp
from jax import lax
from jax.experimental import pallas as pl
from jax.experimental.pallas import tpu as pltpu

D_MODEL = 1024
GRID_W = 64
D_HGRN = 512
HGRN_HEADS = 4
HEAD_DIM = D_HGRN // HGRN_HEADS
D_CONV = 512
CONV_WIDTH = 31
CONV_HALF = CONV_WIDTH // 2
EPS = 1e-6

LANES = 128
SUBLANES = 8
SCAN_CHUNK = 64
SCAN_LEVELS = SCAN_CHUNK.bit_length() - 1
SCAN_GROUP = 8
SCAN_OUT_ROWS = 256
SCAN_STAGGER = 1
PROJ_ROWS = 512
OUT_ROWS = 1024
VMEM_LIMIT = 56 * 1024 * 1024

D_SCAN_IN = 4 * D_HGRN
COL_Q, COL_ZF, COL_ZB, COL_V = (i * HGRN_HEADS for i in range(4))
COL_GA, COL_U, COL_UG, COL_GB = (i * (D_HGRN // LANES) for i in range(4))


def _silu(x):
    return x * jax.nn.sigmoid(x)


def _mod_kernel(c_ref, w_ref, b_ref, o_ref):
    a = _silu(c_ref[...])
    o_ref[...] = jnp.dot(a, w_ref[...], preferred_element_type=jnp.float32,
                         precision=lax.Precision.HIGHEST) + b_ref[...]


def _modulation(cc, w_mod, b_mod):
    rows = cc.shape[0]
    n = w_mod.shape[1]
    return pl.pallas_call(
        _mod_kernel,
        grid=(n // D_MODEL,),
        in_specs=[pl.BlockSpec((rows, D_MODEL), lambda j: (0, 0)),
                  pl.BlockSpec((D_MODEL, D_MODEL), lambda j: (0, j)),
                  pl.BlockSpec((1, D_MODEL), lambda j: (0, j))],
        out_specs=pl.BlockSpec((rows, D_MODEL), lambda j: (0, j)),
        out_shape=jax.ShapeDtypeStruct((rows, n), jnp.float32),
        compiler_params=pltpu.CompilerParams(dimension_semantics=("arbitrary",),
                                             vmem_limit_bytes=VMEM_LIMIT),
        name="modulation",
    )(cc, w_mod, b_mod)


def _proj_kernel(x_ref, g_ref, sh_ref, sc_ref, *refs):
    x = x_ref[0]
    y = x * lax.rsqrt(jnp.mean(x * x, axis=-1, keepdims=True) + EPS) * g_ref[...]
    a = (y * (1.0 + sc_ref[0]) + sh_ref[0]).astype(jnp.bfloat16)
    n = len(refs) // 2
    for w_ref, o_ref in zip(refs[:n], refs[n:]):
        o_ref[0] = jnp.dot(a, w_ref[...], preferred_element_type=jnp.float32).astype(o_ref.dtype)


def _projection(x, norm_g, shift, scale, w_bf16, dtypes):
    bsz, t, _ = x.shape
    rows = min(PROJ_ROWS, t)
    width = D_SCAN_IN
    return pl.pallas_call(
        _proj_kernel,
        grid=(bsz, t // rows),
        in_specs=[pl.BlockSpec((1, rows, D_MODEL), lambda b, i: (b, i, 0)),
                  pl.BlockSpec((1, D_MODEL), lambda b, i: (0, 0)),
                  pl.BlockSpec((1, 1, D_MODEL), lambda b, i: (b, 0, 0)),
                  pl.BlockSpec((1, 1, D_MODEL), lambda b, i: (b, 0, 0))]
        + [pl.BlockSpec((D_MODEL, width), lambda b, i, j=j: (0, j)) for j in range(len(dtypes))],
        out_specs=[pl.BlockSpec((1, rows, width), lambda b, i: (b, i, 0)) for _ in dtypes],
        out_shape=[jax.ShapeDtypeStruct((bsz, t, width), dt) for dt in dtypes],
        compiler_params=pltpu.CompilerParams(dimension_semantics=("arbitrary", "arbitrary"),
                                             vmem_limit_bytes=VMEM_LIMIT),
        name="in_projection",
    )(x, norm_g, shift, scale, *([w_bf16] * len(dtypes)))


def _scan_constants():
    c = SCAN_CHUNK
    idx = np.arange(c)
    t, s = idx[:, None], idx[None, :]
    masks = [t == s]
    for lvl in range(SCAN_LEVELS):
        h = c >> (lvl + 1)
        masks.append(((t // (2 * h)) == (s // (2 * h))) & ((t % (2 * h)) >= h) & ((s % (2 * h)) < h))
    m_f = np.stack([m.astype(np.float32) for m in masks])
    m_b = m_f[:, ::-1, ::-1].copy()
    tri_f = (s <= t).astype(np.float32)
    tri_b = (s >= t).astype(np.float32)
    return np.concatenate([tri_f] * 3, axis=1), np.concatenate([tri_b] * 3, axis=1), m_f, m_b


def _dot_nt(a, b):
    return lax.dot_general(a, b, (((1,), (1,)), ((), ())), preferred_element_type=jnp.float32)


def _dot_tn(a, b):
    return lax.dot_general(a, b, (((0,), (0,)), ((), ())), preferred_element_type=jnp.float32)


def _interleave(*generators):
    pending = list(generators)
    while pending:
        for gen in list(pending):
            try:
                next(gen)
            except StopIteration:
                pending.remove(gen)


def _mix_rows(q, k, half, reverse):
    parts = []
    for lo in range(0, SCAN_CHUNK, 2 * half):
        first, second = (q, k) if reverse else (k, q)
        parts += [first[lo:lo + half], second[lo + half:lo + 2 * half]]
    return jnp.concatenate(parts, axis=0)


def _level_decay(g_cum, f, half, reverse):
    c = SCAN_CHUNK
    if half >= SUBLANES:
        parts = []
        for lo in range(0, c, 2 * half):
            mid = lo + half
            if reverse:
                parts += [g_cum[lo:mid] - g_cum[mid:mid + 1], g_cum[mid:mid + 1] - g_cum[mid:mid + half]]
            else:
                parts += [g_cum[mid - 1:mid] - g_cum[lo:mid], g_cum[mid:mid + half] - g_cum[mid - 1:mid]]
        return jnp.exp2(jnp.concatenate(parts, axis=0))
    if half == SUBLANES // 2:
        g3 = g_cum.reshape(c // SUBLANES, SUBLANES, LANES)
        r = half if reverse else half - 1
        later = lax.broadcasted_iota(jnp.int32, g3.shape, 1) >= half
        sign = jnp.where(later != reverse, 1.0, -1.0)
        return jnp.exp2((g3 - g3[:, r:r + 1, :]) * sign).reshape(c, LANES)
    f3 = f.reshape(c // SUBLANES, SUBLANES, LANES)
    row = lax.broadcasted_iota(jnp.int32, f3.shape, 1)
    if half == 1:
        on_query_side = (row % 2 == 0) if reverse else (row % 2 == 1)
        return jnp.where(on_query_side, f3, 1.0).reshape(c, LANES)
    prev = pltpu.roll(f3, 1, 1)
    nxt = pltpu.roll(f3, SUBLANES - 1, 1)
    m4 = row % 4
    if reverse:
        d = jnp.where(m4 == 0, f3 * nxt, jnp.where(m4 == 1, f3, jnp.where(m4 == 2, 1.0, prev)))
    else:
        d = jnp.where(m4 == 0, nxt, jnp.where(m4 == 1, 1.0, jnp.where(m4 == 2, f3, f3 * prev)))
    return d.reshape(c, LANES)


def _chunk_local(q, k, v, f, g_cum, m_ref, reverse, result):
    c = SCAN_CHUNK
    bf = jnp.bfloat16
    end_row = 0 if reverse else c - 1

    qb, kb = q.astype(bf), k.astype(bf)
    a = m_ref[0] * _dot_nt(qb, kb)
    yield
    for half in (SUBLANES, 2, 1) + tuple(h for h in (c >> (lvl + 1) for lvl in range(SCAN_LEVELS))
                                         if h not in (SUBLANES, 2, 1)):
        d = _level_decay(g_cum, f, half, reverse).astype(bf)
        if half >= 2 * SUBLANES:
            x = _mix_rows(qb, kb, half, reverse) * d
            p = _dot_nt(x, x)
        elif half == SUBLANES:
            x = (_mix_rows(q, k, half, reverse)).astype(bf) * d
            p = _dot_nt(x, x)
        elif half == 1:
            p = _dot_nt(qb * d, kb)
        else:
            p = _dot_nt(qb * d, kb * d)
        a = a + m_ref[SCAN_LEVELS - half.bit_length() + 1] * p
        yield
    d_read = jnp.exp2(g_cum)
    d_state = jnp.exp2(g_cum[end_row:end_row + 1] - g_cum)
    vb = v.astype(bf)
    o_intra = jnp.dot(a.astype(bf), vb, preferred_element_type=jnp.float32)
    yield
    kv = _dot_tn(vb, kb * d_state.astype(bf))
    result.extend([o_intra, kv, qb * d_read.astype(bf), d_read[end_row:end_row + 1]])
    yield


def _chunk_pair(q_r, z_r, v_r, starts, lb, w_ref, m_ref, reverse, results):
    c = SCAN_CHUNK
    bf = jnp.bfloat16
    fs, g3s = [], []
    for s in starts:
        f = lb + (1.0 - lb) * jax.nn.sigmoid(z_r[0, pl.ds(s, c), :])
        g = jnp.log2(f)
        fs.append(f)
        g_hi = g.astype(bf)
        r1 = g - g_hi.astype(jnp.float32)
        g_mid = r1.astype(bf)
        g_lo = (r1 - g_mid.astype(jnp.float32)).astype(bf)
        g3s.append(jnp.concatenate([g_hi, g_mid, g_lo], axis=0))
        yield
    g_cum = jnp.dot(w_ref[...], jnp.concatenate(g3s, axis=1), preferred_element_type=jnp.float32)
    yield
    chains = [_chunk_local(q_r[0, pl.ds(s, c), :], 1.0 - fs[j], v_r[0, pl.ds(s, c), :], fs[j],
                           g_cum[:, j * LANES:(j + 1) * LANES], m_ref, reverse, results[j])
              for j, s in enumerate(starts)]
    while chains:
        for chain in list(chains):
            try:
                next(chain)
            except StopIteration:
                chains.remove(chain)
        yield


def _direction_chunks(q_r, z_r, v_r, starts, lb, w_ref, m_ref, reverse):
    results = [[] for _ in starts]
    gens = [_chunk_pair(q_r, z_r, v_r, starts[j:j + 2], lb, w_ref, m_ref, reverse, results[j:j + 2])
            for j in range(0, len(starts), 2)]
    return gens, results


def _direction_states(starts, results, st_ref, o_r):
    c = SCAN_CHUNK
    st = st_ref[...]
    for s, result in zip(starts, results):
        while not result:
            yield
        o_intra, kv, q_read, d_end = result
        if o_r is not None:
            o_r[pl.ds(s, c), :] = o_intra + _dot_nt(q_read, st.astype(jnp.bfloat16))
        st = st * d_end + kv
        yield
    st_ref[...] = st


def _delayed(generator, rounds):
    for _ in range(rounds):
        yield
    yield from generator


def _scan_kernel(q_ref, zf_ref, zb_ref, v_ref, ga_ref, qc_ref, zfc_ref, zbc_ref, vc_ref,
                 lbl_ref, gn_ref, wf_ref, wb_ref, mf_ref, mb_ref, o_ref,
                 stf_ref, stb_ref, of_ref, ob_ref):
    c = SCAN_CHUNK
    t_lat = q_ref.shape[1]
    t_ctx = qc_ref.shape[1]
    l0, l1 = lbl_ref[0], lbl_ref[1]
    mx = jnp.maximum(l0, l1)
    e0, e1 = jnp.exp(l0 - mx), jnp.exp(l1 - mx)
    lb = e0 / (e0 + e1)
    lb_f, lb_b = lb[0:1], lb[1:2]

    stf_ref[...] = jnp.zeros_like(stf_ref)
    stb_ref[...] = jnp.zeros_like(stb_ref)

    def both_directions(i, t, group, q_r, zf_r, zb_r, v_r, of_r, ob_r):
        span = group * c
        lo = pl.multiple_of(i * span, span)
        hi = pl.multiple_of(t - span - i * span, span)
        starts_f = [pl.multiple_of(lo + j * c, c) for j in range(group)]
        starts_b = [pl.multiple_of(hi + j * c, c) for j in reversed(range(group))]
        gens_f, res_f = _direction_chunks(q_r, zf_r, v_r, starts_f, lb_f, wf_ref, mf_ref, False)
        gens_b, res_b = _direction_chunks(q_r, zb_r, v_r, starts_b, lb_b, wb_ref, mb_ref, True)
        staggered = [_delayed(g, SCAN_STAGGER * j) for j, pair in enumerate(zip(gens_f, gens_b)) for g in pair]
        _interleave(*staggered,
                    _direction_states(starts_f, res_f, stf_ref, of_r),
                    _direction_states(starts_b, res_b, stb_ref, ob_r))

    ctx_group = min(SCAN_GROUP, t_ctx // c)

    def ctx_body(i, carry):
        both_directions(i, t_ctx, ctx_group, qc_ref, zfc_ref, zbc_ref, vc_ref, None, None)
        return carry

    lax.fori_loop(0, t_ctx // (ctx_group * c), ctx_body, 0)

    def lat_body(i, carry):
        both_directions(i, t_lat, SCAN_GROUP, q_ref, zf_ref, zb_ref, v_ref, of_ref, ob_ref)
        return carry

    lax.fori_loop(0, t_lat // (SCAN_GROUP * c), lat_body, 0)

    rows = SCAN_OUT_ROWS
    gn = gn_ref[...]

    def fin_body(i, carry):
        lo = pl.multiple_of(i * rows, rows)
        o = of_ref[pl.ds(lo, rows), :] + ob_ref[pl.ds(lo, rows), :]
        o = o * lax.rsqrt(jnp.mean(o * o, axis=-1, keepdims=True) + EPS) * gn
        gate = _silu(ga_ref[0, pl.ds(lo, rows), :].astype(jnp.float32))
        o_ref[0, pl.ds(lo, rows), :] = (o * gate).astype(o_ref.dtype)
        return carry

    lax.fori_loop(0, t_lat // rows, fin_body, 0, unroll=4)


def _hgrn_scan(p_lat, aux_lat, p_ctx, lb_logits, hgrn_norm_g):
    bsz, t_lat, _ = p_lat.shape
    t_ctx = p_ctx.shape[1]
    w_f3, w_b3, m_f, m_b = _scan_constants()
    wf = jnp.asarray(w_f3, jnp.bfloat16)
    wb = jnp.asarray(w_b3, jnp.bfloat16)
    mf = jnp.asarray(m_f, jnp.float32)
    mb = jnp.asarray(m_b, jnp.float32)

    def col(t, base):
        return pl.BlockSpec((1, t, LANES), lambda b, h, base=base: (b, 0, base + h))

    def whole(a):
        return pl.BlockSpec(a.shape, lambda b, h, nd=a.ndim: (0,) * nd)

    return pl.pallas_call(
        _scan_kernel,
        grid=(bsz, HGRN_HEADS),
        in_specs=[col(t_lat, COL_Q), col(t_lat, COL_ZF), col(t_lat, COL_ZB), col(t_lat, COL_V),
                  col(t_lat, COL_GA),
                  col(t_ctx, COL_Q), col(t_ctx, COL_ZF), col(t_ctx, COL_ZB), col(t_ctx, COL_V),
                  pl.BlockSpec((2, 2, LANES), lambda b, h: (0, 0, h)),
                  pl.BlockSpec((1, LANES), lambda b, h: (0, h)),
                  whole(wf), whole(wb), whole(mf), whole(mb)],
        out_specs=pl.BlockSpec((1, t_lat, LANES), lambda b, h: (b, 0, h)),
        out_shape=jax.ShapeDtypeStruct((bsz, t_lat, D_HGRN), jnp.bfloat16),
        scratch_shapes=[pltpu.VMEM((HEAD_DIM, HEAD_DIM), jnp.float32),
                        pltpu.VMEM((HEAD_DIM, HEAD_DIM), jnp.float32),
                        pltpu.VMEM((t_lat, HEAD_DIM), jnp.float32),
                        pltpu.VMEM((t_lat, HEAD_DIM), jnp.float32)],
        compiler_params=pltpu.CompilerParams(dimension_semantics=("arbitrary", "arbitrary"),
                                             vmem_limit_bytes=VMEM_LIMIT),
        name="hgrn_scan",
    )(p_lat, p_lat, p_lat, p_lat, aux_lat, p_ctx, p_ctx, p_ctx, p_ctx,
      lb_logits, hgrn_norm_g, wf, wb, mf, mb)


CONV_PADW = GRID_W + 32
CONV_BF16_TAPS = 4


def _conv_fill(u_ref, ug_ref, pad_ref, padb_ref, along_rows):
    t = u_ref.shape[1]
    n_rows = t // GRID_W

    @pl.when(along_rows)
    def _along_rows():
        padw = CONV_PADW

        def fill(r, carry):
            src = pl.multiple_of(r * GRID_W, GRID_W)
            dst = pl.multiple_of(r * padw, 32)
            glu = (u_ref[0, pl.ds(src, GRID_W), :].astype(jnp.float32)
                   * jax.nn.sigmoid(ug_ref[0, pl.ds(src, GRID_W), :].astype(jnp.float32)))
            pad_ref[pl.ds(dst, 16), :] = jnp.zeros((16, LANES), jnp.float32)
            pad_ref[pl.ds(dst + 16, GRID_W), :] = glu
            pad_ref[pl.ds(dst + 16 + GRID_W, 16), :] = jnp.zeros((16, LANES), jnp.float32)
            return carry

        lax.fori_loop(0, n_rows, fill, 0, unroll=4)

    @pl.when(jnp.logical_not(along_rows))
    def _along_cols():
        halo = CONV_HALF * GRID_W
        padb_ref[pl.ds(0, halo), :] = jnp.zeros((halo, LANES), jnp.bfloat16)
        padb_ref[pl.ds(halo + t, halo), :] = jnp.zeros((halo, LANES), jnp.bfloat16)

        def fill(r, carry):
            src = pl.multiple_of(r * GRID_W, GRID_W)
            glu = (u_ref[0, pl.ds(src, GRID_W), :].astype(jnp.float32)
                   * jax.nn.sigmoid(ug_ref[0, pl.ds(src, GRID_W), :].astype(jnp.float32)))
            padb_ref[pl.ds(halo + src, GRID_W), :] = glu.astype(jnp.bfloat16)
            return carry

        lax.fori_loop(0, n_rows, fill, 0, unroll=4)


def _conv_kernel(u_ref, ug_ref, w_ref, b_ref, o_ref, pad_ref, padb_ref, wb_ref):
    n_rows = u_ref.shape[1] // GRID_W
    along_rows = pl.program_id(1) < (D_CONV // 2) // LANES
    _conv_fill(u_ref, ug_ref, pad_ref, padb_ref, along_rows)
    bias = b_ref[...]

    @pl.when(along_rows)
    def _():
        def conv(r, carry):
            dst = pl.multiple_of(r * GRID_W, GRID_W)
            base = r * CONV_PADW + (16 - CONV_HALF)
            acc = jnp.zeros((GRID_W, LANES), jnp.float32)
            for k in range(CONV_WIDTH):
                acc = acc + w_ref[k:k + 1, :] * pad_ref[pl.ds(base + k, GRID_W), :]
            o_ref[0, pl.ds(dst, GRID_W), :] = (acc + bias).astype(o_ref.dtype)
            return carry

        lax.fori_loop(0, n_rows, conv, 0, unroll=8)

    @pl.when(jnp.logical_not(along_rows))
    def _():
        for k in range(CONV_WIDTH):
            wb_ref[k] = jnp.broadcast_to(w_ref[k:k + 1, :], (2 * SUBLANES, LANES)).astype(jnp.bfloat16)

        def conv(r, carry):
            dst = pl.multiple_of(r * GRID_W, GRID_W)
            acc = jnp.zeros((GRID_W, LANES), jnp.float32)
            for k0 in range(0, CONV_WIDTH, CONV_BF16_TAPS):
                part = None
                for k in range(k0, min(k0 + CONV_BF16_TAPS, CONV_WIDTH)):
                    window = padb_ref[pl.ds(dst + k * GRID_W, GRID_W), :].reshape(-1, 2 * SUBLANES, LANES)
                    term = (wb_ref[k][None] * window).reshape(GRID_W, LANES)
                    part = term if part is None else part + term
                acc = acc + part.astype(jnp.float32)
            o_ref[0, pl.ds(dst, GRID_W), :] = (acc + bias).astype(o_ref.dtype)
            return carry

        lax.fori_loop(0, n_rows, conv, 0, unroll=8)


def _axial_conv(aux_lat, conv_w, conv_b):
    bsz, t, _ = aux_lat.shape
    n_rows = t // GRID_W
    pad_rows = max(n_rows * CONV_PADW, t + 2 * CONV_HALF * GRID_W)
    w_pad = jnp.zeros((32, D_CONV), jnp.float32).at[:CONV_WIDTH].set(conv_w)
    return pl.pallas_call(
        _conv_kernel,
        grid=(bsz, D_CONV // LANES),
        in_specs=[pl.BlockSpec((1, t, LANES), lambda b, g: (b, 0, COL_U + g)),
                  pl.BlockSpec((1, t, LANES), lambda b, g: (b, 0, COL_UG + g)),
                  pl.BlockSpec((32, LANES), lambda b, g: (0, g)),
                  pl.BlockSpec((1, LANES), lambda b, g: (0, g))],
        out_specs=pl.BlockSpec((1, t, LANES), lambda b, g: (b, 0, g)),
        out_shape=jax.ShapeDtypeStruct((bsz, t, D_CONV), jnp.bfloat16),
        scratch_shapes=[pltpu.VMEM((n_rows * CONV_PADW, LANES), jnp.float32),
                        pltpu.VMEM((t + 2 * CONV_HALF * GRID_W, LANES), jnp.bfloat16),
                        pltpu.VMEM((CONV_WIDTH, 2 * SUBLANES, LANES), jnp.bfloat16)],
        compiler_params=pltpu.CompilerParams(dimension_semantics=("arbitrary", "arbitrary"),
                                             vmem_limit_bytes=VMEM_LIMIT),
        name="axial_conv",
    )(aux_lat, aux_lat, w_pad, conv_b)


def _out_kernel(x_ref, ba_ref, y_ref, gb_ref, gt_ref, lng_ref, lnb_ref, wa_ref, wb_ref, fg_ref, o_ref):
    y = y_ref[0].astype(jnp.float32)
    mu = jnp.mean(y, axis=-1, keepdims=True)
    yc = y - mu
    var = jnp.mean(yc * yc, axis=-1, keepdims=True)
    yn = yc * lax.rsqrt(var + EPS) * lng_ref[...] + lnb_ref[...]
    branch_b = _silu(yn) * _silu(gb_ref[0].astype(jnp.float32))
    mix = jnp.dot(ba_ref[0], wa_ref[...], preferred_element_type=jnp.float32)
    mix = mix + jnp.dot(branch_b.astype(jnp.bfloat16), wb_ref[...], preferred_element_type=jnp.float32)
    h = x_ref[0] + gt_ref[0] * mix
    o_ref[0] = h * lax.rsqrt(jnp.mean(h * h, axis=-1, keepdims=True) + EPS) * fg_ref[...]


def _output(x, branch_a, y_conv, aux_lat, gate, ln_g, ln_b, w_out_bf16, final_g):
    bsz, t, _ = x.shape
    rows = OUT_ROWS
    gb_block = COL_GB * LANES // D_CONV
    return pl.pallas_call(
        _out_kernel,
        grid=(bsz, t // rows),
        in_specs=[pl.BlockSpec((1, rows, D_MODEL), lambda b, i: (b, i, 0)),
                  pl.BlockSpec((1, rows, D_HGRN), lambda b, i: (b, i, 0)),
                  pl.BlockSpec((1, rows, D_CONV), lambda b, i: (b, i, 0)),
                  pl.BlockSpec((1, rows, D_CONV), lambda b, i: (b, i, gb_block)),
                  pl.BlockSpec((1, 1, D_MODEL), lambda b, i: (b, 0, 0)),
                  pl.BlockSpec((1, D_CONV), lambda b, i: (0, 0)),
                  pl.BlockSpec((1, D_CONV), lambda b, i: (0, 0)),
                  pl.BlockSpec((D_HGRN, D_MODEL), lambda b, i: (0, 0)),
                  pl.BlockSpec((D_CONV, D_MODEL), lambda b, i: (1, 0)),
                  pl.BlockSpec((1, D_MODEL), lambda b, i: (0, 0))],
        out_specs=pl.BlockSpec((1, rows, D_MODEL), lambda b, i: (b, i, 0)),
        out_shape=jax.ShapeDtypeStruct((bsz, t, D_MODEL), jnp.float32),
        compiler_params=pltpu.CompilerParams(dimension_semantics=("arbitrary", "arbitrary"),
                                             vmem_limit_bytes=VMEM_LIMIT),
        name="out_projection",
    )(x, branch_a, y_conv, aux_lat, gate, ln_g, ln_b, w_out_bf16, w_out_bf16, final_g)


def kernel(x, c, ctx, c_ctx, norm_g, w_mod, b_mod, w_in, lb_logits, hgrn_norm_g, conv_w, conv_b,
           conv_ln_g, conv_ln_b, w_out, final_norm_g):
    bsz, seq_len, _ = x.shape
    span = SCAN_GROUP * SCAN_CHUNK
    assert norm_g.shape[0] == 1, "single-layer block"
    assert seq_len % GRID_W == 0 and seq_len % span == 0
    assert ctx.shape[1] % (2 * SCAN_CHUNK) == 0 and (ctx.shape[1] % span == 0 or ctx.shape[1] < span)

    pad = (-(bsz + 1)) % SUBLANES
    cc = jnp.concatenate([c, c_ctx[None, :], jnp.zeros((pad, D_MODEL), c.dtype)], axis=0)
    mod = _modulation(cc, w_mod[0], b_mod)
    shift, scale, gate = (mod[:, i * D_MODEL:(i + 1) * D_MODEL] for i in range(3))
    shift_lat, scale_lat, gate_lat = (m[:bsz, None, :] for m in (shift, scale, gate))
    shift_ctx, scale_ctx = (m[bsz][None, None, :] for m in (shift, scale))

    w_in_bf16 = w_in[0].astype(jnp.bfloat16)
    assert w_in_bf16.shape[1] == 2 * D_SCAN_IN
    p_lat, aux_lat = _projection(x, norm_g, shift_lat, scale_lat, w_in_bf16, (jnp.float32, jnp.bfloat16))
    p_ctx, = _projection(ctx.reshape(1, -1, D_MODEL), norm_g, shift_ctx, scale_ctx, w_in_bf16, (jnp.float32,))
    p_ctx = p_ctx.reshape(bsz, ctx.shape[1], D_SCAN_IN)

    branch_a = _hgrn_scan(p_lat, aux_lat, p_ctx, lb_logits, hgrn_norm_g)
    y_conv = _axial_conv(aux_lat, conv_w[0], conv_b)
    return _output(x, branch_a, y_conv, aux_lat, gate_lat, conv_ln_g, conv_ln_b,
                   w_out[0].astype(jnp.bfloat16), final_norm_g[None, :])
```

```python
import numpy as np
import jax
import jax.numpy as jnp
from jax import lax
from jax.experimental import pallas as pl
from jax.experimental.pallas import tpu as pltpu

D_MODEL = 1024
GRID_W = 64
D_HGRN = 512
HGRN_HEADS = 4
HEAD_DIM = D_HGRN // HGRN_HEADS
D_CONV = 512
CONV_WIDTH = 31
CONV_HALF = CONV_WIDTH // 2
EPS = 1e-6

LANES = 128
SUBLANES = 8
SCAN_CHUNK = 64
SCAN_LEVELS = SCAN_CHUNK.bit_length() - 1
SCAN_GROUP = 16
SCAN_OUT_ROWS = 256
SCAN_STAGGER = 1
PROJ_ROWS = 512
OUT_ROWS = 1024
VMEM_LIMIT = 56 * 1024 * 1024

D_SCAN_IN = 4 * D_HGRN
COL_Q, COL_ZF, COL_ZB, COL_V = (i * HGRN_HEADS for i in range(4))
COL_GA, COL_U, COL_UG, COL_GB = (i * (D_HGRN // LANES) for i in range(4))


def _silu(x):
    return x * jax.nn.sigmoid(x)


def _mod_kernel(c_ref, w_ref, b_ref, o_ref):
    a = _silu(c_ref[...])
    o_ref[...] = jnp.dot(a, w_ref[...], preferred_element_type=jnp.float32,
                         precision=lax.Precision.HIGHEST) + b_ref[...]


def _modulation(cc, w_mod, b_mod):
    rows = cc.shape[0]
    n = w_mod.shape[1]
    return pl.pallas_call(
        _mod_kernel,
        grid=(n // D_MODEL,),
        in_specs=[pl.BlockSpec((rows, D_MODEL), lambda j: (0, 0)),
                  pl.BlockSpec((D_MODEL, D_MODEL), lambda j: (0, j)),
                  pl.BlockSpec((1, D_MODEL), lambda j: (0, j))],
        out_specs=pl.BlockSpec((rows, D_MODEL), lambda j: (0, j)),
        out_shape=jax.ShapeDtypeStruct((rows, n), jnp.float32),
        compiler_params=pltpu.CompilerParams(dimension_semantics=("arbitrary",),
                                             vmem_limit_bytes=VMEM_LIMIT),
        name="modulation",
    )(cc, w_mod, b_mod)


def _proj_kernel(x_ref, g_ref, sh_ref, sc_ref, *refs):
    x = x_ref[0]
    y = x * lax.rsqrt(jnp.mean(x * x, axis=-1, keepdims=True) + EPS) * g_ref[...]
    a = (y * (1.0 + sc_ref[0]) + sh_ref[0]).astype(jnp.bfloat16)
    n = len(refs) // 2
    for w_ref, o_ref in zip(refs[:n], refs[n:]):
        o_ref[0] = jnp.dot(a, w_ref[...], preferred_element_type=jnp.float32).astype(o_ref.dtype)


def _projection(x, norm_g, shift, scale, w_bf16, dtypes):
    bsz, t, _ = x.shape
    rows = min(PROJ_ROWS, t)
    width = D_SCAN_IN
    return pl.pallas_call(
        _proj_kernel,
        grid=(bsz, t // rows),
        in_specs=[pl.BlockSpec((1, rows, D_MODEL), lambda b, i: (b, i, 0)),
                  pl.BlockSpec((1, D_MODEL), lambda b, i: (0, 0)),
                  pl.BlockSpec((1, 1, D_MODEL), lambda b, i: (b, 0, 0)),
                  pl.BlockSpec((1, 1, D_MODEL), lambda b, i: (b, 0, 0))]
        + [pl.BlockSpec((D_MODEL, width), lambda b, i, j=j: (0, j)) for j in range(len(dtypes))],
        out_specs=[pl.BlockSpec((1, rows, width), lambda b, i: (b, i, 0)) for _ in dtypes],
        out_shape=[jax.ShapeDtypeStruct((bsz, t, width), dt) for dt in dtypes],
        compiler_params=pltpu.CompilerParams(dimension_semantics=("arbitrary", "arbitrary"),
                                             vmem_limit_bytes=VMEM_LIMIT),
        name="in_projection",
    )(x, norm_g, shift, scale, *([w_bf16] * len(dtypes)))


def _scan_constants():
    c = SCAN_CHUNK
    idx = np.arange(c)
    t, s = idx[:, None], idx[None, :]
    masks = [t == s]
    for lvl in range(SCAN_LEVELS):
        h = c >> (lvl + 1)
        masks.append(((t // (2 * h)) == (s // (2 * h))) & ((t % (2 * h)) >= h) & ((s % (2 * h)) < h))
    m_f = np.stack([m.astype(np.float32) for m in masks])
    m_b = m_f[:, ::-1, ::-1].copy()
    tri_f = (s <= t).astype(np.float32)
    tri_b = (s >= t).astype(np.float32)
    return np.concatenate([tri_f] * 3, axis=1), np.concatenate([tri_b] * 3, axis=1), m_f, m_b


def _dot_nt(a, b):
    return lax.dot_general(a, b, (((1,), (1,)), ((), ())), preferred_element_type=jnp.float32)


def _dot_tn(a, b):
    return lax.dot_general(a, b, (((0,), (0,)), ((), ())), preferred_element_type=jnp.float32)


def _interleave(*generators):
    pending = list(generators)
    while pending:
        for gen in list(pending):
            try:
                next(gen)
            except StopIteration:
                pending.remove(gen)


def _mix_rows(q, k, half, reverse):
    parts = []
    for lo in range(0, SCAN_CHUNK, 2 * half):
        first, second = (q, k) if reverse else (k, q)
        parts += [first[lo:lo + half], second[lo + half:lo + 2 * half]]
    return jnp.concatenate(parts, axis=0)


def _level_decay(g_cum, f, half, reverse):
    c = SCAN_CHUNK
    if half >= SUBLANES:
        parts = []
        for lo in range(0, c, 2 * half):
            mid = lo + half
            if reverse:
                parts += [g_cum[lo:mid] - g_cum[mid:mid + 1], g_cum[mid:mid + 1] - g_cum[mid:mid + half]]
            else:
                parts += [g_cum[mid - 1:mid] - g_cum[lo:mid], g_cum[mid:mid + half] - g_cum[mid - 1:mid]]
        return jnp.exp2(jnp.concatenate(parts, axis=0))
    if half == SUBLANES // 2:
        g3 = g_cum.reshape(c // SUBLANES, SUBLANES, LANES)
        r = half if reverse else half - 1
        later = lax.broadcasted_iota(jnp.int32, g3.shape, 1) >= half
        sign = jnp.where(later != reverse, 1.0, -1.0)
        return jnp.exp2((g3 - g3[:, r:r + 1, :]) * sign).reshape(c, LANES)
    f3 = f.reshape(c // SUBLANES, SUBLANES, LANES)
    row = lax.broadcasted_iota(jnp.int32, f3.shape, 1)
    if half == 1:
        on_query_side = (row % 2 == 0) if reverse else (row % 2 == 1)
        return jnp.where(on_query_side, f3, 1.0).reshape(c, LANES)
    prev = pltpu.roll(f3, 1, 1)
    nxt = pltpu.roll(f3, SUBLANES - 1, 1)
    m4 = row % 4
    if reverse:
        d = jnp.where(m4 == 0, f3 * nxt, jnp.where(m4 == 1, f3, jnp.where(m4 == 2, 1.0, prev)))
    else:
        d = jnp.where(m4 == 0, nxt, jnp.where(m4 == 1, 1.0, jnp.where(m4 == 2, f3, f3 * prev)))
    return d.reshape(c, LANES)


def _chunk_local(q, k, v, f, g_cum, m_ref, reverse, result):
    c = SCAN_CHUNK
    bf = jnp.bfloat16
    end_row = 0 if reverse else c - 1

    qb, kb = q.astype(bf), k.astype(bf)
    a = m_ref[0] * _dot_nt(qb, kb)
    yield
    for half in (SUBLANES, 2, 1) + tuple(h for h in (c >> (lvl + 1) for lvl in range(SCAN_LEVELS))
                                         if h not in (SUBLANES, 2, 1)):
        d = _level_decay(g_cum, f, half, reverse).astype(bf)
        if half >= 2 * SUBLANES:
            x = _mix_rows(qb, kb, half, reverse) * d
            p = _dot_nt(x, x)
        elif half == SUBLANES:
            x = (_mix_rows(q, k, half, reverse)).astype(bf) * d
            p = _dot_nt(x, x)
        elif half == 1:
            p = _dot_nt(qb * d, kb)
        else:
            p = _dot_nt(qb * d, kb * d)
        a = a + m_ref[SCAN_LEVELS - half.bit_length() + 1] * p
        yield
    d_read = jnp.exp2(g_cum)
    d_state = jnp.exp2(g_cum[end_row:end_row + 1] - g_cum)
    vb = v.astype(bf)
    o_intra = jnp.dot(a.astype(bf), vb, preferred_element_type=jnp.float32)
    yield
    kv = _dot_tn(vb, kb * d_state.astype(bf))
    result.extend([o_intra, kv, qb * d_read.astype(bf), d_read[end_row:end_row + 1]])
    yield


def _chunk_pair(q_r, z_r, v_r, starts, lb, w_ref, m_ref, reverse, results):
    c = SCAN_CHUNK
    bf = jnp.bfloat16
    fs, g3s = [], []
    for s in starts:
        f = lb + (1.0 - lb) * jax.nn.sigmoid(z_r[0, pl.ds(s, c), :])
        g = jnp.log2(f)
        fs.append(f)
        g_hi = g.astype(bf)
        r1 = g - g_hi.astype(jnp.float32)
        g_mid = r1.astype(bf)
        g_lo = (r1 - g_mid.astype(jnp.float32)).astype(bf)
        g3s.append(jnp.concatenate([g_hi, g_mid, g_lo], axis=0))
        yield
    g_cum = jnp.dot(w_ref[...], jnp.concatenate(g3s, axis=1), preferred_element_type=jnp.float32)
    yield
    chains = [_chunk_local(q_r[0, pl.ds(s, c), :], 1.0 - fs[j], v_r[0, pl.ds(s, c), :], fs[j],
                           g_cum[:, j * LANES:(j + 1) * LANES], m_ref, reverse, results[j])
              for j, s in enumerate(starts)]
    while chains:
        for chain in list(chains):
            try:
                next(chain)
            except StopIteration:
                chains.remove(chain)
        yield


def _direction_chunks(q_r, z_r, v_r, starts, lb, w_ref, m_ref, reverse):
    results = [[] for _ in starts]
    gens = [_chunk_pair(q_r, z_r, v_r, starts[j:j + 2], lb, w_ref, m_ref, reverse, results[j:j + 2])
            for j in range(0, len(starts), 2)]
    return gens, results


def _direction_states(starts, results, st_ref, o_r):
    c = SCAN_CHUNK
    st = st_ref[...]
    for s, result in zip(starts, results):
        while not result:
            yield
        o_intra, kv, q_read, d_end = result
        if o_r is not None:
            o_r[pl.ds(s, c), :] = o_intra + _dot_nt(q_read, st.astype(jnp.bfloat16))
        st = st * d_end + kv
        yield
    st_ref[...] = st


def _delayed(generator, rounds):
    for _ in range(rounds):
        yield
    yield from generator


def _scan_kernel(q_ref, zf_ref, zb_ref, v_ref, ga_ref, qc_ref, zfc_ref, zbc_ref, vc_ref,
                 lbl_ref, gn_ref, wf_ref, wb_ref, mf_ref, mb_ref, o_ref,
                 stf_ref, stb_ref, of_ref, ob_ref):
    c = SCAN_CHUNK
    t_lat = q_ref.shape[1]
    t_ctx = qc_ref.shape[1]
    l0, l1 = lbl_ref[0], lbl_ref[1]
    mx = jnp.maximum(l0, l1)
    e0, e1 = jnp.exp(l0 - mx), jnp.exp(l1 - mx)
    lb = e0 / (e0 + e1)
    lb_f, lb_b = lb[0:1], lb[1:2]

    stf_ref[...] = jnp.zeros_like(stf_ref)
    stb_ref[...] = jnp.zeros_like(stb_ref)

    def both_directions(i, t, group, q_r, zf_r, zb_r, v_r, of_r, ob_r):
        span = group * c
        lo = pl.multiple_of(i * span, span)
        hi = pl.multiple_of(t - span - i * span, span)
        starts_f = [pl.multiple_of(lo + j * c, c) for j in range(group)]
        starts_b = [pl.multiple_of(hi + j * c, c) for j in reversed(range(group))]
        gens_f, res_f = _direction_chunks(q_r, zf_r, v_r, starts_f, lb_f, wf_ref, mf_ref, False)
        gens_b, res_b = _direction_chunks(q_r, zb_r, v_r, starts_b, lb_b, wb_ref, mb_ref, True)
        staggered = [_delayed(g, SCAN_STAGGER * j) for j, pair in enumerate(zip(gens_f, gens_b)) for g in pair]
        _interleave(*staggered,
                    _direction_states(starts_f, res_f, stf_ref, of_r),
                    _direction_states(starts_b, res_b, stb_ref, ob_r))

    ctx_group = min(SCAN_GROUP, t_ctx // c)

    def ctx_body(i, carry):
        both_directions(i, t_ctx, ctx_group, qc_ref, zfc_ref, zbc_ref, vc_ref, None, None)
        return carry

    lax.fori_loop(0, t_ctx // (ctx_group * c), ctx_body, 0)

    def lat_body(i, carry):
        both_directions(i, t_lat, SCAN_GROUP, q_ref, zf_ref, zb_ref, v_ref, of_ref, ob_ref)
        return carry

    lax.fori_loop(0, t_lat // (SCAN_GROUP * c), lat_body, 0)

    rows = SCAN_OUT_ROWS
    gn = gn_ref[...]

    def fin_body(i, carry):
        lo = pl.multiple_of(i * rows, rows)
        o = of_ref[pl.ds(lo, rows), :] + ob_ref[pl.ds(lo, rows), :]
        o = o * lax.rsqrt(jnp.mean(o * o, axis=-1, keepdims=True) + EPS) * gn
        gate = _silu(ga_ref[0, pl.ds(lo, rows), :].astype(jnp.float32))
        o_ref[0, pl.ds(lo, rows), :] = (o * gate).astype(o_ref.dtype)
        return carry

    lax.fori_loop(0, t_lat // rows, fin_body, 0, unroll=4)


def _hgrn_scan(p_lat, aux_lat, p_ctx, lb_logits, hgrn_norm_g):
    bsz, t_lat, _ = p_lat.shape
    t_ctx = p_ctx.shape[1]
    w_f3, w_b3, m_f, m_b = _scan_constants()
    wf = jnp.asarray(w_f3, jnp.bfloat16)
    wb = jnp.asarray(w_b3, jnp.bfloat16)
    mf = jnp.asarray(m_f, jnp.float32)
    mb = jnp.asarray(m_b, jnp.float32)

    def col(t, base):
        return pl.BlockSpec((1, t, LANES), lambda b, h, base=base: (b, 0, base + h))

    def whole(a):
        return pl.BlockSpec(a.shape, lambda b, h, nd=a.ndim: (0,) * nd)

    return pl.pallas_call(
        _scan_kernel,
        grid=(bsz, HGRN_HEADS),
        in_specs=[col(t_lat, COL_Q), col(t_lat, COL_ZF), col(t_lat, COL_ZB), col(t_lat, COL_V),
                  col(t_lat, COL_GA),
                  col(t_ctx, COL_Q), col(t_ctx, COL_ZF), col(t_ctx, COL_ZB), col(t_ctx, COL_V),
                  pl.BlockSpec((2, 2, LANES), lambda b, h: (0, 0, h)),
                  pl.BlockSpec((1, LANES), lambda b, h: (0, h)),
                  whole(wf), whole(wb), whole(mf), whole(mb)],
        out_specs=pl.BlockSpec((1, t_lat, LANES), lambda b, h: (b, 0, h)),
        out_shape=jax.ShapeDtypeStruct((bsz, t_lat, D_HGRN), jnp.bfloat16),
        scratch_shapes=[pltpu.VMEM((HEAD_DIM, HEAD_DIM), jnp.float32),
                        pltpu.VMEM((HEAD_DIM, HEAD_DIM), jnp.float32),
                        pltpu.VMEM((t_lat, HEAD_DIM), jnp.float32),
                        pltpu.VMEM((t_lat, HEAD_DIM), jnp.float32)],
        compiler_params=pltpu.CompilerParams(dimension_semantics=("arbitrary", "arbitrary"),
                                             vmem_limit_bytes=VMEM_LIMIT),
        name="hgrn_scan",
    )(p_lat, p_lat, p_lat, p_lat, aux_lat, p_ctx, p_ctx, p_ctx, p_ctx,
      lb_logits, hgrn_norm_g, wf, wb, mf, mb)


CONV_PADW = GRID_W + 32


def _conv_fill(u_ref, ug_ref, pad_ref, along_rows):
    t = u_ref.shape[1]
    n_rows = t // GRID_W

    @pl.when(along_rows)
    def _along_rows():
        padw = CONV_PADW

        def fill(r, carry):
            src = pl.multiple_of(r * GRID_W, GRID_W)
            dst = pl.multiple_of(r * padw, 32)
            glu = (u_ref[0, pl.ds(src, GRID_W), :].astype(jnp.float32)
                   * jax.nn.sigmoid(ug_ref[0, pl.ds(src, GRID_W), :].astype(jnp.float32)))
            pad_ref[pl.ds(dst, 16), :] = jnp.zeros((16, LANES), jnp.float32)
            pad_ref[pl.ds(dst + 16, GRID_W), :] = glu
            pad_ref[pl.ds(dst + 16 + GRID_W, 16), :] = jnp.zeros((16, LANES), jnp.float32)
            return carry

        lax.fori_loop(0, n_rows, fill, 0, unroll=4)

    @pl.when(jnp.logical_not(along_rows))
    def _along_cols():
        halo = CONV_HALF * GRID_W
        pad_ref[pl.ds(0, halo), :] = jnp.zeros((halo, LANES), jnp.float32)
        pad_ref[pl.ds(halo + t, halo), :] = jnp.zeros((halo, LANES), jnp.float32)

        def fill(r, carry):
            src = pl.multiple_of(r * GRID_W, GRID_W)
            glu = (u_ref[0, pl.ds(src, GRID_W), :].astype(jnp.float32)
                   * jax.nn.sigmoid(ug_ref[0, pl.ds(src, GRID_W), :].astype(jnp.float32)))
            pad_ref[pl.ds(halo + src, GRID_W), :] = glu
            return carry

        lax.fori_loop(0, n_rows, fill, 0, unroll=4)


def _conv_kernel(u_ref, ug_ref, w_ref, b_ref, o_ref, pad_ref):
    n_rows = u_ref.shape[1] // GRID_W
    along_rows = pl.program_id(1) < (D_CONV // 2) // LANES
    _conv_fill(u_ref, ug_ref, pad_ref, along_rows)
    bias = b_ref[...]

    def taps(first_tap, stride):
        def conv(r, carry):
            dst = pl.multiple_of(r * GRID_W, GRID_W)
            base = first_tap(r)
            acc = jnp.zeros((GRID_W, LANES), jnp.float32)
            for k in range(CONV_WIDTH):
                acc = acc + w_ref[k:k + 1, :] * pad_ref[pl.ds(base + k * stride, GRID_W), :]
            o_ref[0, pl.ds(dst, GRID_W), :] = (acc + bias).astype(o_ref.dtype)
            return carry

        lax.fori_loop(0, n_rows, conv, 0, unroll=8)

    @pl.when(along_rows)
    def _():
        taps(lambda r: r * CONV_PADW + (16 - CONV_HALF), 1)

    @pl.when(jnp.logical_not(along_rows))
    def _():
        taps(lambda r: pl.multiple_of(r * GRID_W, GRID_W), GRID_W)


def _axial_conv(aux_lat, conv_w, conv_b):
    bsz, t, _ = aux_lat.shape
    n_rows = t // GRID_W
    pad_rows = max(n_rows * CONV_PADW, t + 2 * CONV_HALF * GRID_W)
    w_pad = jnp.zeros((32, D_CONV), jnp.float32).at[:CONV_WIDTH].set(conv_w)
    return pl.pallas_call(
        _conv_kernel,
        grid=(bsz, D_CONV // LANES),
        in_specs=[pl.BlockSpec((1, t, LANES), lambda b, g: (b, 0, COL_U + g)),
                  pl.BlockSpec((1, t, LANES), lambda b, g: (b, 0, COL_UG + g)),
                  pl.BlockSpec((32, LANES), lambda b, g: (0, g)),
                  pl.BlockSpec((1, LANES), lambda b, g: (0, g))],
        out_specs=pl.BlockSpec((1, t, LANES), lambda b, g: (b, 0, g)),
        out_shape=jax.ShapeDtypeStruct((bsz, t, D_CONV), jnp.bfloat16),
        scratch_shapes=[pltpu.VMEM((pad_rows, LANES), jnp.float32)],
        compiler_params=pltpu.CompilerParams(dimension_semantics=("arbitrary", "arbitrary"),
                                             vmem_limit_bytes=VMEM_LIMIT),
        name="axial_conv",
    )(aux_lat, aux_lat, w_pad, conv_b)


def _out_kernel(x_ref, ba_ref, y_ref, gb_ref, gt_ref, lng_ref, lnb_ref, wa_ref, wb_ref, fg_ref, o_ref):
    y = y_ref[0].astype(jnp.float32)
    mu = jnp.mean(y, axis=-1, keepdims=True)
    yc = y - mu
    var = jnp.mean(yc * yc, axis=-1, keepdims=True)
    yn = yc * lax.rsqrt(var + EPS) * lng_ref[...] + lnb_ref[...]
    branch_b = _silu(yn) * _silu(gb_ref[0].astype(jnp.float32))
    mix = jnp.dot(ba_ref[0], wa_ref[...], preferred_element_type=jnp.float32)
    mix = mix + jnp.dot(branch_b.astype(jnp.bfloat16), wb_ref[...], preferred_element_type=jnp.float32)
    h = x_ref[0] + gt_ref[0] * mix
    o_ref[0] = h * lax.rsqrt(jnp.mean(h * h, axis=-1, keepdims=True) + EPS) * fg_ref[...]


def _output(x, branch_a, y_conv, aux_lat, gate, ln_g, ln_b, w_out_bf16, final_g):
    bsz, t, _ = x.shape
    rows = OUT_ROWS
    gb_block = COL_GB * LANES // D_CONV
    return pl.pallas_call(
        _out_kernel,
        grid=(bsz, t // rows),
        in_specs=[pl.BlockSpec((1, rows, D_MODEL), lambda b, i: (b, i, 0)),
                  pl.BlockSpec((1, rows, D_HGRN), lambda b, i: (b, i, 0)),
                  pl.BlockSpec((1, rows, D_CONV), lambda b, i: (b, i, 0)),
                  pl.BlockSpec((1, rows, D_CONV), lambda b, i: (b, i, gb_block)),
                  pl.BlockSpec((1, 1, D_MODEL), lambda b, i: (b, 0, 0)),
                  pl.BlockSpec((1, D_CONV), lambda b, i: (0, 0)),
                  pl.BlockSpec((1, D_CONV), lambda b, i: (0, 0)),
                  pl.BlockSpec((D_HGRN, D_MODEL), lambda b, i: (0, 0)),
                  pl.BlockSpec((D_CONV, D_MODEL), lambda b, i: (1, 0)),
                  pl.BlockSpec((1, D_MODEL), lambda b, i: (0, 0))],
        out_specs=pl.BlockSpec((1, rows, D_MODEL), lambda b, i: (b, i, 0)),
        out_shape=jax.ShapeDtypeStruct((bsz, t, D_MODEL), jnp.float32),
        compiler_params=pltpu.CompilerParams(dimension_semantics=("arbitrary", "arbitrary"),
                                             vmem_limit_bytes=VMEM_LIMIT),
        name="out_projection",
    )(x, branch_a, y_conv, aux_lat, gate, ln_g, ln_b, w_out_bf16, w_out_bf16, final_g)


def kernel(x, c, ctx, c_ctx, norm_g, w_mod, b_mod, w_in, lb_logits, hgrn_norm_g, conv_w, conv_b,
           conv_ln_g, conv_ln_b, w_out, final_norm_g):
    bsz, seq_len, _ = x.shape
    span = SCAN_GROUP * SCAN_CHUNK
    assert norm_g.shape[0] == 1, "single-layer block"
    assert seq_len % GRID_W == 0 and seq_len % span == 0
    assert ctx.shape[1] % (2 * SCAN_CHUNK) == 0 and (ctx.shape[1] % span == 0 or ctx.shape[1] < span)

    pad = (-(bsz + 1)) % SUBLANES
    cc = jnp.concatenate([c, c_ctx[None, :], jnp.zeros((pad, D_MODEL), c.dtype)], axis=0)
    mod = _modulation(cc, w_mod[0], b_mod)
    shift, scale, gate = (mod[:, i * D_MODEL:(i + 1) * D_MODEL] for i in range(3))
    shift_lat, scale_lat, gate_lat = (m[:bsz, None, :] for m in (shift, scale, gate))
    shift_ctx, scale_ctx = (m[bsz][None, None, :] for m in (shift, scale))

    w_in_bf16 = w_in[0].astype(jnp.bfloat16)
    assert w_in_bf16.shape[1] == 2 * D_SCAN_IN
    p_lat, aux_lat = _projection(x, norm_g, shift_lat, scale_lat, w_in_bf16, (jnp.float32, jnp.bfloat16))
    p_ctx, = _projection(ctx.reshape(1, -1, D_MODEL), norm_g, shift_ctx, scale_ctx, w_in_bf16, (jnp.float32,))
    p_ctx = p_ctx.reshape(bsz, ctx.shape[1], D_SCAN_IN)

    branch_a = _hgrn_scan(p_lat, aux_lat, p_ctx, lb_logits, hgrn_norm_g)
    y_conv = _axial_conv(aux_lat, conv_w[0], conv_b)
    return _output(x, branch_a, y_conv, aux_lat, gate_lat, conv_ln_g, conv_ln_b,
                   w_out[0].astype(jnp.bfloat16), final_norm_g[None, :])
```

```python
import numpy as np
import jax
import jax.numpy as jnp
from jax import lax
from jax.experimental import pallas as pl
from jax.experimental.pallas import tpu as pltpu

D_MODEL = 1024
GRID_W = 64
D_HGRN = 512
HGRN_HEADS = 4
HEAD_DIM = D_HGRN // HGRN_HEADS
D_CONV = 512
CONV_WIDTH = 31
CONV_HALF = CONV_WIDTH // 2
EPS = 1e-6

LANES = 128
SUBLANES = 8
SCAN_CHUNK = 64
SCAN_LEVELS = SCAN_CHUNK.bit_length() - 1
SCAN_GROUP = 32
SCAN_OUT_ROWS = 256
SCAN_STAGGER = 1
PROJ_ROWS = 512
OUT_ROWS = 1024
VMEM_LIMIT = 56 * 1024 * 1024

D_SCAN_IN = 4 * D_HGRN
COL_Q, COL_ZF, COL_ZB, COL_V = (i * HGRN_HEADS for i in range(4))
COL_GA, COL_U, COL_UG, COL_GB = (i * (D_HGRN // LANES) for i in range(4))


def _silu(x):
    return x * jax.nn.sigmoid(x)


def _mod_kernel(c_ref, w_ref, b_ref, o_ref):
    a = _silu(c_ref[...])
    o_ref[...] = jnp.dot(a, w_ref[...], preferred_element_type=jnp.float32,
                         precision=lax.Precision.HIGHEST) + b_ref[...]


def _modulation(cc, w_mod, b_mod):
    rows = cc.shape[0]
    n = w_mod.shape[1]
    return pl.pallas_call(
        _mod_kernel,
        grid=(n // D_MODEL,),
        in_specs=[pl.BlockSpec((rows, D_MODEL), lambda j: (0, 0)),
                  pl.BlockSpec((D_MODEL, D_MODEL), lambda j: (0, j)),
                  pl.BlockSpec((1, D_MODEL), lambda j: (0, j))],
        out_specs=pl.BlockSpec((rows, D_MODEL), lambda j: (0, j)),
        out_shape=jax.ShapeDtypeStruct((rows, n), jnp.float32),
        compiler_params=pltpu.CompilerParams(dimension_semantics=("arbitrary",),
                                             vmem_limit_bytes=VMEM_LIMIT),
        name="modulation",
    )(cc, w_mod, b_mod)


def _proj_kernel(x_ref, g_ref, sh_ref, sc_ref, *refs):
    x = x_ref[0]
    y = x * lax.rsqrt(jnp.mean(x * x, axis=-1, keepdims=True) + EPS) * g_ref[...]
    a = (y * (1.0 + sc_ref[0]) + sh_ref[0]).astype(jnp.bfloat16)
    n = len(refs) // 2
    for w_ref, o_ref in zip(refs[:n], refs[n:]):
        o_ref[0] = jnp.dot(a, w_ref[...], preferred_element_type=jnp.float32).astype(o_ref.dtype)


def _projection(x, norm_g, shift, scale, w_bf16, dtypes):
    bsz, t, _ = x.shape
    rows = min(PROJ_ROWS, t)
    width = D_SCAN_IN
    return pl.pallas_call(
        _proj_kernel,
        grid=(bsz, t // rows),
        in_specs=[pl.BlockSpec((1, rows, D_MODEL), lambda b, i: (b, i, 0)),
                  pl.BlockSpec((1, D_MODEL), lambda b, i: (0, 0)),
                  pl.BlockSpec((1, 1, D_MODEL), lambda b, i: (b, 0, 0)),
                  pl.BlockSpec((1, 1, D_MODEL), lambda b, i: (b, 0, 0))]
        + [pl.BlockSpec((D_MODEL, width), lambda b, i, j=j: (0, j)) for j in range(len(dtypes))],
        out_specs=[pl.BlockSpec((1, rows, width), lambda b, i: (b, i, 0)) for _ in dtypes],
        out_shape=[jax.ShapeDtypeStruct((bsz, t, width), dt) for dt in dtypes],
        compiler_params=pltpu.CompilerParams(dimension_semantics=("arbitrary", "arbitrary"),
                                             vmem_limit_bytes=VMEM_LIMIT),
        name="in_projection",
    )(x, norm_g, shift, scale, *([w_bf16] * len(dtypes)))


def _scan_constants():
    c = SCAN_CHUNK
    idx = np.arange(c)
    t, s = idx[:, None], idx[None, :]
    masks = [t == s]
    for lvl in range(SCAN_LEVELS):
        h = c >> (lvl + 1)
        masks.append(((t // (2 * h)) == (s // (2 * h))) & ((t % (2 * h)) >= h) & ((s % (2 * h)) < h))
    m_f = np.stack([m.astype(np.float32) for m in masks])
    m_b = m_f[:, ::-1, ::-1].copy()
    tri_f = (s <= t).astype(np.float32)
    tri_b = (s >= t).astype(np.float32)
    return np.concatenate([tri_f] * 3, axis=1), np.concatenate([tri_b] * 3, axis=1), m_f, m_b


def _dot_nt(a, b):
    return lax.dot_general(a, b, (((1,), (1,)), ((), ())), preferred_element_type=jnp.float32)


def _dot_tn(a, b):
    return lax.dot_general(a, b, (((0,), (0,)), ((), ())), preferred_element_type=jnp.float32)


def _interleave(*generators):
    pending = list(generators)
    while pending:
        for gen in list(pending):
            try:
                next(gen)
            except StopIteration:
                pending.remove(gen)


def _mix_rows(q, k, half, reverse):
    parts = []
    for lo in range(0, SCAN_CHUNK, 2 * half):
        first, second = (q, k) if reverse else (k, q)
        parts += [first[lo:lo + half], second[lo + half:lo + 2 * half]]
    return jnp.concatenate(parts, axis=0)


def _level_decay(g_cum, f, half, reverse):
    c = SCAN_CHUNK
    if half >= SUBLANES:
        parts = []
        for lo in range(0, c, 2 * half):
            mid = lo + half
            if reverse:
                parts += [g_cum[lo:mid] - g_cum[mid:mid + 1], g_cum[mid:mid + 1] - g_cum[mid:mid + half]]
            else:
                parts += [g_cum[mid - 1:mid] - g_cum[lo:mid], g_cum[mid:mid + half] - g_cum[mid - 1:mid]]
        return jnp.exp2(jnp.concatenate(parts, axis=0))
    if half == SUBLANES // 2:
        g3 = g_cum.reshape(c // SUBLANES, SUBLANES, LANES)
        r = half if reverse else half - 1
        later = lax.broadcasted_iota(jnp.int32, g3.shape, 1) >= half
        sign = jnp.where(later != reverse, 1.0, -1.0)
        return jnp.exp2((g3 - g3[:, r:r + 1, :]) * sign).reshape(c, LANES)
    f3 = f.reshape(c // SUBLANES, SUBLANES, LANES)
    row = lax.broadcasted_iota(jnp.int32, f3.shape, 1)
    if half == 1:
        on_query_side = (row % 2 == 0) if reverse else (row % 2 == 1)
        return jnp.where(on_query_side, f3, 1.0).reshape(c, LANES)
    prev = pltpu.roll(f3, 1, 1)
    nxt = pltpu.roll(f3, SUBLANES - 1, 1)
    m4 = row % 4
    if reverse:
        d = jnp.where(m4 == 0, f3 * nxt, jnp.where(m4 == 1, f3, jnp.where(m4 == 2, 1.0, prev)))
    else:
        d = jnp.where(m4 == 0, nxt, jnp.where(m4 == 1, 1.0, jnp.where(m4 == 2, f3, f3 * prev)))
    return d.reshape(c, LANES)


def _chunk_local(q, k, v, f, g_cum, m_ref, reverse, result):
    c = SCAN_CHUNK
    bf = jnp.bfloat16
    end_row = 0 if reverse else c - 1

    qb, kb = q.astype(bf), k.astype(bf)
    a = m_ref[0] * _dot_nt(qb, kb)
    yield
    for half in (SUBLANES, 2, 1) + tuple(h for h in (c >> (lvl + 1) for lvl in range(SCAN_LEVELS))
                                         if h not in (SUBLANES, 2, 1)):
        d = _level_decay(g_cum, f, half, reverse).astype(bf)
        if half >= 2 * SUBLANES:
            x = _mix_rows(qb, kb, half, reverse) * d
            p = _dot_nt(x, x)
        elif half == SUBLANES:
            x = (_mix_rows(q, k, half, reverse)).astype(bf) * d
            p = _dot_nt(x, x)
        elif half == 1:
            p = _dot_nt(qb * d, kb)
        else:
            p = _dot_nt(qb * d, kb * d)
        a = a + m_ref[SCAN_LEVELS - half.bit_length() + 1] * p
        yield
    d_read = jnp.exp2(g_cum)
    d_state = jnp.exp2(g_cum[end_row:end_row + 1] - g_cum)
    vb = v.astype(bf)
    o_intra = jnp.dot(a.astype(bf), vb, preferred_element_type=jnp.float32)
    yield
    kv = _dot_tn(vb, kb * d_state.astype(bf))
    result.extend([o_intra, kv, qb * d_read.astype(bf), d_read[end_row:end_row + 1]])
    yield


def _chunk_pair(q_r, z_r, v_r, starts, lb, w_ref, m_ref, reverse, results):
    c = SCAN_CHUNK
    bf = jnp.bfloat16
    fs, g3s = [], []
    for s in starts:
        f = lb + (1.0 - lb) * jax.nn.sigmoid(z_r[0, pl.ds(s, c), :])
        g = jnp.log2(f)
        fs.append(f)
        g_hi = g.astype(bf)
        r1 = g - g_hi.astype(jnp.float32)
        g_mid = r1.astype(bf)
        g_lo = (r1 - g_mid.astype(jnp.float32)).astype(bf)
        g3s.append(jnp.concatenate([g_hi, g_mid, g_lo], axis=0))
        yield
    g_cum = jnp.dot(w_ref[...], jnp.concatenate(g3s, axis=1), preferred_element_type=jnp.float32)
    yield
    chains = [_chunk_local(q_r[0, pl.ds(s, c), :], 1.0 - fs[j], v_r[0, pl.ds(s, c), :], fs[j],
                           g_cum[:, j * LANES:(j + 1) * LANES], m_ref, reverse, results[j])
              for j, s in enumerate(starts)]
    while chains:
        for chain in list(chains):
            try:
                next(chain)
            except StopIteration:
                chains.remove(chain)
        yield


def _direction_chunks(q_r, z_r, v_r, starts, lb, w_ref, m_ref, reverse):
    results = [[] for _ in starts]
    gens = [_chunk_pair(q_r, z_r, v_r, starts[j:j + 2], lb, w_ref, m_ref, reverse, results[j:j + 2])
            for j in range(0, len(starts), 2)]
    return gens, results


def _direction_states(starts, results, st_ref, o_r):
    c = SCAN_CHUNK
    st = st_ref[...]
    for s, result in zip(starts, results):
        while not result:
            yield
        o_intra, kv, q_read, d_end = result
        if o_r is not None:
            o_r[pl.ds(s, c), :] = o_intra + _dot_nt(q_read, st.astype(jnp.bfloat16))
        st = st * d_end + kv
        yield
    st_ref[...] = st


def _delayed(generator, rounds):
    for _ in range(rounds):
        yield
    yield from generator


def _scan_kernel(q_ref, zf_ref, zb_ref, v_ref, ga_ref, qc_ref, zfc_ref, zbc_ref, vc_ref,
                 lbl_ref, gn_ref, wf_ref, wb_ref, mf_ref, mb_ref, o_ref,
                 stf_ref, stb_ref, of_ref, ob_ref):
    c = SCAN_CHUNK
    t_lat = q_ref.shape[1]
    t_ctx = qc_ref.shape[1]
    l0, l1 = lbl_ref[0], lbl_ref[1]
    mx = jnp.maximum(l0, l1)
    e0, e1 = jnp.exp(l0 - mx), jnp.exp(l1 - mx)
    lb = e0 / (e0 + e1)
    lb_f, lb_b = lb[0:1], lb[1:2]

    stf_ref[...] = jnp.zeros_like(stf_ref)
    stb_ref[...] = jnp.zeros_like(stb_ref)

    def both_directions(i, t, group, q_r, zf_r, zb_r, v_r, of_r, ob_r):
        span = group * c
        lo = pl.multiple_of(i * span, span)
        hi = pl.multiple_of(t - span - i * span, span)
        starts_f = [pl.multiple_of(lo + j * c, c) for j in range(group)]
        starts_b = [pl.multiple_of(hi + j * c, c) for j in reversed(range(group))]
        gens_f, res_f = _direction_chunks(q_r, zf_r, v_r, starts_f, lb_f, wf_ref, mf_ref, False)
        gens_b, res_b = _direction_chunks(q_r, zb_r, v_r, starts_b, lb_b, wb_ref, mb_ref, True)
        staggered = [_delayed(g, SCAN_STAGGER * j) for j, pair in enumerate(zip(gens_f, gens_b)) for g in pair]
        _interleave(*staggered,
                    _direction_states(starts_f, res_f, stf_ref, of_r),
                    _direction_states(starts_b, res_b, stb_ref, ob_r))

    ctx_group = min(SCAN_GROUP, t_ctx // c)

    def ctx_body(i, carry):
        both_directions(i, t_ctx, ctx_group, qc_ref, zfc_ref, zbc_ref, vc_ref, None, None)
        return carry

    lax.fori_loop(0, t_ctx // (ctx_group * c), ctx_body, 0)

    def lat_body(i, carry):
        both_directions(i, t_lat, SCAN_GROUP, q_ref, zf_ref, zb_ref, v_ref, of_ref, ob_ref)
        return carry

    lax.fori_loop(0, t_lat // (SCAN_GROUP * c), lat_body, 0)

    rows = SCAN_OUT_ROWS
    gn = gn_ref[...]

    def fin_body(i, carry):
        lo = pl.multiple_of(i * rows, rows)
        o = of_ref[pl.ds(lo, rows), :] + ob_ref[pl.ds(lo, rows), :]
        o = o * lax.rsqrt(jnp.mean(o * o, axis=-1, keepdims=True) + EPS) * gn
        gate = _silu(ga_ref[0, pl.ds(lo, rows), :].astype(jnp.float32))
        o_ref[0, pl.ds(lo, rows), :] = (o * gate).astype(o_ref.dtype)
        return carry

    lax.fori_loop(0, t_lat // rows, fin_body, 0, unroll=4)


def _hgrn_scan(p_lat, aux_lat, p_ctx, lb_logits, hgrn_norm_g):
    bsz, t_lat, _ = p_lat.shape
    t_ctx = p_ctx.shape[1]
    w_f3, w_b3, m_f, m_b = _scan_constants()
    wf = jnp.asarray(w_f3, jnp.bfloat16)
    wb = jnp.asarray(w_b3, jnp.bfloat16)
    mf = jnp.asarray(m_f, jnp.float32)
    mb = jnp.asarray(m_b, jnp.float32)

    def col(t, base):
        return pl.BlockSpec((1, t, LANES), lambda b, h, base=base: (b, 0, base + h))

    def whole(a):
        return pl.BlockSpec(a.shape, lambda b, h, nd=a.ndim: (0,) * nd)

    return pl.pallas_call(
        _scan_kernel,
        grid=(bsz, HGRN_HEADS),
        in_specs=[col(t_lat, COL_Q), col(t_lat, COL_ZF), col(t_lat, COL_ZB), col(t_lat, COL_V),
                  col(t_lat, COL_GA),
                  col(t_ctx, COL_Q), col(t_ctx, COL_ZF), col(t_ctx, COL_ZB), col(t_ctx, COL_V),
                  pl.BlockSpec((2, 2, LANES), lambda b, h: (0, 0, h)),
                  pl.BlockSpec((1, LANES), lambda b, h: (0, h)),
                  whole(wf), whole(wb), whole(mf), whole(mb)],
        out_specs=pl.BlockSpec((1, t_lat, LANES), lambda b, h: (b, 0, h)),
        out_shape=jax.ShapeDtypeStruct((bsz, t_lat, D_HGRN), jnp.bfloat16),
        scratch_shapes=[pltpu.VMEM((HEAD_DIM, HEAD_DIM), jnp.float32),
                        pltpu.VMEM((HEAD_DIM, HEAD_DIM), jnp.float32),
                        pltpu.VMEM((t_lat, HEAD_DIM), jnp.float32),
                        pltpu.VMEM((t_lat, HEAD_DIM), jnp.float32)],
        compiler_params=pltpu.CompilerParams(dimension_semantics=("arbitrary", "arbitrary"),
                                             vmem_limit_bytes=VMEM_LIMIT),
        name="hgrn_scan",
    )(p_lat, p_lat, p_lat, p_lat, aux_lat, p_ctx, p_ctx, p_ctx, p_ctx,
      lb_logits, hgrn_norm_g, wf, wb, mf, mb)


CONV_PADW = GRID_W + 32


def _conv_fill(u_ref, ug_ref, pad_ref, along_rows):
    t = u_ref.shape[1]
    n_rows = t // GRID_W

    @pl.when(along_rows)
    def _along_rows():
        padw = CONV_PADW

        def fill(r, carry):
            src = pl.multiple_of(r * GRID_W, GRID_W)
            dst = pl.multiple_of(r * padw, 32)
            glu = (u_ref[0, pl.ds(src, GRID_W), :].astype(jnp.float32)
                   * jax.nn.sigmoid(ug_ref[0, pl.ds(src, GRID_W), :].astype(jnp.float32)))
            pad_ref[pl.ds(dst, 16), :] = jnp.zeros((16, LANES), jnp.float32)
            pad_ref[pl.ds(dst + 16, GRID_W), :] = glu
            pad_ref[pl.ds(dst + 16 + GRID_W, 16), :] = jnp.zeros((16, LANES), jnp.float32)
            return carry

        lax.fori_loop(0, n_rows, fill, 0, unroll=4)

    @pl.when(jnp.logical_not(along_rows))
    def _along_cols():
        halo = CONV_HALF * GRID_W
        pad_ref[pl.ds(0, halo), :] = jnp.zeros((halo, LANES), jnp.float32)
        pad_ref[pl.ds(halo + t, halo), :] = jnp.zeros((halo, LANES), jnp.float32)

        def fill(r, carry):
            src = pl.multiple_of(r * GRID_W, GRID_W)
            glu = (u_ref[0, pl.ds(src, GRID_W), :].astype(jnp.float32)
                   * jax.nn.sigmoid(ug_ref[0, pl.ds(src, GRID_W), :].astype(jnp.float32)))
            pad_ref[pl.ds(halo + src, GRID_W), :] = glu
            return carry

        lax.fori_loop(0, n_rows, fill, 0, unroll=4)


def _conv_kernel(u_ref, ug_ref, w_ref, b_ref, o_ref, pad_ref):
    n_rows = u_ref.shape[1] // GRID_W
    along_rows = pl.program_id(1) < (D_CONV // 2) // LANES
    _conv_fill(u_ref, ug_ref, pad_ref, along_rows)
    bias = b_ref[...]

    def taps(first_tap, stride):
        def conv(r, carry):
            dst = pl.multiple_of(r * GRID_W, GRID_W)
            base = first_tap(r)
            acc = jnp.zeros((GRID_W, LANES), jnp.float32)
            for k in range(CONV_WIDTH):
                acc = acc + w_ref[k:k + 1, :] * pad_ref[pl.ds(base + k * stride, GRID_W), :]
            o_ref[0, pl.ds(dst, GRID_W), :] = (acc + bias).astype(o_ref.dtype)
            return carry

        lax.fori_loop(0, n_rows, conv, 0, unroll=8)

    @pl.when(along_rows)
    def _():
        taps(lambda r: r * CONV_PADW + (16 - CONV_HALF), 1)

    @pl.when(jnp.logical_not(along_rows))
    def _():
        taps(lambda r: pl.multiple_of(r * GRID_W, GRID_W), GRID_W)


def _axial_conv(aux_lat, conv_w, conv_b):
    bsz, t, _ = aux_lat.shape
    n_rows = t // GRID_W
    pad_rows = max(n_rows * CONV_PADW, t + 2 * CONV_HALF * GRID_W)
    w_pad = jnp.zeros((32, D_CONV), jnp.float32).at[:CONV_WIDTH].set(conv_w)
    return pl.pallas_call(
        _conv_kernel,
        grid=(bsz, D_CONV // LANES),
        in_specs=[pl.BlockSpec((1, t, LANES), lambda b, g: (b, 0, COL_U + g)),
                  pl.BlockSpec((1, t, LANES), lambda b, g: (b, 0, COL_UG + g)),
                  pl.BlockSpec((32, LANES), lambda b, g: (0, g)),
                  pl.BlockSpec((1, LANES), lambda b, g: (0, g))],
        out_specs=pl.BlockSpec((1, t, LANES), lambda b, g: (b, 0, g)),
        out_shape=jax.ShapeDtypeStruct((bsz, t, D_CONV), jnp.bfloat16),
        scratch_shapes=[pltpu.VMEM((pad_rows, LANES), jnp.float32)],
        compiler_params=pltpu.CompilerParams(dimension_semantics=("arbitrary", "arbitrary"),
                                             vmem_limit_bytes=VMEM_LIMIT),
        name="axial_conv",
    )(aux_lat, aux_lat, w_pad, conv_b)


def _out_kernel(x_ref, ba_ref, y_ref, gb_ref, gt_ref, lng_ref, lnb_ref, wa_ref, wb_ref, fg_ref, o_ref):
    y = y_ref[0].astype(jnp.float32)
    mu = jnp.mean(y, axis=-1, keepdims=True)
    yc = y - mu
    var = jnp.mean(yc * yc, axis=-1, keepdims=True)
    yn = yc * lax.rsqrt(var + EPS) * lng_ref[...] + lnb_ref[...]
    branch_b = _silu(yn) * _silu(gb_ref[0].astype(jnp.float32))
    mix = jnp.dot(ba_ref[0], wa_ref[...], preferred_element_type=jnp.float32)
    mix = mix + jnp.dot(branch_b.astype(jnp.bfloat16), wb_ref[...], preferred_element_type=jnp.float32)
    h = x_ref[0] + gt_ref[0] * mix
    o_ref[0] = h * lax.rsqrt(jnp.mean(h * h, axis=-1, keepdims=True) + EPS) * fg_ref[...]


def _output(x, branch_a, y_conv, aux_lat, gate, ln_g, ln_b, w_out_bf16, final_g):
    bsz, t, _ = x.shape
    rows = OUT_ROWS
    gb_block = COL_GB * LANES // D_CONV
    return pl.pallas_call(
        _out_kernel,
        grid=(bsz, t // rows),
        in_specs=[pl.BlockSpec((1, rows, D_MODEL), lambda b, i: (b, i, 0)),
                  pl.BlockSpec((1, rows, D_HGRN), lambda b, i: (b, i, 0)),
                  pl.BlockSpec((1, rows, D_CONV), lambda b, i: (b, i, 0)),
                  pl.BlockSpec((1, rows, D_CONV), lambda b, i: (b, i, gb_block)),
                  pl.BlockSpec((1, 1, D_MODEL), lambda b, i: (b, 0, 0)),
                  pl.BlockSpec((1, D_CONV), lambda b, i: (0, 0)),
                  pl.BlockSpec((1, D_CONV), lambda b, i: (0, 0)),
                  pl.BlockSpec((D_HGRN, D_MODEL), lambda b, i: (0, 0)),
                  pl.BlockSpec((D_CONV, D_MODEL), lambda b, i: (1, 0)),
                  pl.BlockSpec((1, D_MODEL), lambda b, i: (0, 0))],
        out_specs=pl.BlockSpec((1, rows, D_MODEL), lambda b, i: (b, i, 0)),
        out_shape=jax.ShapeDtypeStruct((bsz, t, D_MODEL), jnp.float32),
        compiler_params=pltpu.CompilerParams(dimension_semantics=("arbitrary", "arbitrary"),
                                             vmem_limit_bytes=VMEM_LIMIT),
        name="out_projection",
    )(x, branch_a, y_conv, aux_lat, gate, ln_g, ln_b, w_out_bf16, w_out_bf16, final_g)


def kernel(x, c, ctx, c_ctx, norm_g, w_mod, b_mod, w_in, lb_logits, hgrn_norm_g, conv_w, conv_b,
           conv_ln_g, conv_ln_b, w_out, final_norm_g):
    bsz, seq_len, _ = x.shape
    span = SCAN_GROUP * SCAN_CHUNK
    assert norm_g.shape[0] == 1, "single-layer block"
    assert seq_len % GRID_W == 0 and seq_len % span == 0
    assert ctx.shape[1] % (2 * SCAN_CHUNK) == 0 and (ctx.shape[1] % span == 0 or ctx.shape[1] < span)

    pad = (-(bsz + 1)) % SUBLANES
    cc = jnp.concatenate([c, c_ctx[None, :], jnp.zeros((pad, D_MODEL), c.dtype)], axis=0)
    mod = _modulation(cc, w_mod[0], b_mod)
    shift, scale, gate = (mod[:, i * D_MODEL:(i + 1) * D_MODEL] for i in range(3))
    shift_lat, scale_lat, gate_lat = (m[:bsz, None, :] for m in (shift, scale, gate))
    shift_ctx, scale_ctx = (m[bsz][None, None, :] for m in (shift, scale))

    w_in_bf16 = w_in[0].astype(jnp.bfloat16)
    assert w_in_bf16.shape[1] == 2 * D_SCAN_IN
    p_lat, aux_lat = _projection(x, norm_g, shift_lat, scale_lat, w_in_bf16, (jnp.float32, jnp.bfloat16))
    p_ctx, = _projection(ctx.reshape(1, -1, D_MODEL), norm_g, shift_ctx, scale_ctx, w_in_bf16, (jnp.float32,))
    p_ctx = p_ctx.reshape(bsz, ctx.shape[1], D_SCAN_IN)

    branch_a = _hgrn_scan(p_lat, aux_lat, p_ctx, lb_logits, hgrn_norm_g)
    y_conv = _axial_conv(aux_lat, conv_w[0], conv_b)
    return _output(x, branch_a, y_conv, aux_lat, gate_lat, conv_ln_g, conv_ln_b,
                   w_out[0].astype(jnp.bfloat16), final_norm_g[None, :])
```

```python
import numpy as np
import jax
import jax.numpy as jnp
from jax import lax
from jax.experimental import pallas as pl
from jax.experimental.pallas import tpu as pltpu

D_MODEL = 1024
GRID_W = 64
D_HGRN = 512
HGRN_HEADS = 4
HEAD_DIM = D_HGRN // HGRN_HEADS
D_CONV = 512
CONV_WIDTH = 31
CONV_HALF = CONV_WIDTH // 2
EPS = 1e-6

LANES = 128
SUBLANES = 8
SCAN_CHUNK = 64
SCAN_LEVELS = SCAN_CHUNK.bit_length() - 1
SCAN_GROUP = 64
SCAN_OUT_ROWS = 256
SCAN_STAGGER = 1
PROJ_ROWS = 512
OUT_ROWS = 1024
VMEM_LIMIT = 56 * 1024 * 1024

D_SCAN_IN = 4 * D_HGRN
COL_Q, COL_ZF, COL_ZB, COL_V = (i * HGRN_HEADS for i in range(4))
COL_GA, COL_U, COL_UG, COL_GB = (i * (D_HGRN // LANES) for i in range(4))


def _silu(x):
    return x * jax.nn.sigmoid(x)


def _mod_kernel(c_ref, w_ref, b_ref, o_ref):
    a = _silu(c_ref[...])
    o_ref[...] = jnp.dot(a, w_ref[...], preferred_element_type=jnp.float32,
                         precision=lax.Precision.HIGHEST) + b_ref[...]


def _modulation(cc, w_mod, b_mod):
    rows = cc.shape[0]
    n = w_mod.shape[1]
    return pl.pallas_call(
        _mod_kernel,
        grid=(n // D_MODEL,),
        in_specs=[pl.BlockSpec((rows, D_MODEL), lambda j: (0, 0)),
                  pl.BlockSpec((D_MODEL, D_MODEL), lambda j: (0, j)),
                  pl.BlockSpec((1, D_MODEL), lambda j: (0, j))],
        out_specs=pl.BlockSpec((rows, D_MODEL), lambda j: (0, j)),
        out_shape=jax.ShapeDtypeStruct((rows, n), jnp.float32),
        compiler_params=pltpu.CompilerParams(dimension_semantics=("arbitrary",),
                                             vmem_limit_bytes=VMEM_LIMIT),
        name="modulation",
    )(cc, w_mod, b_mod)


def _proj_kernel(x_ref, g_ref, sh_ref, sc_ref, *refs):
    x = x_ref[0]
    y = x * lax.rsqrt(jnp.mean(x * x, axis=-1, keepdims=True) + EPS) * g_ref[...]
    a = (y * (1.0 + sc_ref[0]) + sh_ref[0]).astype(jnp.bfloat16)
    n = len(refs) // 2
    for w_ref, o_ref in zip(refs[:n], refs[n:]):
        o_ref[0] = jnp.dot(a, w_ref[...], preferred_element_type=jnp.float32).astype(o_ref.dtype)


def _projection(x, norm_g, shift, scale, w_bf16, dtypes):
    bsz, t, _ = x.shape
    rows = min(PROJ_ROWS, t)
    width = D_SCAN_IN
    return pl.pallas_call(
        _proj_kernel,
        grid=(bsz, t // rows),
        in_specs=[pl.BlockSpec((1, rows, D_MODEL), lambda b, i: (b, i, 0)),
                  pl.BlockSpec((1, D_MODEL), lambda b, i: (0, 0)),
                  pl.BlockSpec((1, 1, D_MODEL), lambda b, i: (b, 0, 0)),
                  pl.BlockSpec((1, 1, D_MODEL), lambda b, i: (b, 0, 0))]
        + [pl.BlockSpec((D_MODEL, width), lambda b, i, j=j: (0, j)) for j in range(len(dtypes))],
        out_specs=[pl.BlockSpec((1, rows, width), lambda b, i: (b, i, 0)) for _ in dtypes],
        out_shape=[jax.ShapeDtypeStruct((bsz, t, width), dt) for dt in dtypes],
        compiler_params=pltpu.CompilerParams(dimension_semantics=("arbitrary", "arbitrary"),
                                             vmem_limit_bytes=VMEM_LIMIT),
        name="in_projection",
    )(x, norm_g, shift, scale, *([w_bf16] * len(dtypes)))


def _scan_constants():
    c = SCAN_CHUNK
    idx = np.arange(c)
    t, s = idx[:, None], idx[None, :]
    masks = [t == s]
    for lvl in range(SCAN_LEVELS):
        h = c >> (lvl + 1)
        masks.append(((t // (2 * h)) == (s // (2 * h))) & ((t % (2 * h)) >= h) & ((s % (2 * h)) < h))
    m_f = np.stack([m.astype(np.float32) for m in masks])
    m_b = m_f[:, ::-1, ::-1].copy()
    tri_f = (s <= t).astype(np.float32)
    tri_b = (s >= t).astype(np.float32)
    return np.concatenate([tri_f] * 3, axis=1), np.concatenate([tri_b] * 3, axis=1), m_f, m_b


def _dot_nt(a, b):
    return lax.dot_general(a, b, (((1,), (1,)), ((), ())), preferred_element_type=jnp.float32)


def _dot_tn(a, b):
    return lax.dot_general(a, b, (((0,), (0,)), ((), ())), preferred_element_type=jnp.float32)


def _interleave(*generators):
    pending = list(generators)
    while pending:
        for gen in list(pending):
            try:
                next(gen)
            except StopIteration:
                pending.remove(gen)


def _mix_rows(q, k, half, reverse):
    parts = []
    for lo in range(0, SCAN_CHUNK, 2 * half):
        first, second = (q, k) if reverse else (k, q)
        parts += [first[lo:lo + half], second[lo + half:lo + 2 * half]]
    return jnp.concatenate(parts, axis=0)


def _level_decay(g_cum, f, half, reverse):
    c = SCAN_CHUNK
    if half >= SUBLANES:
        parts = []
        for lo in range(0, c, 2 * half):
            mid = lo + half
            if reverse:
                parts += [g_cum[lo:mid] - g_cum[mid:mid + 1], g_cum[mid:mid + 1] - g_cum[mid:mid + half]]
            else:
                parts += [g_cum[mid - 1:mid] - g_cum[lo:mid], g_cum[mid:mid + half] - g_cum[mid - 1:mid]]
        return jnp.exp2(jnp.concatenate(parts, axis=0))
    if half == SUBLANES // 2:
        g3 = g_cum.reshape(c // SUBLANES, SUBLANES, LANES)
        r = half if reverse else half - 1
        later = lax.broadcasted_iota(jnp.int32, g3.shape, 1) >= half
        sign = jnp.where(later != reverse, 1.0, -1.0)
        return jnp.exp2((g3 - g3[:, r:r + 1, :]) * sign).reshape(c, LANES)
    f3 = f.reshape(c // SUBLANES, SUBLANES, LANES)
    row = lax.broadcasted_iota(jnp.int32, f3.shape, 1)
    if half == 1:
        on_query_side = (row % 2 == 0) if reverse else (row % 2 == 1)
        return jnp.where(on_query_side, f3, 1.0).reshape(c, LANES)
    prev = pltpu.roll(f3, 1, 1)
    nxt = pltpu.roll(f3, SUBLANES - 1, 1)
    m4 = row % 4
    if reverse:
        d = jnp.where(m4 == 0, f3 * nxt, jnp.where(m4 == 1, f3, jnp.where(m4 == 2, 1.0, prev)))
    else:
        d = jnp.where(m4 == 0, nxt, jnp.where(m4 == 1, 1.0, jnp.where(m4 == 2, f3, f3 * prev)))
    return d.reshape(c, LANES)


def _chunk_local(q, k, v, f, g_cum, m_ref, reverse, result):
    c = SCAN_CHUNK
    bf = jnp.bfloat16
    end_row = 0 if reverse else c - 1

    qb, kb = q.astype(bf), k.astype(bf)
    a = m_ref[0] * _dot_nt(qb, kb)
    yield
    for half in (SUBLANES, 2, 1) + tuple(h for h in (c >> (lvl + 1) for lvl in range(SCAN_LEVELS))
                                         if h not in (SUBLANES, 2, 1)):
        d = _level_decay(g_cum, f, half, reverse).astype(bf)
        if half >= 2 * SUBLANES:
            x = _mix_rows(qb, kb, half, reverse) * d
            p = _dot_nt(x, x)
        elif half == SUBLANES:
            x = (_mix_rows(q, k, half, reverse)).astype(bf) * d
            p = _dot_nt(x, x)
        elif half == 1:
            p = _dot_nt(qb * d, kb)
        else:
            p = _dot_nt(qb * d, kb * d)
        a = a + m_ref[SCAN_LEVELS - half.bit_length() + 1] * p
        yield
    d_read = jnp.exp2(g_cum)
    d_state = jnp.exp2(g_cum[end_row:end_row + 1] - g_cum)
    vb = v.astype(bf)
    o_intra = jnp.dot(a.astype(bf), vb, preferred_element_type=jnp.float32)
    yield
    kv = _dot_tn(vb, kb * d_state.astype(bf))
    result.extend([o_intra, kv, qb * d_read.astype(bf), d_read[end_row:end_row + 1]])
    yield


def _chunk_pair(q_r, z_r, v_r, starts, lb, w_ref, m_ref, reverse, results):
    c = SCAN_CHUNK
    bf = jnp.bfloat16
    fs, g3s = [], []
    for s in starts:
        f = lb + (1.0 - lb) * jax.nn.sigmoid(z_r[0, pl.ds(s, c), :])
        g = jnp.log2(f)
        fs.append(f)
        g_hi = g.astype(bf)
        r1 = g - g_hi.astype(jnp.float32)
        g_mid = r1.astype(bf)
        g_lo = (r1 - g_mid.astype(jnp.float32)).astype(bf)
        g3s.append(jnp.concatenate([g_hi, g_mid, g_lo], axis=0))
        yield
    g_cum = jnp.dot(w_ref[...], jnp.concatenate(g3s, axis=1), preferred_element_type=jnp.float32)
    yield
    chains = [_chunk_local(q_r[0, pl.ds(s, c), :], 1.0 - fs[j], v_r[0, pl.ds(s, c), :], fs[j],
                           g_cum[:, j * LANES:(j + 1) * LANES], m_ref, reverse, results[j])
              for j, s in enumerate(starts)]
    while chains:
        for chain in list(chains):
            try:
                next(chain)
            except StopIteration:
                chains.remove(chain)
        yield


def _direction_chunks(q_r, z_r, v_r, starts, lb, w_ref, m_ref, reverse):
    results = [[] for _ in starts]
    gens = [_chunk_pair(q_r, z_r, v_r, starts[j:j + 2], lb, w_ref, m_ref, reverse, results[j:j + 2])
            for j in range(0, len(starts), 2)]
    return gens, results


def _direction_states(starts, results, st_ref, o_r):
    c = SCAN_CHUNK
    st = st_ref[...]
    for s, result in zip(starts, results):
        while not result:
            yield
        o_intra, kv, q_read, d_end = result
        if o_r is not None:
            o_r[pl.ds(s, c), :] = o_intra + _dot_nt(q_read, st.astype(jnp.bfloat16))
        st = st * d_end + kv
        yield
    st_ref[...] = st


def _delayed(generator, rounds):
    for _ in range(rounds):
        yield
    yield from generator


def _scan_kernel(q_ref, zf_ref, zb_ref, v_ref, ga_ref, qc_ref, zfc_ref, zbc_ref, vc_ref,
                 lbl_ref, gn_ref, wf_ref, wb_ref, mf_ref, mb_ref, o_ref,
                 stf_ref, stb_ref, of_ref, ob_ref):
    c = SCAN_CHUNK
    t_lat = q_ref.shape[1]
    t_ctx = qc_ref.shape[1]
    l0, l1 = lbl_ref[0], lbl_ref[1]
    mx = jnp.maximum(l0, l1)
    e0, e1 = jnp.exp(l0 - mx), jnp.exp(l1 - mx)
    lb = e0 / (e0 + e1)
    lb_f, lb_b = lb[0:1], lb[1:2]

    stf_ref[...] = jnp.zeros_like(stf_ref)
    stb_ref[...] = jnp.zeros_like(stb_ref)

    def both_directions(i, t, group, q_r, zf_r, zb_r, v_r, of_r, ob_r):
        span = group * c
        lo = pl.multiple_of(i * span, span)
        hi = pl.multiple_of(t - span - i * span, span)
        starts_f = [pl.multiple_of(lo + j * c, c) for j in range(group)]
        starts_b = [pl.multiple_of(hi + j * c, c) for j in reversed(range(group))]
        gens_f, res_f = _direction_chunks(q_r, zf_r, v_r, starts_f, lb_f, wf_ref, mf_ref, False)
        gens_b, res_b = _direction_chunks(q_r, zb_r, v_r, starts_b, lb_b, wb_ref, mb_ref, True)
        staggered = [_delayed(g, SCAN_STAGGER * j) for j, pair in enumerate(zip(gens_f, gens_b)) for g in pair]
        _interleave(*staggered,
                    _direction_states(starts_f, res_f, stf_ref, of_r),
                    _direction_states(starts_b, res_b, stb_ref, ob_r))

    ctx_group = min(SCAN_GROUP, t_ctx // c)

    def ctx_body(i, carry):
        both_directions(i, t_ctx, ctx_group, qc_ref, zfc_ref, zbc_ref, vc_ref, None, None)
        return carry

    lax.fori_loop(0, t_ctx // (ctx_group * c), ctx_body, 0)

    def lat_body(i, carry):
        both_directions(i, t_lat, SCAN_GROUP, q_ref, zf_ref, zb_ref, v_ref, of_ref, ob_ref)
        return carry

    lax.fori_loop(0, t_lat // (SCAN_GROUP * c), lat_body, 0)

    rows = SCAN_OUT_ROWS
    gn = gn_ref[...]

    def fin_body(i, carry):
        lo = pl.multiple_of(i * rows, rows)
        o = of_ref[pl.ds(lo, rows), :] + ob_ref[pl.ds(lo, rows), :]
        o = o * lax.rsqrt(jnp.mean(o * o, axis=-1, keepdims=True) + EPS) * gn
        gate = _silu(ga_ref[0, pl.ds(lo, rows), :].astype(jnp.float32))
        o_ref[0, pl.ds(lo, rows), :] = (o * gate).astype(o_ref.dtype)
        return carry

    lax.fori_loop(0, t_lat // rows, fin_body, 0, unroll=4)


def _hgrn_scan(p_lat, aux_lat, p_ctx, lb_logits, hgrn_norm_g):
    bsz, t_lat, _ = p_lat.shape
    t_ctx = p_ctx.shape[1]
    w_f3, w_b3, m_f, m_b = _scan_constants()
    wf = jnp.asarray(w_f3, jnp.bfloat16)
    wb = jnp.asarray(w_b3, jnp.bfloat16)
    mf = jnp.asarray(m_f, jnp.float32)
    mb = jnp.asarray(m_b, jnp.float32)

    def col(t, base):
        return pl.BlockSpec((1, t, LANES), lambda b, h, base=base: (b, 0, base + h))

    def whole(a):
        return pl.BlockSpec(a.shape, lambda b, h, nd=a.ndim: (0,) * nd)

    return pl.pallas_call(
        _scan_kernel,
        grid=(bsz, HGRN_HEADS),
        in_specs=[col(t_lat, COL_Q), col(t_lat, COL_ZF), col(t_lat, COL_ZB), col(t_lat, COL_V),
                  col(t_lat, COL_GA),
                  col(t_ctx, COL_Q), col(t_ctx, COL_ZF), col(t_ctx, COL_ZB), col(t_ctx, COL_V),
                  pl.BlockSpec((2, 2, LANES), lambda b, h: (0, 0, h)),
                  pl.BlockSpec((1, LANES), lambda b, h: (0, h)),
                  whole(wf), whole(wb), whole(mf), whole(mb)],
        out_specs=pl.BlockSpec((1, t_lat, LANES), lambda b, h: (b, 0, h)),
        out_shape=jax.ShapeDtypeStruct((bsz, t_lat, D_HGRN), jnp.bfloat16),
        scratch_shapes=[pltpu.VMEM((HEAD_DIM, HEAD_DIM), jnp.float32),
                        pltpu.VMEM((HEAD_DIM, HEAD_DIM), jnp.float32),
                        pltpu.VMEM((t_lat, HEAD_DIM), jnp.float32),
                        pltpu.VMEM((t_lat, HEAD_DIM), jnp.float32)],
        compiler_params=pltpu.CompilerParams(dimension_semantics=("arbitrary", "arbitrary"),
                                             vmem_limit_bytes=VMEM_LIMIT),
        name="hgrn_scan",
    )(p_lat, p_lat, p_lat, p_lat, aux_lat, p_ctx, p_ctx, p_ctx, p_ctx,
      lb_logits, hgrn_norm_g, wf, wb, mf, mb)


CONV_PADW = GRID_W + 32


def _conv_fill(u_ref, ug_ref, pad_ref, along_rows):
    t = u_ref.shape[1]
    n_rows = t // GRID_W

    @pl.when(along_rows)
    def _along_rows():
        padw = CONV_PADW

        def fill(r, carry):
            src = pl.multiple_of(r * GRID_W, GRID_W)
            dst = pl.multiple_of(r * padw, 32)
            glu = (u_ref[0, pl.ds(src, GRID_W), :].astype(jnp.float32)
                   * jax.nn.sigmoid(ug_ref[0, pl.ds(src, GRID_W), :].astype(jnp.float32)))
            pad_ref[pl.ds(dst, 16), :] = jnp.zeros((16, LANES), jnp.float32)
            pad_ref[pl.ds(dst + 16, GRID_W), :] = glu
            pad_ref[pl.ds(dst + 16 + GRID_W, 16), :] = jnp.zeros((16, LANES), jnp.float32)
            return carry

        lax.fori_loop(0, n_rows, fill, 0, unroll=4)

    @pl.when(jnp.logical_not(along_rows))
    def _along_cols():
        halo = CONV_HALF * GRID_W
        pad_ref[pl.ds(0, halo), :] = jnp.zeros((halo, LANES), jnp.float32)
        pad_ref[pl.ds(halo + t, halo), :] = jnp.zeros((halo, LANES), jnp.float32)

        def fill(r, carry):
            src = pl.multiple_of(r * GRID_W, GRID_W)
            glu = (u_ref[0, pl.ds(src, GRID_W), :].astype(jnp.float32)
                   * jax.nn.sigmoid(ug_ref[0, pl.ds(src, GRID_W), :].astype(jnp.float32)))
            pad_ref[pl.ds(halo + src, GRID_W), :] = glu
            return carry

        lax.fori_loop(0, n_rows, fill, 0, unroll=4)


def _conv_kernel(u_ref, ug_ref, w_ref, b_ref, o_ref, pad_ref):
    n_rows = u_ref.shape[1] // GRID_W
    along_rows = pl.program_id(1) < (D_CONV // 2) // LANES
    _conv_fill(u_ref, ug_ref, pad_ref, along_rows)
    bias = b_ref[...]

    def taps(first_tap, stride):
        def conv(r, carry):
            dst = pl.multiple_of(r * GRID_W, GRID_W)
            base = first_tap(r)
            acc = jnp.zeros((GRID_W, LANES), jnp.float32)
            for k in range(CONV_WIDTH):
                acc = acc + w_ref[k:k + 1, :] * pad_ref[pl.ds(base + k * stride, GRID_W), :]
            o_ref[0, pl.ds(dst, GRID_W), :] = (acc + bias).astype(o_ref.dtype)
            return carry

        lax.fori_loop(0, n_rows, conv, 0, unroll=8)

    @pl.when(along_rows)
    def _():
        taps(lambda r: r * CONV_PADW + (16 - CONV_HALF), 1)

    @pl.when(jnp.logical_not(along_rows))
    def _():
        taps(lambda r: pl.multiple_of(r * GRID_W, GRID_W), GRID_W)


def _axial_conv(aux_lat, conv_w, conv_b):
    bsz, t, _ = aux_lat.shape
    n_rows = t // GRID_W
    pad_rows = max(n_rows * CONV_PADW, t + 2 * CONV_HALF * GRID_W)
    w_pad = jnp.zeros((32, D_CONV), jnp.float32).at[:CONV_WIDTH].set(conv_w)
    return pl.pallas_call(
        _conv_kernel,
        grid=(bsz, D_CONV // LANES),
        in_specs=[pl.BlockSpec((1, t, LANES), lambda b, g: (b, 0, COL_U + g)),
                  pl.BlockSpec((1, t, LANES), lambda b, g: (b, 0, COL_UG + g)),
                  pl.BlockSpec((32, LANES), lambda b, g: (0, g)),
                  pl.BlockSpec((1, LANES), lambda b, g: (0, g))],
        out_specs=pl.BlockSpec((1, t, LANES), lambda b, g: (b, 0, g)),
        out_shape=jax.ShapeDtypeStruct((bsz, t, D_CONV), jnp.bfloat16),
        scratch_shapes=[pltpu.VMEM((pad_rows, LANES), jnp.float32)],
        compiler_params=pltpu.CompilerParams(dimension_semantics=("arbitrary", "arbitrary"),
                                             vmem_limit_bytes=VMEM_LIMIT),
        name="axial_conv",
    )(aux_lat, aux_lat, w_pad, conv_b)


def _out_kernel(x_ref, ba_ref, y_ref, gb_ref, gt_ref, lng_ref, lnb_ref, wa_ref, wb_ref, fg_ref, o_ref):
    y = y_ref[0].astype(jnp.float32)
    mu = jnp.mean(y, axis=-1, keepdims=True)
    yc = y - mu
    var = jnp.mean(yc * yc, axis=-1, keepdims=True)
    yn = yc * lax.rsqrt(var + EPS) * lng_ref[...] + lnb_ref[...]
    branch_b = _silu(yn) * _silu(gb_ref[0].astype(jnp.float32))
    mix = jnp.dot(ba_ref[0], wa_ref[...], preferred_element_type=jnp.float32)
    mix = mix + jnp.dot(branch_b.astype(jnp.bfloat16), wb_ref[...], preferred_element_type=jnp.float32)
    h = x_ref[0] + gt_ref[0] * mix
    o_ref[0] = h * lax.rsqrt(jnp.mean(h * h, axis=-1, keepdims=True) + EPS) * fg_ref[...]


def _output(x, branch_a, y_conv, aux_lat, gate, ln_g, ln_b, w_out_bf16, final_g):
    bsz, t, _ = x.shape
    rows = OUT_ROWS
    gb_block = COL_GB * LANES // D_CONV
    return pl.pallas_call(
        _out_kernel,
        grid=(bsz, t // rows),
        in_specs=[pl.BlockSpec((1, rows, D_MODEL), lambda b, i: (b, i, 0)),
                  pl.BlockSpec((1, rows, D_HGRN), lambda b, i: (b, i, 0)),
                  pl.BlockSpec((1, rows, D_CONV), lambda b, i: (b, i, 0)),
                  pl.BlockSpec((1, rows, D_CONV), lambda b, i: (b, i, gb_block)),
                  pl.BlockSpec((1, 1, D_MODEL), lambda b, i: (b, 0, 0)),
                  pl.BlockSpec((1, D_CONV), lambda b, i: (0, 0)),
                  pl.BlockSpec((1, D_CONV), lambda b, i: (0, 0)),
                  pl.BlockSpec((D_HGRN, D_MODEL), lambda b, i: (0, 0)),
                  pl.BlockSpec((D_CONV, D_MODEL), lambda b, i: (1, 0)),
                  pl.BlockSpec((1, D_MODEL), lambda b, i: (0, 0))],
        out_specs=pl.BlockSpec((1, rows, D_MODEL), lambda b, i: (b, i, 0)),
        out_shape=jax.ShapeDtypeStruct((bsz, t, D_MODEL), jnp.float32),
        compiler_params=pltpu.CompilerParams(dimension_semantics=("arbitrary", "arbitrary"),
                                             vmem_limit_bytes=VMEM_LIMIT),
        name="out_projection",
    )(x, branch_a, y_conv, aux_lat, gate, ln_g, ln_b, w_out_bf16, w_out_bf16, final_g)


def kernel(x, c, ctx, c_ctx, norm_g, w_mod, b_mod, w_in, lb_logits, hgrn_norm_g, conv_w, conv_b,
           conv_ln_g, conv_ln_b, w_out, final_norm_g):
    bsz, seq_len, _ = x.shape
    span = SCAN_GROUP * SCAN_CHUNK
    assert norm_g.shape[0] == 1, "single-layer block"
    assert seq_len % GRID_W == 0 and seq_len % span == 0
    assert ctx.shape[1] % (2 * SCAN_CHUNK) == 0 and (ctx.shape[1] % span == 0 or ctx.shape[1] < span)

    pad = (-(bsz + 1)) % SUBLANES
    cc = jnp.concatenate([c, c_ctx[None, :], jnp.zeros((pad, D_MODEL), c.dtype)], axis=0)
    mod = _modulation(cc, w_mod[0], b_mod)
    shift, scale, gate = (mod[:, i * D_MODEL:(i + 1) * D_MODEL] for i in range(3))
    shift_lat, scale_lat, gate_lat = (m[:bsz, None, :] for m in (shift, scale, gate))
    shift_ctx, scale_ctx = (m[bsz][None, None, :] for m in (shift, scale))

    w_in_bf16 = w_in[0].astype(jnp.bfloat16)
    assert w_in_bf16.shape[1] == 2 * D_SCAN_IN
    p_lat, aux_lat = _projection(x, norm_g, shift_lat, scale_lat, w_in_bf16, (jnp.float32, jnp.bfloat16))
    p_ctx, = _projection(ctx.reshape(1, -1, D_MODEL), norm_g, shift_ctx, scale_ctx, w_in_bf16, (jnp.float32,))
    p_ctx = p_ctx.reshape(bsz, ctx.shape[1], D_SCAN_IN)

    branch_a = _hgrn_scan(p_lat, aux_lat, p_ctx, lb_logits, hgrn_norm_g)
    y_conv = _axial_conv(aux_lat, conv_w[0], conv_b)
    return _output(x, branch_a, y_conv, aux_lat, gate_lat, conv_ln_g, conv_ln_b,
                   w_out[0].astype(jnp.bfloat16), final_norm_g[None, :])
```

```python
import numpy as np
import jax
import jax.numpy as jnp
from jax import lax
from jax.experimental import pallas as pl
from jax.experimental.pallas import tpu as pltpu

D_MODEL = 1024
GRID_W = 64
D_HGRN = 512
HGRN_HEADS = 4
HEAD_DIM = D_HGRN // HGRN_HEADS
D_CONV = 512
CONV_WIDTH = 31
CONV_HALF = CONV_WIDTH // 2
EPS = 1e-6

LANES = 128
SUBLANES = 8
SCAN_CHUNK = 64
SCAN_LEVELS = SCAN_CHUNK.bit_length() - 1
SCAN_GROUP = 64
SCAN_OUT_ROWS = 256
SCAN_STAGGER = 1
PROJ_ROWS = 512
OUT_ROWS = 1024
VMEM_LIMIT = 56 * 1024 * 1024

D_SCAN_IN = 4 * D_HGRN
COL_Q, COL_ZF, COL_ZB, COL_V = (i * HGRN_HEADS for i in range(4))
COL_GA, COL_U, COL_UG, COL_GB = (i * (D_HGRN // LANES) for i in range(4))


def _silu(x):
    return x * jax.nn.sigmoid(x)


def _mod_kernel(c_ref, w_ref, b_ref, o_ref):
    a = _silu(c_ref[...])
    o_ref[...] = jnp.dot(a, w_ref[...], preferred_element_type=jnp.float32,
                         precision=lax.Precision.HIGHEST) + b_ref[...]


def _modulation(cc, w_mod, b_mod):
    rows = cc.shape[0]
    n = w_mod.shape[1]
    return pl.pallas_call(
        _mod_kernel,
        grid=(n // D_MODEL,),
        in_specs=[pl.BlockSpec((rows, D_MODEL), lambda j: (0, 0)),
                  pl.BlockSpec((D_MODEL, D_MODEL), lambda j: (0, j)),
                  pl.BlockSpec((1, D_MODEL), lambda j: (0, j))],
        out_specs=pl.BlockSpec((rows, D_MODEL), lambda j: (0, j)),
        out_shape=jax.ShapeDtypeStruct((rows, n), jnp.float32),
        compiler_params=pltpu.CompilerParams(dimension_semantics=("arbitrary",),
                                             vmem_limit_bytes=VMEM_LIMIT),
        name="modulation",
    )(cc, w_mod, b_mod)


def _proj_kernel(x_ref, g_ref, sh_ref, sc_ref, *refs):
    x = x_ref[0]
    y = x * lax.rsqrt(jnp.mean(x * x, axis=-1, keepdims=True) + EPS) * g_ref[...]
    a = (y * (1.0 + sc_ref[0]) + sh_ref[0]).astype(jnp.bfloat16)
    n = len(refs) // 2
    for w_ref, o_ref in zip(refs[:n], refs[n:]):
        o_ref[0] = jnp.dot(a, w_ref[...], preferred_element_type=jnp.float32).astype(o_ref.dtype)


def _projection(x, norm_g, shift, scale, w_bf16, dtypes):
    bsz, t, _ = x.shape
    rows = min(PROJ_ROWS, t)
    width = D_SCAN_IN
    return pl.pallas_call(
        _proj_kernel,
        grid=(bsz, t // rows),
        in_specs=[pl.BlockSpec((1, rows, D_MODEL), lambda b, i: (b, i, 0)),
                  pl.BlockSpec((1, D_MODEL), lambda b, i: (0, 0)),
                  pl.BlockSpec((1, 1, D_MODEL), lambda b, i: (b, 0, 0)),
                  pl.BlockSpec((1, 1, D_MODEL), lambda b, i: (b, 0, 0))]
        + [pl.BlockSpec((D_MODEL, width), lambda b, i, j=j: (0, j)) for j in range(len(dtypes))],
        out_specs=[pl.BlockSpec((1, rows, width), lambda b, i: (b, i, 0)) for _ in dtypes],
        out_shape=[jax.ShapeDtypeStruct((bsz, t, width), dt) for dt in dtypes],
        compiler_params=pltpu.CompilerParams(dimension_semantics=("arbitrary", "arbitrary"),
                                             vmem_limit_bytes=VMEM_LIMIT),
        name="in_projection",
    )(x, norm_g, shift, scale, *([w_bf16] * len(dtypes)))


def _scan_constants():
    c = SCAN_CHUNK
    idx = np.arange(c)
    t, s = idx[:, None], idx[None, :]
    masks = [t == s]
    for lvl in range(SCAN_LEVELS):
        h = c >> (lvl + 1)
        masks.append(((t // (2 * h)) == (s // (2 * h))) & ((t % (2 * h)) >= h) & ((s % (2 * h)) < h))
    m_f = np.stack([m.astype(np.float32) for m in masks])
    m_b = m_f[:, ::-1, ::-1].copy()
    tri_f = (s <= t).astype(np.float32)
    tri_b = (s >= t).astype(np.float32)
    return np.concatenate([tri_f] * 3, axis=1), np.concatenate([tri_b] * 3, axis=1), m_f, m_b


def _dot_nt(a, b):
    return lax.dot_general(a, b, (((1,), (1,)), ((), ())), preferred_element_type=jnp.float32)


def _dot_tn(a, b):
    return lax.dot_general(a, b, (((0,), (0,)), ((), ())), preferred_element_type=jnp.float32)


def _interleave(*generators):
    pending = list(generators)
    while pending:
        for gen in list(pending):
            try:
                next(gen)
            except StopIteration:
                pending.remove(gen)


def _mix_rows(q, k, half, reverse):
    parts = []
    for lo in range(0, SCAN_CHUNK, 2 * half):
        first, second = (q, k) if reverse else (k, q)
        parts += [first[lo:lo + half], second[lo + half:lo + 2 * half]]
    return jnp.concatenate(parts, axis=0)


def _level_decay(g_cum, f, half, reverse):
    c = SCAN_CHUNK
    if half >= SUBLANES:
        parts = []
        for lo in range(0, c, 2 * half):
            mid = lo + half
            if reverse:
                parts += [g_cum[lo:mid] - g_cum[mid:mid + 1], g_cum[mid:mid + 1] - g_cum[mid:mid + half]]
            else:
                parts += [g_cum[mid - 1:mid] - g_cum[lo:mid], g_cum[mid:mid + half] - g_cum[mid - 1:mid]]
        return jnp.exp2(jnp.concatenate(parts, axis=0))
    if half == SUBLANES // 2:
        g3 = g_cum.reshape(c // SUBLANES, SUBLANES, LANES)
        r = half if reverse else half - 1
        later = lax.broadcasted_iota(jnp.int32, g3.shape, 1) >= half
        sign = jnp.where(later != reverse, 1.0, -1.0)
        return jnp.exp2((g3 - g3[:, r:r + 1, :]) * sign).reshape(c, LANES)
    f3 = f.reshape(c // SUBLANES, SUBLANES, LANES)
    row = lax.broadcasted_iota(jnp.int32, f3.shape, 1)
    if half == 1:
        on_query_side = (row % 2 == 0) if reverse else (row % 2 == 1)
        return jnp.where(on_query_side, f3, 1.0).reshape(c, LANES)
    prev = pltpu.roll(f3, 1, 1)
    nxt = pltpu.roll(f3, SUBLANES - 1, 1)
    m4 = row % 4
    if reverse:
        d = jnp.where(m4 == 0, f3 * nxt, jnp.where(m4 == 1, f3, jnp.where(m4 == 2, 1.0, prev)))
    else:
        d = jnp.where(m4 == 0, nxt, jnp.where(m4 == 1, 1.0, jnp.where(m4 == 2, f3, f3 * prev)))
    return d.reshape(c, LANES)


def _chunk_local(q, k, v, f, g_cum, m_ref, reverse, result):
    c = SCAN_CHUNK
    bf = jnp.bfloat16
    end_row = 0 if reverse else c - 1

    qb, kb = q.astype(bf), k.astype(bf)
    a = m_ref[0] * _dot_nt(qb, kb)
    yield
    for half in (SUBLANES, 2, 1) + tuple(h for h in (c >> (lvl + 1) for lvl in range(SCAN_LEVELS))
                                         if h not in (SUBLANES, 2, 1)):
        d = _level_decay(g_cum, f, half, reverse).astype(bf)
        if half >= 2 * SUBLANES:
            x = _mix_rows(qb, kb, half, reverse) * d
            p = _dot_nt(x, x)
        elif half == SUBLANES:
            x = (_mix_rows(q, k, half, reverse)).astype(bf) * d
            p = _dot_nt(x, x)
        elif half == 1:
            p = _dot_nt(qb * d, kb)
        else:
            p = _dot_nt(qb * d, kb * d)
        a = a + m_ref[SCAN_LEVELS - half.bit_length() + 1] * p
        yield
    d_read = jnp.exp2(g_cum)
    d_state = jnp.exp2(g_cum[end_row:end_row + 1] - g_cum)
    vb = v.astype(bf)
    o_intra = jnp.dot(a.astype(bf), vb, preferred_element_type=jnp.float32)
    yield
    kv = _dot_tn(vb, kb * d_state.astype(bf))
    result.extend([o_intra, kv, qb * d_read.astype(bf), d_read[end_row:end_row + 1]])
    yield


def _chunk_pair(q_r, z_r, v_r, starts, lb, w_ref, m_ref, reverse, results):
    c = SCAN_CHUNK
    bf = jnp.bfloat16
    fs, g3s = [], []
    for s in starts:
        f = lb + (1.0 - lb) * jax.nn.sigmoid(z_r[0, pl.ds(s, c), :])
        g = jnp.log2(f)
        fs.append(f)
        g_hi = g.astype(bf)
        r1 = g - g_hi.astype(jnp.float32)
        g_mid = r1.astype(bf)
        g_lo = (r1 - g_mid.astype(jnp.float32)).astype(bf)
        g3s.append(jnp.concatenate([g_hi, g_mid, g_lo], axis=0))
        yield
    g_cum = jnp.dot(w_ref[...], jnp.concatenate(g3s, axis=1), preferred_element_type=jnp.float32)
    yield
    chains = [_chunk_local(q_r[0, pl.ds(s, c), :], 1.0 - fs[j], v_r[0, pl.ds(s, c), :], fs[j],
                           g_cum[:, j * LANES:(j + 1) * LANES], m_ref, reverse, results[j])
              for j, s in enumerate(starts)]
    while chains:
        for chain in list(chains):
            try:
                next(chain)
            except StopIteration:
                chains.remove(chain)
        yield


def _direction_chunks(q_r, z_r, v_r, starts, lb, w_ref, m_ref, reverse):
    results = [[] for _ in starts]
    gens = [_chunk_pair(q_r, z_r, v_r, starts[j:j + 2], lb, w_ref, m_ref, reverse, results[j:j + 2])
            for j in range(0, len(starts), 2)]
    return gens, results


def _direction_states(starts, results, st_ref, o_r, stored):
    c = SCAN_CHUNK
    st = st_ref[...]
    for s, result in zip(starts, results):
        while not result:
            yield
        o_intra, kv, q_read, d_end = result
        if o_r is not None:
            o_r[pl.ds(s, c), :] = o_intra + _dot_nt(q_read, st.astype(jnp.bfloat16))
        stored[0] += 1
        st = st * d_end + kv
        yield
    st_ref[...] = st


def _normalise_and_gate(of_ref, ob_ref, ga_ref, gn, o_ref, lo):
    rows = pl.ds(lo, SCAN_OUT_ROWS)
    o = of_ref[rows, :] + ob_ref[rows, :]
    o = o * lax.rsqrt(jnp.mean(o * o, axis=-1, keepdims=True) + EPS) * gn
    gate = _silu(ga_ref[0, rows, :].astype(jnp.float32))
    o_ref[0, rows, :] = (o * gate).astype(o_ref.dtype)


def _output_rows(of_ref, ob_ref, ga_ref, gn, o_ref, n_chunks, stored_f, stored_b):
    per = SCAN_OUT_ROWS // SCAN_CHUNK

    def ready_after(b):
        return per * (b + 1), n_chunks - per * b

    for b in sorted(range(n_chunks // per), key=lambda b: max(ready_after(b))):
        need_f, need_b = ready_after(b)
        while stored_f[0] < need_f or stored_b[0] < need_b:
            yield
        _normalise_and_gate(of_ref, ob_ref, ga_ref, gn, o_ref, b * SCAN_OUT_ROWS)
        yield


def _delayed(generator, rounds):
    for _ in range(rounds):
        yield
    yield from generator


def _scan_kernel(q_ref, zf_ref, zb_ref, v_ref, ga_ref, qc_ref, zfc_ref, zbc_ref, vc_ref,
                 lbl_ref, gn_ref, wf_ref, wb_ref, mf_ref, mb_ref, o_ref,
                 stf_ref, stb_ref, of_ref, ob_ref):
    c = SCAN_CHUNK
    t_lat = q_ref.shape[1]
    t_ctx = qc_ref.shape[1]
    l0, l1 = lbl_ref[0], lbl_ref[1]
    mx = jnp.maximum(l0, l1)
    e0, e1 = jnp.exp(l0 - mx), jnp.exp(l1 - mx)
    lb = e0 / (e0 + e1)
    lb_f, lb_b = lb[0:1], lb[1:2]

    stf_ref[...] = jnp.zeros_like(stf_ref)
    stb_ref[...] = jnp.zeros_like(stb_ref)

    gn = gn_ref[...]

    def both_directions(i, t, group, q_r, zf_r, zb_r, v_r, of_r, ob_r, finish=False):
        span = group * c
        lo, hi = i * span, t - span - i * span
        if not isinstance(i, int):
            lo, hi = pl.multiple_of(lo, span), pl.multiple_of(hi, span)
        starts_f = [lo + j * c for j in range(group)]
        starts_b = [hi + j * c for j in reversed(range(group))]
        if not isinstance(i, int):
            starts_f = [pl.multiple_of(s, c) for s in starts_f]
            starts_b = [pl.multiple_of(s, c) for s in starts_b]
        gens_f, res_f = _direction_chunks(q_r, zf_r, v_r, starts_f, lb_f, wf_ref, mf_ref, False)
        gens_b, res_b = _direction_chunks(q_r, zb_r, v_r, starts_b, lb_b, wb_ref, mb_ref, True)
        staggered = [_delayed(g, SCAN_STAGGER * j) for j, pair in enumerate(zip(gens_f, gens_b)) for g in pair]
        stored_f, stored_b = [0], [0]
        chains = staggered + [_direction_states(starts_f, res_f, stf_ref, of_r, stored_f),
                              _direction_states(starts_b, res_b, stb_ref, ob_r, stored_b)]
        if finish:
            chains.append(_output_rows(of_ref, ob_ref, ga_ref, gn, o_ref, group, stored_f, stored_b))
        _interleave(*chains)

    ctx_group = min(SCAN_GROUP, t_ctx // c)

    def ctx_body(i, carry):
        both_directions(i, t_ctx, ctx_group, qc_ref, zfc_ref, zbc_ref, vc_ref, None, None)
        return carry

    lax.fori_loop(0, t_ctx // (ctx_group * c), ctx_body, 0)

    lat_group = min(SCAN_GROUP, t_lat // c)
    if lat_group * c == t_lat:
        both_directions(0, t_lat, lat_group, q_ref, zf_ref, zb_ref, v_ref, of_ref, ob_ref, finish=True)
    else:
        def lat_body(i, carry):
            both_directions(i, t_lat, lat_group, q_ref, zf_ref, zb_ref, v_ref, of_ref, ob_ref)
            return carry

        lax.fori_loop(0, t_lat // (lat_group * c), lat_body, 0)

        def fin_body(i, carry):
            _normalise_and_gate(of_ref, ob_ref, ga_ref, gn, o_ref, pl.multiple_of(i * SCAN_OUT_ROWS, SCAN_OUT_ROWS))
            return carry

        lax.fori_loop(0, t_lat // SCAN_OUT_ROWS, fin_body, 0, unroll=4)


def _hgrn_scan(p_lat, aux_lat, p_ctx, lb_logits, hgrn_norm_g):
    bsz, t_lat, _ = p_lat.shape
    t_ctx = p_ctx.shape[1]
    w_f3, w_b3, m_f, m_b = _scan_constants()
    wf = jnp.asarray(w_f3, jnp.bfloat16)
    wb = jnp.asarray(w_b3, jnp.bfloat16)
    mf = jnp.asarray(m_f, jnp.float32)
    mb = jnp.asarray(m_b, jnp.float32)

    def col(t, base):
        return pl.BlockSpec((1, t, LANES), lambda b, h, base=base: (b, 0, base + h))

    def whole(a):
        return pl.BlockSpec(a.shape, lambda b, h, nd=a.ndim: (0,) * nd)

    return pl.pallas_call(
        _scan_kernel,
        grid=(bsz, HGRN_HEADS),
        in_specs=[col(t_lat, COL_Q), col(t_lat, COL_ZF), col(t_lat, COL_ZB), col(t_lat, COL_V),
                  col(t_lat, COL_GA),
                  col(t_ctx, COL_Q), col(t_ctx, COL_ZF), col(t_ctx, COL_ZB), col(t_ctx, COL_V),
                  pl.BlockSpec((2, 2, LANES), lambda b, h: (0, 0, h)),
                  pl.BlockSpec((1, LANES), lambda b, h: (0, h)),
                  whole(wf), whole(wb), whole(mf), whole(mb)],
        out_specs=pl.BlockSpec((1, t_lat, LANES), lambda b, h: (b, 0, h)),
        out_shape=jax.ShapeDtypeStruct((bsz, t_lat, D_HGRN), jnp.bfloat16),
        scratch_shapes=[pltpu.VMEM((HEAD_DIM, HEAD_DIM), jnp.float32),
                        pltpu.VMEM((HEAD_DIM, HEAD_DIM), jnp.float32),
                        pltpu.VMEM((t_lat, HEAD_DIM), jnp.float32),
                        pltpu.VMEM((t_lat, HEAD_DIM), jnp.float32)],
        compiler_params=pltpu.CompilerParams(dimension_semantics=("arbitrary", "arbitrary"),
                                             vmem_limit_bytes=VMEM_LIMIT),
        name="hgrn_scan",
    )(p_lat, p_lat, p_lat, p_lat, aux_lat, p_ctx, p_ctx, p_ctx, p_ctx,
      lb_logits, hgrn_norm_g, wf, wb, mf, mb)


CONV_PADW = GRID_W + 32


def _conv_fill(u_ref, ug_ref, pad_ref, along_rows):
    t = u_ref.shape[1]
    n_rows = t // GRID_W

    @pl.when(along_rows)
    def _along_rows():
        padw = CONV_PADW

        def fill(r, carry):
            src = pl.multiple_of(r * GRID_W, GRID_W)
            dst = pl.multiple_of(r * padw, 32)
            glu = (u_ref[0, pl.ds(src, GRID_W), :].astype(jnp.float32)
                   * jax.nn.sigmoid(ug_ref[0, pl.ds(src, GRID_W), :].astype(jnp.float32)))
            pad_ref[pl.ds(dst, 16), :] = jnp.zeros((16, LANES), jnp.float32)
            pad_ref[pl.ds(dst + 16, GRID_W), :] = glu
            pad_ref[pl.ds(dst + 16 + GRID_W, 16), :] = jnp.zeros((16, LANES), jnp.float32)
            return carry

        lax.fori_loop(0, n_rows, fill, 0, unroll=4)

    @pl.when(jnp.logical_not(along_rows))
    def _along_cols():
        halo = CONV_HALF * GRID_W
        pad_ref[pl.ds(0, halo), :] = jnp.zeros((halo, LANES), jnp.float32)
        pad_ref[pl.ds(halo + t, halo), :] = jnp.zeros((halo, LANES), jnp.float32)

        def fill(r, carry):
            src = pl.multiple_of(r * GRID_W, GRID_W)
            glu = (u_ref[0, pl.ds(src, GRID_W), :].astype(jnp.float32)
                   * jax.nn.sigmoid(ug_ref[0, pl.ds(src, GRID_W), :].astype(jnp.float32)))
            pad_ref[pl.ds(halo + src, GRID_W), :] = glu
            return carry

        lax.fori_loop(0, n_rows, fill, 0, unroll=4)


def _conv_kernel(u_ref, ug_ref, w_ref, b_ref, o_ref, pad_ref):
    n_rows = u_ref.shape[1] // GRID_W
    along_rows = pl.program_id(1) < (D_CONV // 2) // LANES
    _conv_fill(u_ref, ug_ref, pad_ref, along_rows)
    bias = b_ref[...]

    def taps(first_tap, stride):
        def conv(r, carry):
            dst = pl.multiple_of(r * GRID_W, GRID_W)
            base = first_tap(r)
            acc = jnp.zeros((GRID_W, LANES), jnp.float32)
            for k in range(CONV_WIDTH):
                acc = acc + w_ref[k:k + 1, :] * pad_ref[pl.ds(base + k * stride, GRID_W), :]
            o_ref[0, pl.ds(dst, GRID_W), :] = (acc + bias).astype(o_ref.dtype)
            return carry

        lax.fori_loop(0, n_rows, conv, 0, unroll=8)

    @pl.when(along_rows)
    def _():
        taps(lambda r: r * CONV_PADW + (16 - CONV_HALF), 1)

    @pl.when(jnp.logical_not(along_rows))
    def _():
        taps(lambda r: pl.multiple_of(r * GRID_W, GRID_W), GRID_W)


def _axial_conv(aux_lat, conv_w, conv_b):
    bsz, t, _ = aux_lat.shape
    n_rows = t // GRID_W
    pad_rows = max(n_rows * CONV_PADW, t + 2 * CONV_HALF * GRID_W)
    w_pad = jnp.zeros((32, D_CONV), jnp.float32).at[:CONV_WIDTH].set(conv_w)
    return pl.pallas_call(
        _conv_kernel,
        grid=(bsz, D_CONV // LANES),
        in_specs=[pl.BlockSpec((1, t, LANES), lambda b, g: (b, 0, COL_U + g)),
                  pl.BlockSpec((1, t, LANES), lambda b, g: (b, 0, COL_UG + g)),
                  pl.BlockSpec((32, LANES), lambda b, g: (0, g)),
                  pl.BlockSpec((1, LANES), lambda b, g: (0, g))],
        out_specs=pl.BlockSpec((1, t, LANES), lambda b, g: (b, 0, g)),
        out_shape=jax.ShapeDtypeStruct((bsz, t, D_CONV), jnp.bfloat16),
        scratch_shapes=[pltpu.VMEM((pad_rows, LANES), jnp.float32)],
        compiler_params=pltpu.CompilerParams(dimension_semantics=("arbitrary", "arbitrary"),
                                             vmem_limit_bytes=VMEM_LIMIT),
        name="axial_conv",
    )(aux_lat, aux_lat, w_pad, conv_b)


def _out_kernel(x_ref, ba_ref, y_ref, gb_ref, gt_ref, lng_ref, lnb_ref, wa_ref, wb_ref, fg_ref, o_ref):
    y = y_ref[0].astype(jnp.float32)
    mu = jnp.mean(y, axis=-1, keepdims=True)
    yc = y - mu
    var = jnp.mean(yc * yc, axis=-1, keepdims=True)
    yn = yc * lax.rsqrt(var + EPS) * lng_ref[...] + lnb_ref[...]
    branch_b = _silu(yn) * _silu(gb_ref[0].astype(jnp.float32))
    mix = jnp.dot(ba_ref[0], wa_ref[...], preferred_element_type=jnp.float32)
    mix = mix + jnp.dot(branch_b.astype(jnp.bfloat16), wb_ref[...], preferred_element_type=jnp.float32)
    h = x_ref[0] + gt_ref[0] * mix
    o_ref[0] = h * lax.rsqrt(jnp.mean(h * h, axis=-1, keepdims=True) + EPS) * fg_ref[...]


def _output(x, branch_a, y_conv, aux_lat, gate, ln_g, ln_b, w_out_bf16, final_g):
    bsz, t, _ = x.shape
    rows = OUT_ROWS
    gb_block = COL_GB * LANES // D_CONV
    return pl.pallas_call(
        _out_kernel,
        grid=(bsz, t // rows),
        in_specs=[pl.BlockSpec((1, rows, D_MODEL), lambda b, i: (b, i, 0)),
                  pl.BlockSpec((1, rows, D_HGRN), lambda b, i: (b, i, 0)),
                  pl.BlockSpec((1, rows, D_CONV), lambda b, i: (b, i, 0)),
                  pl.BlockSpec((1, rows, D_CONV), lambda b, i: (b, i, gb_block)),
                  pl.BlockSpec((1, 1, D_MODEL), lambda b, i: (b, 0, 0)),
                  pl.BlockSpec((1, D_CONV), lambda b, i: (0, 0)),
                  pl.BlockSpec((1, D_CONV), lambda b, i: (0, 0)),
                  pl.BlockSpec((D_HGRN, D_MODEL), lambda b, i: (0, 0)),
                  pl.BlockSpec((D_CONV, D_MODEL), lambda b, i: (1, 0)),
                  pl.BlockSpec((1, D_MODEL), lambda b, i: (0, 0))],
        out_specs=pl.BlockSpec((1, rows, D_MODEL), lambda b, i: (b, i, 0)),
        out_shape=jax.ShapeDtypeStruct((bsz, t, D_MODEL), jnp.float32),
        compiler_params=pltpu.CompilerParams(dimension_semantics=("arbitrary", "arbitrary"),
                                             vmem_limit_bytes=VMEM_LIMIT),
        name="out_projection",
    )(x, branch_a, y_conv, aux_lat, gate, ln_g, ln_b, w_out_bf16, w_out_bf16, final_g)


def kernel(x, c, ctx, c_ctx, norm_g, w_mod, b_mod, w_in, lb_logits, hgrn_norm_g, conv_w, conv_b,
           conv_ln_g, conv_ln_b, w_out, final_norm_g):
    bsz, seq_len, _ = x.shape
    span = SCAN_GROUP * SCAN_CHUNK
    assert norm_g.shape[0] == 1, "single-layer block"
    assert seq_len % GRID_W == 0 and seq_len % span == 0
    assert ctx.shape[1] % (2 * SCAN_CHUNK) == 0 and (ctx.shape[1] % span == 0 or ctx.shape[1] < span)

    pad = (-(bsz + 1)) % SUBLANES
    cc = jnp.concatenate([c, c_ctx[None, :], jnp.zeros((pad, D_MODEL), c.dtype)], axis=0)
    mod = _modulation(cc, w_mod[0], b_mod)
    shift, scale, gate = (mod[:, i * D_MODEL:(i + 1) * D_MODEL] for i in range(3))
    shift_lat, scale_lat, gate_lat = (m[:bsz, None, :] for m in (shift, scale, gate))
    shift_ctx, scale_ctx = (m[bsz][None, None, :] for m in (shift, scale))

    w_in_bf16 = w_in[0].astype(jnp.bfloat16)
    assert w_in_bf16.shape[1] == 2 * D_SCAN_IN
    p_lat, aux_lat = _projection(x, norm_g, shift_lat, scale_lat, w_in_bf16, (jnp.float32, jnp.bfloat16))
    p_ctx, = _projection(ctx.reshape(1, -1, D_MODEL), norm_g, shift_ctx, scale_ctx, w_in_bf16, (jnp.float32,))
    p_ctx = p_ctx.reshape(bsz, ctx.shape[1], D_SCAN_IN)

    branch_a = _hgrn_scan(p_lat, aux_lat, p_ctx, lb_logits, hgrn_norm_g)
    y_conv = _axial_conv(aux_lat, conv_w[0], conv_b)
    return _output(x, branch_a, y_conv, aux_lat, gate_lat, conv_ln_g, conv_ln_b,
                   w_out[0].astype(jnp.bfloat16), final_norm_g[None, :])
```

```python
import numpy as np
import jax
import jax.numpy as jnp
from jax import lax
from jax.experimental import pallas as pl
from jax.experimental.pallas import tpu as pltpu

D_MODEL = 1024
GRID_W = 64
D_HGRN = 512
HGRN_HEADS = 4
HEAD_DIM = D_HGRN // HGRN_HEADS
D_CONV = 512
CONV_WIDTH = 31
CONV_HALF = CONV_WIDTH // 2
EPS = 1e-6

LANES = 128
SUBLANES = 8
SCAN_CHUNK = 64
SCAN_LEVELS = SCAN_CHUNK.bit_length() - 1
SCAN_OUT_ROWS = 256
SCAN_STAGGER = 1
PROJ_ROWS = 512
OUT_ROWS = 1024
VMEM_LIMIT = 56 * 1024 * 1024

D_SCAN_IN = 4 * D_HGRN
COL_Q, COL_ZF, COL_ZB, COL_V = (i * HGRN_HEADS for i in range(4))
COL_GA, COL_U, COL_UG, COL_GB = (i * (D_HGRN // LANES) for i in range(4))


def _silu(x):
    return x * jax.nn.sigmoid(x)


def _mod_kernel(c_ref, w_ref, b_ref, o_ref):
    a = _silu(c_ref[...])
    o_ref[...] = jnp.dot(a, w_ref[...], preferred_element_type=jnp.float32,
                         precision=lax.Precision.HIGHEST) + b_ref[...]


def _modulation(cc, w_mod, b_mod):
    rows = cc.shape[0]
    n = w_mod.shape[1]
    return pl.pallas_call(
        _mod_kernel,
        grid=(n // D_MODEL,),
        in_specs=[pl.BlockSpec((rows, D_MODEL), lambda j: (0, 0)),
                  pl.BlockSpec((D_MODEL, D_MODEL), lambda j: (0, j)),
                  pl.BlockSpec((1, D_MODEL), lambda j: (0, j))],
        out_specs=pl.BlockSpec((rows, D_MODEL), lambda j: (0, j)),
        out_shape=jax.ShapeDtypeStruct((rows, n), jnp.float32),
        compiler_params=pltpu.CompilerParams(dimension_semantics=("arbitrary",),
                                             vmem_limit_bytes=VMEM_LIMIT),
        name="modulation",
    )(cc, w_mod, b_mod)


def _proj_kernel(x_ref, g_ref, sh_ref, sc_ref, *refs):
    x = x_ref[0]
    y = x * lax.rsqrt(jnp.mean(x * x, axis=-1, keepdims=True) + EPS) * g_ref[...]
    a = (y * (1.0 + sc_ref[0]) + sh_ref[0]).astype(jnp.bfloat16)
    n = len(refs) // 2
    for w_ref, o_ref in zip(refs[:n], refs[n:]):
        o_ref[0] = jnp.dot(a, w_ref[...], preferred_element_type=jnp.float32).astype(o_ref.dtype)


def _projection(x, norm_g, shift, scale, w_bf16, dtypes):
    bsz, t, _ = x.shape
    rows = min(PROJ_ROWS, t)
    width = D_SCAN_IN
    return pl.pallas_call(
        _proj_kernel,
        grid=(bsz, t // rows),
        in_specs=[pl.BlockSpec((1, rows, D_MODEL), lambda b, i: (b, i, 0)),
                  pl.BlockSpec((1, D_MODEL), lambda b, i: (0, 0)),
                  pl.BlockSpec((1, 1, D_MODEL), lambda b, i: (b, 0, 0)),
                  pl.BlockSpec((1, 1, D_MODEL), lambda b, i: (b, 0, 0))]
        + [pl.BlockSpec((D_MODEL, width), lambda b, i, j=j: (0, j)) for j in range(len(dtypes))],
        out_specs=[pl.BlockSpec((1, rows, width), lambda b, i: (b, i, 0)) for _ in dtypes],
        out_shape=[jax.ShapeDtypeStruct((bsz, t, width), dt) for dt in dtypes],
        compiler_params=pltpu.CompilerParams(dimension_semantics=("arbitrary", "arbitrary"),
                                             vmem_limit_bytes=VMEM_LIMIT),
        name="in_projection",
    )(x, norm_g, shift, scale, *([w_bf16] * len(dtypes)))


def _scan_constants():
    c = SCAN_CHUNK
    idx = np.arange(c)
    t, s = idx[:, None], idx[None, :]
    masks = [t == s]
    for lvl in range(SCAN_LEVELS):
        h = c >> (lvl + 1)
        masks.append(((t // (2 * h)) == (s // (2 * h))) & ((t % (2 * h)) >= h) & ((s % (2 * h)) < h))
    m_f = np.stack([m.astype(np.float32) for m in masks])
    m_b = m_f[:, ::-1, ::-1].copy()
    tri_f = (s <= t).astype(np.float32)
    tri_b = (s >= t).astype(np.float32)
    return np.concatenate([tri_f] * 2, axis=1), np.concatenate([tri_b] * 2, axis=1), m_f, m_b


def _dot_nt(a, b):
    return lax.dot_general(a, b, (((1,), (1,)), ((), ())), preferred_element_type=jnp.float32)


def _dot_tn(a, b):
    return lax.dot_general(a, b, (((0,), (0,)), ((), ())), preferred_element_type=jnp.float32)


def _interleave(*generators):
    pending = list(generators)
    while pending:
        for gen in list(pending):
            try:
                next(gen)
            except StopIteration:
                pending.remove(gen)


def _mix_rows(q, k, half, reverse):
    parts = []
    for lo in range(0, SCAN_CHUNK, 2 * half):
        first, second = (q, k) if reverse else (k, q)
        parts += [first[lo:lo + half], second[lo + half:lo + 2 * half]]
    return jnp.concatenate(parts, axis=0)


def _level_decay(g_cum, f, half, reverse):
    c = SCAN_CHUNK
    if half >= SUBLANES:
        parts = []
        for lo in range(0, c, 2 * half):
            mid = lo + half
            if reverse:
                parts += [g_cum[lo:mid] - g_cum[mid:mid + 1], g_cum[mid:mid + 1] - g_cum[mid:mid + half]]
            else:
                parts += [g_cum[mid - 1:mid] - g_cum[lo:mid], g_cum[mid:mid + half] - g_cum[mid - 1:mid]]
        return jnp.exp2(jnp.concatenate(parts, axis=0))
    if half == SUBLANES // 2:
        g3 = g_cum.reshape(c // SUBLANES, SUBLANES, LANES)
        r = half if reverse else half - 1
        later = lax.broadcasted_iota(jnp.int32, g3.shape, 1) >= half
        sign = jnp.where(later != reverse, 1.0, -1.0)
        return jnp.exp2((g3 - g3[:, r:r + 1, :]) * sign).reshape(c, LANES)
    f3 = f.reshape(c // SUBLANES, SUBLANES, LANES)
    row = lax.broadcasted_iota(jnp.int32, f3.shape, 1)
    if half == 1:
        on_query_side = (row % 2 == 0) if reverse else (row % 2 == 1)
        return jnp.where(on_query_side, f3, 1.0).reshape(c, LANES)
    prev = pltpu.roll(f3, 1, 1)
    nxt = pltpu.roll(f3, SUBLANES - 1, 1)
    m4 = row % 4
    if reverse:
        d = jnp.where(m4 == 0, f3 * nxt, jnp.where(m4 == 1, f3, jnp.where(m4 == 2, 1.0, prev)))
    else:
        d = jnp.where(m4 == 0, nxt, jnp.where(m4 == 1, 1.0, jnp.where(m4 == 2, f3, f3 * prev)))
    return d.reshape(c, LANES)


def _chunk_local(q, k, v, f, g_cum, m_ref, reverse, result):
    c = SCAN_CHUNK
    bf = jnp.bfloat16
    end_row = 0 if reverse else c - 1

    qb, kb = q.astype(bf), k.astype(bf)
    piece = 2 * SUBLANES
    p = _dot_nt(qb, kb)
    a = [m_ref[0, r:r + piece] * p[r:r + piece] for r in range(0, c, piece)]
    yield
    for half in (SUBLANES, 2, 1) + tuple(h for h in (c >> (lvl + 1) for lvl in range(SCAN_LEVELS))
                                         if h not in (SUBLANES, 2, 1)):
        d = _level_decay(g_cum, f, half, reverse).astype(bf)
        rows = list(range(0, c, piece))
        if half >= piece:
            x = _mix_rows(qb, kb, half, reverse) * d
            firsts = [lo + (0 if reverse else half) for lo in range(0, c, 2 * half)]
            rows = [r for s in firsts for r in range(s, s + half, piece)]
            p = _dot_nt(jnp.concatenate([x[s:s + half] for s in firsts], axis=0), x)
        elif half == SUBLANES:
            x = (_mix_rows(q, k, half, reverse)).astype(bf) * d
            p = _dot_nt(x, x)
        elif half == 1:
            p = _dot_nt(qb * d, kb)
        else:
            p = _dot_nt(qb * d, kb * d)
        m_level = SCAN_LEVELS - half.bit_length() + 1
        for i, r in enumerate(rows):
            a[r // piece] = a[r // piece] + m_ref[m_level, r:r + piece] * p[i * piece:(i + 1) * piece]
        yield
    d_read = jnp.exp2(g_cum)
    d_state = jnp.exp2(g_cum[end_row:end_row + 1] - g_cum)
    vb = v.astype(bf)
    o_intra = jnp.dot(jnp.concatenate(a, axis=0).astype(bf), vb, preferred_element_type=jnp.float32)
    yield
    kv = _dot_tn(vb, kb * d_state.astype(bf))
    result.extend([o_intra, kv, qb * d_read.astype(bf), d_read[end_row:end_row + 1]])
    yield


def _chunk_pair(q_r, z_r, v_r, starts, lb, w_ref, m_ref, reverse, results):
    c = SCAN_CHUNK
    bf = jnp.bfloat16
    fs, g3s = [], []
    for s in starts:
        f = lb + (1.0 - lb) * jax.nn.sigmoid(z_r[0, pl.ds(s, c), :])
        g = jnp.log2(f)
        fs.append(f)
        g_hi = g.astype(bf)
        g_lo = (g - g_hi.astype(jnp.float32)).astype(bf)
        g3s.append(jnp.concatenate([g_hi, g_lo], axis=0))
        yield
    g_cum = jnp.dot(w_ref[...], jnp.concatenate(g3s, axis=1), preferred_element_type=jnp.float32)
    yield
    chains = [_chunk_local(q_r[0, pl.ds(s, c), :], 1.0 - fs[j], v_r[0, pl.ds(s, c), :], fs[j],
                           g_cum[:, j * LANES:(j + 1) * LANES], m_ref, reverse, results[j])
              for j, s in enumerate(starts)]
    while chains:
        for chain in list(chains):
            try:
                next(chain)
            except StopIteration:
                chains.remove(chain)
        yield


def _direction_chunks(q_r, z_r, v_r, starts, lb, w_ref, m_ref, reverse):
    results = [[] for _ in starts]
    gens = [_chunk_pair(q_r, z_r, v_r, starts[j:j + 2], lb, w_ref, m_ref, reverse, results[j:j + 2])
            for j in range(0, len(starts), 2)]
    return gens, results


def _direction_states(chunks, stored):
    c = SCAN_CHUNK
    st = jnp.zeros((HEAD_DIM, HEAD_DIM), jnp.float32)
    for s, result, o_r in chunks:
        while not result:
            yield
        o_intra, kv, q_read, d_end = result
        if o_r is not None:
            o_r[pl.ds(s, c), :] = o_intra + _dot_nt(q_read, st.astype(jnp.bfloat16))
            stored[0] += 1
        st = st * d_end + kv
        yield


def _normalise_and_gate(of_ref, ob_ref, ga_ref, gn, o_ref, lo):
    rows = pl.ds(lo, SCAN_OUT_ROWS)
    o = of_ref[rows, :] + ob_ref[rows, :]
    o = o * lax.rsqrt(jnp.mean(o * o, axis=-1, keepdims=True) + EPS) * gn
    gate = _silu(ga_ref[0, rows, :].astype(jnp.float32))
    o_ref[0, rows, :] = (o * gate).astype(o_ref.dtype)


def _output_rows(of_ref, ob_ref, ga_ref, gn, o_ref, n_chunks, stored_f, stored_b):
    per = SCAN_OUT_ROWS // SCAN_CHUNK

    def ready_after(b):
        return per * (b + 1), n_chunks - per * b

    for b in sorted(range(n_chunks // per), key=lambda b: max(ready_after(b))):
        need_f, need_b = ready_after(b)
        while stored_f[0] < need_f or stored_b[0] < need_b:
            yield
        _normalise_and_gate(of_ref, ob_ref, ga_ref, gn, o_ref, b * SCAN_OUT_ROWS)
        yield


def _delayed(generator, rounds):
    for _ in range(rounds):
        yield
    yield from generator


def _scan_kernel(q_ref, zf_ref, zb_ref, v_ref, ga_ref, qc_ref, zfc_ref, zbc_ref, vc_ref,
                 lbl_ref, gn_ref, wf_ref, wb_ref, mf_ref, mb_ref, o_ref, of_ref, ob_ref):
    c = SCAN_CHUNK
    l0, l1 = lbl_ref[0], lbl_ref[1]
    mx = jnp.maximum(l0, l1)
    e0, e1 = jnp.exp(l0 - mx), jnp.exp(l1 - mx)
    lb = e0 / (e0 + e1)
    lb_f, lb_b = lb[0:1], lb[1:2]

    gens_f, gens_b, chunks_f, chunks_b = [], [], [], []
    for q_r, zf_r, zb_r, v_r, of_r, ob_r in ((qc_ref, zfc_ref, zbc_ref, vc_ref, None, None),
                                             (q_ref, zf_ref, zb_ref, v_ref, of_ref, ob_ref)):
        t = q_r.shape[1]
        starts_f = list(range(0, t, c))
        starts_b = starts_f[::-1]
        g_f, res_f = _direction_chunks(q_r, zf_r, v_r, starts_f, lb_f, wf_ref, mf_ref, False)
        g_b, res_b = _direction_chunks(q_r, zb_r, v_r, starts_b, lb_b, wb_ref, mb_ref, True)
        gens_f += g_f
        gens_b += g_b
        chunks_f += [(s, r, of_r) for s, r in zip(starts_f, res_f)]
        chunks_b += [(s, r, ob_r) for s, r in zip(starts_b, res_b)]
    staggered = [_delayed(g, SCAN_STAGGER * j) for j, pair in enumerate(zip(gens_f, gens_b)) for g in pair]
    stored_f, stored_b = [0], [0]
    _interleave(*staggered,
                _direction_states(chunks_f, stored_f),
                _direction_states(chunks_b, stored_b),
                _output_rows(of_ref, ob_ref, ga_ref, gn_ref[...], o_ref, q_ref.shape[1] // c, stored_f, stored_b))


def _hgrn_scan(p_lat, aux_lat, p_ctx, lb_logits, hgrn_norm_g):
    bsz, t_lat, _ = p_lat.shape
    t_ctx = p_ctx.shape[1]
    w_f3, w_b3, m_f, m_b = _scan_constants()
    wf = jnp.asarray(w_f3, jnp.bfloat16)
    wb = jnp.asarray(w_b3, jnp.bfloat16)
    mf = jnp.asarray(m_f, jnp.float32)
    mb = jnp.asarray(m_b, jnp.float32)

    def col(t, base):
        return pl.BlockSpec((1, t, LANES), lambda b, h, base=base: (b, 0, base + h))

    def whole(a):
        return pl.BlockSpec(a.shape, lambda b, h, nd=a.ndim: (0,) * nd)

    return pl.pallas_call(
        _scan_kernel,
        grid=(bsz, HGRN_HEADS),
        in_specs=[col(t_lat, COL_Q), col(t_lat, COL_ZF), col(t_lat, COL_ZB), col(t_lat, COL_V),
                  col(t_lat, COL_GA),
                  col(t_ctx, COL_Q), col(t_ctx, COL_ZF), col(t_ctx, COL_ZB), col(t_ctx, COL_V),
                  pl.BlockSpec((2, 2, LANES), lambda b, h: (0, 0, h)),
                  pl.BlockSpec((1, LANES), lambda b, h: (0, h)),
                  whole(wf), whole(wb), whole(mf), whole(mb)],
        out_specs=pl.BlockSpec((1, t_lat, LANES), lambda b, h: (b, 0, h)),
        out_shape=jax.ShapeDtypeStruct((bsz, t_lat, D_HGRN), jnp.bfloat16),
        scratch_shapes=[pltpu.VMEM((t_lat, HEAD_DIM), jnp.float32),
                        pltpu.VMEM((t_lat, HEAD_DIM), jnp.float32)],
        compiler_params=pltpu.CompilerParams(dimension_semantics=("arbitrary", "arbitrary"),
                                             vmem_limit_bytes=VMEM_LIMIT),
        name="hgrn_scan",
    )(p_lat, p_lat, p_lat, p_lat, aux_lat, p_ctx, p_ctx, p_ctx, p_ctx,
      lb_logits, hgrn_norm_g, wf, wb, mf, mb)


CONV_PADW = GRID_W + 32


def _conv_fill(u_ref, ug_ref, pad_ref, along_rows):
    t = u_ref.shape[1]
    n_rows = t // GRID_W

    @pl.when(along_rows)
    def _along_rows():
        padw = CONV_PADW

        def fill(r, carry):
            src = pl.multiple_of(r * GRID_W, GRID_W)
            dst = pl.multiple_of(r * padw, 32)
            glu = (u_ref[0, pl.ds(src, GRID_W), :].astype(jnp.float32)
                   * jax.nn.sigmoid(ug_ref[0, pl.ds(src, GRID_W), :].astype(jnp.float32)))
            pad_ref[pl.ds(dst, 16), :] = jnp.zeros((16, LANES), jnp.float32)
            pad_ref[pl.ds(dst + 16, GRID_W), :] = glu
            pad_ref[pl.ds(dst + 16 + GRID_W, 16), :] = jnp.zeros((16, LANES), jnp.float32)
            return carry

        lax.fori_loop(0, n_rows, fill, 0, unroll=4)

    @pl.when(jnp.logical_not(along_rows))
    def _along_cols():
        halo = CONV_HALF * GRID_W
        pad_ref[pl.ds(0, halo), :] = jnp.zeros((halo, LANES), jnp.float32)
        pad_ref[pl.ds(halo + t, halo), :] = jnp.zeros((halo, LANES), jnp.float32)

        def fill(r, carry):
            src = pl.multiple_of(r * GRID_W, GRID_W)
            glu = (u_ref[0, pl.ds(src, GRID_W), :].astype(jnp.float32)
                   * jax.nn.sigmoid(ug_ref[0, pl.ds(src, GRID_W), :].astype(jnp.float32)))
            pad_ref[pl.ds(halo + src, GRID_W), :] = glu
            return carry

        lax.fori_loop(0, n_rows, fill, 0, unroll=4)


def _conv_kernel(u_ref, ug_ref, w_ref, b_ref, o_ref, pad_ref):
    n_rows = u_ref.shape[1] // GRID_W
    along_rows = pl.program_id(1) < (D_CONV // 2) // LANES
    _conv_fill(u_ref, ug_ref, pad_ref, along_rows)
    bias = b_ref[...]

    def taps(first_tap, stride):
        def conv(r, carry):
            dst = pl.multiple_of(r * GRID_W, GRID_W)
            base = first_tap(r)
            acc = jnp.zeros((GRID_W, LANES), jnp.float32)
            for k in range(CONV_WIDTH):
                acc = acc + w_ref[k:k + 1, :] * pad_ref[pl.ds(base + k * stride, GRID_W), :]
            o_ref[0, pl.ds(dst, GRID_W), :] = (acc + bias).astype(o_ref.dtype)
            return carry

        lax.fori_loop(0, n_rows, conv, 0, unroll=8)

    @pl.when(along_rows)
    def _():
        taps(lambda r: r * CONV_PADW + (16 - CONV_HALF), 1)

    @pl.when(jnp.logical_not(along_rows))
    def _():
        taps(lambda r: pl.multiple_of(r * GRID_W, GRID_W), GRID_W)


def _axial_conv(aux_lat, conv_w, conv_b):
    bsz, t, _ = aux_lat.shape
    n_rows = t // GRID_W
    pad_rows = max(n_rows * CONV_PADW, t + 2 * CONV_HALF * GRID_W)
    w_pad = jnp.zeros((32, D_CONV), jnp.float32).at[:CONV_WIDTH].set(conv_w)
    return pl.pallas_call(
        _conv_kernel,
        grid=(bsz, D_CONV // LANES),
        in_specs=[pl.BlockSpec((1, t, LANES), lambda b, g: (b, 0, COL_U + g)),
                  pl.BlockSpec((1, t, LANES), lambda b, g: (b, 0, COL_UG + g)),
                  pl.BlockSpec((32, LANES), lambda b, g: (0, g)),
                  pl.BlockSpec((1, LANES), lambda b, g: (0, g))],
        out_specs=pl.BlockSpec((1, t, LANES), lambda b, g: (b, 0, g)),
        out_shape=jax.ShapeDtypeStruct((bsz, t, D_CONV), jnp.bfloat16),
        scratch_shapes=[pltpu.VMEM((pad_rows, LANES), jnp.float32)],
        compiler_params=pltpu.CompilerParams(dimension_semantics=("arbitrary", "arbitrary"),
                                             vmem_limit_bytes=VMEM_LIMIT),
        name="axial_conv",
    )(aux_lat, aux_lat, w_pad, conv_b)


def _out_kernel(x_ref, ba_ref, y_ref, gb_ref, gt_ref, lng_ref, lnb_ref, wa_ref, wb_ref, fg_ref, o_ref):
    y = y_ref[0].astype(jnp.float32)
    mu = jnp.mean(y, axis=-1, keepdims=True)
    yc = y - mu
    var = jnp.mean(yc * yc, axis=-1, keepdims=True)
    yn = yc * lax.rsqrt(var + EPS) * lng_ref[...] + lnb_ref[...]
    branch_b = _silu(yn) * _silu(gb_ref[0].astype(jnp.float32))
    mix = jnp.dot(ba_ref[0], wa_ref[...], preferred_element_type=jnp.float32)
    mix = mix + jnp.dot(branch_b.astype(jnp.bfloat16), wb_ref[...], preferred_element_type=jnp.float32)
    h = x_ref[0] + gt_ref[0] * mix
    o_ref[0] = h * lax.rsqrt(jnp.mean(h * h, axis=-1, keepdims=True) + EPS) * fg_ref[...]


def _output(x, branch_a, y_conv, aux_lat, gate, ln_g, ln_b, w_out_bf16, final_g):
    bsz, t, _ = x.shape
    rows = OUT_ROWS
    gb_block = COL_GB * LANES // D_CONV
    return pl.pallas_call(
        _out_kernel,
        grid=(bsz, t // rows),
        in_specs=[pl.BlockSpec((1, rows, D_MODEL), lambda b, i: (b, i, 0)),
                  pl.BlockSpec((1, rows, D_HGRN), lambda b, i: (b, i, 0)),
                  pl.BlockSpec((1, rows, D_CONV), lambda b, i: (b, i, 0)),
                  pl.BlockSpec((1, rows, D_CONV), lambda b, i: (b, i, gb_block)),
                  pl.BlockSpec((1, 1, D_MODEL), lambda b, i: (b, 0, 0)),
                  pl.BlockSpec((1, D_CONV), lambda b, i: (0, 0)),
                  pl.BlockSpec((1, D_CONV), lambda b, i: (0, 0)),
                  pl.BlockSpec((D_HGRN, D_MODEL), lambda b, i: (0, 0)),
                  pl.BlockSpec((D_CONV, D_MODEL), lambda b, i: (1, 0)),
                  pl.BlockSpec((1, D_MODEL), lambda b, i: (0, 0))],
        out_specs=pl.BlockSpec((1, rows, D_MODEL), lambda b, i: (b, i, 0)),
        out_shape=jax.ShapeDtypeStruct((bsz, t, D_MODEL), jnp.float32),
        compiler_params=pltpu.CompilerParams(dimension_semantics=("arbitrary", "arbitrary"),
                                             vmem_limit_bytes=VMEM_LIMIT),
        name="out_projection",
    )(x, branch_a, y_conv, aux_lat, gate, ln_g, ln_b, w_out_bf16, w_out_bf16, final_g)


def kernel(x, c, ctx, c_ctx, norm_g, w_mod, b_mod, w_in, lb_logits, hgrn_norm_g, conv_w, conv_b,
           conv_ln_g, conv_ln_b, w_out, final_norm_g):
    bsz, seq_len, _ = x.shape
    assert norm_g.shape[0] == 1, "single-layer block"
    assert seq_len % GRID_W == 0 and seq_len % SCAN_OUT_ROWS == 0 and seq_len % PROJ_ROWS == 0
    assert ctx.shape[1] % (2 * SCAN_CHUNK) == 0, "chunks are handled in pairs"

    pad = (-(bsz + 1)) % SUBLANES
    cc = jnp.concatenate([c, c_ctx[None, :], jnp.zeros((pad, D_MODEL), c.dtype)], axis=0)
    mod = _modulation(cc, w_mod[0], b_mod)
    shift, scale, gate = (mod[:, i * D_MODEL:(i + 1) * D_MODEL] for i in range(3))
    shift_lat, scale_lat, gate_lat = (m[:bsz, None, :] for m in (shift, scale, gate))
    shift_ctx, scale_ctx = (m[bsz][None, None, :] for m in (shift, scale))

    w_in_bf16 = w_in[0].astype(jnp.bfloat16)
    assert w_in_bf16.shape[1] == 2 * D_SCAN_IN
    p_lat, aux_lat = _projection(x, norm_g, shift_lat, scale_lat, w_in_bf16, (jnp.float32, jnp.bfloat16))
    p_ctx, = _projection(ctx.reshape(1, -1, D_MODEL), norm_g, shift_ctx, scale_ctx, w_in_bf16, (jnp.float32,))
    p_ctx = p_ctx.reshape(bsz, ctx.shape[1], D_SCAN_IN)

    branch_a = _hgrn_scan(p_lat, aux_lat, p_ctx, lb_logits, hgrn_norm_g)
    y_conv = _axial_conv(aux_lat, conv_w[0], conv_b)
    return _output(x, branch_a, y_conv, aux_lat, gate_lat, conv_ln_g, conv_ln_b,
                   w_out[0].astype(jnp.bfloat16), final_norm_g[None, :])
```

```python
import numpy as np
import jax
import jax.numpy as jnp
from jax import lax
from jax.experimental import pallas as pl
from jax.experimental.pallas import tpu as pltpu

D_MODEL = 1024
GRID_W = 64
D_HGRN = 512
HGRN_HEADS = 4
HEAD_DIM = D_HGRN // HGRN_HEADS
D_CONV = 512
CONV_WIDTH = 31
CONV_HALF = CONV_WIDTH // 2
EPS = 1e-6

LANES = 128
SUBLANES = 8
SCAN_CHUNK = 64
SCAN_LEVELS = SCAN_CHUNK.bit_length() - 1
SCAN_OUT_ROWS = 256
SCAN_STAGGER = 1
PROJ_ROWS = 512
OUT_ROWS = 1024
VMEM_LIMIT = 56 * 1024 * 1024

D_SCAN_IN = 4 * D_HGRN
COL_Q, COL_ZF, COL_ZB, COL_V = (i * HGRN_HEADS for i in range(4))
COL_GA, COL_U, COL_UG, COL_GB = (i * (D_HGRN // LANES) for i in range(4))


def _silu(x):
    return x * jax.nn.sigmoid(x)


def _mod_kernel(c_ref, w_ref, b_ref, o_ref):
    a = _silu(c_ref[...])
    o_ref[...] = jnp.dot(a, w_ref[...], preferred_element_type=jnp.float32,
                         precision=lax.Precision.HIGHEST) + b_ref[...]


def _modulation(cc, w_mod, b_mod):
    rows = cc.shape[0]
    n = w_mod.shape[1]
    return pl.pallas_call(
        _mod_kernel,
        grid=(n // D_MODEL,),
        in_specs=[pl.BlockSpec((rows, D_MODEL), lambda j: (0, 0)),
                  pl.BlockSpec((D_MODEL, D_MODEL), lambda j: (0, j)),
                  pl.BlockSpec((1, D_MODEL), lambda j: (0, j))],
        out_specs=pl.BlockSpec((rows, D_MODEL), lambda j: (0, j)),
        out_shape=jax.ShapeDtypeStruct((rows, n), jnp.float32),
        compiler_params=pltpu.CompilerParams(dimension_semantics=("arbitrary",),
                                             vmem_limit_bytes=VMEM_LIMIT),
        name="modulation",
    )(cc, w_mod, b_mod)


def _proj_kernel(x_ref, g_ref, sh_ref, sc_ref, *refs):
    x = x_ref[0]
    y = x * lax.rsqrt(jnp.mean(x * x, axis=-1, keepdims=True) + EPS) * g_ref[...]
    a = (y * (1.0 + sc_ref[0]) + sh_ref[0]).astype(jnp.bfloat16)
    n = len(refs) // 2
    for w_ref, o_ref in zip(refs[:n], refs[n:]):
        o_ref[0] = jnp.dot(a, w_ref[...], preferred_element_type=jnp.float32).astype(o_ref.dtype)


def _projection(x, norm_g, shift, scale, w_bf16, dtypes):
    bsz, t, _ = x.shape
    rows = min(PROJ_ROWS, t)
    width = D_SCAN_IN
    return pl.pallas_call(
        _proj_kernel,
        grid=(bsz, t // rows),
        in_specs=[pl.BlockSpec((1, rows, D_MODEL), lambda b, i: (b, i, 0)),
                  pl.BlockSpec((1, D_MODEL), lambda b, i: (0, 0)),
                  pl.BlockSpec((1, 1, D_MODEL), lambda b, i: (b, 0, 0)),
                  pl.BlockSpec((1, 1, D_MODEL), lambda b, i: (b, 0, 0))]
        + [pl.BlockSpec((D_MODEL, width), lambda b, i, j=j: (0, j)) for j in range(len(dtypes))],
        out_specs=[pl.BlockSpec((1, rows, width), lambda b, i: (b, i, 0)) for _ in dtypes],
        out_shape=[jax.ShapeDtypeStruct((bsz, t, width), dt) for dt in dtypes],
        compiler_params=pltpu.CompilerParams(dimension_semantics=("arbitrary", "arbitrary"),
                                             vmem_limit_bytes=VMEM_LIMIT),
        name="in_projection",
    )(x, norm_g, shift, scale, *([w_bf16] * len(dtypes)))


def _scan_constants():
    c = SCAN_CHUNK
    idx = np.arange(c)
    t, s = idx[:, None], idx[None, :]
    masks = [t == s]
    for lvl in range(SCAN_LEVELS):
        h = c >> (lvl + 1)
        masks.append(((t // (2 * h)) == (s // (2 * h))) & ((t % (2 * h)) >= h) & ((s % (2 * h)) < h))
    m_f = np.stack([m.astype(np.float32) for m in masks])
    m_b = m_f[:, ::-1, ::-1].copy()
    tri_f = (s <= t).astype(np.float32)
    tri_b = (s >= t).astype(np.float32)
    return np.concatenate([tri_f] * 2, axis=1), np.concatenate([tri_b] * 2, axis=1), m_f, m_b


def _dot_nt(a, b):
    return lax.dot_general(a, b, (((1,), (1,)), ((), ())), preferred_element_type=jnp.float32)


def _dot_tn(a, b):
    return lax.dot_general(a, b, (((0,), (0,)), ((), ())), preferred_element_type=jnp.float32)


def _interleave(*generators):
    pending = list(generators)
    while pending:
        for gen in list(pending):
            try:
                next(gen)
            except StopIteration:
                pending.remove(gen)


def _mix_rows(q, k, half, reverse):
    parts = []
    for lo in range(0, SCAN_CHUNK, 2 * half):
        first, second = (q, k) if reverse else (k, q)
        parts += [first[lo:lo + half], second[lo + half:lo + 2 * half]]
    return jnp.concatenate(parts, axis=0)


def _level_decay(g_cum, f, half, reverse):
    c = SCAN_CHUNK
    if half >= SUBLANES:
        parts = []
        for lo in range(0, c, 2 * half):
            mid = lo + half
            if reverse:
                parts += [g_cum[lo:mid] - g_cum[mid:mid + 1], g_cum[mid:mid + 1] - g_cum[mid:mid + half]]
            else:
                parts += [g_cum[mid - 1:mid] - g_cum[lo:mid], g_cum[mid:mid + half] - g_cum[mid - 1:mid]]
        return jnp.exp2(jnp.concatenate(parts, axis=0))
    if half == SUBLANES // 2:
        g3 = g_cum.reshape(c // SUBLANES, SUBLANES, LANES)
        r = half if reverse else half - 1
        later = lax.broadcasted_iota(jnp.int32, g3.shape, 1) >= half
        sign = jnp.where(later != reverse, 1.0, -1.0)
        return jnp.exp2((g3 - g3[:, r:r + 1, :]) * sign).reshape(c, LANES)
    f3 = f.reshape(c // SUBLANES, SUBLANES, LANES)
    row = lax.broadcasted_iota(jnp.int32, f3.shape, 1)
    if half == 1:
        on_query_side = (row % 2 == 0) if reverse else (row % 2 == 1)
        return jnp.where(on_query_side, f3, 1.0).reshape(c, LANES)
    prev = pltpu.roll(f3, 1, 1)
    nxt = pltpu.roll(f3, SUBLANES - 1, 1)
    m4 = row % 4
    if reverse:
        d = jnp.where(m4 == 0, f3 * nxt, jnp.where(m4 == 1, f3, jnp.where(m4 == 2, 1.0, prev)))
    else:
        d = jnp.where(m4 == 0, nxt, jnp.where(m4 == 1, 1.0, jnp.where(m4 == 2, f3, f3 * prev)))
    return d.reshape(c, LANES)


def _chunk_local(q, k, v, f, g_cum, m_ref, reverse, result):
    c = SCAN_CHUNK
    bf = jnp.bfloat16
    end_row = 0 if reverse else c - 1

    qb, kb = q.astype(bf), k.astype(bf)
    piece = 2 * SUBLANES
    q_fine = qb * _level_decay(g_cum, f, 1, reverse).astype(bf)
    p = _dot_nt(jnp.concatenate([qb, q_fine], axis=0), kb)
    a = [m_ref[0, r:r + piece] * p[r:r + piece] + m_ref[SCAN_LEVELS, r:r + piece] * p[c + r:c + r + piece]
         for r in range(0, c, piece)]
    yield
    for half in (SUBLANES, 2) + tuple(h for h in (c >> (lvl + 1) for lvl in range(SCAN_LEVELS))
                                      if h not in (SUBLANES, 2, 1)):
        d = _level_decay(g_cum, f, half, reverse).astype(bf)
        rows = list(range(0, c, piece))
        if half >= piece:
            x = _mix_rows(qb, kb, half, reverse) * d
            firsts = [lo + (0 if reverse else half) for lo in range(0, c, 2 * half)]
            rows = [r for s in firsts for r in range(s, s + half, piece)]
            p = _dot_nt(jnp.concatenate([x[s:s + half] for s in firsts], axis=0), x)
        elif half == SUBLANES:
            x = (_mix_rows(q, k, half, reverse)).astype(bf) * d
            p = _dot_nt(x, x)
        else:
            p = _dot_nt(qb * d, kb * d)
        m_level = SCAN_LEVELS - half.bit_length() + 1
        for i, r in enumerate(rows):
            a[r // piece] = a[r // piece] + m_ref[m_level, r:r + piece] * p[i * piece:(i + 1) * piece]
        yield
    d_read = jnp.exp2(g_cum)
    d_state = jnp.exp2(g_cum[end_row:end_row + 1] - g_cum)
    vb = v.astype(bf)
    o_intra = jnp.dot(jnp.concatenate(a, axis=0).astype(bf), vb, preferred_element_type=jnp.float32)
    yield
    kv = _dot_tn(vb, kb * d_state.astype(bf))
    result.extend([o_intra, kv, qb * d_read.astype(bf), d_read[end_row:end_row + 1]])
    yield


def _chunk_pair(q_r, z_r, v_r, starts, lb, w_ref, m_ref, reverse, results):
    c = SCAN_CHUNK
    bf = jnp.bfloat16
    fs, g3s = [], []
    for s in starts:
        f = lb + (1.0 - lb) * jax.nn.sigmoid(z_r[0, pl.ds(s, c), :])
        g = jnp.log2(f)
        fs.append(f)
        g_hi = g.astype(bf)
        g_lo = (g - g_hi.astype(jnp.float32)).astype(bf)
        g3s.append(jnp.concatenate([g_hi, g_lo], axis=0))
        yield
    g_cum = jnp.dot(w_ref[...], jnp.concatenate(g3s, axis=1), preferred_element_type=jnp.float32)
    yield
    chains = [_chunk_local(q_r[0, pl.ds(s, c), :], 1.0 - fs[j], v_r[0, pl.ds(s, c), :], fs[j],
                           g_cum[:, j * LANES:(j + 1) * LANES], m_ref, reverse, results[j])
              for j, s in enumerate(starts)]
    while chains:
        for chain in list(chains):
            try:
                next(chain)
            except StopIteration:
                chains.remove(chain)
        yield


def _direction_chunks(q_r, z_r, v_r, starts, lb, w_ref, m_ref, reverse):
    results = [[] for _ in starts]
    gens = [_chunk_pair(q_r, z_r, v_r, starts[j:j + 2], lb, w_ref, m_ref, reverse, results[j:j + 2])
            for j in range(0, len(starts), 2)]
    return gens, results


def _direction_states(chunks, stored):
    c = SCAN_CHUNK
    st = jnp.zeros((HEAD_DIM, HEAD_DIM), jnp.float32)
    for s, result, o_r in chunks:
        while not result:
            yield
        o_intra, kv, q_read, d_end = result
        if o_r is not None:
            o_r[pl.ds(s, c), :] = o_intra + _dot_nt(q_read, st.astype(jnp.bfloat16))
            stored[0] += 1
        st = st * d_end + kv
        yield


def _normalise_and_gate(of_ref, ob_ref, ga_ref, gn, o_ref, lo):
    rows = pl.ds(lo, SCAN_OUT_ROWS)
    o = of_ref[rows, :] + ob_ref[rows, :]
    o = o * lax.rsqrt(jnp.mean(o * o, axis=-1, keepdims=True) + EPS) * gn
    gate = _silu(ga_ref[0, rows, :].astype(jnp.float32))
    o_ref[0, rows, :] = (o * gate).astype(o_ref.dtype)


def _output_rows(of_ref, ob_ref, ga_ref, gn, o_ref, n_chunks, stored_f, stored_b):
    per = SCAN_OUT_ROWS // SCAN_CHUNK

    def ready_after(b):
        return per * (b + 1), n_chunks - per * b

    for b in sorted(range(n_chunks // per), key=lambda b: max(ready_after(b))):
        need_f, need_b = ready_after(b)
        while stored_f[0] < need_f or stored_b[0] < need_b:
            yield
        _normalise_and_gate(of_ref, ob_ref, ga_ref, gn, o_ref, b * SCAN_OUT_ROWS)
        yield


def _delayed(generator, rounds):
    for _ in range(rounds):
        yield
    yield from generator


def _scan_kernel(q_ref, zf_ref, zb_ref, v_ref, ga_ref, qc_ref, zfc_ref, zbc_ref, vc_ref,
                 lbl_ref, gn_ref, wf_ref, wb_ref, mf_ref, mb_ref, o_ref, of_ref, ob_ref):
    c = SCAN_CHUNK
    l0, l1 = lbl_ref[0], lbl_ref[1]
    mx = jnp.maximum(l0, l1)
    e0, e1 = jnp.exp(l0 - mx), jnp.exp(l1 - mx)
    lb = e0 / (e0 + e1)
    lb_f, lb_b = lb[0:1], lb[1:2]

    gens_f, gens_b, chunks_f, chunks_b = [], [], [], []
    for q_r, zf_r, zb_r, v_r, of_r, ob_r in ((qc_ref, zfc_ref, zbc_ref, vc_ref, None, None),
                                             (q_ref, zf_ref, zb_ref, v_ref, of_ref, ob_ref)):
        t = q_r.shape[1]
        starts_f = list(range(0, t, c))
        starts_b = starts_f[::-1]
        g_f, res_f = _direction_chunks(q_r, zf_r, v_r, starts_f, lb_f, wf_ref, mf_ref, False)
        g_b, res_b = _direction_chunks(q_r, zb_r, v_r, starts_b, lb_b, wb_ref, mb_ref, True)
        gens_f += g_f
        gens_b += g_b
        chunks_f += [(s, r, of_r) for s, r in zip(starts_f, res_f)]
        chunks_b += [(s, r, ob_r) for s, r in zip(starts_b, res_b)]
    staggered = [_delayed(g, SCAN_STAGGER * j) for j, pair in enumerate(zip(gens_f, gens_b)) for g in pair]
    stored_f, stored_b = [0], [0]
    _interleave(*staggered,
                _direction_states(chunks_f, stored_f),
                _direction_states(chunks_b, stored_b),
                _output_rows(of_ref, ob_ref, ga_ref, gn_ref[...], o_ref, q_ref.shape[1] // c, stored_f, stored_b))


def _hgrn_scan(p_lat, aux_lat, p_ctx, lb_logits, hgrn_norm_g):
    bsz, t_lat, _ = p_lat.shape
    t_ctx = p_ctx.shape[1]
    w_f3, w_b3, m_f, m_b = _scan_constants()
    wf = jnp.asarray(w_f3, jnp.bfloat16)
    wb = jnp.asarray(w_b3, jnp.bfloat16)
    mf = jnp.asarray(m_f, jnp.float32)
    mb = jnp.asarray(m_b, jnp.float32)

    def col(t, base):
        return pl.BlockSpec((1, t, LANES), lambda b, h, base=base: (b, 0, base + h))

    def whole(a):
        return pl.BlockSpec(a.shape, lambda b, h, nd=a.ndim: (0,) * nd)

    return pl.pallas_call(
        _scan_kernel,
        grid=(bsz, HGRN_HEADS),
        in_specs=[col(t_lat, COL_Q), col(t_lat, COL_ZF), col(t_lat, COL_ZB), col(t_lat, COL_V),
                  col(t_lat, COL_GA),
                  col(t_ctx, COL_Q), col(t_ctx, COL_ZF), col(t_ctx, COL_ZB), col(t_ctx, COL_V),
                  pl.BlockSpec((2, 2, LANES), lambda b, h: (0, 0, h)),
                  pl.BlockSpec((1, LANES), lambda b, h: (0, h)),
                  whole(wf), whole(wb), whole(mf), whole(mb)],
        out_specs=pl.BlockSpec((1, t_lat, LANES), lambda b, h: (b, 0, h)),
        out_shape=jax.ShapeDtypeStruct((bsz, t_lat, D_HGRN), jnp.bfloat16),
        scratch_shapes=[pltpu.VMEM((t_lat, HEAD_DIM), jnp.float32),
                        pltpu.VMEM((t_lat, HEAD_DIM), jnp.float32)],
        compiler_params=pltpu.CompilerParams(dimension_semantics=("arbitrary", "arbitrary"),
                                             vmem_limit_bytes=VMEM_LIMIT),
        name="hgrn_scan",
    )(p_lat, p_lat, p_lat, p_lat, aux_lat, p_ctx, p_ctx, p_ctx, p_ctx,
      lb_logits, hgrn_norm_g, wf, wb, mf, mb)


CONV_PADW = GRID_W + 32


def _conv_fill(u_ref, ug_ref, pad_ref, along_rows):
    t = u_ref.shape[1]
    n_rows = t // GRID_W

    @pl.when(along_rows)
    def _along_rows():
        padw = CONV_PADW

        def fill(r, carry):
            src = pl.multiple_of(r * GRID_W, GRID_W)
            dst = pl.multiple_of(r * padw, 32)
            glu = (u_ref[0, pl.ds(src, GRID_W), :].astype(jnp.float32)
                   * jax.nn.sigmoid(ug_ref[0, pl.ds(src, GRID_W), :].astype(jnp.float32)))
            pad_ref[pl.ds(dst, 16), :] = jnp.zeros((16, LANES), jnp.float32)
            pad_ref[pl.ds(dst + 16, GRID_W), :] = glu
            pad_ref[pl.ds(dst + 16 + GRID_W, 16), :] = jnp.zeros((16, LANES), jnp.float32)
            return carry

        lax.fori_loop(0, n_rows, fill, 0, unroll=4)

    @pl.when(jnp.logical_not(along_rows))
    def _along_cols():
        halo = CONV_HALF * GRID_W
        pad_ref[pl.ds(0, halo), :] = jnp.zeros((halo, LANES), jnp.float32)
        pad_ref[pl.ds(halo + t, halo), :] = jnp.zeros((halo, LANES), jnp.float32)

        def fill(r, carry):
            src = pl.multiple_of(r * GRID_W, GRID_W)
            glu = (u_ref[0, pl.ds(src, GRID_W), :].astype(jnp.float32)
                   * jax.nn.sigmoid(ug_ref[0, pl.ds(src, GRID_W), :].astype(jnp.float32)))
            pad_ref[pl.ds(halo + src, GRID_W), :] = glu
            return carry

        lax.fori_loop(0, n_rows, fill, 0, unroll=4)


def _conv_kernel(u_ref, ug_ref, w_ref, b_ref, o_ref, pad_ref):
    n_rows = u_ref.shape[1] // GRID_W
    along_rows = pl.program_id(1) < (D_CONV // 2) // LANES
    _conv_fill(u_ref, ug_ref, pad_ref, along_rows)
    bias = b_ref[...]

    def taps(first_tap, stride):
        def conv(r, carry):
            dst = pl.multiple_of(r * GRID_W, GRID_W)
            base = first_tap(r)
            acc = jnp.zeros((GRID_W, LANES), jnp.float32)
            for k in range(CONV_WIDTH):
                acc = acc + w_ref[k:k + 1, :] * pad_ref[pl.ds(base + k * stride, GRID_W), :]
            o_ref[0, pl.ds(dst, GRID_W), :] = (acc + bias).astype(o_ref.dtype)
            return carry

        lax.fori_loop(0, n_rows, conv, 0, unroll=8)

    @pl.when(along_rows)
    def _():
        taps(lambda r: r * CONV_PADW + (16 - CONV_HALF), 1)

    @pl.when(jnp.logical_not(along_rows))
    def _():
        taps(lambda r: pl.multiple_of(r * GRID_W, GRID_W), GRID_W)


def _axial_conv(aux_lat, conv_w, conv_b):
    bsz, t, _ = aux_lat.shape
    n_rows = t // GRID_W
    pad_rows = max(n_rows * CONV_PADW, t + 2 * CONV_HALF * GRID_W)
    w_pad = jnp.zeros((32, D_CONV), jnp.float32).at[:CONV_WIDTH].set(conv_w)
    return pl.pallas_call(
        _conv_kernel,
        grid=(bsz, D_CONV // LANES),
        in_specs=[pl.BlockSpec((1, t, LANES), lambda b, g: (b, 0, COL_U + g)),
                  pl.BlockSpec((1, t, LANES), lambda b, g: (b, 0, COL_UG + g)),
                  pl.BlockSpec((32, LANES), lambda b, g: (0, g)),
                  pl.BlockSpec((1, LANES), lambda b, g: (0, g))],
        out_specs=pl.BlockSpec((1, t, LANES), lambda b, g: (b, 0, g)),
        out_shape=jax.ShapeDtypeStruct((bsz, t, D_CONV), jnp.bfloat16),
        scratch_shapes=[pltpu.VMEM((pad_rows, LANES), jnp.float32)],
        compiler_params=pltpu.CompilerParams(dimension_semantics=("arbitrary", "arbitrary"),
                                             vmem_limit_bytes=VMEM_LIMIT),
        name="axial_conv",
    )(aux_lat, aux_lat, w_pad, conv_b)


def _out_kernel(x_ref, ba_ref, y_ref, gb_ref, gt_ref, lng_ref, lnb_ref, wa_ref, wb_ref, fg_ref, o_ref):
    y = y_ref[0].astype(jnp.float32)
    mu = jnp.mean(y, axis=-1, keepdims=True)
    yc = y - mu
    var = jnp.mean(yc * yc, axis=-1, keepdims=True)
    yn = yc * lax.rsqrt(var + EPS) * lng_ref[...] + lnb_ref[...]
    branch_b = _silu(yn) * _silu(gb_ref[0].astype(jnp.float32))
    mix = jnp.dot(ba_ref[0], wa_ref[...], preferred_element_type=jnp.float32)
    mix = mix + jnp.dot(branch_b.astype(jnp.bfloat16), wb_ref[...], preferred_element_type=jnp.float32)
    h = x_ref[0] + gt_ref[0] * mix
    o_ref[0] = h * lax.rsqrt(jnp.mean(h * h, axis=-1, keepdims=True) + EPS) * fg_ref[...]


def _output(x, branch_a, y_conv, aux_lat, gate, ln_g, ln_b, w_out_bf16, final_g):
    bsz, t, _ = x.shape
    rows = OUT_ROWS
    gb_block = COL_GB * LANES // D_CONV
    return pl.pallas_call(
        _out_kernel,
        grid=(bsz, t // rows),
        in_specs=[pl.BlockSpec((1, rows, D_MODEL), lambda b, i: (b, i, 0)),
                  pl.BlockSpec((1, rows, D_HGRN), lambda b, i: (b, i, 0)),
                  pl.BlockSpec((1, rows, D_CONV), lambda b, i: (b, i, 0)),
                  pl.BlockSpec((1, rows, D_CONV), lambda b, i: (b, i, gb_block)),
                  pl.BlockSpec((1, 1, D_MODEL), lambda b, i: (b, 0, 0)),
                  pl.BlockSpec((1, D_CONV), lambda b, i: (0, 0)),
                  pl.BlockSpec((1, D_CONV), lambda b, i: (0, 0)),
                  pl.BlockSpec((D_HGRN, D_MODEL), lambda b, i: (0, 0)),
                  pl.BlockSpec((D_CONV, D_MODEL), lambda b, i: (1, 0)),
                  pl.BlockSpec((1, D_MODEL), lambda b, i: (0, 0))],
        out_specs=pl.BlockSpec((1, rows, D_MODEL), lambda b, i: (b, i, 0)),
        out_shape=jax.ShapeDtypeStruct((bsz, t, D_MODEL), jnp.float32),
        compiler_params=pltpu.CompilerParams(dimension_semantics=("arbitrary", "arbitrary"),
                                             vmem_limit_bytes=VMEM_LIMIT),
        name="out_projection",
    )(x, branch_a, y_conv, aux_lat, gate, ln_g, ln_b, w_out_bf16, w_out_bf16, final_g)


def kernel(x, c, ctx, c_ctx, norm_g, w_mod, b_mod, w_in, lb_logits, hgrn_norm_g, conv_w, conv_b,
           conv_ln_g, conv_ln_b, w_out, final_norm_g):
    bsz, seq_len, _ = x.shape
    assert norm_g.shape[0] == 1, "single-layer block"
    assert seq_len % GRID_W == 0 and seq_len % SCAN_OUT_ROWS == 0 and seq_len % PROJ_ROWS == 0
    assert ctx.shape[1] % (2 * SCAN_CHUNK) == 0, "chunks are handled in pairs"

    pad = (-(bsz + 1)) % SUBLANES
    cc = jnp.concatenate([c, c_ctx[None, :], jnp.zeros((pad, D_MODEL), c.dtype)], axis=0)
    mod = _modulation(cc, w_mod[0], b_mod)
    shift, scale, gate = (mod[:, i * D_MODEL:(i + 1) * D_MODEL] for i in range(3))
    shift_lat, scale_lat, gate_lat = (m[:bsz, None, :] for m in (shift, scale, gate))
    shift_ctx, scale_ctx = (m[bsz][None, None, :] for m in (shift, scale))

    w_in_bf16 = w_in[0].astype(jnp.bfloat16)
    assert w_in_bf16.shape[1] == 2 * D_SCAN_IN
    p_lat, aux_lat = _projection(x, norm_g, shift_lat, scale_lat, w_in_bf16, (jnp.float32, jnp.bfloat16))
    p_ctx, = _projection(ctx.reshape(1, -1, D_MODEL), norm_g, shift_ctx, scale_ctx, w_in_bf16, (jnp.float32,))
    p_ctx = p_ctx.reshape(bsz, ctx.shape[1], D_SCAN_IN)

    branch_a = _hgrn_scan(p_lat, aux_lat, p_ctx, lb_logits, hgrn_norm_g)
    y_conv = _axial_conv(aux_lat, conv_w[0], conv_b)
    return _output(x, branch_a, y_conv, aux_lat, gate_lat, conv_ln_g, conv_ln_b,
                   w_out[0].astype(jnp.bfloat16), final_norm_g[None, :])
```

```python
import numpy as np
import jax
import jax.numpy as jnp
from jax import lax
from jax.experimental import pallas as pl
from jax.experimental.pallas import tpu as pltpu

D_MODEL = 1024
GRID_W = 64
D_HGRN = 512
HGRN_HEADS = 4
HEAD_DIM = D_HGRN // HGRN_HEADS
D_CONV = 512
CONV_WIDTH = 31
CONV_HALF = CONV_WIDTH // 2
EPS = 1e-6

LANES = 128
SUBLANES = 8
SCAN_CHUNK = 64
SCAN_LEVELS = SCAN_CHUNK.bit_length() - 1
SCAN_OUT_ROWS = 256
SCAN_STAGGER = 1
PROJ_ROWS = 512
OUT_ROWS = 1024
VMEM_LIMIT = 56 * 1024 * 1024

D_SCAN_IN = 4 * D_HGRN
COL_Q, COL_ZF, COL_ZB, COL_V = (i * HGRN_HEADS for i in range(4))
COL_GA, COL_U, COL_UG, COL_GB = (i * (D_HGRN // LANES) for i in range(4))


def _silu(x):
    return x * jax.nn.sigmoid(x)


def _mod_kernel(c_ref, w_ref, b_ref, o_ref):
    a = _silu(c_ref[...])
    o_ref[...] = jnp.dot(a, w_ref[...], preferred_element_type=jnp.float32,
                         precision=lax.Precision.HIGHEST) + b_ref[...]


def _modulation(cc, w_mod, b_mod):
    rows = cc.shape[0]
    n = w_mod.shape[1]
    return pl.pallas_call(
        _mod_kernel,
        grid=(n // D_MODEL,),
        in_specs=[pl.BlockSpec((rows, D_MODEL), lambda j: (0, 0)),
                  pl.BlockSpec((D_MODEL, D_MODEL), lambda j: (0, j)),
                  pl.BlockSpec((1, D_MODEL), lambda j: (0, j))],
        out_specs=pl.BlockSpec((rows, D_MODEL), lambda j: (0, j)),
        out_shape=jax.ShapeDtypeStruct((rows, n), jnp.float32),
        compiler_params=pltpu.CompilerParams(dimension_semantics=("arbitrary",),
                                             vmem_limit_bytes=VMEM_LIMIT),
        name="modulation",
    )(cc, w_mod, b_mod)


def _proj_kernel(x_ref, g_ref, sh_ref, sc_ref, *refs):
    x = x_ref[0]
    y = x * lax.rsqrt(jnp.mean(x * x, axis=-1, keepdims=True) + EPS) * g_ref[...]
    a = (y * (1.0 + sc_ref[0]) + sh_ref[0]).astype(jnp.bfloat16)
    n = len(refs) // 2
    for w_ref, o_ref in zip(refs[:n], refs[n:]):
        o_ref[0] = jnp.dot(a, w_ref[...], preferred_element_type=jnp.float32).astype(o_ref.dtype)


def _projection(x, norm_g, shift, scale, w_bf16, dtypes):
    bsz, t, _ = x.shape
    rows = min(PROJ_ROWS, t)
    width = D_SCAN_IN
    return pl.pallas_call(
        _proj_kernel,
        grid=(bsz, t // rows),
        in_specs=[pl.BlockSpec((1, rows, D_MODEL), lambda b, i: (b, i, 0)),
                  pl.BlockSpec((1, D_MODEL), lambda b, i: (0, 0)),
                  pl.BlockSpec((1, 1, D_MODEL), lambda b, i: (b, 0, 0)),
                  pl.BlockSpec((1, 1, D_MODEL), lambda b, i: (b, 0, 0))]
        + [pl.BlockSpec((D_MODEL, width), lambda b, i, j=j: (0, j)) for j in range(len(dtypes))],
        out_specs=[pl.BlockSpec((1, rows, width), lambda b, i: (b, i, 0)) for _ in dtypes],
        out_shape=[jax.ShapeDtypeStruct((bsz, t, width), dt) for dt in dtypes],
        compiler_params=pltpu.CompilerParams(dimension_semantics=("arbitrary", "arbitrary"),
                                             vmem_limit_bytes=VMEM_LIMIT),
        name="in_projection",
    )(x, norm_g, shift, scale, *([w_bf16] * len(dtypes)))


def _scan_constants():
    c = SCAN_CHUNK
    idx = np.arange(c)
    t, s = idx[:, None], idx[None, :]
    masks = [t == s]
    for lvl in range(SCAN_LEVELS):
        h = c >> (lvl + 1)
        masks.append(((t // (2 * h)) == (s // (2 * h))) & ((t % (2 * h)) >= h) & ((s % (2 * h)) < h))
    m_f = np.stack([m.astype(np.float32) for m in masks])
    m_b = m_f[:, ::-1, ::-1].copy()
    tri_f = (s <= t).astype(np.float32)
    tri_b = (s >= t).astype(np.float32)
    return np.concatenate([tri_f] * 2, axis=1), np.concatenate([tri_b] * 2, axis=1), m_f, m_b


def _dot_nt(a, b):
    return lax.dot_general(a, b, (((1,), (1,)), ((), ())), preferred_element_type=jnp.float32)


def _dot_tn(a, b):
    return lax.dot_general(a, b, (((0,), (0,)), ((), ())), preferred_element_type=jnp.float32)


def _interleave(*generators):
    pending = list(generators)
    while pending:
        for gen in list(pending):
            try:
                next(gen)
            except StopIteration:
                pending.remove(gen)


def _mix_rows(q, k, half, reverse):
    parts = []
    for lo in range(0, SCAN_CHUNK, 2 * half):
        first, second = (q, k) if reverse else (k, q)
        parts += [first[lo:lo + half], second[lo + half:lo + 2 * half]]
    return jnp.concatenate(parts, axis=0)


def _level_decay(g_cum, f, half, reverse):
    c = SCAN_CHUNK
    if half >= SUBLANES:
        parts = []
        for lo in range(0, c, 2 * half):
            mid = lo + half
            if reverse:
                parts += [g_cum[lo:mid] - g_cum[mid:mid + 1], g_cum[mid:mid + 1] - g_cum[mid:mid + half]]
            else:
                parts += [g_cum[mid - 1:mid] - g_cum[lo:mid], g_cum[mid:mid + half] - g_cum[mid - 1:mid]]
        return jnp.exp2(jnp.concatenate(parts, axis=0))
    if half == SUBLANES // 2:
        g3 = g_cum.reshape(c // SUBLANES, SUBLANES, LANES)
        r = half if reverse else half - 1
        later = lax.broadcasted_iota(jnp.int32, g3.shape, 1) >= half
        sign = jnp.where(later != reverse, 1.0, -1.0)
        return jnp.exp2((g3 - g3[:, r:r + 1, :]) * sign).reshape(c, LANES)
    f3 = f.reshape(c // SUBLANES, SUBLANES, LANES)
    row = lax.broadcasted_iota(jnp.int32, f3.shape, 1)
    if half == 1:
        on_query_side = (row % 2 == 0) if reverse else (row % 2 == 1)
        return jnp.where(on_query_side, f3, 1.0).reshape(c, LANES)
    prev = pltpu.roll(f3, 1, 1)
    nxt = pltpu.roll(f3, SUBLANES - 1, 1)
    m4 = row % 4
    if reverse:
        d = jnp.where(m4 == 0, f3 * nxt, jnp.where(m4 == 1, f3, jnp.where(m4 == 2, 1.0, prev)))
    else:
        d = jnp.where(m4 == 0, nxt, jnp.where(m4 == 1, 1.0, jnp.where(m4 == 2, f3, f3 * prev)))
    return d.reshape(c, LANES)


def _chunk_local(q, k, v, f, g_cum, m_ref, reverse, result):
    c = SCAN_CHUNK
    bf = jnp.bfloat16
    end_row = 0 if reverse else c - 1

    qb, kb = q.astype(bf), k.astype(bf)
    piece = 2 * SUBLANES
    q_fine = qb * _level_decay(g_cum, f, 1, reverse).astype(bf)
    p = _dot_nt(jnp.concatenate([qb, q_fine], axis=0), kb)
    a = [m_ref[0, r:r + piece] * p[r:r + piece] + m_ref[SCAN_LEVELS, r:r + piece] * p[c + r:c + r + piece]
         for r in range(0, c, piece)]
    yield
    for half in (SUBLANES, 2) + tuple(h for h in (c >> (lvl + 1) for lvl in range(SCAN_LEVELS))
                                      if h not in (SUBLANES, 2, 1)):
        d = _level_decay(g_cum, f, half, reverse).astype(bf)
        rows = list(range(0, c, piece))
        if half >= piece:
            x = _mix_rows(qb, kb, half, reverse) * d
            firsts = [lo + (0 if reverse else half) for lo in range(0, c, 2 * half)]
            rows = [r for s in firsts for r in range(s, s + half, piece)]
            p = _dot_nt(jnp.concatenate([x[s:s + half] for s in firsts], axis=0), x)
        elif half == SUBLANES:
            x = (_mix_rows(q, k, half, reverse)).astype(bf) * d
            p = _dot_nt(x, x)
        else:
            p = _dot_nt(qb * d, kb * d)
        m_level = SCAN_LEVELS - half.bit_length() + 1
        for i, r in enumerate(rows):
            a[r // piece] = a[r // piece] + m_ref[m_level, r:r + piece] * p[i * piece:(i + 1) * piece]
        yield
    d_read = jnp.exp2(g_cum)
    d_state = jnp.exp2(g_cum[end_row:end_row + 1] - g_cum)
    vb = v.astype(bf)
    o_intra = jnp.dot(jnp.concatenate(a, axis=0).astype(bf), vb, preferred_element_type=jnp.float32)
    yield
    kv = _dot_tn(vb, kb * d_state.astype(bf))
    result.extend([o_intra, kv, qb * d_read.astype(bf), d_read[end_row:end_row + 1]])
    yield


def _chunk_pair(q_r, z_r, v_r, starts, lb, w_ref, m_ref, reverse, results):
    c = SCAN_CHUNK
    bf = jnp.bfloat16
    fs, g3s = [], []
    for s in starts:
        f = lb + (1.0 - lb) * jax.nn.sigmoid(z_r[0, pl.ds(s, c), :])
        g = jnp.log2(f)
        fs.append(f)
        g_hi = g.astype(bf)
        g_lo = (g - g_hi.astype(jnp.float32)).astype(bf)
        g3s.append(jnp.concatenate([g_hi, g_lo], axis=0))
        yield
    g_cum = jnp.dot(w_ref[...], jnp.concatenate(g3s, axis=1), preferred_element_type=jnp.float32)
    yield
    chains = [_chunk_local(q_r[0, pl.ds(s, c), :], 1.0 - fs[j], v_r[0, pl.ds(s, c), :], fs[j],
                           g_cum[:, j * LANES:(j + 1) * LANES], m_ref, reverse, results[j])
              for j, s in enumerate(starts)]
    while chains:
        for chain in list(chains):
            try:
                next(chain)
            except StopIteration:
                chains.remove(chain)
        yield


def _direction_chunks(q_r, z_r, v_r, starts, lb, w_ref, m_ref, reverse):
    results = [[] for _ in starts]
    gens = [_chunk_pair(q_r, z_r, v_r, starts[j:j + 2], lb, w_ref, m_ref, reverse, results[j:j + 2])
            for j in range(0, len(starts), 2)]
    return gens, results


def _direction_states(chunks, stored):
    c = SCAN_CHUNK
    st = jnp.zeros((HEAD_DIM, HEAD_DIM), jnp.float32)
    for s, result, o_r in chunks:
        while not result:
            yield
        o_intra, kv, q_read, d_end = result
        if o_r is not None:
            o_r[pl.ds(s, c), :] = o_intra + jnp.dot(q_read, st.T.astype(jnp.bfloat16),
                                                    preferred_element_type=jnp.float32)
            stored[0] += 1
        st = st * d_end + kv
        yield


def _normalise_and_gate(of_ref, ob_ref, ga_ref, gn, o_ref, lo):
    rows = pl.ds(lo, SCAN_OUT_ROWS)
    o = of_ref[rows, :] + ob_ref[rows, :]
    o = o * lax.rsqrt(jnp.mean(o * o, axis=-1, keepdims=True) + EPS) * gn
    gate = _silu(ga_ref[0, rows, :].astype(jnp.float32))
    o_ref[0, rows, :] = (o * gate).astype(o_ref.dtype)


def _output_rows(of_ref, ob_ref, ga_ref, gn, o_ref, n_chunks, stored_f, stored_b):
    per = SCAN_OUT_ROWS // SCAN_CHUNK

    def ready_after(b):
        return per * (b + 1), n_chunks - per * b

    for b in sorted(range(n_chunks // per), key=lambda b: max(ready_after(b))):
        need_f, need_b = ready_after(b)
        while stored_f[0] < need_f or stored_b[0] < need_b:
            yield
        _normalise_and_gate(of_ref, ob_ref, ga_ref, gn, o_ref, b * SCAN_OUT_ROWS)
        yield


def _delayed(generator, rounds):
    for _ in range(rounds):
        yield
    yield from generator


def _scan_kernel(q_ref, zf_ref, zb_ref, v_ref, ga_ref, qc_ref, zfc_ref, zbc_ref, vc_ref,
                 lbl_ref, gn_ref, wf_ref, wb_ref, mf_ref, mb_ref, o_ref, of_ref, ob_ref):
    c = SCAN_CHUNK
    l0, l1 = lbl_ref[0], lbl_ref[1]
    mx = jnp.maximum(l0, l1)
    e0, e1 = jnp.exp(l0 - mx), jnp.exp(l1 - mx)
    lb = e0 / (e0 + e1)
    lb_f, lb_b = lb[0:1], lb[1:2]

    gens_f, gens_b, chunks_f, chunks_b = [], [], [], []
    for q_r, zf_r, zb_r, v_r, of_r, ob_r in ((qc_ref, zfc_ref, zbc_ref, vc_ref, None, None),
                                             (q_ref, zf_ref, zb_ref, v_ref, of_ref, ob_ref)):
        t = q_r.shape[1]
        starts_f = list(range(0, t, c))
        starts_b = starts_f[::-1]
        g_f, res_f = _direction_chunks(q_r, zf_r, v_r, starts_f, lb_f, wf_ref, mf_ref, False)
        g_b, res_b = _direction_chunks(q_r, zb_r, v_r, starts_b, lb_b, wb_ref, mb_ref, True)
        gens_f += g_f
        gens_b += g_b
        chunks_f += [(s, r, of_r) for s, r in zip(starts_f, res_f)]
        chunks_b += [(s, r, ob_r) for s, r in zip(starts_b, res_b)]
    staggered = [_delayed(g, SCAN_STAGGER * j) for j, pair in enumerate(zip(gens_f, gens_b)) for g in pair]
    stored_f, stored_b = [0], [0]
    _interleave(*staggered,
                _direction_states(chunks_f, stored_f),
                _direction_states(chunks_b, stored_b),
                _output_rows(of_ref, ob_ref, ga_ref, gn_ref[...], o_ref, q_ref.shape[1] // c, stored_f, stored_b))


def _hgrn_scan(p_lat, aux_lat, p_ctx, lb_logits, hgrn_norm_g):
    bsz, t_lat, _ = p_lat.shape
    t_ctx = p_ctx.shape[1]
    w_f3, w_b3, m_f, m_b = _scan_constants()
    wf = jnp.asarray(w_f3, jnp.bfloat16)
    wb = jnp.asarray(w_b3, jnp.bfloat16)
    mf = jnp.asarray(m_f, jnp.float32)
    mb = jnp.asarray(m_b, jnp.float32)

    def col(t, base):
        return pl.BlockSpec((1, t, LANES), lambda b, h, base=base: (b, 0, base + h))

    def whole(a):
        return pl.BlockSpec(a.shape, lambda b, h, nd=a.ndim: (0,) * nd)

    return pl.pallas_call(
        _scan_kernel,
        grid=(bsz, HGRN_HEADS),
        in_specs=[col(t_lat, COL_Q), col(t_lat, COL_ZF), col(t_lat, COL_ZB), col(t_lat, COL_V),
                  col(t_lat, COL_GA),
                  col(t_ctx, COL_Q), col(t_ctx, COL_ZF), col(t_ctx, COL_ZB), col(t_ctx, COL_V),
                  pl.BlockSpec((2, 2, LANES), lambda b, h: (0, 0, h)),
                  pl.BlockSpec((1, LANES), lambda b, h: (0, h)),
                  whole(wf), whole(wb), whole(mf), whole(mb)],
        out_specs=pl.BlockSpec((1, t_lat, LANES), lambda b, h: (b, 0, h)),
        out_shape=jax.ShapeDtypeStruct((bsz, t_lat, D_HGRN), jnp.bfloat16),
        scratch_shapes=[pltpu.VMEM((t_lat, HEAD_DIM), jnp.float32),
                        pltpu.VMEM((t_lat, HEAD_DIM), jnp.float32)],
        compiler_params=pltpu.CompilerParams(dimension_semantics=("arbitrary", "arbitrary"),
                                             vmem_limit_bytes=VMEM_LIMIT),
        name="hgrn_scan",
    )(p_lat, p_lat, p_lat, p_lat, aux_lat, p_ctx, p_ctx, p_ctx, p_ctx,
      lb_logits, hgrn_norm_g, wf, wb, mf, mb)


CONV_PADW = GRID_W + 32


def _conv_fill(u_ref, ug_ref, pad_ref, along_rows):
    t = u_ref.shape[1]
    n_rows = t // GRID_W

    @pl.when(along_rows)
    def _along_rows():
        padw = CONV_PADW

        def fill(r, carry):
            src = pl.multiple_of(r * GRID_W, GRID_W)
            dst = pl.multiple_of(r * padw, 32)
            glu = (u_ref[0, pl.ds(src, GRID_W), :].astype(jnp.float32)
                   * jax.nn.sigmoid(ug_ref[0, pl.ds(src, GRID_W), :].astype(jnp.float32)))
            pad_ref[pl.ds(dst, 16), :] = jnp.zeros((16, LANES), jnp.float32)
            pad_ref[pl.ds(dst + 16, GRID_W), :] = glu
            pad_ref[pl.ds(dst + 16 + GRID_W, 16), :] = jnp.zeros((16, LANES), jnp.float32)
            return carry

        lax.fori_loop(0, n_rows, fill, 0, unroll=4)

    @pl.when(jnp.logical_not(along_rows))
    def _along_cols():
        halo = CONV_HALF * GRID_W
        pad_ref[pl.ds(0, halo), :] = jnp.zeros((halo, LANES), jnp.float32)
        pad_ref[pl.ds(halo + t, halo), :] = jnp.zeros((halo, LANES), jnp.float32)

        def fill(r, carry):
            src = pl.multiple_of(r * GRID_W, GRID_W)
            glu = (u_ref[0, pl.ds(src, GRID_W), :].astype(jnp.float32)
                   * jax.nn.sigmoid(ug_ref[0, pl.ds(src, GRID_W), :].astype(jnp.float32)))
            pad_ref[pl.ds(halo + src, GRID_W), :] = glu
            return carry

        lax.fori_loop(0, n_rows, fill, 0, unroll=4)


def _conv_kernel(u_ref, ug_ref, w_ref, b_ref, o_ref, pad_ref):
    n_rows = u_ref.shape[1] // GRID_W
    along_rows = pl.program_id(1) < (D_CONV // 2) // LANES
    _conv_fill(u_ref, ug_ref, pad_ref, along_rows)
    bias = b_ref[...]

    def taps(first_tap, stride):
        def conv(r, carry):
            dst = pl.multiple_of(r * GRID_W, GRID_W)
            base = first_tap(r)
            acc = jnp.zeros((GRID_W, LANES), jnp.float32)
            for k in range(CONV_WIDTH):
                acc = acc + w_ref[k:k + 1, :] * pad_ref[pl.ds(base + k * stride, GRID_W), :]
            o_ref[0, pl.ds(dst, GRID_W), :] = (acc + bias).astype(o_ref.dtype)
            return carry

        lax.fori_loop(0, n_rows, conv, 0, unroll=8)

    @pl.when(along_rows)
    def _():
        taps(lambda r: r * CONV_PADW + (16 - CONV_HALF), 1)

    @pl.when(jnp.logical_not(along_rows))
    def _():
        taps(lambda r: pl.multiple_of(r * GRID_W, GRID_W), GRID_W)


def _axial_conv(aux_lat, conv_w, conv_b):
    bsz, t, _ = aux_lat.shape
    n_rows = t // GRID_W
    pad_rows = max(n_rows * CONV_PADW, t + 2 * CONV_HALF * GRID_W)
    w_pad = jnp.zeros((32, D_CONV), jnp.float32).at[:CONV_WIDTH].set(conv_w)
    return pl.pallas_call(
        _conv_kernel,
        grid=(bsz, D_CONV // LANES),
        in_specs=[pl.BlockSpec((1, t, LANES), lambda b, g: (b, 0, COL_U + g)),
                  pl.BlockSpec((1, t, LANES), lambda b, g: (b, 0, COL_UG + g)),
                  pl.BlockSpec((32, LANES), lambda b, g: (0, g)),
                  pl.BlockSpec((1, LANES), lambda b, g: (0, g))],
        out_specs=pl.BlockSpec((1, t, LANES), lambda b, g: (b, 0, g)),
        out_shape=jax.ShapeDtypeStruct((bsz, t, D_CONV), jnp.bfloat16),
        scratch_shapes=[pltpu.VMEM((pad_rows, LANES), jnp.float32)],
        compiler_params=pltpu.CompilerParams(dimension_semantics=("arbitrary", "arbitrary"),
                                             vmem_limit_bytes=VMEM_LIMIT),
        name="axial_conv",
    )(aux_lat, aux_lat, w_pad, conv_b)


def _out_kernel(x_ref, ba_ref, y_ref, gb_ref, gt_ref, lng_ref, lnb_ref, wa_ref, wb_ref, fg_ref, o_ref):
    y = y_ref[0].astype(jnp.float32)
    mu = jnp.mean(y, axis=-1, keepdims=True)
    yc = y - mu
    var = jnp.mean(yc * yc, axis=-1, keepdims=True)
    yn = yc * lax.rsqrt(var + EPS) * lng_ref[...] + lnb_ref[...]
    branch_b = _silu(yn) * _silu(gb_ref[0].astype(jnp.float32))
    mix = jnp.dot(ba_ref[0], wa_ref[...], preferred_element_type=jnp.float32)
    mix = mix + jnp.dot(branch_b.astype(jnp.bfloat16), wb_ref[...], preferred_element_type=jnp.float32)
    h = x_ref[0] + gt_ref[0] * mix
    o_ref[0] = h * lax.rsqrt(jnp.mean(h * h, axis=-1, keepdims=True) + EPS) * fg_ref[...]


def _output(x, branch_a, y_conv, aux_lat, gate, ln_g, ln_b, w_out_bf16, final_g):
    bsz, t, _ = x.shape
    rows = OUT_ROWS
    gb_block = COL_GB * LANES // D_CONV
    return pl.pallas_call(
        _out_kernel,
        grid=(bsz, t // rows),
        in_specs=[pl.BlockSpec((1, rows, D_MODEL), lambda b, i: (b, i, 0)),
                  pl.BlockSpec((1, rows, D_HGRN), lambda b, i: (b, i, 0)),
                  pl.BlockSpec((1, rows, D_CONV), lambda b, i: (b, i, 0)),
                  pl.BlockSpec((1, rows, D_CONV), lambda b, i: (b, i, gb_block)),
                  pl.BlockSpec((1, 1, D_MODEL), lambda b, i: (b, 0, 0)),
                  pl.BlockSpec((1, D_CONV), lambda b, i: (0, 0)),
                  pl.BlockSpec((1, D_CONV), lambda b, i: (0, 0)),
                  pl.BlockSpec((D_HGRN, D_MODEL), lambda b, i: (0, 0)),
                  pl.BlockSpec((D_CONV, D_MODEL), lambda b, i: (1, 0)),
                  pl.BlockSpec((1, D_MODEL), lambda b, i: (0, 0))],
        out_specs=pl.BlockSpec((1, rows, D_MODEL), lambda b, i: (b, i, 0)),
        out_shape=jax.ShapeDtypeStruct((bsz, t, D_MODEL), jnp.float32),
        compiler_params=pltpu.CompilerParams(dimension_semantics=("arbitrary", "arbitrary"),
                                             vmem_limit_bytes=VMEM_LIMIT),
        name="out_projection",
    )(x, branch_a, y_conv, aux_lat, gate, ln_g, ln_b, w_out_bf16, w_out_bf16, final_g)


def kernel(x, c, ctx, c_ctx, norm_g, w_mod, b_mod, w_in, lb_logits, hgrn_norm_g, conv_w, conv_b,
           conv_ln_g, conv_ln_b, w_out, final_norm_g):
    bsz, seq_len, _ = x.shape
    assert norm_g.shape[0] == 1, "single-layer block"
    assert seq_len % GRID_W == 0 and seq_len % SCAN_OUT_ROWS == 0 and seq_len % PROJ_ROWS == 0
    assert ctx.shape[1] % (2 * SCAN_CHUNK) == 0, "chunks are handled in pairs"

    pad = (-(bsz + 1)) % SUBLANES
    cc = jnp.concatenate([c, c_ctx[None, :], jnp.zeros((pad, D_MODEL), c.dtype)], axis=0)
    mod = _modulation(cc, w_mod[0], b_mod)
    shift, scale, gate = (mod[:, i * D_MODEL:(i + 1) * D_MODEL] for i in range(3))
    shift_lat, scale_lat, gate_lat = (m[:bsz, None, :] for m in (shift, scale, gate))
    shift_ctx, scale_ctx = (m[bsz][None, None, :] for m in (shift, scale))

    w_in_bf16 = w_in[0].astype(jnp.bfloat16)
    assert w_in_bf16.shape[1] == 2 * D_SCAN_IN
    p_lat, aux_lat = _projection(x, norm_g, shift_lat, scale_lat, w_in_bf16, (jnp.float32, jnp.bfloat16))
    p_ctx, = _projection(ctx.reshape(1, -1, D_MODEL), norm_g, shift_ctx, scale_ctx, w_in_bf16, (jnp.float32,))
    p_ctx = p_ctx.reshape(bsz, ctx.shape[1], D_SCAN_IN)

    branch_a = _hgrn_scan(p_lat, aux_lat, p_ctx, lb_logits, hgrn_norm_g)
    y_conv = _axial_conv(aux_lat, conv_w[0], conv_b)
    return _output(x, branch_a, y_conv, aux_lat, gate_lat, conv_ln_g, conv_ln_b,
                   w_out[0].astype(jnp.bfloat16), final_norm_g[None, :])
```

```python
import numpy as np
import jax
import jax.numpy as jnp
from jax import lax
from jax.experimental import pallas as pl
from jax.experimental.pallas import tpu as pltpu

D_MODEL = 1024
GRID_W = 64
D_HGRN = 512
HGRN_HEADS = 4
HEAD_DIM = D_HGRN // HGRN_HEADS
D_CONV = 512
CONV_WIDTH = 31
CONV_HALF = CONV_WIDTH // 2
EPS = 1e-6

LANES = 128
SUBLANES = 8
SCAN_CHUNK = 64
SCAN_LEVELS = SCAN_CHUNK.bit_length() - 1
SCAN_OUT_ROWS = 256
SCAN_STAGGER = 1
PROJ_ROWS = 512
OUT_ROWS = 1024
VMEM_LIMIT = 56 * 1024 * 1024

D_SCAN_IN = 4 * D_HGRN
COL_Q, COL_ZF, COL_ZB, COL_V = (i * HGRN_HEADS for i in range(4))
COL_GA, COL_U, COL_UG, COL_GB = (i * (D_HGRN // LANES) for i in range(4))


def _silu(x):
    return x * jax.nn.sigmoid(x)


def _mod_kernel(c_ref, w_ref, b_ref, o_ref):
    a = _silu(c_ref[...])
    o_ref[...] = jnp.dot(a, w_ref[...], preferred_element_type=jnp.float32,
                         precision=lax.Precision.HIGHEST) + b_ref[...]


def _modulation(cc, w_mod, b_mod):
    rows = cc.shape[0]
    n = w_mod.shape[1]
    return pl.pallas_call(
        _mod_kernel,
        grid=(n // D_MODEL,),
        in_specs=[pl.BlockSpec((rows, D_MODEL), lambda j: (0, 0)),
                  pl.BlockSpec((D_MODEL, D_MODEL), lambda j: (0, j)),
                  pl.BlockSpec((1, D_MODEL), lambda j: (0, j))],
        out_specs=pl.BlockSpec((rows, D_MODEL), lambda j: (0, j)),
        out_shape=jax.ShapeDtypeStruct((rows, n), jnp.float32),
        compiler_params=pltpu.CompilerParams(dimension_semantics=("arbitrary",),
                                             vmem_limit_bytes=VMEM_LIMIT),
        name="modulation",
    )(cc, w_mod, b_mod)


def _proj_kernel(x_ref, g_ref, sh_ref, sc_ref, *refs):
    x = x_ref[0]
    y = x * lax.rsqrt(jnp.mean(x * x, axis=-1, keepdims=True) + EPS) * g_ref[...]
    a = (y * (1.0 + sc_ref[0]) + sh_ref[0]).astype(jnp.bfloat16)
    n = len(refs) // 2
    for w_ref, o_ref in zip(refs[:n], refs[n:]):
        o_ref[0] = jnp.dot(a, w_ref[...], preferred_element_type=jnp.float32).astype(o_ref.dtype)


def _projection(x, norm_g, shift, scale, w_bf16, dtypes):
    bsz, t, _ = x.shape
    rows = min(PROJ_ROWS, t)
    width = D_SCAN_IN
    return pl.pallas_call(
        _proj_kernel,
        grid=(bsz, t // rows),
        in_specs=[pl.BlockSpec((1, rows, D_MODEL), lambda b, i: (b, i, 0)),
                  pl.BlockSpec((1, D_MODEL), lambda b, i: (0, 0)),
                  pl.BlockSpec((1, 1, D_MODEL), lambda b, i: (b, 0, 0)),
                  pl.BlockSpec((1, 1, D_MODEL), lambda b, i: (b, 0, 0))]
        + [pl.BlockSpec((D_MODEL, width), lambda b, i, j=j: (0, j)) for j in range(len(dtypes))],
        out_specs=[pl.BlockSpec((1, rows, width), lambda b, i: (b, i, 0)) for _ in dtypes],
        out_shape=[jax.ShapeDtypeStruct((bsz, t, width), dt) for dt in dtypes],
        compiler_params=pltpu.CompilerParams(dimension_semantics=("arbitrary", "arbitrary"),
                                             vmem_limit_bytes=VMEM_LIMIT),
        name="in_projection",
    )(x, norm_g, shift, scale, *([w_bf16] * len(dtypes)))


SCORE_BLOCKS = ((0, 0), (32, 1), (1, 0), (16, 0), (8, 1), (4, 0), (2, 1))


def _scan_constants():
    c = SCAN_CHUNK
    idx = np.arange(c)
    t, s = idx[:, None], idx[None, :]
    m_f, m_b = [], []
    for h, lane_half in SCORE_BLOCKS:
        if h == 0:
            m = t == s
        else:
            m = ((t // (2 * h)) == (s // (2 * h))) & ((t % (2 * h)) >= h) & ((s % (2 * h)) < h)
        for out, mm in ((m_f, m), (m_b, m[::-1, ::-1])):
            wide = np.zeros((c, 2 * c), np.float32)
            wide[:, lane_half * c:(lane_half + 1) * c] = mm
            out.append(wide)
    m_f, m_b = np.stack(m_f), np.stack(m_b)
    tri_f = (s <= t).astype(np.float32)
    tri_b = (s >= t).astype(np.float32)
    return np.concatenate([tri_f] * 2, axis=1), np.concatenate([tri_b] * 2, axis=1), m_f, m_b


def _dot_nt(a, b):
    return lax.dot_general(a, b, (((1,), (1,)), ((), ())), preferred_element_type=jnp.float32)


def _dot_tn(a, b):
    return lax.dot_general(a, b, (((0,), (0,)), ((), ())), preferred_element_type=jnp.float32)


def _interleave(*generators):
    pending = list(generators)
    while pending:
        for gen in list(pending):
            try:
                next(gen)
            except StopIteration:
                pending.remove(gen)


def _mix_rows(q, k, half, reverse):
    parts = []
    for lo in range(0, SCAN_CHUNK, 2 * half):
        first, second = (q, k) if reverse else (k, q)
        parts += [first[lo:lo + half], second[lo + half:lo + 2 * half]]
    return jnp.concatenate(parts, axis=0)


def _level_decay(g_cum, f, half, reverse):
    c = SCAN_CHUNK
    if half >= SUBLANES:
        parts = []
        for lo in range(0, c, 2 * half):
            mid = lo + half
            if reverse:
                parts += [g_cum[lo:mid] - g_cum[mid:mid + 1], g_cum[mid:mid + 1] - g_cum[mid:mid + half]]
            else:
                parts += [g_cum[mid - 1:mid] - g_cum[lo:mid], g_cum[mid:mid + half] - g_cum[mid - 1:mid]]
        return jnp.exp2(jnp.concatenate(parts, axis=0))
    if half == SUBLANES // 2:
        g3 = g_cum.reshape(c // SUBLANES, SUBLANES, LANES)
        r = half if reverse else half - 1
        later = lax.broadcasted_iota(jnp.int32, g3.shape, 1) >= half
        sign = jnp.where(later != reverse, 1.0, -1.0)
        return jnp.exp2((g3 - g3[:, r:r + 1, :]) * sign).reshape(c, LANES)
    f3 = f.reshape(c // SUBLANES, SUBLANES, LANES)
    row = lax.broadcasted_iota(jnp.int32, f3.shape, 1)
    if half == 1:
        on_query_side = (row % 2 == 0) if reverse else (row % 2 == 1)
        return jnp.where(on_query_side, f3, 1.0).reshape(c, LANES)
    prev = pltpu.roll(f3, 1, 1)
    nxt = pltpu.roll(f3, SUBLANES - 1, 1)
    m4 = row % 4
    if reverse:
        d = jnp.where(m4 == 0, f3 * nxt, jnp.where(m4 == 1, f3, jnp.where(m4 == 2, 1.0, prev)))
    else:
        d = jnp.where(m4 == 0, nxt, jnp.where(m4 == 1, 1.0, jnp.where(m4 == 2, f3, f3 * prev)))
    return d.reshape(c, LANES)


def _chunk_local(q, k, v, f, g_cum, m_ref, reverse, result):
    c = SCAN_CHUNK
    bf = jnp.bfloat16
    end_row = 0 if reverse else c - 1

    def scaled(half):
        d = _level_decay(g_cum, f, half, reverse)
        if half >= SUBLANES:
            x = _mix_rows(q, k, half, reverse) * d
            return x, x
        return q * d, k * d

    def operands(n):
        if n == 0:
            x32 = scaled(32)[0]
            return (q, x32, scaled(1)[0]), (k, x32)
        if n == 1:
            x16, x8 = scaled(16)[0], scaled(8)[0]
            return (x16, x8), (x16, x8)
        (q4, k4), (q2, k2) = scaled(4), scaled(2)
        return (q4, q2), (k4, k2)

    a = None
    block = 0
    for n in range(3):
        rows, keys = operands(n)
        w = jnp.concatenate(keys, axis=0).T.astype(bf)
        p = jnp.dot(jnp.concatenate([r.astype(bf) for r in rows], axis=0), w, preferred_element_type=jnp.float32)
        for i in range(len(rows)):
            term = m_ref[block] * p[i * c:(i + 1) * c]
            a = term if a is None else a + term
            block += 1
        yield
    d_read = jnp.exp2(g_cum)
    d_state = jnp.exp2(g_cum[end_row:end_row + 1] - g_cum)
    vb = v.astype(bf)
    o_intra = jnp.dot(a.astype(bf), jnp.concatenate([vb, vb], axis=0), preferred_element_type=jnp.float32)
    yield
    kv = _dot_tn(vb, (k * d_state).astype(bf))
    result.extend([o_intra, kv, (q * d_read).astype(bf), d_read[end_row:end_row + 1]])
    yield


def _chunk_pair(q_r, z_r, v_r, starts, lb, w_ref, m_ref, reverse, results):
    c = SCAN_CHUNK
    bf = jnp.bfloat16
    fs, g3s = [], []
    for s in starts:
        f = lb + (1.0 - lb) * jax.nn.sigmoid(z_r[0, pl.ds(s, c), :])
        g = jnp.log2(f)
        fs.append(f)
        g_hi = g.astype(bf)
        g_lo = (g - g_hi.astype(jnp.float32)).astype(bf)
        g3s.append(jnp.concatenate([g_hi, g_lo], axis=0))
        yield
    g_cum = jnp.dot(w_ref[...], jnp.concatenate(g3s, axis=1), preferred_element_type=jnp.float32)
    yield
    chains = [_chunk_local(q_r[0, pl.ds(s, c), :], 1.0 - fs[j], v_r[0, pl.ds(s, c), :], fs[j],
                           g_cum[:, j * LANES:(j + 1) * LANES], m_ref, reverse, results[j])
              for j, s in enumerate(starts)]
    while chains:
        for chain in list(chains):
            try:
                next(chain)
            except StopIteration:
                chains.remove(chain)
        yield


def _direction_chunks(q_r, z_r, v_r, starts, lb, w_ref, m_ref, reverse):
    results = [[] for _ in starts]
    gens = [_chunk_pair(q_r, z_r, v_r, starts[j:j + 2], lb, w_ref, m_ref, reverse, results[j:j + 2])
            for j in range(0, len(starts), 2)]
    return gens, results


def _direction_states(chunks, stored):
    c = SCAN_CHUNK
    st = jnp.zeros((HEAD_DIM, HEAD_DIM), jnp.float32)
    for s, result, o_r in chunks:
        while not result:
            yield
        o_intra, kv, q_read, d_end = result
        if o_r is not None:
            o_r[pl.ds(s, c), :] = o_intra + jnp.dot(q_read, st.T.astype(jnp.bfloat16),
                                                    preferred_element_type=jnp.float32)
            stored[0] += 1
        st = st * d_end + kv
        yield


def _normalise_and_gate(of_ref, ob_ref, ga_ref, gn, o_ref, lo):
    rows = pl.ds(lo, SCAN_OUT_ROWS)
    o = of_ref[rows, :] + ob_ref[rows, :]
    o = o * lax.rsqrt(jnp.mean(o * o, axis=-1, keepdims=True) + EPS) * gn
    gate = _silu(ga_ref[0, rows, :].astype(jnp.float32))
    o_ref[0, rows, :] = (o * gate).astype(o_ref.dtype)


def _output_rows(of_ref, ob_ref, ga_ref, gn, o_ref, n_chunks, stored_f, stored_b):
    per = SCAN_OUT_ROWS // SCAN_CHUNK

    def ready_after(b):
        return per * (b + 1), n_chunks - per * b

    for b in sorted(range(n_chunks // per), key=lambda b: max(ready_after(b))):
        need_f, need_b = ready_after(b)
        while stored_f[0] < need_f or stored_b[0] < need_b:
            yield
        _normalise_and_gate(of_ref, ob_ref, ga_ref, gn, o_ref, b * SCAN_OUT_ROWS)
        yield


def _delayed(generator, rounds):
    for _ in range(rounds):
        yield
    yield from generator


def _scan_kernel(q_ref, zf_ref, zb_ref, v_ref, ga_ref, qc_ref, zfc_ref, zbc_ref, vc_ref,
                 lbl_ref, gn_ref, wf_ref, wb_ref, mf_ref, mb_ref, o_ref, of_ref, ob_ref):
    c = SCAN_CHUNK
    l0, l1 = lbl_ref[0], lbl_ref[1]
    mx = jnp.maximum(l0, l1)
    e0, e1 = jnp.exp(l0 - mx), jnp.exp(l1 - mx)
    lb = e0 / (e0 + e1)
    lb_f, lb_b = lb[0:1], lb[1:2]

    gens_f, gens_b, chunks_f, chunks_b = [], [], [], []
    for q_r, zf_r, zb_r, v_r, of_r, ob_r in ((qc_ref, zfc_ref, zbc_ref, vc_ref, None, None),
                                             (q_ref, zf_ref, zb_ref, v_ref, of_ref, ob_ref)):
        t = q_r.shape[1]
        starts_f = list(range(0, t, c))
        starts_b = starts_f[::-1]
        g_f, res_f = _direction_chunks(q_r, zf_r, v_r, starts_f, lb_f, wf_ref, mf_ref, False)
        g_b, res_b = _direction_chunks(q_r, zb_r, v_r, starts_b, lb_b, wb_ref, mb_ref, True)
        gens_f += g_f
        gens_b += g_b
        chunks_f += [(s, r, of_r) for s, r in zip(starts_f, res_f)]
        chunks_b += [(s, r, ob_r) for s, r in zip(starts_b, res_b)]
    staggered = [_delayed(g, SCAN_STAGGER * j) for j, pair in enumerate(zip(gens_f, gens_b)) for g in pair]
    stored_f, stored_b = [0], [0]
    _interleave(*staggered,
                _direction_states(chunks_f, stored_f),
                _direction_states(chunks_b, stored_b),
                _output_rows(of_ref, ob_ref, ga_ref, gn_ref[...], o_ref, q_ref.shape[1] // c, stored_f, stored_b))


def _hgrn_scan(p_lat, aux_lat, p_ctx, lb_logits, hgrn_norm_g):
    bsz, t_lat, _ = p_lat.shape
    t_ctx = p_ctx.shape[1]
    w_f3, w_b3, m_f, m_b = _scan_constants()
    wf = jnp.asarray(w_f3, jnp.bfloat16)
    wb = jnp.asarray(w_b3, jnp.bfloat16)
    mf = jnp.asarray(m_f, jnp.float32)
    mb = jnp.asarray(m_b, jnp.float32)

    def col(t, base):
        return pl.BlockSpec((1, t, LANES), lambda b, h, base=base: (b, 0, base + h))

    def whole(a):
        return pl.BlockSpec(a.shape, lambda b, h, nd=a.ndim: (0,) * nd)

    return pl.pallas_call(
        _scan_kernel,
        grid=(bsz, HGRN_HEADS),
        in_specs=[col(t_lat, COL_Q), col(t_lat, COL_ZF), col(t_lat, COL_ZB), col(t_lat, COL_V),
                  col(t_lat, COL_GA),
                  col(t_ctx, COL_Q), col(t_ctx, COL_ZF), col(t_ctx, COL_ZB), col(t_ctx, COL_V),
                  pl.BlockSpec((2, 2, LANES), lambda b, h: (0, 0, h)),
                  pl.BlockSpec((1, LANES), lambda b, h: (0, h)),
                  whole(wf), whole(wb), whole(mf), whole(mb)],
        out_specs=pl.BlockSpec((1, t_lat, LANES), lambda b, h: (b, 0, h)),
        out_shape=jax.ShapeDtypeStruct((bsz, t_lat, D_HGRN), jnp.bfloat16),
        scratch_shapes=[pltpu.VMEM((t_lat, HEAD_DIM), jnp.float32),
                        pltpu.VMEM((t_lat, HEAD_DIM), jnp.float32)],
        compiler_params=pltpu.CompilerParams(dimension_semantics=("arbitrary", "arbitrary"),
                                             vmem_limit_bytes=VMEM_LIMIT),
        name="hgrn_scan",
    )(p_lat, p_lat, p_lat, p_lat, aux_lat, p_ctx, p_ctx, p_ctx, p_ctx,
      lb_logits, hgrn_norm_g, wf, wb, mf, mb)


CONV_PADW = GRID_W + 32


def _conv_fill(u_ref, ug_ref, pad_ref, along_rows):
    t = u_ref.shape[1]
    n_rows = t // GRID_W

    @pl.when(along_rows)
    def _along_rows():
        padw = CONV_PADW

        def fill(r, carry):
            src = pl.multiple_of(r * GRID_W, GRID_W)
            dst = pl.multiple_of(r * padw, 32)
            glu = (u_ref[0, pl.ds(src, GRID_W), :].astype(jnp.float32)
                   * jax.nn.sigmoid(ug_ref[0, pl.ds(src, GRID_W), :].astype(jnp.float32)))
            pad_ref[pl.ds(dst, 16), :] = jnp.zeros((16, LANES), jnp.float32)
            pad_ref[pl.ds(dst + 16, GRID_W), :] = glu
            pad_ref[pl.ds(dst + 16 + GRID_W, 16), :] = jnp.zeros((16, LANES), jnp.float32)
            return carry

        lax.fori_loop(0, n_rows, fill, 0, unroll=4)

    @pl.when(jnp.logical_not(along_rows))
    def _along_cols():
        halo = CONV_HALF * GRID_W
        pad_ref[pl.ds(0, halo), :] = jnp.zeros((halo, LANES), jnp.float32)
        pad_ref[pl.ds(halo + t, halo), :] = jnp.zeros((halo, LANES), jnp.float32)

        def fill(r, carry):
            src = pl.multiple_of(r * GRID_W, GRID_W)
            glu = (u_ref[0, pl.ds(src, GRID_W), :].astype(jnp.float32)
                   * jax.nn.sigmoid(ug_ref[0, pl.ds(src, GRID_W), :].astype(jnp.float32)))
            pad_ref[pl.ds(halo + src, GRID_W), :] = glu
            return carry

        lax.fori_loop(0, n_rows, fill, 0, unroll=4)


def _conv_kernel(u_ref, ug_ref, w_ref, b_ref, o_ref, pad_ref):
    n_rows = u_ref.shape[1] // GRID_W
    along_rows = pl.program_id(1) < (D_CONV // 2) // LANES
    _conv_fill(u_ref, ug_ref, pad_ref, along_rows)
    bias = b_ref[...]

    def taps(first_tap, stride):
        def conv(r, carry):
            dst = pl.multiple_of(r * GRID_W, GRID_W)
            base = first_tap(r)
            acc = jnp.zeros((GRID_W, LANES), jnp.float32)
            for k in range(CONV_WIDTH):
                acc = acc + w_ref[k:k + 1, :] * pad_ref[pl.ds(base + k * stride, GRID_W), :]
            o_ref[0, pl.ds(dst, GRID_W), :] = (acc + bias).astype(o_ref.dtype)
            return carry

        lax.fori_loop(0, n_rows, conv, 0, unroll=8)

    @pl.when(along_rows)
    def _():
        taps(lambda r: r * CONV_PADW + (16 - CONV_HALF), 1)

    @pl.when(jnp.logical_not(along_rows))
    def _():
        taps(lambda r: pl.multiple_of(r * GRID_W, GRID_W), GRID_W)


def _axial_conv(aux_lat, conv_w, conv_b):
    bsz, t, _ = aux_lat.shape
    n_rows = t // GRID_W
    pad_rows = max(n_rows * CONV_PADW, t + 2 * CONV_HALF * GRID_W)
    w_pad = jnp.zeros((32, D_CONV), jnp.float32).at[:CONV_WIDTH].set(conv_w)
    return pl.pallas_call(
        _conv_kernel,
        grid=(bsz, D_CONV // LANES),
        in_specs=[pl.BlockSpec((1, t, LANES), lambda b, g: (b, 0, COL_U + g)),
                  pl.BlockSpec((1, t, LANES), lambda b, g: (b, 0, COL_UG + g)),
                  pl.BlockSpec((32, LANES), lambda b, g: (0, g)),
                  pl.BlockSpec((1, LANES), lambda b, g: (0, g))],
        out_specs=pl.BlockSpec((1, t, LANES), lambda b, g: (b, 0, g)),
        out_shape=jax.ShapeDtypeStruct((bsz, t, D_CONV), jnp.bfloat16),
        scratch_shapes=[pltpu.VMEM((pad_rows, LANES), jnp.float32)],
        compiler_params=pltpu.CompilerParams(dimension_semantics=("arbitrary", "arbitrary"),
                                             vmem_limit_bytes=VMEM_LIMIT),
        name="axial_conv",
    )(aux_lat, aux_lat, w_pad, conv_b)


def _out_kernel(x_ref, ba_ref, y_ref, gb_ref, gt_ref, lng_ref, lnb_ref, wa_ref, wb_ref, fg_ref, o_ref):
    y = y_ref[0].astype(jnp.float32)
    mu = jnp.mean(y, axis=-1, keepdims=True)
    yc = y - mu
    var = jnp.mean(yc * yc, axis=-1, keepdims=True)
    yn = yc * lax.rsqrt(var + EPS) * lng_ref[...] + lnb_ref[...]
    branch_b = _silu(yn) * _silu(gb_ref[0].astype(jnp.float32))
    mix = jnp.dot(ba_ref[0], wa_ref[...], preferred_element_type=jnp.float32)
    mix = mix + jnp.dot(branch_b.astype(jnp.bfloat16), wb_ref[...], preferred_element_type=jnp.float32)
    h = x_ref[0] + gt_ref[0] * mix
    o_ref[0] = h * lax.rsqrt(jnp.mean(h * h, axis=-1, keepdims=True) + EPS) * fg_ref[...]


def _output(x, branch_a, y_conv, aux_lat, gate, ln_g, ln_b, w_out_bf16, final_g):
    bsz, t, _ = x.shape
    rows = OUT_ROWS
    gb_block = COL_GB * LANES // D_CONV
    return pl.pallas_call(
        _out_kernel,
        grid=(bsz, t // rows),
        in_specs=[pl.BlockSpec((1, rows, D_MODEL), lambda b, i: (b, i, 0)),
                  pl.BlockSpec((1, rows, D_HGRN), lambda b, i: (b, i, 0)),
                  pl.BlockSpec((1, rows, D_CONV), lambda b, i: (b, i, 0)),
                  pl.BlockSpec((1, rows, D_CONV), lambda b, i: (b, i, gb_block)),
                  pl.BlockSpec((1, 1, D_MODEL), lambda b, i: (b, 0, 0)),
                  pl.BlockSpec((1, D_CONV), lambda b, i: (0, 0)),
                  pl.BlockSpec((1, D_CONV), lambda b, i: (0, 0)),
                  pl.BlockSpec((D_HGRN, D_MODEL), lambda b, i: (0, 0)),
                  pl.BlockSpec((D_CONV, D_MODEL), lambda b, i: (1, 0)),
                  pl.BlockSpec((1, D_MODEL), lambda b, i: (0, 0))],
        out_specs=pl.BlockSpec((1, rows, D_MODEL), lambda b, i: (b, i, 0)),
        out_shape=jax.ShapeDtypeStruct((bsz, t, D_MODEL), jnp.float32),
        compiler_params=pltpu.CompilerParams(dimension_semantics=("arbitrary", "arbitrary"),
                                             vmem_limit_bytes=VMEM_LIMIT),
        name="out_projection",
    )(x, branch_a, y_conv, aux_lat, gate, ln_g, ln_b, w_out_bf16, w_out_bf16, final_g)


def kernel(x, c, ctx, c_ctx, norm_g, w_mod, b_mod, w_in, lb_logits, hgrn_norm_g, conv_w, conv_b,
           conv_ln_g, conv_ln_b, w_out, final_norm_g):
    bsz, seq_len, _ = x.shape
    assert norm_g.shape[0] == 1, "single-layer block"
    assert seq_len % GRID_W == 0 and seq_len % SCAN_OUT_ROWS == 0 and seq_len % PROJ_ROWS == 0
    assert ctx.shape[1] % (2 * SCAN_CHUNK) == 0, "chunks are handled in pairs"

    pad = (-(bsz + 1)) % SUBLANES
    cc = jnp.concatenate([c, c_ctx[None, :], jnp.zeros((pad, D_MODEL), c.dtype)], axis=0)
    mod = _modulation(cc, w_mod[0], b_mod)
    shift, scale, gate = (mod[:, i * D_MODEL:(i + 1) * D_MODEL] for i in range(3))
    shift_lat, scale_lat, gate_lat = (m[:bsz, None, :] for m in (shift, scale, gate))
    shift_ctx, scale_ctx = (m[bsz][None, None, :] for m in (shift, scale))

    w_in_bf16 = w_in[0].astype(jnp.bfloat16)
    assert w_in_bf16.shape[1] == 2 * D_SCAN_IN
    p_lat, aux_lat = _projection(x, norm_g, shift_lat, scale_lat, w_in_bf16, (jnp.float32, jnp.bfloat16))
    p_ctx, = _projection(ctx.reshape(1, -1, D_MODEL), norm_g, shift_ctx, scale_ctx, w_in_bf16, (jnp.float32,))
    p_ctx = p_ctx.reshape(bsz, ctx.shape[1], D_SCAN_IN)

    branch_a = _hgrn_scan(p_lat, aux_lat, p_ctx, lb_logits, hgrn_norm_g)
    y_conv = _axial_conv(aux_lat, conv_w[0], conv_b)
    return _output(x, branch_a, y_conv, aux_lat, gate_lat, conv_ln_g, conv_ln_b,
                   w_out[0].astype(jnp.bfloat16), final_norm_g[None, :])
```

```python
import numpy as np
import jax
import jax.numpy as jnp
from jax import lax
from jax.experimental import pallas as pl
from jax.experimental.pallas import tpu as pltpu

D_MODEL = 1024
GRID_W = 64
D_HGRN = 512
HGRN_HEADS = 4
HEAD_DIM = D_HGRN // HGRN_HEADS
D_CONV = 512
CONV_WIDTH = 31
CONV_HALF = CONV_WIDTH // 2
EPS = 1e-6

LANES = 128
SUBLANES = 8
SCAN_CHUNK = 64
SCAN_LEVELS = SCAN_CHUNK.bit_length() - 1
SCAN_OUT_ROWS = 256
SCAN_STAGGER = 2
PROJ_ROWS = 512
OUT_ROWS = 1024
VMEM_LIMIT = 56 * 1024 * 1024

D_SCAN_IN = 4 * D_HGRN
COL_Q, COL_ZF, COL_ZB, COL_V = (i * HGRN_HEADS for i in range(4))
COL_GA, COL_U, COL_UG, COL_GB = (i * (D_HGRN // LANES) for i in range(4))


def _silu(x):
    return x * jax.nn.sigmoid(x)


def _mod_kernel(c_ref, w_ref, b_ref, o_ref):
    a = _silu(c_ref[...])
    o_ref[...] = jnp.dot(a, w_ref[...], preferred_element_type=jnp.float32,
                         precision=lax.Precision.HIGHEST) + b_ref[...]


def _modulation(cc, w_mod, b_mod):
    rows = cc.shape[0]
    n = w_mod.shape[1]
    return pl.pallas_call(
        _mod_kernel,
        grid=(n // D_MODEL,),
        in_specs=[pl.BlockSpec((rows, D_MODEL), lambda j: (0, 0)),
                  pl.BlockSpec((D_MODEL, D_MODEL), lambda j: (0, j)),
                  pl.BlockSpec((1, D_MODEL), lambda j: (0, j))],
        out_specs=pl.BlockSpec((rows, D_MODEL), lambda j: (0, j)),
        out_shape=jax.ShapeDtypeStruct((rows, n), jnp.float32),
        compiler_params=pltpu.CompilerParams(dimension_semantics=("arbitrary",),
                                             vmem_limit_bytes=VMEM_LIMIT),
        name="modulation",
    )(cc, w_mod, b_mod)


def _proj_kernel(x_ref, g_ref, sh_ref, sc_ref, *refs):
    x = x_ref[0]
    y = x * lax.rsqrt(jnp.mean(x * x, axis=-1, keepdims=True) + EPS) * g_ref[...]
    a = (y * (1.0 + sc_ref[0]) + sh_ref[0]).astype(jnp.bfloat16)
    n = len(refs) // 2
    for w_ref, o_ref in zip(refs[:n], refs[n:]):
        o_ref[0] = jnp.dot(a, w_ref[...], preferred_element_type=jnp.float32).astype(o_ref.dtype)


def _projection(x, norm_g, shift, scale, w_bf16, dtypes):
    bsz, t, _ = x.shape
    rows = min(PROJ_ROWS, t)
    width = D_SCAN_IN
    return pl.pallas_call(
        _proj_kernel,
        grid=(bsz, t // rows),
        in_specs=[pl.BlockSpec((1, rows, D_MODEL), lambda b, i: (b, i, 0)),
                  pl.BlockSpec((1, D_MODEL), lambda b, i: (0, 0)),
                  pl.BlockSpec((1, 1, D_MODEL), lambda b, i: (b, 0, 0)),
                  pl.BlockSpec((1, 1, D_MODEL), lambda b, i: (b, 0, 0))]
        + [pl.BlockSpec((D_MODEL, width), lambda b, i, j=j: (0, j)) for j in range(len(dtypes))],
        out_specs=[pl.BlockSpec((1, rows, width), lambda b, i: (b, i, 0)) for _ in dtypes],
        out_shape=[jax.ShapeDtypeStruct((bsz, t, width), dt) for dt in dtypes],
        compiler_params=pltpu.CompilerParams(dimension_semantics=("arbitrary", "arbitrary"),
                                             vmem_limit_bytes=VMEM_LIMIT),
        name="in_projection",
    )(x, norm_g, shift, scale, *([w_bf16] * len(dtypes)))


def _scan_constants():
    c = SCAN_CHUNK
    idx = np.arange(c)
    t, s = idx[:, None], idx[None, :]
    masks = [t == s]
    for lvl in range(SCAN_LEVELS):
        h = c >> (lvl + 1)
        masks.append(((t // (2 * h)) == (s // (2 * h))) & ((t % (2 * h)) >= h) & ((s % (2 * h)) < h))
    m_f = np.stack([m.astype(np.float32) for m in masks])
    m_b = m_f[:, ::-1, ::-1].copy()
    tri_f = (s <= t).astype(np.float32)
    tri_b = (s >= t).astype(np.float32)
    return np.concatenate([tri_f] * 2, axis=1), np.concatenate([tri_b] * 2, axis=1), m_f, m_b


def _dot_nt(a, b):
    return lax.dot_general(a, b, (((1,), (1,)), ((), ())), preferred_element_type=jnp.float32)


def _dot_tn(a, b):
    return lax.dot_general(a, b, (((0,), (0,)), ((), ())), preferred_element_type=jnp.float32)


def _interleave(*generators):
    pending = list(generators)
    while pending:
        for gen in list(pending):
            try:
                next(gen)
            except StopIteration:
                pending.remove(gen)


def _mix_rows(q, k, half, reverse):
    parts = []
    for lo in range(0, SCAN_CHUNK, 2 * half):
        first, second = (q, k) if reverse else (k, q)
        parts += [first[lo:lo + half], second[lo + half:lo + 2 * half]]
    return jnp.concatenate(parts, axis=0)


def _level_decay(g_cum, f, half, reverse):
    c = SCAN_CHUNK
    if half >= SUBLANES:
        parts = []
        for lo in range(0, c, 2 * half):
            mid = lo + half
            if reverse:
                parts += [g_cum[lo:mid] - g_cum[mid:mid + 1], g_cum[mid:mid + 1] - g_cum[mid:mid + half]]
            else:
                parts += [g_cum[mid - 1:mid] - g_cum[lo:mid], g_cum[mid:mid + half] - g_cum[mid - 1:mid]]
        return jnp.exp2(jnp.concatenate(parts, axis=0))
    if half == SUBLANES // 2:
        g3 = g_cum.reshape(c // SUBLANES, SUBLANES, LANES)
        r = half if reverse else half - 1
        later = lax.broadcasted_iota(jnp.int32, g3.shape, 1) >= half
        sign = jnp.where(later != reverse, 1.0, -1.0)
        return jnp.exp2((g3 - g3[:, r:r + 1, :]) * sign).reshape(c, LANES)
    f3 = f.reshape(c // SUBLANES, SUBLANES, LANES)
    row = lax.broadcasted_iota(jnp.int32, f3.shape, 1)
    if half == 1:
        on_query_side = (row % 2 == 0) if reverse else (row % 2 == 1)
        return jnp.where(on_query_side, f3, 1.0).reshape(c, LANES)
    prev = pltpu.roll(f3, 1, 1)
    nxt = pltpu.roll(f3, SUBLANES - 1, 1)
    m4 = row % 4
    if reverse:
        d = jnp.where(m4 == 0, f3 * nxt, jnp.where(m4 == 1, f3, jnp.where(m4 == 2, 1.0, prev)))
    else:
        d = jnp.where(m4 == 0, nxt, jnp.where(m4 == 1, 1.0, jnp.where(m4 == 2, f3, f3 * prev)))
    return d.reshape(c, LANES)


def _chunk_local(q, k, v, f, g_cum, m_ref, reverse, result):
    c = SCAN_CHUNK
    bf = jnp.bfloat16
    end_row = 0 if reverse else c - 1

    qb, kb = q.astype(bf), k.astype(bf)
    piece = 2 * SUBLANES
    q_fine = qb * _level_decay(g_cum, f, 1, reverse).astype(bf)
    p = _dot_nt(jnp.concatenate([qb, q_fine], axis=0), kb)
    a = [m_ref[0, r:r + piece] * p[r:r + piece] + m_ref[SCAN_LEVELS, r:r + piece] * p[c + r:c + r + piece]
         for r in range(0, c, piece)]
    yield
    for half in (SUBLANES, 2) + tuple(h for h in (c >> (lvl + 1) for lvl in range(SCAN_LEVELS))
                                      if h not in (SUBLANES, 2, 1)):
        d = _level_decay(g_cum, f, half, reverse).astype(bf)
        rows = list(range(0, c, piece))
        if half >= piece:
            x = _mix_rows(qb, kb, half, reverse) * d
            firsts = [lo + (0 if reverse else half) for lo in range(0, c, 2 * half)]
            rows = [r for s in firsts for r in range(s, s + half, piece)]
            p = _dot_nt(jnp.concatenate([x[s:s + half] for s in firsts], axis=0), x)
        elif half == SUBLANES:
            x = (_mix_rows(q, k, half, reverse)).astype(bf) * d
            p = _dot_nt(x, x)
        else:
            p = _dot_nt(qb * d, kb * d)
        m_level = SCAN_LEVELS - half.bit_length() + 1
        for i, r in enumerate(rows):
            a[r // piece] = a[r // piece] + m_ref[m_level, r:r + piece] * p[i * piece:(i + 1) * piece]
        yield
    d_read = jnp.exp2(g_cum)
    d_state = jnp.exp2(g_cum[end_row:end_row + 1] - g_cum)
    vb = v.astype(bf)
    o_intra = jnp.dot(jnp.concatenate(a, axis=0).astype(bf), vb, preferred_element_type=jnp.float32)
    yield
    kv = _dot_tn(vb, kb * d_state.astype(bf))
    result.extend([o_intra, kv, qb * d_read.astype(bf), d_read[end_row:end_row + 1]])
    yield


def _chunk_pair(q_r, z_r, v_r, starts, lb, w_ref, m_ref, reverse, results):
    c = SCAN_CHUNK
    bf = jnp.bfloat16
    fs, g3s = [], []
    for s in starts:
        f = lb + (1.0 - lb) * jax.nn.sigmoid(z_r[0, pl.ds(s, c), :])
        g = jnp.log2(f)
        fs.append(f)
        g_hi = g.astype(bf)
        g_lo = (g - g_hi.astype(jnp.float32)).astype(bf)
        g3s.append(jnp.concatenate([g_hi, g_lo], axis=0))
        yield
    g_cum = jnp.dot(w_ref[...], jnp.concatenate(g3s, axis=1), preferred_element_type=jnp.float32)
    yield
    chains = [_chunk_local(q_r[0, pl.ds(s, c), :], 1.0 - fs[j], v_r[0, pl.ds(s, c), :], fs[j],
                           g_cum[:, j * LANES:(j + 1) * LANES], m_ref, reverse, results[j])
              for j, s in enumerate(starts)]
    while chains:
        for chain in list(chains):
            try:
                next(chain)
            except StopIteration:
                chains.remove(chain)
        yield


def _direction_chunks(q_r, z_r, v_r, starts, lb, w_ref, m_ref, reverse):
    results = [[] for _ in starts]
    gens = [_chunk_pair(q_r, z_r, v_r, starts[j:j + 2], lb, w_ref, m_ref, reverse, results[j:j + 2])
            for j in range(0, len(starts), 2)]
    return gens, results


def _direction_states(chunks, stored):
    c = SCAN_CHUNK
    st = jnp.zeros((HEAD_DIM, HEAD_DIM), jnp.float32)
    for s, result, o_r in chunks:
        while not result:
            yield
        o_intra, kv, q_read, d_end = result
        if o_r is not None:
            o_r[pl.ds(s, c), :] = o_intra + jnp.dot(q_read, st.T.astype(jnp.bfloat16),
                                                    preferred_element_type=jnp.float32)
            stored[0] += 1
        st = st * d_end + kv
        yield


def _normalise_and_gate(of_ref, ob_ref, ga_ref, gn, o_ref, lo):
    rows = pl.ds(lo, SCAN_OUT_ROWS)
    o = of_ref[rows, :] + ob_ref[rows, :]
    o = o * lax.rsqrt(jnp.mean(o * o, axis=-1, keepdims=True) + EPS) * gn
    gate = _silu(ga_ref[0, rows, :].astype(jnp.float32))
    o_ref[0, rows, :] = (o * gate).astype(o_ref.dtype)


def _output_rows(of_ref, ob_ref, ga_ref, gn, o_ref, n_chunks, stored_f, stored_b):
    per = SCAN_OUT_ROWS // SCAN_CHUNK

    def ready_after(b):
        return per * (b + 1), n_chunks - per * b

    for b in sorted(range(n_chunks // per), key=lambda b: max(ready_after(b))):
        need_f, need_b = ready_after(b)
        while stored_f[0] < need_f or stored_b[0] < need_b:
            yield
        _normalise_and_gate(of_ref, ob_ref, ga_ref, gn, o_ref, b * SCAN_OUT_ROWS)
        yield


def _delayed(generator, rounds):
    for _ in range(rounds):
        yield
    yield from generator


def _scan_kernel(q_ref, zf_ref, zb_ref, v_ref, ga_ref, qc_ref, zfc_ref, zbc_ref, vc_ref,
                 lbl_ref, gn_ref, wf_ref, wb_ref, mf_ref, mb_ref, o_ref, of_ref, ob_ref):
    c = SCAN_CHUNK
    l0, l1 = lbl_ref[0], lbl_ref[1]
    mx = jnp.maximum(l0, l1)
    e0, e1 = jnp.exp(l0 - mx), jnp.exp(l1 - mx)
    lb = e0 / (e0 + e1)
    lb_f, lb_b = lb[0:1], lb[1:2]

    gens_f, gens_b, chunks_f, chunks_b = [], [], [], []
    for q_r, zf_r, zb_r, v_r, of_r, ob_r in ((qc_ref, zfc_ref, zbc_ref, vc_ref, None, None),
                                             (q_ref, zf_ref, zb_ref, v_ref, of_ref, ob_ref)):
        t = q_r.shape[1]
        starts_f = list(range(0, t, c))
        starts_b = starts_f[::-1]
        g_f, res_f = _direction_chunks(q_r, zf_r, v_r, starts_f, lb_f, wf_ref, mf_ref, False)
        g_b, res_b = _direction_chunks(q_r, zb_r, v_r, starts_b, lb_b, wb_ref, mb_ref, True)
        gens_f += g_f
        gens_b += g_b
        chunks_f += [(s, r, of_r) for s, r in zip(starts_f, res_f)]
        chunks_b += [(s, r, ob_r) for s, r in zip(starts_b, res_b)]
    staggered = [_delayed(g, SCAN_STAGGER * j) for j, pair in enumerate(zip(gens_f, gens_b)) for g in pair]
    stored_f, stored_b = [0], [0]
    _interleave(*staggered,
                _direction_states(chunks_f, stored_f),
                _direction_states(chunks_b, stored_b),
                _output_rows(of_ref, ob_ref, ga_ref, gn_ref[...], o_ref, q_ref.shape[1] // c, stored_f, stored_b))


def _hgrn_scan(p_lat, aux_lat, p_ctx, lb_logits, hgrn_norm_g):
    bsz, t_lat, _ = p_lat.shape
    t_ctx = p_ctx.shape[1]
    w_f3, w_b3, m_f, m_b = _scan_constants()
    wf = jnp.asarray(w_f3, jnp.bfloat16)
    wb = jnp.asarray(w_b3, jnp.bfloat16)
    mf = jnp.asarray(m_f, jnp.float32)
    mb = jnp.asarray(m_b, jnp.float32)

    def col(t, base):
        return pl.BlockSpec((1, t, LANES), lambda b, h, base=base: (b, 0, base + h))

    def whole(a):
        return pl.BlockSpec(a.shape, lambda b, h, nd=a.ndim: (0,) * nd)

    return pl.pallas_call(
        _scan_kernel,
        grid=(bsz, HGRN_HEADS),
        in_specs=[col(t_lat, COL_Q), col(t_lat, COL_ZF), col(t_lat, COL_ZB), col(t_lat, COL_V),
                  col(t_lat, COL_GA),
                  col(t_ctx, COL_Q), col(t_ctx, COL_ZF), col(t_ctx, COL_ZB), col(t_ctx, COL_V),
                  pl.BlockSpec((2, 2, LANES), lambda b, h: (0, 0, h)),
                  pl.BlockSpec((1, LANES), lambda b, h: (0, h)),
                  whole(wf), whole(wb), whole(mf), whole(mb)],
        out_specs=pl.BlockSpec((1, t_lat, LANES), lambda b, h: (b, 0, h)),
        out_shape=jax.ShapeDtypeStruct((bsz, t_lat, D_HGRN), jnp.bfloat16),
        scratch_shapes=[pltpu.VMEM((t_lat, HEAD_DIM), jnp.float32),
                        pltpu.VMEM((t_lat, HEAD_DIM), jnp.float32)],
        compiler_params=pltpu.CompilerParams(dimension_semantics=("arbitrary", "arbitrary"),
                                             vmem_limit_bytes=VMEM_LIMIT),
        name="hgrn_scan",
    )(p_lat, p_lat, p_lat, p_lat, aux_lat, p_ctx, p_ctx, p_ctx, p_ctx,
      lb_logits, hgrn_norm_g, wf, wb, mf, mb)


CONV_PADW = GRID_W + 32


def _conv_fill(u_ref, ug_ref, pad_ref, along_rows):
    t = u_ref.shape[1]
    n_rows = t // GRID_W

    @pl.when(along_rows)
    def _along_rows():
        padw = CONV_PADW

        def fill(r, carry):
            src = pl.multiple_of(r * GRID_W, GRID_W)
            dst = pl.multiple_of(r * padw, 32)
            glu = (u_ref[0, pl.ds(src, GRID_W), :].astype(jnp.float32)
                   * jax.nn.sigmoid(ug_ref[0, pl.ds(src, GRID_W), :].astype(jnp.float32)))
            pad_ref[pl.ds(dst, 16), :] = jnp.zeros((16, LANES), jnp.float32)
            pad_ref[pl.ds(dst + 16, GRID_W), :] = glu
            pad_ref[pl.ds(dst + 16 + GRID_W, 16), :] = jnp.zeros((16, LANES), jnp.float32)
            return carry

        lax.fori_loop(0, n_rows, fill, 0, unroll=4)

    @pl.when(jnp.logical_not(along_rows))
    def _along_cols():
        halo = CONV_HALF * GRID_W
        pad_ref[pl.ds(0, halo), :] = jnp.zeros((halo, LANES), jnp.float32)
        pad_ref[pl.ds(halo + t, halo), :] = jnp.zeros((halo, LANES), jnp.float32)

        def fill(r, carry):
            src = pl.multiple_of(r * GRID_W, GRID_W)
            glu = (u_ref[0, pl.ds(src, GRID_W), :].astype(jnp.float32)
                   * jax.nn.sigmoid(ug_ref[0, pl.ds(src, GRID_W), :].astype(jnp.float32)))
            pad_ref[pl.ds(halo + src, GRID_W), :] = glu
            return carry

        lax.fori_loop(0, n_rows, fill, 0, unroll=4)


def _conv_kernel(u_ref, ug_ref, w_ref, b_ref, o_ref, pad_ref):
    n_rows = u_ref.shape[1] // GRID_W
    along_rows = pl.program_id(1) < (D_CONV // 2) // LANES
    _conv_fill(u_ref, ug_ref, pad_ref, along_rows)
    bias = b_ref[...]

    def taps(first_tap, stride):
        def conv(r, carry):
            dst = pl.multiple_of(r * GRID_W, GRID_W)
            base = first_tap(r)
            acc = jnp.zeros((GRID_W, LANES), jnp.float32)
            for k in range(CONV_WIDTH):
                acc = acc + w_ref[k:k + 1, :] * pad_ref[pl.ds(base + k * stride, GRID_W), :]
            o_ref[0, pl.ds(dst, GRID_W), :] = (acc + bias).astype(o_ref.dtype)
            return carry

        lax.fori_loop(0, n_rows, conv, 0, unroll=8)

    @pl.when(along_rows)
    def _():
        taps(lambda r: r * CONV_PADW + (16 - CONV_HALF), 1)

    @pl.when(jnp.logical_not(along_rows))
    def _():
        taps(lambda r: pl.multiple_of(r * GRID_W, GRID_W), GRID_W)


def _axial_conv(aux_lat, conv_w, conv_b):
    bsz, t, _ = aux_lat.shape
    n_rows = t // GRID_W
    pad_rows = max(n_rows * CONV_PADW, t + 2 * CONV_HALF * GRID_W)
    w_pad = jnp.zeros((32, D_CONV), jnp.float32).at[:CONV_WIDTH].set(conv_w)
    return pl.pallas_call(
        _conv_kernel,
        grid=(bsz, D_CONV // LANES),
        in_specs=[pl.BlockSpec((1, t, LANES), lambda b, g: (b, 0, COL_U + g)),
                  pl.BlockSpec((1, t, LANES), lambda b, g: (b, 0, COL_UG + g)),
                  pl.BlockSpec((32, LANES), lambda b, g: (0, g)),
                  pl.BlockSpec((1, LANES), lambda b, g: (0, g))],
        out_specs=pl.BlockSpec((1, t, LANES), lambda b, g: (b, 0, g)),
        out_shape=jax.ShapeDtypeStruct((bsz, t, D_CONV), jnp.bfloat16),
        scratch_shapes=[pltpu.VMEM((pad_rows, LANES), jnp.float32)],
        compiler_params=pltpu.CompilerParams(dimension_semantics=("arbitrary", "arbitrary"),
                                             vmem_limit_bytes=VMEM_LIMIT),
        name="axial_conv",
    )(aux_lat, aux_lat, w_pad, conv_b)


def _out_kernel(x_ref, ba_ref, y_ref, gb_ref, gt_ref, lng_ref, lnb_ref, wa_ref, wb_ref, fg_ref, o_ref):
    y = y_ref[0].astype(jnp.float32)
    mu = jnp.mean(y, axis=-1, keepdims=True)
    yc = y - mu
    var = jnp.mean(yc * yc, axis=-1, keepdims=True)
    yn = yc * lax.rsqrt(var + EPS) * lng_ref[...] + lnb_ref[...]
    branch_b = _silu(yn) * _silu(gb_ref[0].astype(jnp.float32))
    mix = jnp.dot(ba_ref[0], wa_ref[...], preferred_element_type=jnp.float32)
    mix = mix + jnp.dot(branch_b.astype(jnp.bfloat16), wb_ref[...], preferred_element_type=jnp.float32)
    h = x_ref[0] + gt_ref[0] * mix
    o_ref[0] = h * lax.rsqrt(jnp.mean(h * h, axis=-1, keepdims=True) + EPS) * fg_ref[...]


def _output(x, branch_a, y_conv, aux_lat, gate, ln_g, ln_b, w_out_bf16, final_g):
    bsz, t, _ = x.shape
    rows = OUT_ROWS
    gb_block = COL_GB * LANES // D_CONV
    return pl.pallas_call(
        _out_kernel,
        grid=(bsz, t // rows),
        in_specs=[pl.BlockSpec((1, rows, D_MODEL), lambda b, i: (b, i, 0)),
                  pl.BlockSpec((1, rows, D_HGRN), lambda b, i: (b, i, 0)),
                  pl.BlockSpec((1, rows, D_CONV), lambda b, i: (b, i, 0)),
                  pl.BlockSpec((1, rows, D_CONV), lambda b, i: (b, i, gb_block)),
                  pl.BlockSpec((1, 1, D_MODEL), lambda b, i: (b, 0, 0)),
                  pl.BlockSpec((1, D_CONV), lambda b, i: (0, 0)),
                  pl.BlockSpec((1, D_CONV), lambda b, i: (0, 0)),
                  pl.BlockSpec((D_HGRN, D_MODEL), lambda b, i: (0, 0)),
                  pl.BlockSpec((D_CONV, D_MODEL), lambda b, i: (1, 0)),
                  pl.BlockSpec((1, D_MODEL), lambda b, i: (0, 0))],
        out_specs=pl.BlockSpec((1, rows, D_MODEL), lambda b, i: (b, i, 0)),
        out_shape=jax.ShapeDtypeStruct((bsz, t, D_MODEL), jnp.float32),
        compiler_params=pltpu.CompilerParams(dimension_semantics=("arbitrary", "arbitrary"),
                                             vmem_limit_bytes=VMEM_LIMIT),
        name="out_projection",
    )(x, branch_a, y_conv, aux_lat, gate, ln_g, ln_b, w_out_bf16, w_out_bf16, final_g)


def kernel(x, c, ctx, c_ctx, norm_g, w_mod, b_mod, w_in, lb_logits, hgrn_norm_g, conv_w, conv_b,
           conv_ln_g, conv_ln_b, w_out, final_norm_g):
    bsz, seq_len, _ = x.shape
    assert norm_g.shape[0] == 1, "single-layer block"
    assert seq_len % GRID_W == 0 and seq_len % SCAN_OUT_ROWS == 0 and seq_len % PROJ_ROWS == 0
    assert ctx.shape[1] % (2 * SCAN_CHUNK) == 0, "chunks are handled in pairs"

    pad = (-(bsz + 1)) % SUBLANES
    cc = jnp.concatenate([c, c_ctx[None, :], jnp.zeros((pad, D_MODEL), c.dtype)], axis=0)
    mod = _modulation(cc, w_mod[0], b_mod)
    shift, scale, gate = (mod[:, i * D_MODEL:(i + 1) * D_MODEL] for i in range(3))
    shift_lat, scale_lat, gate_lat = (m[:bsz, None, :] for m in (shift, scale, gate))
    shift_ctx, scale_ctx = (m[bsz][None, None, :] for m in (shift, scale))

    w_in_bf16 = w_in[0].astype(jnp.bfloat16)
    assert w_in_bf16.shape[1] == 2 * D_SCAN_IN
    p_lat, aux_lat = _projection(x, norm_g, shift_lat, scale_lat, w_in_bf16, (jnp.float32, jnp.bfloat16))
    p_ctx, = _projection(ctx.reshape(1, -1, D_MODEL), norm_g, shift_ctx, scale_ctx, w_in_bf16, (jnp.float32,))
    p_ctx = p_ctx.reshape(bsz, ctx.shape[1], D_SCAN_IN)

    branch_a = _hgrn_scan(p_lat, aux_lat, p_ctx, lb_logits, hgrn_norm_g)
    y_conv = _axial_conv(aux_lat, conv_w[0], conv_b)
    return _output(x, branch_a, y_conv, aux_lat, gate_lat, conv_ln_g, conv_ln_b,
                   w_out[0].astype(jnp.bfloat16), final_norm_g[None, :])
```

```python
import numpy as np
import jax
import jax.numpy as jnp
from jax import lax
from jax.experimental import pallas as pl
from jax.experimental.pallas import tpu as pltpu

D_MODEL = 1024
GRID_W = 64
D_HGRN = 512
HGRN_HEADS = 4
HEAD_DIM = D_HGRN // HGRN_HEADS
D_CONV = 512
CONV_WIDTH = 31
CONV_HALF = CONV_WIDTH // 2
EPS = 1e-6

LANES = 128
SUBLANES = 8
SCAN_CHUNK = 64
SCAN_LEVELS = SCAN_CHUNK.bit_length() - 1
SCAN_OUT_ROWS = 256
SCAN_STAGGER = 0
PROJ_ROWS = 512
OUT_ROWS = 1024
VMEM_LIMIT = 56 * 1024 * 1024

D_SCAN_IN = 4 * D_HGRN
COL_Q, COL_ZF, COL_ZB, COL_V = (i * HGRN_HEADS for i in range(4))
COL_GA, COL_U, COL_UG, COL_GB = (i * (D_HGRN // LANES) for i in range(4))


def _silu(x):
    return x * jax.nn.sigmoid(x)


def _mod_kernel(c_ref, w_ref, b_ref, o_ref):
    a = _silu(c_ref[...])
    o_ref[...] = jnp.dot(a, w_ref[...], preferred_element_type=jnp.float32,
                         precision=lax.Precision.HIGHEST) + b_ref[...]


def _modulation(cc, w_mod, b_mod):
    rows = cc.shape[0]
    n = w_mod.shape[1]
    return pl.pallas_call(
        _mod_kernel,
        grid=(n // D_MODEL,),
        in_specs=[pl.BlockSpec((rows, D_MODEL), lambda j: (0, 0)),
                  pl.BlockSpec((D_MODEL, D_MODEL), lambda j: (0, j)),
                  pl.BlockSpec((1, D_MODEL), lambda j: (0, j))],
        out_specs=pl.BlockSpec((rows, D_MODEL), lambda j: (0, j)),
        out_shape=jax.ShapeDtypeStruct((rows, n), jnp.float32),
        compiler_params=pltpu.CompilerParams(dimension_semantics=("arbitrary",),
                                             vmem_limit_bytes=VMEM_LIMIT),
        name="modulation",
    )(cc, w_mod, b_mod)


def _proj_kernel(x_ref, g_ref, sh_ref, sc_ref, *refs):
    x = x_ref[0]
    y = x * lax.rsqrt(jnp.mean(x * x, axis=-1, keepdims=True) + EPS) * g_ref[...]
    a = (y * (1.0 + sc_ref[0]) + sh_ref[0]).astype(jnp.bfloat16)
    n = len(refs) // 2
    for w_ref, o_ref in zip(refs[:n], refs[n:]):
        o_ref[0] = jnp.dot(a, w_ref[...], preferred_element_type=jnp.float32).astype(o_ref.dtype)


def _projection(x, norm_g, shift, scale, w_bf16, dtypes):
    bsz, t, _ = x.shape
    rows = min(PROJ_ROWS, t)
    width = D_SCAN_IN
    return pl.pallas_call(
        _proj_kernel,
        grid=(bsz, t // rows),
        in_specs=[pl.BlockSpec((1, rows, D_MODEL), lambda b, i: (b, i, 0)),
                  pl.BlockSpec((1, D_MODEL), lambda b, i: (0, 0)),
                  pl.BlockSpec((1, 1, D_MODEL), lambda b, i: (b, 0, 0)),
                  pl.BlockSpec((1, 1, D_MODEL), lambda b, i: (b, 0, 0))]
        + [pl.BlockSpec((D_MODEL, width), lambda b, i, j=j: (0, j)) for j in range(len(dtypes))],
        out_specs=[pl.BlockSpec((1, rows, width), lambda b, i: (b, i, 0)) for _ in dtypes],
        out_shape=[jax.ShapeDtypeStruct((bsz, t, width), dt) for dt in dtypes],
        compiler_params=pltpu.CompilerParams(dimension_semantics=("arbitrary", "arbitrary"),
                                             vmem_limit_bytes=VMEM_LIMIT),
        name="in_projection",
    )(x, norm_g, shift, scale, *([w_bf16] * len(dtypes)))


def _scan_constants():
    c = SCAN_CHUNK
    idx = np.arange(c)
    t, s = idx[:, None], idx[None, :]
    masks = [t == s]
    for lvl in range(SCAN_LEVELS):
        h = c >> (lvl + 1)
        masks.append(((t // (2 * h)) == (s // (2 * h))) & ((t % (2 * h)) >= h) & ((s % (2 * h)) < h))
    m_f = np.stack([m.astype(np.float32) for m in masks])
    m_b = m_f[:, ::-1, ::-1].copy()
    tri_f = (s <= t).astype(np.float32)
    tri_b = (s >= t).astype(np.float32)
    return np.concatenate([tri_f] * 2, axis=1), np.concatenate([tri_b] * 2, axis=1), m_f, m_b


def _dot_nt(a, b):
    return lax.dot_general(a, b, (((1,), (1,)), ((), ())), preferred_element_type=jnp.float32)


def _dot_tn(a, b):
    return lax.dot_general(a, b, (((0,), (0,)), ((), ())), preferred_element_type=jnp.float32)


def _interleave(*generators):
    pending = list(generators)
    while pending:
        for gen in list(pending):
            try:
                next(gen)
            except StopIteration:
                pending.remove(gen)


def _mix_rows(q, k, half, reverse):
    parts = []
    for lo in range(0, SCAN_CHUNK, 2 * half):
        first, second = (q, k) if reverse else (k, q)
        parts += [first[lo:lo + half], second[lo + half:lo + 2 * half]]
    return jnp.concatenate(parts, axis=0)


def _level_decay(g_cum, f, half, reverse):
    c = SCAN_CHUNK
    if half >= SUBLANES:
        parts = []
        for lo in range(0, c, 2 * half):
            mid = lo + half
            if reverse:
                parts += [g_cum[lo:mid] - g_cum[mid:mid + 1], g_cum[mid:mid + 1] - g_cum[mid:mid + half]]
            else:
                parts += [g_cum[mid - 1:mid] - g_cum[lo:mid], g_cum[mid:mid + half] - g_cum[mid - 1:mid]]
        return jnp.exp2(jnp.concatenate(parts, axis=0))
    if half == SUBLANES // 2:
        g3 = g_cum.reshape(c // SUBLANES, SUBLANES, LANES)
        r = half if reverse else half - 1
        later = lax.broadcasted_iota(jnp.int32, g3.shape, 1) >= half
        sign = jnp.where(later != reverse, 1.0, -1.0)
        return jnp.exp2((g3 - g3[:, r:r + 1, :]) * sign).reshape(c, LANES)
    f3 = f.reshape(c // SUBLANES, SUBLANES, LANES)
    row = lax.broadcasted_iota(jnp.int32, f3.shape, 1)
    if half == 1:
        on_query_side = (row % 2 == 0) if reverse else (row % 2 == 1)
        return jnp.where(on_query_side, f3, 1.0).reshape(c, LANES)
    prev = pltpu.roll(f3, 1, 1)
    nxt = pltpu.roll(f3, SUBLANES - 1, 1)
    m4 = row % 4
    if reverse:
        d = jnp.where(m4 == 0, f3 * nxt, jnp.where(m4 == 1, f3, jnp.where(m4 == 2, 1.0, prev)))
    else:
        d = jnp.where(m4 == 0, nxt, jnp.where(m4 == 1, 1.0, jnp.where(m4 == 2, f3, f3 * prev)))
    return d.reshape(c, LANES)


def _chunk_local(q, k, v, f, g_cum, m_ref, reverse, result):
    c = SCAN_CHUNK
    bf = jnp.bfloat16
    end_row = 0 if reverse else c - 1

    qb, kb = q.astype(bf), k.astype(bf)
    piece = 2 * SUBLANES
    q_fine = qb * _level_decay(g_cum, f, 1, reverse).astype(bf)
    p = _dot_nt(jnp.concatenate([qb, q_fine], axis=0), kb)
    a = [m_ref[0, r:r + piece] * p[r:r + piece] + m_ref[SCAN_LEVELS, r:r + piece] * p[c + r:c + r + piece]
         for r in range(0, c, piece)]
    yield
    for half in (SUBLANES, 2) + tuple(h for h in (c >> (lvl + 1) for lvl in range(SCAN_LEVELS))
                                      if h not in (SUBLANES, 2, 1)):
        d = _level_decay(g_cum, f, half, reverse).astype(bf)
        rows = list(range(0, c, piece))
        if half >= piece:
            x = _mix_rows(qb, kb, half, reverse) * d
            firsts = [lo + (0 if reverse else half) for lo in range(0, c, 2 * half)]
            rows = [r for s in firsts for r in range(s, s + half, piece)]
            p = _dot_nt(jnp.concatenate([x[s:s + half] for s in firsts], axis=0), x)
        elif half == SUBLANES:
            x = (_mix_rows(q, k, half, reverse)).astype(bf) * d
            p = _dot_nt(x, x)
        else:
            p = _dot_nt(qb * d, kb * d)
        m_level = SCAN_LEVELS - half.bit_length() + 1
        for i, r in enumerate(rows):
            a[r // piece] = a[r // piece] + m_ref[m_level, r:r + piece] * p[i * piece:(i + 1) * piece]
        yield
    d_read = jnp.exp2(g_cum)
    d_state = jnp.exp2(g_cum[end_row:end_row + 1] - g_cum)
    vb = v.astype(bf)
    o_intra = jnp.dot(jnp.concatenate(a, axis=0).astype(bf), vb, preferred_element_type=jnp.float32)
    yield
    kv = _dot_tn(vb, kb * d_state.astype(bf))
    result.extend([o_intra, kv, qb * d_read.astype(bf), d_read[end_row:end_row + 1]])
    yield


def _chunk_pair(q_r, z_r, v_r, starts, lb, w_ref, m_ref, reverse, results):
    c = SCAN_CHUNK
    bf = jnp.bfloat16
    fs, g3s = [], []
    for s in starts:
        f = lb + (1.0 - lb) * jax.nn.sigmoid(z_r[0, pl.ds(s, c), :])
        g = jnp.log2(f)
        fs.append(f)
        g_hi = g.astype(bf)
        g_lo = (g - g_hi.astype(jnp.float32)).astype(bf)
        g3s.append(jnp.concatenate([g_hi, g_lo], axis=0))
        yield
    g_cum = jnp.dot(w_ref[...], jnp.concatenate(g3s, axis=1), preferred_element_type=jnp.float32)
    yield
    chains = [_chunk_local(q_r[0, pl.ds(s, c), :], 1.0 - fs[j], v_r[0, pl.ds(s, c), :], fs[j],
                           g_cum[:, j * LANES:(j + 1) * LANES], m_ref, reverse, results[j])
              for j, s in enumerate(starts)]
    while chains:
        for chain in list(chains):
            try:
                next(chain)
            except StopIteration:
                chains.remove(chain)
        yield


def _direction_chunks(q_r, z_r, v_r, starts, lb, w_ref, m_ref, reverse):
    results = [[] for _ in starts]
    gens = [_chunk_pair(q_r, z_r, v_r, starts[j:j + 2], lb, w_ref, m_ref, reverse, results[j:j + 2])
            for j in range(0, len(starts), 2)]
    return gens, results


def _direction_states(chunks, stored):
    c = SCAN_CHUNK
    st = jnp.zeros((HEAD_DIM, HEAD_DIM), jnp.float32)
    for s, result, o_r in chunks:
        while not result:
            yield
        o_intra, kv, q_read, d_end = result
        if o_r is not None:
            o_r[pl.ds(s, c), :] = o_intra + jnp.dot(q_read, st.T.astype(jnp.bfloat16),
                                                    preferred_element_type=jnp.float32)
            stored[0] += 1
        st = st * d_end + kv
        yield


def _normalise_and_gate(of_ref, ob_ref, ga_ref, gn, o_ref, lo):
    rows = pl.ds(lo, SCAN_OUT_ROWS)
    o = of_ref[rows, :] + ob_ref[rows, :]
    o = o * lax.rsqrt(jnp.mean(o * o, axis=-1, keepdims=True) + EPS) * gn
    gate = _silu(ga_ref[0, rows, :].astype(jnp.float32))
    o_ref[0, rows, :] = (o * gate).astype(o_ref.dtype)


def _output_rows(of_ref, ob_ref, ga_ref, gn, o_ref, n_chunks, stored_f, stored_b):
    per = SCAN_OUT_ROWS // SCAN_CHUNK

    def ready_after(b):
        return per * (b + 1), n_chunks - per * b

    for b in sorted(range(n_chunks // per), key=lambda b: max(ready_after(b))):
        need_f, need_b = ready_after(b)
        while stored_f[0] < need_f or stored_b[0] < need_b:
            yield
        _normalise_and_gate(of_ref, ob_ref, ga_ref, gn, o_ref, b * SCAN_OUT_ROWS)
        yield


def _delayed(generator, rounds):
    for _ in range(rounds):
        yield
    yield from generator


def _scan_kernel(q_ref, zf_ref, zb_ref, v_ref, ga_ref, qc_ref, zfc_ref, zbc_ref, vc_ref,
                 lbl_ref, gn_ref, wf_ref, wb_ref, mf_ref, mb_ref, o_ref, of_ref, ob_ref):
    c = SCAN_CHUNK
    l0, l1 = lbl_ref[0], lbl_ref[1]
    mx = jnp.maximum(l0, l1)
    e0, e1 = jnp.exp(l0 - mx), jnp.exp(l1 - mx)
    lb = e0 / (e0 + e1)
    lb_f, lb_b = lb[0:1], lb[1:2]

    gens_f, gens_b, chunks_f, chunks_b = [], [], [], []
    for q_r, zf_r, zb_r, v_r, of_r, ob_r in ((qc_ref, zfc_ref, zbc_ref, vc_ref, None, None),
                                             (q_ref, zf_ref, zb_ref, v_ref, of_ref, ob_ref)):
        t = q_r.shape[1]
        starts_f = list(range(0, t, c))
        starts_b = starts_f[::-1]
        g_f, res_f = _direction_chunks(q_r, zf_r, v_r, starts_f, lb_f, wf_ref, mf_ref, False)
        g_b, res_b = _direction_chunks(q_r, zb_r, v_r, starts_b, lb_b, wb_ref, mb_ref, True)
        gens_f += g_f
        gens_b += g_b
        chunks_f += [(s, r, of_r) for s, r in zip(starts_f, res_f)]
        chunks_b += [(s, r, ob_r) for s, r in zip(starts_b, res_b)]
    staggered = [_delayed(g, SCAN_STAGGER * j) for j, pair in enumerate(zip(gens_f, gens_b)) for g in pair]
    stored_f, stored_b = [0], [0]
    _interleave(*staggered,
                _direction_states(chunks_f, stored_f),
                _direction_states(chunks_b, stored_b),
                _output_rows(of_ref, ob_ref, ga_ref, gn_ref[...], o_ref, q_ref.shape[1] // c, stored_f, stored_b))


def _hgrn_scan(p_lat, aux_lat, p_ctx, lb_logits, hgrn_norm_g):
    bsz, t_lat, _ = p_lat.shape
    t_ctx = p_ctx.shape[1]
    w_f3, w_b3, m_f, m_b = _scan_constants()
    wf = jnp.asarray(w_f3, jnp.bfloat16)
    wb = jnp.asarray(w_b3, jnp.bfloat16)
    mf = jnp.asarray(m_f, jnp.float32)
    mb = jnp.asarray(m_b, jnp.float32)

    def col(t, base):
        return pl.BlockSpec((1, t, LANES), lambda b, h, base=base: (b, 0, base + h))

    def whole(a):
        return pl.BlockSpec(a.shape, lambda b, h, nd=a.ndim: (0,) * nd)

    return pl.pallas_call(
        _scan_kernel,
        grid=(bsz, HGRN_HEADS),
        in_specs=[col(t_lat, COL_Q), col(t_lat, COL_ZF), col(t_lat, COL_ZB), col(t_lat, COL_V),
                  col(t_lat, COL_GA),
                  col(t_ctx, COL_Q), col(t_ctx, COL_ZF), col(t_ctx, COL_ZB), col(t_ctx, COL_V),
                  pl.BlockSpec((2, 2, LANES), lambda b, h: (0, 0, h)),
                  pl.BlockSpec((1, LANES), lambda b, h: (0, h)),
                  whole(wf), whole(wb), whole(mf), whole(mb)],
        out_specs=pl.BlockSpec((1, t_lat, LANES), lambda b, h: (b, 0, h)),
        out_shape=jax.ShapeDtypeStruct((bsz, t_lat, D_HGRN), jnp.bfloat16),
        scratch_shapes=[pltpu.VMEM((t_lat, HEAD_DIM), jnp.float32),
                        pltpu.VMEM((t_lat, HEAD_DIM), jnp.float32)],
        compiler_params=pltpu.CompilerParams(dimension_semantics=("arbitrary", "arbitrary"),
                                             vmem_limit_bytes=VMEM_LIMIT),
        name="hgrn_scan",
    )(p_lat, p_lat, p_lat, p_lat, aux_lat, p_ctx, p_ctx, p_ctx, p_ctx,
      lb_logits, hgrn_norm_g, wf, wb, mf, mb)


CONV_PADW = GRID_W + 32


def _conv_fill(u_ref, ug_ref, pad_ref, along_rows):
    t = u_ref.shape[1]
    n_rows = t // GRID_W

    @pl.when(along_rows)
    def _along_rows():
        padw = CONV_PADW

        def fill(r, carry):
            src = pl.multiple_of(r * GRID_W, GRID_W)
            dst = pl.multiple_of(r * padw, 32)
            glu = (u_ref[0, pl.ds(src, GRID_W), :].astype(jnp.float32)
                   * jax.nn.sigmoid(ug_ref[0, pl.ds(src, GRID_W), :].astype(jnp.float32)))
            pad_ref[pl.ds(dst, 16), :] = jnp.zeros((16, LANES), jnp.float32)
            pad_ref[pl.ds(dst + 16, GRID_W), :] = glu
            pad_ref[pl.ds(dst + 16 + GRID_W, 16), :] = jnp.zeros((16, LANES), jnp.float32)
            return carry

        lax.fori_loop(0, n_rows, fill, 0, unroll=4)

    @pl.when(jnp.logical_not(along_rows))
    def _along_cols():
        halo = CONV_HALF * GRID_W
        pad_ref[pl.ds(0, halo), :] = jnp.zeros((halo, LANES), jnp.float32)
        pad_ref[pl.ds(halo + t, halo), :] = jnp.zeros((halo, LANES), jnp.float32)

        def fill(r, carry):
            src = pl.multiple_of(r * GRID_W, GRID_W)
            glu = (u_ref[0, pl.ds(src, GRID_W), :].astype(jnp.float32)
                   * jax.nn.sigmoid(ug_ref[0, pl.ds(src, GRID_W), :].astype(jnp.float32)))
            pad_ref[pl.ds(halo + src, GRID_W), :] = glu
            return carry

        lax.fori_loop(0, n_rows, fill, 0, unroll=4)


def _conv_kernel(u_ref, ug_ref, w_ref, b_ref, o_ref, pad_ref):
    n_rows = u_ref.shape[1] // GRID_W
    along_rows = pl.program_id(1) < (D_CONV // 2) // LANES
    _conv_fill(u_ref, ug_ref, pad_ref, along_rows)
    bias = b_ref[...]

    def taps(first_tap, stride):
        def conv(r, carry):
            dst = pl.multiple_of(r * GRID_W, GRID_W)
            base = first_tap(r)
            acc = jnp.zeros((GRID_W, LANES), jnp.float32)
            for k in range(CONV_WIDTH):
                acc = acc + w_ref[k:k + 1, :] * pad_ref[pl.ds(base + k * stride, GRID_W), :]
            o_ref[0, pl.ds(dst, GRID_W), :] = (acc + bias).astype(o_ref.dtype)
            return carry

        lax.fori_loop(0, n_rows, conv, 0, unroll=8)

    @pl.when(along_rows)
    def _():
        taps(lambda r: r * CONV_PADW + (16 - CONV_HALF), 1)

    @pl.when(jnp.logical_not(along_rows))
    def _():
        taps(lambda r: pl.multiple_of(r * GRID_W, GRID_W), GRID_W)


def _axial_conv(aux_lat, conv_w, conv_b):
    bsz, t, _ = aux_lat.shape
    n_rows = t // GRID_W
    pad_rows = max(n_rows * CONV_PADW, t + 2 * CONV_HALF * GRID_W)
    w_pad = jnp.zeros((32, D_CONV), jnp.float32).at[:CONV_WIDTH].set(conv_w)
    return pl.pallas_call(
        _conv_kernel,
        grid=(bsz, D_CONV // LANES),
        in_specs=[pl.BlockSpec((1, t, LANES), lambda b, g: (b, 0, COL_U + g)),
                  pl.BlockSpec((1, t, LANES), lambda b, g: (b, 0, COL_UG + g)),
                  pl.BlockSpec((32, LANES), lambda b, g: (0, g)),
                  pl.BlockSpec((1, LANES), lambda b, g: (0, g))],
        out_specs=pl.BlockSpec((1, t, LANES), lambda b, g: (b, 0, g)),
        out_shape=jax.ShapeDtypeStruct((bsz, t, D_CONV), jnp.bfloat16),
        scratch_shapes=[pltpu.VMEM((pad_rows, LANES), jnp.float32)],
        compiler_params=pltpu.CompilerParams(dimension_semantics=("arbitrary", "arbitrary"),
                                             vmem_limit_bytes=VMEM_LIMIT),
        name="axial_conv",
    )(aux_lat, aux_lat, w_pad, conv_b)


def _out_kernel(x_ref, ba_ref, y_ref, gb_ref, gt_ref, lng_ref, lnb_ref, wa_ref, wb_ref, fg_ref, o_ref):
    y = y_ref[0].astype(jnp.float32)
    mu = jnp.mean(y, axis=-1, keepdims=True)
    yc = y - mu
    var = jnp.mean(yc * yc, axis=-1, keepdims=True)
    yn = yc * lax.rsqrt(var + EPS) * lng_ref[...] + lnb_ref[...]
    branch_b = _silu(yn) * _silu(gb_ref[0].astype(jnp.float32))
    mix = jnp.dot(ba_ref[0], wa_ref[...], preferred_element_type=jnp.float32)
    mix = mix + jnp.dot(branch_b.astype(jnp.bfloat16), wb_ref[...], preferred_element_type=jnp.float32)
    h = x_ref[0] + gt_ref[0] * mix
    o_ref[0] = h * lax.rsqrt(jnp.mean(h * h, axis=-1, keepdims=True) + EPS) * fg_ref[...]


def _output(x, branch_a, y_conv, aux_lat, gate, ln_g, ln_b, w_out_bf16, final_g):
    bsz, t, _ = x.shape
    rows = OUT_ROWS
    gb_block = COL_GB * LANES // D_CONV
    return pl.pallas_call(
        _out_kernel,
        grid=(bsz, t // rows),
        in_specs=[pl.BlockSpec((1, rows, D_MODEL), lambda b, i: (b, i, 0)),
                  pl.BlockSpec((1, rows, D_HGRN), lambda b, i: (b, i, 0)),
                  pl.BlockSpec((1, rows, D_CONV), lambda b, i: (b, i, 0)),
                  pl.BlockSpec((1, rows, D_CONV), lambda b, i: (b, i, gb_block)),
                  pl.BlockSpec((1, 1, D_MODEL), lambda b, i: (b, 0, 0)),
                  pl.BlockSpec((1, D_CONV), lambda b, i: (0, 0)),
                  pl.BlockSpec((1, D_CONV), lambda b, i: (0, 0)),
                  pl.BlockSpec((D_HGRN, D_MODEL), lambda b, i: (0, 0)),
                  pl.BlockSpec((D_CONV, D_MODEL), lambda b, i: (1, 0)),
                  pl.BlockSpec((1, D_MODEL), lambda b, i: (0, 0))],
        out_specs=pl.BlockSpec((1, rows, D_MODEL), lambda b, i: (b, i, 0)),
        out_shape=jax.ShapeDtypeStruct((bsz, t, D_MODEL), jnp.float32),
        compiler_params=pltpu.CompilerParams(dimension_semantics=("arbitrary", "arbitrary"),
                                             vmem_limit_bytes=VMEM_LIMIT),
        name="out_projection",
    )(x, branch_a, y_conv, aux_lat, gate, ln_g, ln_b, w_out_bf16, w_out_bf16, final_g)


def kernel(x, c, ctx, c_ctx, norm_g, w_mod, b_mod, w_in, lb_logits, hgrn_norm_g, conv_w, conv_b,
           conv_ln_g, conv_ln_b, w_out, final_norm_g):
    bsz, seq_len, _ = x.shape
    assert norm_g.shape[0] == 1, "single-layer block"
    assert seq_len % GRID_W == 0 and seq_len % SCAN_OUT_ROWS == 0 and seq_len % PROJ_ROWS == 0
    assert ctx.shape[1] % (2 * SCAN_CHUNK) == 0, "chunks are handled in pairs"

    pad = (-(bsz + 1)) % SUBLANES
    cc = jnp.concatenate([c, c_ctx[None, :], jnp.zeros((pad, D_MODEL), c.dtype)], axis=0)
    mod = _modulation(cc, w_mod[0], b_mod)
    shift, scale, gate = (mod[:, i * D_MODEL:(i + 1) * D_MODEL] for i in range(3))
    shift_lat, scale_lat, gate_lat = (m[:bsz, None, :] for m in (shift, scale, gate))
    shift_ctx, scale_ctx = (m[bsz][None, None, :] for m in (shift, scale))

    w_in_bf16 = w_in[0].astype(jnp.bfloat16)
    assert w_in_bf16.shape[1] == 2 * D_SCAN_IN
    p_lat, aux_lat = _projection(x, norm_g, shift_lat, scale_lat, w_in_bf16, (jnp.float32, jnp.bfloat16))
    p_ctx, = _projection(ctx.reshape(1, -1, D_MODEL), norm_g, shift_ctx, scale_ctx, w_in_bf16, (jnp.float32,))
    p_ctx = p_ctx.reshape(bsz, ctx.shape[1], D_SCAN_IN)

    branch_a = _hgrn_scan(p_lat, aux_lat, p_ctx, lb_logits, hgrn_norm_g)
    y_conv = _axial_conv(aux_lat, conv_w[0], conv_b)
    return _output(x, branch_a, y_conv, aux_lat, gate_lat, conv_ln_g, conv_ln_b,
                   w_out[0].astype(jnp.bfloat16), final_norm_g[None, :])
```

```python
import numpy as np
import jax
import jax.numpy as jnp
from jax import lax
from jax.experimental import pallas as pl
from jax.experimental.pallas import tpu as pltpu

D_MODEL = 1024
GRID_W = 64
D_HGRN = 512
HGRN_HEADS = 4
HEAD_DIM = D_HGRN // HGRN_HEADS
D_CONV = 512
CONV_WIDTH = 31
CONV_HALF = CONV_WIDTH // 2
EPS = 1e-6

LANES = 128
SUBLANES = 8
SCAN_CHUNK = 64
SCAN_LEVELS = SCAN_CHUNK.bit_length() - 1
SCAN_OUT_ROWS = 256
SCAN_STAGGER = 1
PROJ_ROWS = 512
OUT_ROWS = 1024
VMEM_LIMIT = 56 * 1024 * 1024

D_SCAN_IN = 4 * D_HGRN
COL_Q, COL_ZF, COL_ZB, COL_V = (i * HGRN_HEADS for i in range(4))
COL_GA, COL_U, COL_UG, COL_GB = (i * (D_HGRN // LANES) for i in range(4))


def _silu(x):
    return x * jax.nn.sigmoid(x)


def _mod_kernel(c_ref, w_ref, b_ref, o_ref):
    a = _silu(c_ref[...])
    o_ref[...] = jnp.dot(a, w_ref[...], preferred_element_type=jnp.float32,
                         precision=lax.Precision.HIGHEST) + b_ref[...]


def _modulation(cc, w_mod, b_mod):
    rows = cc.shape[0]
    n = w_mod.shape[1]
    return pl.pallas_call(
        _mod_kernel,
        grid=(n // D_MODEL,),
        in_specs=[pl.BlockSpec((rows, D_MODEL), lambda j: (0, 0)),
                  pl.BlockSpec((D_MODEL, D_MODEL), lambda j: (0, j)),
                  pl.BlockSpec((1, D_MODEL), lambda j: (0, j))],
        out_specs=pl.BlockSpec((rows, D_MODEL), lambda j: (0, j)),
        out_shape=jax.ShapeDtypeStruct((rows, n), jnp.float32),
        compiler_params=pltpu.CompilerParams(dimension_semantics=("arbitrary",),
                                             vmem_limit_bytes=VMEM_LIMIT),
        name="modulation",
    )(cc, w_mod, b_mod)


def _proj_kernel(x_ref, g_ref, sh_ref, sc_ref, *refs):
    x = x_ref[0]
    y = x * lax.rsqrt(jnp.mean(x * x, axis=-1, keepdims=True) + EPS) * g_ref[...]
    a = (y * (1.0 + sc_ref[0]) + sh_ref[0]).astype(jnp.bfloat16)
    n = len(refs) // 2
    for w_ref, o_ref in zip(refs[:n], refs[n:]):
        o_ref[0] = jnp.dot(a, w_ref[...], preferred_element_type=jnp.float32).astype(o_ref.dtype)


def _projection(x, norm_g, shift, scale, w_bf16, dtypes):
    bsz, t, _ = x.shape
    rows = min(PROJ_ROWS, t)
    width = D_SCAN_IN
    return pl.pallas_call(
        _proj_kernel,
        grid=(bsz, t // rows),
        in_specs=[pl.BlockSpec((1, rows, D_MODEL), lambda b, i: (b, i, 0)),
                  pl.BlockSpec((1, D_MODEL), lambda b, i: (0, 0)),
                  pl.BlockSpec((1, 1, D_MODEL), lambda b, i: (b, 0, 0)),
                  pl.BlockSpec((1, 1, D_MODEL), lambda b, i: (b, 0, 0))]
        + [pl.BlockSpec((D_MODEL, width), lambda b, i, j=j: (0, j)) for j in range(len(dtypes))],
        out_specs=[pl.BlockSpec((1, rows, width), lambda b, i: (b, i, 0)) for _ in dtypes],
        out_shape=[jax.ShapeDtypeStruct((bsz, t, width), dt) for dt in dtypes],
        compiler_params=pltpu.CompilerParams(dimension_semantics=("arbitrary", "arbitrary"),
                                             vmem_limit_bytes=VMEM_LIMIT),
        name="in_projection",
    )(x, norm_g, shift, scale, *([w_bf16] * len(dtypes)))


EXPONENT_LEVELS = tuple(SCAN_CHUNK >> (lvl + 1) for lvl in range(SCAN_LEVELS - 1))


def _scan_constants():
    c = SCAN_CHUNK
    idx = np.arange(c)
    t, s = idx[:, None], idx[None, :]
    masks = [t == s]
    for lvl in range(SCAN_LEVELS):
        h = c >> (lvl + 1)
        masks.append(((t // (2 * h)) == (s // (2 * h))) & ((t % (2 * h)) >= h) & ((s % (2 * h)) < h))
    m_f = np.stack([m.astype(np.float32) for m in masks])
    m_b = m_f[:, ::-1, ::-1].copy()
    u = s
    blocks = [u <= t, u > t]
    for h in EXPONENT_LEVELS:
        r = (t // (2 * h)) * (2 * h) + h - 1
        blocks.append(np.where((t % (2 * h)) >= h, (u > r) & (u <= t), (u > t) & (u <= r)))
    w_f = np.concatenate([b.astype(np.float32) for b in blocks], axis=0)
    w_b = np.concatenate([b[::-1, ::-1].astype(np.float32) for b in blocks], axis=0)
    return np.concatenate([w_f] * 2, axis=1), np.concatenate([w_b] * 2, axis=1), m_f, m_b


def _dot_nt(a, b):
    return lax.dot_general(a, b, (((1,), (1,)), ((), ())), preferred_element_type=jnp.float32)


def _dot_tn(a, b):
    return lax.dot_general(a, b, (((0,), (0,)), ((), ())), preferred_element_type=jnp.float32)


def _interleave(*generators):
    pending = list(generators)
    while pending:
        for gen in list(pending):
            try:
                next(gen)
            except StopIteration:
                pending.remove(gen)


def _mix_rows(q, k, half, reverse):
    parts = []
    for lo in range(0, SCAN_CHUNK, 2 * half):
        first, second = (q, k) if reverse else (k, q)
        parts += [first[lo:lo + half], second[lo + half:lo + 2 * half]]
    return jnp.concatenate(parts, axis=0)


def _finest_decay(f, reverse):
    c = SCAN_CHUNK
    f3 = f.reshape(c // SUBLANES, SUBLANES, LANES)
    row = lax.broadcasted_iota(jnp.int32, f3.shape, 1)
    on_query_side = (row % 2 == 0) if reverse else (row % 2 == 1)
    return jnp.where(on_query_side, f3, 1.0).reshape(c, LANES)


def _chunk_local(q, k, v, f, exponents, m_ref, reverse, result):
    c = SCAN_CHUNK
    bf = jnp.bfloat16
    end_row = 0 if reverse else c - 1

    qb, kb = q.astype(bf), k.astype(bf)
    piece = 2 * SUBLANES
    def decay(block):
        return jnp.exp2(exponents[block * c:(block + 1) * c])

    q_fine = qb * _finest_decay(f, reverse).astype(bf)
    p = _dot_nt(jnp.concatenate([qb, q_fine], axis=0), kb)
    a = [m_ref[0, r:r + piece] * p[r:r + piece] + m_ref[SCAN_LEVELS, r:r + piece] * p[c + r:c + r + piece]
         for r in range(0, c, piece)]
    yield
    for half in (SUBLANES,) + tuple(h for h in EXPONENT_LEVELS if h != SUBLANES):
        d = decay(2 + EXPONENT_LEVELS.index(half)).astype(bf)
        rows = list(range(0, c, piece))
        if half >= piece:
            x = _mix_rows(qb, kb, half, reverse) * d
            firsts = [lo + (0 if reverse else half) for lo in range(0, c, 2 * half)]
            rows = [r for s in firsts for r in range(s, s + half, piece)]
            p = _dot_nt(jnp.concatenate([x[s:s + half] for s in firsts], axis=0), x)
        elif half == SUBLANES:
            x = (_mix_rows(q, k, half, reverse)).astype(bf) * d
            p = _dot_nt(x, x)
        else:
            p = _dot_nt(qb * d, kb * d)
        m_level = SCAN_LEVELS - half.bit_length() + 1
        for i, r in enumerate(rows):
            a[r // piece] = a[r // piece] + m_ref[m_level, r:r + piece] * p[i * piece:(i + 1) * piece]
        yield
    d_read, d_state = decay(0), decay(1)
    vb = v.astype(bf)
    o_intra = jnp.dot(jnp.concatenate(a, axis=0).astype(bf), vb, preferred_element_type=jnp.float32)
    yield
    kv = _dot_tn(vb, kb * d_state.astype(bf))
    result.extend([o_intra, kv, qb * d_read.astype(bf), d_read[end_row:end_row + 1]])
    yield


def _chunk_pair(q_r, z_r, v_r, starts, lb, w_ref, m_ref, reverse, results):
    c = SCAN_CHUNK
    bf = jnp.bfloat16
    fs, g3s = [], []
    for s in starts:
        f = lb + (1.0 - lb) * jax.nn.sigmoid(z_r[0, pl.ds(s, c), :])
        g = jnp.log2(f)
        fs.append(f)
        g_hi = g.astype(bf)
        g_lo = (g - g_hi.astype(jnp.float32)).astype(bf)
        g3s.append(jnp.concatenate([g_hi, g_lo], axis=0))
        yield
    exponents = jnp.dot(w_ref[...], jnp.concatenate(g3s, axis=1), preferred_element_type=jnp.float32)
    yield
    chains = [_chunk_local(q_r[0, pl.ds(s, c), :], 1.0 - fs[j], v_r[0, pl.ds(s, c), :], fs[j],
                           exponents[:, j * LANES:(j + 1) * LANES], m_ref, reverse, results[j])
              for j, s in enumerate(starts)]
    while chains:
        for chain in list(chains):
            try:
                next(chain)
            except StopIteration:
                chains.remove(chain)
        yield


def _direction_chunks(q_r, z_r, v_r, starts, lb, w_ref, m_ref, reverse):
    results = [[] for _ in starts]
    gens = [_chunk_pair(q_r, z_r, v_r, starts[j:j + 2], lb, w_ref, m_ref, reverse, results[j:j + 2])
            for j in range(0, len(starts), 2)]
    return gens, results


def _direction_states(chunks, stored):
    c = SCAN_CHUNK
    st = jnp.zeros((HEAD_DIM, HEAD_DIM), jnp.float32)
    for s, result, o_r in chunks:
        while not result:
            yield
        o_intra, kv, q_read, d_end = result
        if o_r is not None:
            o_r[pl.ds(s, c), :] = o_intra + jnp.dot(q_read, st.T.astype(jnp.bfloat16),
                                                    preferred_element_type=jnp.float32)
            stored[0] += 1
        st = st * d_end + kv
        yield


def _normalise_and_gate(of_ref, ob_ref, ga_ref, gn, o_ref, lo):
    rows = pl.ds(lo, SCAN_OUT_ROWS)
    o = of_ref[rows, :] + ob_ref[rows, :]
    o = o * lax.rsqrt(jnp.mean(o * o, axis=-1, keepdims=True) + EPS) * gn
    gate = _silu(ga_ref[0, rows, :].astype(jnp.float32))
    o_ref[0, rows, :] = (o * gate).astype(o_ref.dtype)


def _output_rows(of_ref, ob_ref, ga_ref, gn, o_ref, n_chunks, stored_f, stored_b):
    per = SCAN_OUT_ROWS // SCAN_CHUNK

    def ready_after(b):
        return per * (b + 1), n_chunks - per * b

    for b in sorted(range(n_chunks // per), key=lambda b: max(ready_after(b))):
        need_f, need_b = ready_after(b)
        while stored_f[0] < need_f or stored_b[0] < need_b:
            yield
        _normalise_and_gate(of_ref, ob_ref, ga_ref, gn, o_ref, b * SCAN_OUT_ROWS)
        yield


def _delayed(generator, rounds):
    for _ in range(rounds):
        yield
    yield from generator


def _scan_kernel(q_ref, zf_ref, zb_ref, v_ref, ga_ref, qc_ref, zfc_ref, zbc_ref, vc_ref,
                 lbl_ref, gn_ref, wf_ref, wb_ref, mf_ref, mb_ref, o_ref, of_ref, ob_ref):
    c = SCAN_CHUNK
    l0, l1 = lbl_ref[0], lbl_ref[1]
    mx = jnp.maximum(l0, l1)
    e0, e1 = jnp.exp(l0 - mx), jnp.exp(l1 - mx)
    lb = e0 / (e0 + e1)
    lb_f, lb_b = lb[0:1], lb[1:2]

    gens_f, gens_b, chunks_f, chunks_b = [], [], [], []
    for q_r, zf_r, zb_r, v_r, of_r, ob_r in ((qc_ref, zfc_ref, zbc_ref, vc_ref, None, None),
                                             (q_ref, zf_ref, zb_ref, v_ref, of_ref, ob_ref)):
        t = q_r.shape[1]
        starts_f = list(range(0, t, c))
        starts_b = starts_f[::-1]
        g_f, res_f = _direction_chunks(q_r, zf_r, v_r, starts_f, lb_f, wf_ref, mf_ref, False)
        g_b, res_b = _direction_chunks(q_r, zb_r, v_r, starts_b, lb_b, wb_ref, mb_ref, True)
        gens_f += g_f
        gens_b += g_b
        chunks_f += [(s, r, of_r) for s, r in zip(starts_f, res_f)]
        chunks_b += [(s, r, ob_r) for s, r in zip(starts_b, res_b)]
    staggered = [_delayed(g, SCAN_STAGGER * j) for j, pair in enumerate(zip(gens_f, gens_b)) for g in pair]
    stored_f, stored_b = [0], [0]
    _interleave(*staggered,
                _direction_states(chunks_f, stored_f),
                _direction_states(chunks_b, stored_b),
                _output_rows(of_ref, ob_ref, ga_ref, gn_ref[...], o_ref, q_ref.shape[1] // c, stored_f, stored_b))


def _hgrn_scan(p_lat, aux_lat, p_ctx, lb_logits, hgrn_norm_g):
    bsz, t_lat, _ = p_lat.shape
    t_ctx = p_ctx.shape[1]
    w_f3, w_b3, m_f, m_b = _scan_constants()
    wf = jnp.asarray(w_f3, jnp.bfloat16)
    wb = jnp.asarray(w_b3, jnp.bfloat16)
    mf = jnp.asarray(m_f, jnp.float32)
    mb = jnp.asarray(m_b, jnp.float32)

    def col(t, base):
        return pl.BlockSpec((1, t, LANES), lambda b, h, base=base: (b, 0, base + h))

    def whole(a):
        return pl.BlockSpec(a.shape, lambda b, h, nd=a.ndim: (0,) * nd)

    return pl.pallas_call(
        _scan_kernel,
        grid=(bsz, HGRN_HEADS),
        in_specs=[col(t_lat, COL_Q), col(t_lat, COL_ZF), col(t_lat, COL_ZB), col(t_lat, COL_V),
                  col(t_lat, COL_GA),
                  col(t_ctx, COL_Q), col(t_ctx, COL_ZF), col(t_ctx, COL_ZB), col(t_ctx, COL_V),
                  pl.BlockSpec((2, 2, LANES), lambda b, h: (0, 0, h)),
                  pl.BlockSpec((1, LANES), lambda b, h: (0, h)),
                  whole(wf), whole(wb), whole(mf), whole(mb)],
        out_specs=pl.BlockSpec((1, t_lat, LANES), lambda b, h: (b, 0, h)),
        out_shape=jax.ShapeDtypeStruct((bsz, t_lat, D_HGRN), jnp.bfloat16),
        scratch_shapes=[pltpu.VMEM((t_lat, HEAD_DIM), jnp.float32),
                        pltpu.VMEM((t_lat, HEAD_DIM), jnp.float32)],
        compiler_params=pltpu.CompilerParams(dimension_semantics=("arbitrary", "arbitrary"),
                                             vmem_limit_bytes=VMEM_LIMIT),
        name="hgrn_scan",
    )(p_lat, p_lat, p_lat, p_lat, aux_lat, p_ctx, p_ctx, p_ctx, p_ctx,
      lb_logits, hgrn_norm_g, wf, wb, mf, mb)


CONV_PADW = GRID_W + 32


def _conv_fill(u_ref, ug_ref, pad_ref, along_rows):
    t = u_ref.shape[1]
    n_rows = t // GRID_W

    @pl.when(along_rows)
    def _along_rows():
        padw = CONV_PADW

        def fill(r, carry):
            src = pl.multiple_of(r * GRID_W, GRID_W)
            dst = pl.multiple_of(r * padw, 32)
            glu = (u_ref[0, pl.ds(src, GRID_W), :].astype(jnp.float32)
                   * jax.nn.sigmoid(ug_ref[0, pl.ds(src, GRID_W), :].astype(jnp.float32)))
            pad_ref[pl.ds(dst, 16), :] = jnp.zeros((16, LANES), jnp.float32)
            pad_ref[pl.ds(dst + 16, GRID_W), :] = glu
            pad_ref[pl.ds(dst + 16 + GRID_W, 16), :] = jnp.zeros((16, LANES), jnp.float32)
            return carry

        lax.fori_loop(0, n_rows, fill, 0, unroll=4)

    @pl.when(jnp.logical_not(along_rows))
    def _along_cols():
        halo = CONV_HALF * GRID_W
        pad_ref[pl.ds(0, halo), :] = jnp.zeros((halo, LANES), jnp.float32)
        pad_ref[pl.ds(halo + t, halo), :] = jnp.zeros((halo, LANES), jnp.float32)

        def fill(r, carry):
            src = pl.multiple_of(r * GRID_W, GRID_W)
            glu = (u_ref[0, pl.ds(src, GRID_W), :].astype(jnp.float32)
                   * jax.nn.sigmoid(ug_ref[0, pl.ds(src, GRID_W), :].astype(jnp.float32)))
            pad_ref[pl.ds(halo + src, GRID_W), :] = glu
            return carry

        lax.fori_loop(0, n_rows, fill, 0, unroll=4)


def _conv_kernel(u_ref, ug_ref, w_ref, b_ref, o_ref, pad_ref):
    n_rows = u_ref.shape[1] // GRID_W
    along_rows = pl.program_id(1) < (D_CONV // 2) // LANES
    _conv_fill(u_ref, ug_ref, pad_ref, along_rows)
    bias = b_ref[...]

    def taps(first_tap, stride):
        def conv(r, carry):
            dst = pl.multiple_of(r * GRID_W, GRID_W)
            base = first_tap(r)
            acc = jnp.zeros((GRID_W, LANES), jnp.float32)
            for k in range(CONV_WIDTH):
                acc = acc + w_ref[k:k + 1, :] * pad_ref[pl.ds(base + k * stride, GRID_W), :]
            o_ref[0, pl.ds(dst, GRID_W), :] = (acc + bias).astype(o_ref.dtype)
            return carry

        lax.fori_loop(0, n_rows, conv, 0, unroll=8)

    @pl.when(along_rows)
    def _():
        taps(lambda r: r * CONV_PADW + (16 - CONV_HALF), 1)

    @pl.when(jnp.logical_not(along_rows))
    def _():
        taps(lambda r: pl.multiple_of(r * GRID_W, GRID_W), GRID_W)


def _axial_conv(aux_lat, conv_w, conv_b):
    bsz, t, _ = aux_lat.shape
    n_rows = t // GRID_W
    pad_rows = max(n_rows * CONV_PADW, t + 2 * CONV_HALF * GRID_W)
    w_pad = jnp.zeros((32, D_CONV), jnp.float32).at[:CONV_WIDTH].set(conv_w)
    return pl.pallas_call(
        _conv_kernel,
        grid=(bsz, D_CONV // LANES),
        in_specs=[pl.BlockSpec((1, t, LANES), lambda b, g: (b, 0, COL_U + g)),
                  pl.BlockSpec((1, t, LANES), lambda b, g: (b, 0, COL_UG + g)),
                  pl.BlockSpec((32, LANES), lambda b, g: (0, g)),
                  pl.BlockSpec((1, LANES), lambda b, g: (0, g))],
        out_specs=pl.BlockSpec((1, t, LANES), lambda b, g: (b, 0, g)),
        out_shape=jax.ShapeDtypeStruct((bsz, t, D_CONV), jnp.bfloat16),
        scratch_shapes=[pltpu.VMEM((pad_rows, LANES), jnp.float32)],
        compiler_params=pltpu.CompilerParams(dimension_semantics=("arbitrary", "arbitrary"),
                                             vmem_limit_bytes=VMEM_LIMIT),
        name="axial_conv",
    )(aux_lat, aux_lat, w_pad, conv_b)


def _out_kernel(x_ref, ba_ref, y_ref, gb_ref, gt_ref, lng_ref, lnb_ref, wa_ref, wb_ref, fg_ref, o_ref):
    y = y_ref[0].astype(jnp.float32)
    mu = jnp.mean(y, axis=-1, keepdims=True)
    yc = y - mu
    var = jnp.mean(yc * yc, axis=-1, keepdims=True)
    yn = yc * lax.rsqrt(var + EPS) * lng_ref[...] + lnb_ref[...]
    branch_b = _silu(yn) * _silu(gb_ref[0].astype(jnp.float32))
    mix = jnp.dot(ba_ref[0], wa_ref[...], preferred_element_type=jnp.float32)
    mix = mix + jnp.dot(branch_b.astype(jnp.bfloat16), wb_ref[...], preferred_element_type=jnp.float32)
    h = x_ref[0] + gt_ref[0] * mix
    o_ref[0] = h * lax.rsqrt(jnp.mean(h * h, axis=-1, keepdims=True) + EPS) * fg_ref[...]


def _output(x, branch_a, y_conv, aux_lat, gate, ln_g, ln_b, w_out_bf16, final_g):
    bsz, t, _ = x.shape
    rows = OUT_ROWS
    gb_block = COL_GB * LANES // D_CONV
    return pl.pallas_call(
        _out_kernel,
        grid=(bsz, t // rows),
        in_specs=[pl.BlockSpec((1, rows, D_MODEL), lambda b, i: (b, i, 0)),
                  pl.BlockSpec((1, rows, D_HGRN), lambda b, i: (b, i, 0)),
                  pl.BlockSpec((1, rows, D_CONV), lambda b, i: (b, i, 0)),
                  pl.BlockSpec((1, rows, D_CONV), lambda b, i: (b, i, gb_block)),
                  pl.BlockSpec((1, 1, D_MODEL), lambda b, i: (b, 0, 0)),
                  pl.BlockSpec((1, D_CONV), lambda b, i: (0, 0)),
                  pl.BlockSpec((1, D_CONV), lambda b, i: (0, 0)),
                  pl.BlockSpec((D_HGRN, D_MODEL), lambda b, i: (0, 0)),
                  pl.BlockSpec((D_CONV, D_MODEL), lambda b, i: (1, 0)),
                  pl.BlockSpec((1, D_MODEL), lambda b, i: (0, 0))],
        out_specs=pl.BlockSpec((1, rows, D_MODEL), lambda b, i: (b, i, 0)),
        out_shape=jax.ShapeDtypeStruct((bsz, t, D_MODEL), jnp.float32),
        compiler_params=pltpu.CompilerParams(dimension_semantics=("arbitrary", "arbitrary"),
                                             vmem_limit_bytes=VMEM_LIMIT),
        name="out_projection",
    )(x, branch_a, y_conv, aux_lat, gate, ln_g, ln_b, w_out_bf16, w_out_bf16, final_g)


def kernel(x, c, ctx, c_ctx, norm_g, w_mod, b_mod, w_in, lb_logits, hgrn_norm_g, conv_w, conv_b,
           conv_ln_g, conv_ln_b, w_out, final_norm_g):
    bsz, seq_len, _ = x.shape
    assert norm_g.shape[0] == 1, "single-layer block"
    assert seq_len % GRID_W == 0 and seq_len % SCAN_OUT_ROWS == 0 and seq_len % PROJ_ROWS == 0
    assert ctx.shape[1] % (2 * SCAN_CHUNK) == 0, "chunks are handled in pairs"

    pad = (-(bsz + 1)) % SUBLANES
    cc = jnp.concatenate([c, c_ctx[None, :], jnp.zeros((pad, D_MODEL), c.dtype)], axis=0)
    mod = _modulation(cc, w_mod[0], b_mod)
    shift, scale, gate = (mod[:, i * D_MODEL:(i + 1) * D_MODEL] for i in range(3))
    shift_lat, scale_lat, gate_lat = (m[:bsz, None, :] for m in (shift, scale, gate))
    shift_ctx, scale_ctx = (m[bsz][None, None, :] for m in (shift, scale))

    w_in_bf16 = w_in[0].astype(jnp.bfloat16)
    assert w_in_bf16.shape[1] == 2 * D_SCAN_IN
    p_lat, aux_lat = _projection(x, norm_g, shift_lat, scale_lat, w_in_bf16, (jnp.float32, jnp.bfloat16))
    p_ctx, = _projection(ctx.reshape(1, -1, D_MODEL), norm_g, shift_ctx, scale_ctx, w_in_bf16, (jnp.float32,))
    p_ctx = p_ctx.reshape(bsz, ctx.shape[1], D_SCAN_IN)

    branch_a = _hgrn_scan(p_lat, aux_lat, p_ctx, lb_logits, hgrn_norm_g)
    y_conv = _axial_conv(aux_lat, conv_w[0], conv_b)
    return _output(x, branch_a, y_conv, aux_lat, gate_lat, conv_ln_g, conv_ln_b,
                   w_out[0].astype(jnp.bfloat16), final_norm_g[None, :])
```

```python
import numpy as np
import jax
import jax.numpy as jnp
from jax import lax
from jax.experimental import pallas as pl
from jax.experimental.pallas import tpu as pltpu

D_MODEL = 1024
GRID_W = 64
D_HGRN = 512
HGRN_HEADS = 4
HEAD_DIM = D_HGRN // HGRN_HEADS
D_CONV = 512
CONV_WIDTH = 31
CONV_HALF = CONV_WIDTH // 2
EPS = 1e-6

LANES = 128
SUBLANES = 8
SCAN_CHUNK = 64
SCAN_LEVELS = SCAN_CHUNK.bit_length() - 1
SCAN_OUT_ROWS = 256
SCAN_STAGGER = 1.5
PROJ_ROWS = 512
OUT_ROWS = 1024
VMEM_LIMIT = 56 * 1024 * 1024

D_SCAN_IN = 4 * D_HGRN
COL_Q, COL_ZF, COL_ZB, COL_V = (i * HGRN_HEADS for i in range(4))
COL_GA, COL_U, COL_UG, COL_GB = (i * (D_HGRN // LANES) for i in range(4))


def _silu(x):
    return x * jax.nn.sigmoid(x)


def _mod_kernel(c_ref, w_ref, b_ref, o_ref):
    a = _silu(c_ref[...])
    o_ref[...] = jnp.dot(a, w_ref[...], preferred_element_type=jnp.float32,
                         precision=lax.Precision.HIGHEST) + b_ref[...]


def _modulation(cc, w_mod, b_mod):
    rows = cc.shape[0]
    n = w_mod.shape[1]
    return pl.pallas_call(
        _mod_kernel,
        grid=(n // D_MODEL,),
        in_specs=[pl.BlockSpec((rows, D_MODEL), lambda j: (0, 0)),
                  pl.BlockSpec((D_MODEL, D_MODEL), lambda j: (0, j)),
                  pl.BlockSpec((1, D_MODEL), lambda j: (0, j))],
        out_specs=pl.BlockSpec((rows, D_MODEL), lambda j: (0, j)),
        out_shape=jax.ShapeDtypeStruct((rows, n), jnp.float32),
        compiler_params=pltpu.CompilerParams(dimension_semantics=("arbitrary",),
                                             vmem_limit_bytes=VMEM_LIMIT),
        name="modulation",
    )(cc, w_mod, b_mod)


def _proj_kernel(x_ref, g_ref, sh_ref, sc_ref, *refs):
    x = x_ref[0]
    y = x * lax.rsqrt(jnp.mean(x * x, axis=-1, keepdims=True) + EPS) * g_ref[...]
    a = (y * (1.0 + sc_ref[0]) + sh_ref[0]).astype(jnp.bfloat16)
    n = len(refs) // 2
    for w_ref, o_ref in zip(refs[:n], refs[n:]):
        o_ref[0] = jnp.dot(a, w_ref[...], preferred_element_type=jnp.float32).astype(o_ref.dtype)


def _projection(x, norm_g, shift, scale, w_bf16, dtypes):
    bsz, t, _ = x.shape
    rows = min(PROJ_ROWS, t)
    width = D_SCAN_IN
    return pl.pallas_call(
        _proj_kernel,
        grid=(bsz, t // rows),
        in_specs=[pl.BlockSpec((1, rows, D_MODEL), lambda b, i: (b, i, 0)),
                  pl.BlockSpec((1, D_MODEL), lambda b, i: (0, 0)),
                  pl.BlockSpec((1, 1, D_MODEL), lambda b, i: (b, 0, 0)),
                  pl.BlockSpec((1, 1, D_MODEL), lambda b, i: (b, 0, 0))]
        + [pl.BlockSpec((D_MODEL, width), lambda b, i, j=j: (0, j)) for j in range(len(dtypes))],
        out_specs=[pl.BlockSpec((1, rows, width), lambda b, i: (b, i, 0)) for _ in dtypes],
        out_shape=[jax.ShapeDtypeStruct((bsz, t, width), dt) for dt in dtypes],
        compiler_params=pltpu.CompilerParams(dimension_semantics=("arbitrary", "arbitrary"),
                                             vmem_limit_bytes=VMEM_LIMIT),
        name="in_projection",
    )(x, norm_g, shift, scale, *([w_bf16] * len(dtypes)))


def _scan_constants():
    c = SCAN_CHUNK
    idx = np.arange(c)
    t, s = idx[:, None], idx[None, :]
    masks = [t == s]
    for lvl in range(SCAN_LEVELS):
        h = c >> (lvl + 1)
        masks.append(((t // (2 * h)) == (s // (2 * h))) & ((t % (2 * h)) >= h) & ((s % (2 * h)) < h))
    m_f = np.stack([m.astype(np.float32) for m in masks])
    m_b = m_f[:, ::-1, ::-1].copy()
    tri_f = (s <= t).astype(np.float32)
    tri_b = (s >= t).astype(np.float32)
    return np.concatenate([tri_f] * 2, axis=1), np.concatenate([tri_b] * 2, axis=1), m_f, m_b


def _dot_nt(a, b):
    return lax.dot_general(a, b, (((1,), (1,)), ((), ())), preferred_element_type=jnp.float32)


def _dot_tn(a, b):
    return lax.dot_general(a, b, (((0,), (0,)), ((), ())), preferred_element_type=jnp.float32)


def _interleave(*generators):
    pending = list(generators)
    while pending:
        for gen in list(pending):
            try:
                next(gen)
            except StopIteration:
                pending.remove(gen)


def _mix_rows(q, k, half, reverse):
    parts = []
    for lo in range(0, SCAN_CHUNK, 2 * half):
        first, second = (q, k) if reverse else (k, q)
        parts += [first[lo:lo + half], second[lo + half:lo + 2 * half]]
    return jnp.concatenate(parts, axis=0)


def _level_decay(g_cum, f, half, reverse):
    c = SCAN_CHUNK
    if half >= SUBLANES:
        parts = []
        for lo in range(0, c, 2 * half):
            mid = lo + half
            if reverse:
                parts += [g_cum[lo:mid] - g_cum[mid:mid + 1], g_cum[mid:mid + 1] - g_cum[mid:mid + half]]
            else:
                parts += [g_cum[mid - 1:mid] - g_cum[lo:mid], g_cum[mid:mid + half] - g_cum[mid - 1:mid]]
        return jnp.exp2(jnp.concatenate(parts, axis=0))
    if half == SUBLANES // 2:
        g3 = g_cum.reshape(c // SUBLANES, SUBLANES, LANES)
        r = half if reverse else half - 1
        later = lax.broadcasted_iota(jnp.int32, g3.shape, 1) >= half
        sign = jnp.where(later != reverse, 1.0, -1.0)
        return jnp.exp2((g3 - g3[:, r:r + 1, :]) * sign).reshape(c, LANES)
    f3 = f.reshape(c // SUBLANES, SUBLANES, LANES)
    row = lax.broadcasted_iota(jnp.int32, f3.shape, 1)
    if half == 1:
        on_query_side = (row % 2 == 0) if reverse else (row % 2 == 1)
        return jnp.where(on_query_side, f3, 1.0).reshape(c, LANES)
    prev = pltpu.roll(f3, 1, 1)
    nxt = pltpu.roll(f3, SUBLANES - 1, 1)
    m4 = row % 4
    if reverse:
        d = jnp.where(m4 == 0, f3 * nxt, jnp.where(m4 == 1, f3, jnp.where(m4 == 2, 1.0, prev)))
    else:
        d = jnp.where(m4 == 0, nxt, jnp.where(m4 == 1, 1.0, jnp.where(m4 == 2, f3, f3 * prev)))
    return d.reshape(c, LANES)


def _chunk_local(q, k, v, f, g_cum, m_ref, reverse, result):
    c = SCAN_CHUNK
    bf = jnp.bfloat16
    end_row = 0 if reverse else c - 1

    qb, kb = q.astype(bf), k.astype(bf)
    piece = 2 * SUBLANES
    q_fine = qb * _level_decay(g_cum, f, 1, reverse).astype(bf)
    p = _dot_nt(jnp.concatenate([qb, q_fine], axis=0), kb)
    a = [m_ref[0, r:r + piece] * p[r:r + piece] + m_ref[SCAN_LEVELS, r:r + piece] * p[c + r:c + r + piece]
         for r in range(0, c, piece)]
    yield
    for half in (SUBLANES, 2) + tuple(h for h in (c >> (lvl + 1) for lvl in range(SCAN_LEVELS))
                                      if h not in (SUBLANES, 2, 1)):
        d = _level_decay(g_cum, f, half, reverse).astype(bf)
        rows = list(range(0, c, piece))
        if half >= piece:
            x = _mix_rows(qb, kb, half, reverse) * d
            firsts = [lo + (0 if reverse else half) for lo in range(0, c, 2 * half)]
            rows = [r for s in firsts for r in range(s, s + half, piece)]
            p = _dot_nt(jnp.concatenate([x[s:s + half] for s in firsts], axis=0), x)
        elif half == SUBLANES:
            x = (_mix_rows(q, k, half, reverse)).astype(bf) * d
            p = _dot_nt(x, x)
        else:
            p = _dot_nt(qb * d, kb * d)
        m_level = SCAN_LEVELS - half.bit_length() + 1
        for i, r in enumerate(rows):
            a[r // piece] = a[r // piece] + m_ref[m_level, r:r + piece] * p[i * piece:(i + 1) * piece]
        yield
    d_read = jnp.exp2(g_cum)
    d_state = jnp.exp2(g_cum[end_row:end_row + 1] - g_cum)
    vb = v.astype(bf)
    o_intra = jnp.dot(jnp.concatenate(a, axis=0).astype(bf), vb, preferred_element_type=jnp.float32)
    yield
    kv = _dot_tn(vb, kb * d_state.astype(bf))
    result.extend([o_intra, kv, qb * d_read.astype(bf), d_read[end_row:end_row + 1]])
    yield


def _chunk_pair(q_r, z_r, v_r, starts, lb, w_ref, m_ref, reverse, results):
    c = SCAN_CHUNK
    bf = jnp.bfloat16
    fs, g3s = [], []
    for s in starts:
        f = lb + (1.0 - lb) * jax.nn.sigmoid(z_r[0, pl.ds(s, c), :])
        g = jnp.log2(f)
        fs.append(f)
        g_hi = g.astype(bf)
        g_lo = (g - g_hi.astype(jnp.float32)).astype(bf)
        g3s.append(jnp.concatenate([g_hi, g_lo], axis=0))
        yield
    g_cum = jnp.dot(w_ref[...], jnp.concatenate(g3s, axis=1), preferred_element_type=jnp.float32)
    yield
    chains = [_chunk_local(q_r[0, pl.ds(s, c), :], 1.0 - fs[j], v_r[0, pl.ds(s, c), :], fs[j],
                           g_cum[:, j * LANES:(j + 1) * LANES], m_ref, reverse, results[j])
              for j, s in enumerate(starts)]
    while chains:
        for chain in list(chains):
            try:
                next(chain)
            except StopIteration:
                chains.remove(chain)
        yield


def _direction_chunks(q_r, z_r, v_r, starts, lb, w_ref, m_ref, reverse):
    results = [[] for _ in starts]
    gens = [_chunk_pair(q_r, z_r, v_r, starts[j:j + 2], lb, w_ref, m_ref, reverse, results[j:j + 2])
            for j in range(0, len(starts), 2)]
    return gens, results


def _direction_states(chunks, stored):
    c = SCAN_CHUNK
    st = jnp.zeros((HEAD_DIM, HEAD_DIM), jnp.float32)
    for s, result, o_r in chunks:
        while not result:
            yield
        o_intra, kv, q_read, d_end = result
        if o_r is not None:
            o_r[pl.ds(s, c), :] = o_intra + jnp.dot(q_read, st.T.astype(jnp.bfloat16),
                                                    preferred_element_type=jnp.float32)
            stored[0] += 1
        st = st * d_end + kv
        yield


def _normalise_and_gate(of_ref, ob_ref, ga_ref, gn, o_ref, lo):
    rows = pl.ds(lo, SCAN_OUT_ROWS)
    o = of_ref[rows, :] + ob_ref[rows, :]
    o = o * lax.rsqrt(jnp.mean(o * o, axis=-1, keepdims=True) + EPS) * gn
    gate = _silu(ga_ref[0, rows, :].astype(jnp.float32))
    o_ref[0, rows, :] = (o * gate).astype(o_ref.dtype)


def _output_rows(of_ref, ob_ref, ga_ref, gn, o_ref, n_chunks, stored_f, stored_b):
    per = SCAN_OUT_ROWS // SCAN_CHUNK

    def ready_after(b):
        return per * (b + 1), n_chunks - per * b

    for b in sorted(range(n_chunks // per), key=lambda b: max(ready_after(b))):
        need_f, need_b = ready_after(b)
        while stored_f[0] < need_f or stored_b[0] < need_b:
            yield
        _normalise_and_gate(of_ref, ob_ref, ga_ref, gn, o_ref, b * SCAN_OUT_ROWS)
        yield


def _delayed(generator, rounds):
    for _ in range(rounds):
        yield
    yield from generator


def _scan_kernel(q_ref, zf_ref, zb_ref, v_ref, ga_ref, qc_ref, zfc_ref, zbc_ref, vc_ref,
                 lbl_ref, gn_ref, wf_ref, wb_ref, mf_ref, mb_ref, o_ref, of_ref, ob_ref):
    c = SCAN_CHUNK
    l0, l1 = lbl_ref[0], lbl_ref[1]
    mx = jnp.maximum(l0, l1)
    e0, e1 = jnp.exp(l0 - mx), jnp.exp(l1 - mx)
    lb = e0 / (e0 + e1)
    lb_f, lb_b = lb[0:1], lb[1:2]

    gens_f, gens_b, chunks_f, chunks_b = [], [], [], []
    for q_r, zf_r, zb_r, v_r, of_r, ob_r in ((qc_ref, zfc_ref, zbc_ref, vc_ref, None, None),
                                             (q_ref, zf_ref, zb_ref, v_ref, of_ref, ob_ref)):
        t = q_r.shape[1]
        starts_f = list(range(0, t, c))
        starts_b = starts_f[::-1]
        g_f, res_f = _direction_chunks(q_r, zf_r, v_r, starts_f, lb_f, wf_ref, mf_ref, False)
        g_b, res_b = _direction_chunks(q_r, zb_r, v_r, starts_b, lb_b, wb_ref, mb_ref, True)
        gens_f += g_f
        gens_b += g_b
        chunks_f += [(s, r, of_r) for s, r in zip(starts_f, res_f)]
        chunks_b += [(s, r, ob_r) for s, r in zip(starts_b, res_b)]
    staggered = [_delayed(g, int(SCAN_STAGGER * j)) for j, pair in enumerate(zip(gens_f, gens_b)) for g in pair]
    stored_f, stored_b = [0], [0]
    _interleave(*staggered,
                _direction_states(chunks_f, stored_f),
                _direction_states(chunks_b, stored_b),
                _output_rows(of_ref, ob_ref, ga_ref, gn_ref[...], o_ref, q_ref.shape[1] // c, stored_f, stored_b))


def _hgrn_scan(p_lat, aux_lat, p_ctx, lb_logits, hgrn_norm_g):
    bsz, t_lat, _ = p_lat.shape
    t_ctx = p_ctx.shape[1]
    w_f3, w_b3, m_f, m_b = _scan_constants()
    wf = jnp.asarray(w_f3, jnp.bfloat16)
    wb = jnp.asarray(w_b3, jnp.bfloat16)
    mf = jnp.asarray(m_f, jnp.float32)
    mb = jnp.asarray(m_b, jnp.float32)

    def col(t, base):
        return pl.BlockSpec((1, t, LANES), lambda b, h, base=base: (b, 0, base + h))

    def whole(a):
        return pl.BlockSpec(a.shape, lambda b, h, nd=a.ndim: (0,) * nd)

    return pl.pallas_call(
        _scan_kernel,
        grid=(bsz, HGRN_HEADS),
        in_specs=[col(t_lat, COL_Q), col(t_lat, COL_ZF), col(t_lat, COL_ZB), col(t_lat, COL_V),
                  col(t_lat, COL_GA),
                  col(t_ctx, COL_Q), col(t_ctx, COL_ZF), col(t_ctx, COL_ZB), col(t_ctx, COL_V),
                  pl.BlockSpec((2, 2, LANES), lambda b, h: (0, 0, h)),
                  pl.BlockSpec((1, LANES), lambda b, h: (0, h)),
                  whole(wf), whole(wb), whole(mf), whole(mb)],
        out_specs=pl.BlockSpec((1, t_lat, LANES), lambda b, h: (b, 0, h)),
        out_shape=jax.ShapeDtypeStruct((bsz, t_lat, D_HGRN), jnp.bfloat16),
        scratch_shapes=[pltpu.VMEM((t_lat, HEAD_DIM), jnp.float32),
                        pltpu.VMEM((t_lat, HEAD_DIM), jnp.float32)],
        compiler_params=pltpu.CompilerParams(dimension_semantics=("arbitrary", "arbitrary"),
                                             vmem_limit_bytes=VMEM_LIMIT),
        name="hgrn_scan",
    )(p_lat, p_lat, p_lat, p_lat, aux_lat, p_ctx, p_ctx, p_ctx, p_ctx,
      lb_logits, hgrn_norm_g, wf, wb, mf, mb)


CONV_PADW = GRID_W + 32


def _conv_fill(u_ref, ug_ref, pad_ref, along_rows):
    t = u_ref.shape[1]
    n_rows = t // GRID_W

    @pl.when(along_rows)
    def _along_rows():
        padw = CONV_PADW

        def fill(r, carry):
            src = pl.multiple_of(r * GRID_W, GRID_W)
            dst = pl.multiple_of(r * padw, 32)
            glu = (u_ref[0, pl.ds(src, GRID_W), :].astype(jnp.float32)
                   * jax.nn.sigmoid(ug_ref[0, pl.ds(src, GRID_W), :].astype(jnp.float32)))
            pad_ref[pl.ds(dst, 16), :] = jnp.zeros((16, LANES), jnp.float32)
            pad_ref[pl.ds(dst + 16, GRID_W), :] = glu
            pad_ref[pl.ds(dst + 16 + GRID_W, 16), :] = jnp.zeros((16, LANES), jnp.float32)
            return carry

        lax.fori_loop(0, n_rows, fill, 0, unroll=4)

    @pl.when(jnp.logical_not(along_rows))
    def _along_cols():
        halo = CONV_HALF * GRID_W
        pad_ref[pl.ds(0, halo), :] = jnp.zeros((halo, LANES), jnp.float32)
        pad_ref[pl.ds(halo + t, halo), :] = jnp.zeros((halo, LANES), jnp.float32)

        def fill(r, carry):
            src = pl.multiple_of(r * GRID_W, GRID_W)
            glu = (u_ref[0, pl.ds(src, GRID_W), :].astype(jnp.float32)
                   * jax.nn.sigmoid(ug_ref[0, pl.ds(src, GRID_W), :].astype(jnp.float32)))
            pad_ref[pl.ds(halo + src, GRID_W), :] = glu
            return carry

        lax.fori_loop(0, n_rows, fill, 0, unroll=4)


def _conv_kernel(u_ref, ug_ref, w_ref, b_ref, o_ref, pad_ref):
    n_rows = u_ref.shape[1] // GRID_W
    along_rows = pl.program_id(1) < (D_CONV // 2) // LANES
    _conv_fill(u_ref, ug_ref, pad_ref, along_rows)
    bias = b_ref[...]

    def taps(first_tap, stride):
        def conv(r, carry):
            dst = pl.multiple_of(r * GRID_W, GRID_W)
            base = first_tap(r)
            acc = jnp.zeros((GRID_W, LANES), jnp.float32)
            for k in range(CONV_WIDTH):
                acc = acc + w_ref[k:k + 1, :] * pad_ref[pl.ds(base + k * stride, GRID_W), :]
            o_ref[0, pl.ds(dst, GRID_W), :] = (acc + bias).astype(o_ref.dtype)
            return carry

        lax.fori_loop(0, n_rows, conv, 0, unroll=8)

    @pl.when(along_rows)
    def _():
        taps(lambda r: r * CONV_PADW + (16 - CONV_HALF), 1)

    @pl.when(jnp.logical_not(along_rows))
    def _():
        taps(lambda r: pl.multiple_of(r * GRID_W, GRID_W), GRID_W)


def _axial_conv(aux_lat, conv_w, conv_b):
    bsz, t, _ = aux_lat.shape
    n_rows = t // GRID_W
    pad_rows = max(n_rows * CONV_PADW, t + 2 * CONV_HALF * GRID_W)
    w_pad = jnp.zeros((32, D_CONV), jnp.float32).at[:CONV_WIDTH].set(conv_w)
    return pl.pallas_call(
        _conv_kernel,
        grid=(bsz, D_CONV // LANES),
        in_specs=[pl.BlockSpec((1, t, LANES), lambda b, g: (b, 0, COL_U + g)),
                  pl.BlockSpec((1, t, LANES), lambda b, g: (b, 0, COL_UG + g)),
                  pl.BlockSpec((32, LANES), lambda b, g: (0, g)),
                  pl.BlockSpec((1, LANES), lambda b, g: (0, g))],
        out_specs=pl.BlockSpec((1, t, LANES), lambda b, g: (b, 0, g)),
        out_shape=jax.ShapeDtypeStruct((bsz, t, D_CONV), jnp.bfloat16),
        scratch_shapes=[pltpu.VMEM((pad_rows, LANES), jnp.float32)],
        compiler_params=pltpu.CompilerParams(dimension_semantics=("arbitrary", "arbitrary"),
                                             vmem_limit_bytes=VMEM_LIMIT),
        name="axial_conv",
    )(aux_lat, aux_lat, w_pad, conv_b)


def _out_kernel(x_ref, ba_ref, y_ref, gb_ref, gt_ref, lng_ref, lnb_ref, wa_ref, wb_ref, fg_ref, o_ref):
    y = y_ref[0].astype(jnp.float32)
    mu = jnp.mean(y, axis=-1, keepdims=True)
    yc = y - mu
    var = jnp.mean(yc * yc, axis=-1, keepdims=True)
    yn = yc * lax.rsqrt(var + EPS) * lng_ref[...] + lnb_ref[...]
    branch_b = _silu(yn) * _silu(gb_ref[0].astype(jnp.float32))
    mix = jnp.dot(ba_ref[0], wa_ref[...], preferred_element_type=jnp.float32)
    mix = mix + jnp.dot(branch_b.astype(jnp.bfloat16), wb_ref[...], preferred_element_type=jnp.float32)
    h = x_ref[0] + gt_ref[0] * mix
    o_ref[0] = h * lax.rsqrt(jnp.mean(h * h, axis=-1, keepdims=True) + EPS) * fg_ref[...]


def _output(x, branch_a, y_conv, aux_lat, gate, ln_g, ln_b, w_out_bf16, final_g):
    bsz, t, _ = x.shape
    rows = OUT_ROWS
    gb_block = COL_GB * LANES // D_CONV
    return pl.pallas_call(
        _out_kernel,
        grid=(bsz, t // rows),
        in_specs=[pl.BlockSpec((1, rows, D_MODEL), lambda b, i: (b, i, 0)),
                  pl.BlockSpec((1, rows, D_HGRN), lambda b, i: (b, i, 0)),
                  pl.BlockSpec((1, rows, D_CONV), lambda b, i: (b, i, 0)),
                  pl.BlockSpec((1, rows, D_CONV), lambda b, i: (b, i, gb_block)),
                  pl.BlockSpec((1, 1, D_MODEL), lambda b, i: (b, 0, 0)),
                  pl.BlockSpec((1, D_CONV), lambda b, i: (0, 0)),
                  pl.BlockSpec((1, D_CONV), lambda b, i: (0, 0)),
                  pl.BlockSpec((D_HGRN, D_MODEL), lambda b, i: (0, 0)),
                  pl.BlockSpec((D_CONV, D_MODEL), lambda b, i: (1, 0)),
                  pl.BlockSpec((1, D_MODEL), lambda b, i: (0, 0))],
        out_specs=pl.BlockSpec((1, rows, D_MODEL), lambda b, i: (b, i, 0)),
        out_shape=jax.ShapeDtypeStruct((bsz, t, D_MODEL), jnp.float32),
        compiler_params=pltpu.CompilerParams(dimension_semantics=("arbitrary", "arbitrary"),
                                             vmem_limit_bytes=VMEM_LIMIT),
        name="out_projection",
    )(x, branch_a, y_conv, aux_lat, gate, ln_g, ln_b, w_out_bf16, w_out_bf16, final_g)


def kernel(x, c, ctx, c_ctx, norm_g, w_mod, b_mod, w_in, lb_logits, hgrn_norm_g, conv_w, conv_b,
           conv_ln_g, conv_ln_b, w_out, final_norm_g):
    bsz, seq_len, _ = x.shape
    assert norm_g.shape[0] == 1, "single-layer block"
    assert seq_len % GRID_W == 0 and seq_len % SCAN_OUT_ROWS == 0 and seq_len % PROJ_ROWS == 0
    assert ctx.shape[1] % (2 * SCAN_CHUNK) == 0, "chunks are handled in pairs"

    pad = (-(bsz + 1)) % SUBLANES
    cc = jnp.concatenate([c, c_ctx[None, :], jnp.zeros((pad, D_MODEL), c.dtype)], axis=0)
    mod = _modulation(cc, w_mod[0], b_mod)
    shift, scale, gate = (mod[:, i * D_MODEL:(i + 1) * D_MODEL] for i in range(3))
    shift_lat, scale_lat, gate_lat = (m[:bsz, None, :] for m in (shift, scale, gate))
    shift_ctx, scale_ctx = (m[bsz][None, None, :] for m in (shift, scale))

    w_in_bf16 = w_in[0].astype(jnp.bfloat16)
    assert w_in_bf16.shape[1] == 2 * D_SCAN_IN
    p_lat, aux_lat = _projection(x, norm_g, shift_lat, scale_lat, w_in_bf16, (jnp.float32, jnp.bfloat16))
    p_ctx, = _projection(ctx.reshape(1, -1, D_MODEL), norm_g, shift_ctx, scale_ctx, w_in_bf16, (jnp.float32,))
    p_ctx = p_ctx.reshape(bsz, ctx.shape[1], D_SCAN_IN)

    branch_a = _hgrn_scan(p_lat, aux_lat, p_ctx, lb_logits, hgrn_norm_g)
    y_conv = _axial_conv(aux_lat, conv_w[0], conv_b)
    return _output(x, branch_a, y_conv, aux_lat, gate_lat, conv_ln_g, conv_ln_b,
                   w_out[0].astype(jnp.bfloat16), final_norm_g[None, :])
```

```python
import numpy as np
import jax
import jax.numpy as jnp
from jax import lax
from jax.experimental import pallas as pl
from jax.experimental.pallas import tpu as pltpu

D_MODEL = 1024
GRID_W = 64
D_HGRN = 512
HGRN_HEADS = 4
HEAD_DIM = D_HGRN // HGRN_HEADS
D_CONV = 512
CONV_WIDTH = 31
CONV_HALF = CONV_WIDTH // 2
EPS = 1e-6

LANES = 128
SUBLANES = 8
SCAN_CHUNK = 64
SCAN_LEVELS = SCAN_CHUNK.bit_length() - 1
SCAN_OUT_ROWS = 256
SCAN_STAGGER = 1
PROJ_ROWS = 512
OUT_ROWS = 512
VMEM_LIMIT = 56 * 1024 * 1024

D_SCAN_IN = 4 * D_HGRN
COL_Q, COL_ZF, COL_ZB, COL_V = (i * HGRN_HEADS for i in range(4))
COL_GA, COL_U, COL_UG, COL_GB = (i * (D_HGRN // LANES) for i in range(4))


def _silu(x):
    return x * jax.nn.sigmoid(x)


def _mod_kernel(c_ref, w_ref, b_ref, o_ref):
    a = _silu(c_ref[...])
    o_ref[...] = jnp.dot(a, w_ref[...], preferred_element_type=jnp.float32,
                         precision=lax.Precision.HIGHEST) + b_ref[...]


def _modulation(cc, w_mod, b_mod):
    rows = cc.shape[0]
    n = w_mod.shape[1]
    return pl.pallas_call(
        _mod_kernel,
        grid=(n // D_MODEL,),
        in_specs=[pl.BlockSpec((rows, D_MODEL), lambda j: (0, 0)),
                  pl.BlockSpec((D_MODEL, D_MODEL), lambda j: (0, j)),
                  pl.BlockSpec((1, D_MODEL), lambda j: (0, j))],
        out_specs=pl.BlockSpec((rows, D_MODEL), lambda j: (0, j)),
        out_shape=jax.ShapeDtypeStruct((rows, n), jnp.float32),
        compiler_params=pltpu.CompilerParams(dimension_semantics=("arbitrary",),
                                             vmem_limit_bytes=VMEM_LIMIT),
        name="modulation",
    )(cc, w_mod, b_mod)


def _proj_kernel(x_ref, g_ref, sh_ref, sc_ref, *refs):
    x = x_ref[0]
    y = x * lax.rsqrt(jnp.mean(x * x, axis=-1, keepdims=True) + EPS) * g_ref[...]
    a = (y * (1.0 + sc_ref[0]) + sh_ref[0]).astype(jnp.bfloat16)
    n = len(refs) // 2
    for w_ref, o_ref in zip(refs[:n], refs[n:]):
        o_ref[0] = jnp.dot(a, w_ref[...], preferred_element_type=jnp.float32).astype(o_ref.dtype)


def _projection(x, norm_g, shift, scale, w_bf16, dtypes):
    bsz, t, _ = x.shape
    rows = min(PROJ_ROWS, t)
    width = D_SCAN_IN
    return pl.pallas_call(
        _proj_kernel,
        grid=(bsz, t // rows),
        in_specs=[pl.BlockSpec((1, rows, D_MODEL), lambda b, i: (b, i, 0)),
                  pl.BlockSpec((1, D_MODEL), lambda b, i: (0, 0)),
                  pl.BlockSpec((1, 1, D_MODEL), lambda b, i: (b, 0, 0)),
                  pl.BlockSpec((1, 1, D_MODEL), lambda b, i: (b, 0, 0))]
        + [pl.BlockSpec((D_MODEL, width), lambda b, i, j=j: (0, j)) for j in range(len(dtypes))],
        out_specs=[pl.BlockSpec((1, rows, width), lambda b, i: (b, i, 0)) for _ in dtypes],
        out_shape=[jax.ShapeDtypeStruct((bsz, t, width), dt) for dt in dtypes],
        compiler_params=pltpu.CompilerParams(dimension_semantics=("arbitrary", "arbitrary"),
                                             vmem_limit_bytes=VMEM_LIMIT),
        name="in_projection",
    )(x, norm_g, shift, scale, *([w_bf16] * len(dtypes)))


def _scan_constants():
    c = SCAN_CHUNK
    idx = np.arange(c)
    t, s = idx[:, None], idx[None, :]
    masks = [t == s]
    for lvl in range(SCAN_LEVELS):
        h = c >> (lvl + 1)
        masks.append(((t // (2 * h)) == (s // (2 * h))) & ((t % (2 * h)) >= h) & ((s % (2 * h)) < h))
    m_f = np.stack([m.astype(np.float32) for m in masks])
    m_b = m_f[:, ::-1, ::-1].copy()
    tri_f = (s <= t).astype(np.float32)
    tri_b = (s >= t).astype(np.float32)
    return np.concatenate([tri_f] * 2, axis=1), np.concatenate([tri_b] * 2, axis=1), m_f, m_b


def _dot_nt(a, b):
    return lax.dot_general(a, b, (((1,), (1,)), ((), ())), preferred_element_type=jnp.float32)


def _dot_tn(a, b):
    return lax.dot_general(a, b, (((0,), (0,)), ((), ())), preferred_element_type=jnp.float32)


def _interleave(*generators):
    pending = list(generators)
    while pending:
        for gen in list(pending):
            try:
                next(gen)
            except StopIteration:
                pending.remove(gen)


def _mix_rows(q, k, half, reverse):
    parts = []
    for lo in range(0, SCAN_CHUNK, 2 * half):
        first, second = (q, k) if reverse else (k, q)
        parts += [first[lo:lo + half], second[lo + half:lo + 2 * half]]
    return jnp.concatenate(parts, axis=0)


def _level_decay(g_cum, f, half, reverse):
    c = SCAN_CHUNK
    if half >= SUBLANES:
        parts = []
        for lo in range(0, c, 2 * half):
            mid = lo + half
            if reverse:
                parts += [g_cum[lo:mid] - g_cum[mid:mid + 1], g_cum[mid:mid + 1] - g_cum[mid:mid + half]]
            else:
                parts += [g_cum[mid - 1:mid] - g_cum[lo:mid], g_cum[mid:mid + half] - g_cum[mid - 1:mid]]
        return jnp.exp2(jnp.concatenate(parts, axis=0))
    if half == SUBLANES // 2:
        g3 = g_cum.reshape(c // SUBLANES, SUBLANES, LANES)
        r = half if reverse else half - 1
        later = lax.broadcasted_iota(jnp.int32, g3.shape, 1) >= half
        sign = jnp.where(later != reverse, 1.0, -1.0)
        return jnp.exp2((g3 - g3[:, r:r + 1, :]) * sign).reshape(c, LANES)
    f3 = f.reshape(c // SUBLANES, SUBLANES, LANES)
    row = lax.broadcasted_iota(jnp.int32, f3.shape, 1)
    if half == 1:
        on_query_side = (row % 2 == 0) if reverse else (row % 2 == 1)
        return jnp.where(on_query_side, f3, 1.0).reshape(c, LANES)
    prev = pltpu.roll(f3, 1, 1)
    nxt = pltpu.roll(f3, SUBLANES - 1, 1)
    m4 = row % 4
    if reverse:
        d = jnp.where(m4 == 0, f3 * nxt, jnp.where(m4 == 1, f3, jnp.where(m4 == 2, 1.0, prev)))
    else:
        d = jnp.where(m4 == 0, nxt, jnp.where(m4 == 1, 1.0, jnp.where(m4 == 2, f3, f3 * prev)))
    return d.reshape(c, LANES)


def _chunk_local(q, k, v, f, g_cum, m_ref, reverse, result):
    c = SCAN_CHUNK
    bf = jnp.bfloat16
    end_row = 0 if reverse else c - 1

    qb, kb = q.astype(bf), k.astype(bf)
    piece = 2 * SUBLANES
    q_fine = qb * _level_decay(g_cum, f, 1, reverse).astype(bf)
    p = _dot_nt(jnp.concatenate([qb, q_fine], axis=0), kb)
    a = [m_ref[0, r:r + piece] * p[r:r + piece] + m_ref[SCAN_LEVELS, r:r + piece] * p[c + r:c + r + piece]
         for r in range(0, c, piece)]
    yield
    for half in (SUBLANES, 2) + tuple(h for h in (c >> (lvl + 1) for lvl in range(SCAN_LEVELS))
                                      if h not in (SUBLANES, 2, 1)):
        d = _level_decay(g_cum, f, half, reverse).astype(bf)
        rows = list(range(0, c, piece))
        if half >= piece:
            x = _mix_rows(qb, kb, half, reverse) * d
            firsts = [lo + (0 if reverse else half) for lo in range(0, c, 2 * half)]
            rows = [r for s in firsts for r in range(s, s + half, piece)]
            p = _dot_nt(jnp.concatenate([x[s:s + half] for s in firsts], axis=0), x)
        elif half == SUBLANES:
            x = (_mix_rows(q, k, half, reverse)).astype(bf) * d
            p = _dot_nt(x, x)
        else:
            p = _dot_nt(qb * d, kb * d)
        m_level = SCAN_LEVELS - half.bit_length() + 1
        for i, r in enumerate(rows):
            a[r // piece] = a[r // piece] + m_ref[m_level, r:r + piece] * p[i * piece:(i + 1) * piece]
        yield
    d_read = jnp.exp2(g_cum)
    d_state = jnp.exp2(g_cum[end_row:end_row + 1] - g_cum)
    vb = v.astype(bf)
    o_intra = jnp.dot(jnp.concatenate(a, axis=0).astype(bf), vb, preferred_element_type=jnp.float32)
    yield
    kv = _dot_tn(vb, kb * d_state.astype(bf))
    result.extend([o_intra, kv, qb * d_read.astype(bf), d_read[end_row:end_row + 1]])
    yield


def _chunk_pair(q_r, z_r, v_r, starts, lb, w_ref, m_ref, reverse, results):
    c = SCAN_CHUNK
    bf = jnp.bfloat16
    fs, g3s = [], []
    for s in starts:
        f = lb + (1.0 - lb) * jax.nn.sigmoid(z_r[0, pl.ds(s, c), :])
        g = jnp.log2(f)
        fs.append(f)
        g_hi = g.astype(bf)
        g_lo = (g - g_hi.astype(jnp.float32)).astype(bf)
        g3s.append(jnp.concatenate([g_hi, g_lo], axis=0))
        yield
    g_cum = jnp.dot(w_ref[...], jnp.concatenate(g3s, axis=1), preferred_element_type=jnp.float32)
    yield
    chains = [_chunk_local(q_r[0, pl.ds(s, c), :], 1.0 - fs[j], v_r[0, pl.ds(s, c), :], fs[j],
                           g_cum[:, j * LANES:(j + 1) * LANES], m_ref, reverse, results[j])
              for j, s in enumerate(starts)]
    while chains:
        for chain in list(chains):
            try:
                next(chain)
            except StopIteration:
                chains.remove(chain)
        yield


def _direction_chunks(q_r, z_r, v_r, starts, lb, w_ref, m_ref, reverse):
    results = [[] for _ in starts]
    gens = [_chunk_pair(q_r, z_r, v_r, starts[j:j + 2], lb, w_ref, m_ref, reverse, results[j:j + 2])
            for j in range(0, len(starts), 2)]
    return gens, results


def _direction_states(chunks, stored):
    c = SCAN_CHUNK
    st = jnp.zeros((HEAD_DIM, HEAD_DIM), jnp.float32)
    for s, result, o_r in chunks:
        while not result:
            yield
        o_intra, kv, q_read, d_end = result
        if o_r is not None:
            o_r[pl.ds(s, c), :] = o_intra + jnp.dot(q_read, st.T.astype(jnp.bfloat16),
                                                    preferred_element_type=jnp.float32)
            stored[0] += 1
        st = st * d_end + kv
        yield


def _normalise_and_gate(of_ref, ob_ref, ga_ref, gn, o_ref, lo):
    rows = pl.ds(lo, SCAN_OUT_ROWS)
    o = of_ref[rows, :] + ob_ref[rows, :]
    o = o * lax.rsqrt(jnp.mean(o * o, axis=-1, keepdims=True) + EPS) * gn
    gate = _silu(ga_ref[0, rows, :].astype(jnp.float32))
    o_ref[0, rows, :] = (o * gate).astype(o_ref.dtype)


def _output_rows(of_ref, ob_ref, ga_ref, gn, o_ref, n_chunks, stored_f, stored_b):
    per = SCAN_OUT_ROWS // SCAN_CHUNK

    def ready_after(b):
        return per * (b + 1), n_chunks - per * b

    for b in sorted(range(n_chunks // per), key=lambda b: max(ready_after(b))):
        need_f, need_b = ready_after(b)
        while stored_f[0] < need_f or stored_b[0] < need_b:
            yield
        _normalise_and_gate(of_ref, ob_ref, ga_ref, gn, o_ref, b * SCAN_OUT_ROWS)
        yield


def _delayed(generator, rounds):
    for _ in range(rounds):
        yield
    yield from generator


def _scan_kernel(q_ref, zf_ref, zb_ref, v_ref, ga_ref, qc_ref, zfc_ref, zbc_ref, vc_ref,
                 lbl_ref, gn_ref, wf_ref, wb_ref, mf_ref, mb_ref, o_ref, of_ref, ob_ref):
    c = SCAN_CHUNK
    l0, l1 = lbl_ref[0], lbl_ref[1]
    mx = jnp.maximum(l0, l1)
    e0, e1 = jnp.exp(l0 - mx), jnp.exp(l1 - mx)
    lb = e0 / (e0 + e1)
    lb_f, lb_b = lb[0:1], lb[1:2]

    gens_f, gens_b, chunks_f, chunks_b = [], [], [], []
    for q_r, zf_r, zb_r, v_r, of_r, ob_r in ((qc_ref, zfc_ref, zbc_ref, vc_ref, None, None),
                                             (q_ref, zf_ref, zb_ref, v_ref, of_ref, ob_ref)):
        t = q_r.shape[1]
        starts_f = list(range(0, t, c))
        starts_b = starts_f[::-1]
        g_f, res_f = _direction_chunks(q_r, zf_r, v_r, starts_f, lb_f, wf_ref, mf_ref, False)
        g_b, res_b = _direction_chunks(q_r, zb_r, v_r, starts_b, lb_b, wb_ref, mb_ref, True)
        gens_f += g_f
        gens_b += g_b
        chunks_f += [(s, r, of_r) for s, r in zip(starts_f, res_f)]
        chunks_b += [(s, r, ob_r) for s, r in zip(starts_b, res_b)]
    staggered = [_delayed(g, SCAN_STAGGER * j) for j, pair in enumerate(zip(gens_f, gens_b)) for g in pair]
    stored_f, stored_b = [0], [0]
    _interleave(*staggered,
                _direction_states(chunks_f, stored_f),
                _direction_states(chunks_b, stored_b),
                _output_rows(of_ref, ob_ref, ga_ref, gn_ref[...], o_ref, q_ref.shape[1] // c, stored_f, stored_b))


def _hgrn_scan(p_lat, aux_lat, p_ctx, lb_logits, hgrn_norm_g):
    bsz, t_lat, _ = p_lat.shape
    t_ctx = p_ctx.shape[1]
    w_f3, w_b3, m_f, m_b = _scan_constants()
    wf = jnp.asarray(w_f3, jnp.bfloat16)
    wb = jnp.asarray(w_b3, jnp.bfloat16)
    mf = jnp.asarray(m_f, jnp.float32)
    mb = jnp.asarray(m_b, jnp.float32)

    def col(t, base):
        return pl.BlockSpec((1, t, LANES), lambda b, h, base=base: (b, 0, base + h))

    def whole(a):
        return pl.BlockSpec(a.shape, lambda b, h, nd=a.ndim: (0,) * nd)

    return pl.pallas_call(
        _scan_kernel,
        grid=(bsz, HGRN_HEADS),
        in_specs=[col(t_lat, COL_Q), col(t_lat, COL_ZF), col(t_lat, COL_ZB), col(t_lat, COL_V),
                  col(t_lat, COL_GA),
                  col(t_ctx, COL_Q), col(t_ctx, COL_ZF), col(t_ctx, COL_ZB), col(t_ctx, COL_V),
                  pl.BlockSpec((2, 2, LANES), lambda b, h: (0, 0, h)),
                  pl.BlockSpec((1, LANES), lambda b, h: (0, h)),
                  whole(wf), whole(wb), whole(mf), whole(mb)],
        out_specs=pl.BlockSpec((1, t_lat, LANES), lambda b, h: (b, 0, h)),
        out_shape=jax.ShapeDtypeStruct((bsz, t_lat, D_HGRN), jnp.bfloat16),
        scratch_shapes=[pltpu.VMEM((t_lat, HEAD_DIM), jnp.float32),
                        pltpu.VMEM((t_lat, HEAD_DIM), jnp.float32)],
        compiler_params=pltpu.CompilerParams(dimension_semantics=("arbitrary", "arbitrary"),
                                             vmem_limit_bytes=VMEM_LIMIT),
        name="hgrn_scan",
    )(p_lat, p_lat, p_lat, p_lat, aux_lat, p_ctx, p_ctx, p_ctx, p_ctx,
      lb_logits, hgrn_norm_g, wf, wb, mf, mb)


CONV_PADW = GRID_W + 32


def _conv_fill(u_ref, ug_ref, pad_ref, along_rows):
    t = u_ref.shape[1]
    n_rows = t // GRID_W

    @pl.when(along_rows)
    def _along_rows():
        padw = CONV_PADW

        def fill(r, carry):
            src = pl.multiple_of(r * GRID_W, GRID_W)
            dst = pl.multiple_of(r * padw, 32)
            glu = (u_ref[0, pl.ds(src, GRID_W), :].astype(jnp.float32)
                   * jax.nn.sigmoid(ug_ref[0, pl.ds(src, GRID_W), :].astype(jnp.float32)))
            pad_ref[pl.ds(dst, 16), :] = jnp.zeros((16, LANES), jnp.float32)
            pad_ref[pl.ds(dst + 16, GRID_W), :] = glu
            pad_ref[pl.ds(dst + 16 + GRID_W, 16), :] = jnp.zeros((16, LANES), jnp.float32)
            return carry

        lax.fori_loop(0, n_rows, fill, 0, unroll=4)

    @pl.when(jnp.logical_not(along_rows))
    def _along_cols():
        halo = CONV_HALF * GRID_W
        pad_ref[pl.ds(0, halo), :] = jnp.zeros((halo, LANES), jnp.float32)
        pad_ref[pl.ds(halo + t, halo), :] = jnp.zeros((halo, LANES), jnp.float32)

        def fill(r, carry):
            src = pl.multiple_of(r * GRID_W, GRID_W)
            glu = (u_ref[0, pl.ds(src, GRID_W), :].astype(jnp.float32)
                   * jax.nn.sigmoid(ug_ref[0, pl.ds(src, GRID_W), :].astype(jnp.float32)))
            pad_ref[pl.ds(halo + src, GRID_W), :] = glu
            return carry

        lax.fori_loop(0, n_rows, fill, 0, unroll=4)


def _conv_kernel(u_ref, ug_ref, w_ref, b_ref, o_ref, pad_ref):
    n_rows = u_ref.shape[1] // GRID_W
    along_rows = pl.program_id(1) < (D_CONV // 2) // LANES
    _conv_fill(u_ref, ug_ref, pad_ref, along_rows)
    bias = b_ref[...]

    def taps(first_tap, stride):
        def conv(r, carry):
            dst = pl.multiple_of(r * GRID_W, GRID_W)
            base = first_tap(r)
            acc = jnp.zeros((GRID_W, LANES), jnp.float32)
            for k in range(CONV_WIDTH):
                acc = acc + w_ref[k:k + 1, :] * pad_ref[pl.ds(base + k * stride, GRID_W), :]
            o_ref[0, pl.ds(dst, GRID_W), :] = (acc + bias).astype(o_ref.dtype)
            return carry

        lax.fori_loop(0, n_rows, conv, 0, unroll=8)

    @pl.when(along_rows)
    def _():
        taps(lambda r: r * CONV_PADW + (16 - CONV_HALF), 1)

    @pl.when(jnp.logical_not(along_rows))
    def _():
        taps(lambda r: pl.multiple_of(r * GRID_W, GRID_W), GRID_W)


def _axial_conv(aux_lat, conv_w, conv_b):
    bsz, t, _ = aux_lat.shape
    n_rows = t // GRID_W
    pad_rows = max(n_rows * CONV_PADW, t + 2 * CONV_HALF * GRID_W)
    w_pad = jnp.zeros((32, D_CONV), jnp.float32).at[:CONV_WIDTH].set(conv_w)
    return pl.pallas_call(
        _conv_kernel,
        grid=(bsz, D_CONV // LANES),
        in_specs=[pl.BlockSpec((1, t, LANES), lambda b, g: (b, 0, COL_U + g)),
                  pl.BlockSpec((1, t, LANES), lambda b, g: (b, 0, COL_UG + g)),
                  pl.BlockSpec((32, LANES), lambda b, g: (0, g)),
                  pl.BlockSpec((1, LANES), lambda b, g: (0, g))],
        out_specs=pl.BlockSpec((1, t, LANES), lambda b, g: (b, 0, g)),
        out_shape=jax.ShapeDtypeStruct((bsz, t, D_CONV), jnp.bfloat16),
        scratch_shapes=[pltpu.VMEM((pad_rows, LANES), jnp.float32)],
        compiler_params=pltpu.CompilerParams(dimension_semantics=("arbitrary", "arbitrary"),
                                             vmem_limit_bytes=VMEM_LIMIT),
        name="axial_conv",
    )(aux_lat, aux_lat, w_pad, conv_b)


def _out_kernel(x_ref, ba_ref, y_ref, gb_ref, gt_ref, lng_ref, lnb_ref, wa_ref, wb_ref, fg_ref, o_ref):
    y = y_ref[0].astype(jnp.float32)
    mu = jnp.mean(y, axis=-1, keepdims=True)
    yc = y - mu
    var = jnp.mean(yc * yc, axis=-1, keepdims=True)
    yn = yc * lax.rsqrt(var + EPS) * lng_ref[...] + lnb_ref[...]
    branch_b = _silu(yn) * _silu(gb_ref[0].astype(jnp.float32))
    mix = jnp.dot(ba_ref[0], wa_ref[...], preferred_element_type=jnp.float32)
    mix = mix + jnp.dot(branch_b.astype(jnp.bfloat16), wb_ref[...], preferred_element_type=jnp.float32)
    h = x_ref[0] + gt_ref[0] * mix
    o_ref[0] = h * lax.rsqrt(jnp.mean(h * h, axis=-1, keepdims=True) + EPS) * fg_ref[...]


def _output(x, branch_a, y_conv, aux_lat, gate, ln_g, ln_b, w_out_bf16, final_g):
    bsz, t, _ = x.shape
    rows = OUT_ROWS
    gb_block = COL_GB * LANES // D_CONV
    return pl.pallas_call(
        _out_kernel,
        grid=(bsz, t // rows),
        in_specs=[pl.BlockSpec((1, rows, D_MODEL), lambda b, i: (b, i, 0)),
                  pl.BlockSpec((1, rows, D_HGRN), lambda b, i: (b, i, 0)),
                  pl.BlockSpec((1, rows, D_CONV), lambda b, i: (b, i, 0)),
                  pl.BlockSpec((1, rows, D_CONV), lambda b, i: (b, i, gb_block)),
                  pl.BlockSpec((1, 1, D_MODEL), lambda b, i: (b, 0, 0)),
                  pl.BlockSpec((1, D_CONV), lambda b, i: (0, 0)),
                  pl.BlockSpec((1, D_CONV), lambda b, i: (0, 0)),
                  pl.BlockSpec((D_HGRN, D_MODEL), lambda b, i: (0, 0)),
                  pl.BlockSpec((D_CONV, D_MODEL), lambda b, i: (1, 0)),
                  pl.BlockSpec((1, D_MODEL), lambda b, i: (0, 0))],
        out_specs=pl.BlockSpec((1, rows, D_MODEL), lambda b, i: (b, i, 0)),
        out_shape=jax.ShapeDtypeStruct((bsz, t, D_MODEL), jnp.float32),
        compiler_params=pltpu.CompilerParams(dimension_semantics=("arbitrary", "arbitrary"),
                                             vmem_limit_bytes=VMEM_LIMIT),
        name="out_projection",
    )(x, branch_a, y_conv, aux_lat, gate, ln_g, ln_b, w_out_bf16, w_out_bf16, final_g)


def kernel(x, c, ctx, c_ctx, norm_g, w_mod, b_mod, w_in, lb_logits, hgrn_norm_g, conv_w, conv_b,
           conv_ln_g, conv_ln_b, w_out, final_norm_g):
    bsz, seq_len, _ = x.shape
    assert norm_g.shape[0] == 1, "single-layer block"
    assert seq_len % GRID_W == 0 and seq_len % SCAN_OUT_ROWS == 0 and seq_len % PROJ_ROWS == 0
    assert ctx.shape[1] % (2 * SCAN_CHUNK) == 0, "chunks are handled in pairs"

    pad = (-(bsz + 1)) % SUBLANES
    cc = jnp.concatenate([c, c_ctx[None, :], jnp.zeros((pad, D_MODEL), c.dtype)], axis=0)
    mod = _modulation(cc, w_mod[0], b_mod)
    shift, scale, gate = (mod[:, i * D_MODEL:(i + 1) * D_MODEL] for i in range(3))
    shift_lat, scale_lat, gate_lat = (m[:bsz, None, :] for m in (shift, scale, gate))
    shift_ctx, scale_ctx = (m[bsz][None, None, :] for m in (shift, scale))

    w_in_bf16 = w_in[0].astype(jnp.bfloat16)
    assert w_in_bf16.shape[1] == 2 * D_SCAN_IN
    p_lat, aux_lat = _projection(x, norm_g, shift_lat, scale_lat, w_in_bf16, (jnp.float32, jnp.bfloat16))
    p_ctx, = _projection(ctx.reshape(1, -1, D_MODEL), norm_g, shift_ctx, scale_ctx, w_in_bf16, (jnp.float32,))
    p_ctx = p_ctx.reshape(bsz, ctx.shape[1], D_SCAN_IN)

    branch_a = _hgrn_scan(p_lat, aux_lat, p_ctx, lb_logits, hgrn_norm_g)
    y_conv = _axial_conv(aux_lat, conv_w[0], conv_b)
    return _output(x, branch_a, y_conv, aux_lat, gate_lat, conv_ln_g, conv_ln_b,
                   w_out[0].astype(jnp.bfloat16), final_norm_g[None, :])
```

```python
import numpy as np
import jax
import jax.numpy as jnp
from jax import lax
from jax.experimental import pallas as pl
from jax.experimental.pallas import tpu as pltpu

D_MODEL = 1024
GRID_W = 64
D_HGRN = 512
HGRN_HEADS = 4
HEAD_DIM = D_HGRN // HGRN_HEADS
D_CONV = 512
CONV_WIDTH = 31
CONV_HALF = CONV_WIDTH // 2
EPS = 1e-6

LANES = 128
SUBLANES = 8
SCAN_CHUNK = 64
SCAN_LEVELS = SCAN_CHUNK.bit_length() - 1
SCAN_OUT_ROWS = 256
SCAN_STAGGER = 1
PROJ_ROWS = 1024
OUT_ROWS = 1024
VMEM_LIMIT = 56 * 1024 * 1024

D_SCAN_IN = 4 * D_HGRN
COL_Q, COL_ZF, COL_ZB, COL_V = (i * HGRN_HEADS for i in range(4))
COL_GA, COL_U, COL_UG, COL_GB = (i * (D_HGRN // LANES) for i in range(4))


def _silu(x):
    return x * jax.nn.sigmoid(x)


def _mod_kernel(c_ref, w_ref, b_ref, o_ref):
    a = _silu(c_ref[...])
    o_ref[...] = jnp.dot(a, w_ref[...], preferred_element_type=jnp.float32,
                         precision=lax.Precision.HIGHEST) + b_ref[...]


def _modulation(cc, w_mod, b_mod):
    rows = cc.shape[0]
    n = w_mod.shape[1]
    return pl.pallas_call(
        _mod_kernel,
        grid=(n // D_MODEL,),
        in_specs=[pl.BlockSpec((rows, D_MODEL), lambda j: (0, 0)),
                  pl.BlockSpec((D_MODEL, D_MODEL), lambda j: (0, j)),
                  pl.BlockSpec((1, D_MODEL), lambda j: (0, j))],
        out_specs=pl.BlockSpec((rows, D_MODEL), lambda j: (0, j)),
        out_shape=jax.ShapeDtypeStruct((rows, n), jnp.float32),
        compiler_params=pltpu.CompilerParams(dimension_semantics=("arbitrary",),
                                             vmem_limit_bytes=VMEM_LIMIT),
        name="modulation",
    )(cc, w_mod, b_mod)


def _proj_kernel(x_ref, g_ref, sh_ref, sc_ref, *refs):
    x = x_ref[0]
    y = x * lax.rsqrt(jnp.mean(x * x, axis=-1, keepdims=True) + EPS) * g_ref[...]
    a = (y * (1.0 + sc_ref[0]) + sh_ref[0]).astype(jnp.bfloat16)
    n = len(refs) // 2
    for w_ref, o_ref in zip(refs[:n], refs[n:]):
        o_ref[0] = jnp.dot(a, w_ref[...], preferred_element_type=jnp.float32).astype(o_ref.dtype)


def _projection(x, norm_g, shift, scale, w_bf16, dtypes):
    bsz, t, _ = x.shape
    rows = min(PROJ_ROWS, t)
    width = D_SCAN_IN
    return pl.pallas_call(
        _proj_kernel,
        grid=(bsz, t // rows),
        in_specs=[pl.BlockSpec((1, rows, D_MODEL), lambda b, i: (b, i, 0)),
                  pl.BlockSpec((1, D_MODEL), lambda b, i: (0, 0)),
                  pl.BlockSpec((1, 1, D_MODEL), lambda b, i: (b, 0, 0)),
                  pl.BlockSpec((1, 1, D_MODEL), lambda b, i: (b, 0, 0))]
        + [pl.BlockSpec((D_MODEL, width), lambda b, i, j=j: (0, j)) for j in range(len(dtypes))],
        out_specs=[pl.BlockSpec((1, rows, width), lambda b, i: (b, i, 0)) for _ in dtypes],
        out_shape=[jax.ShapeDtypeStruct((bsz, t, width), dt) for dt in dtypes],
        compiler_params=pltpu.CompilerParams(dimension_semantics=("arbitrary", "arbitrary"),
                                             vmem_limit_bytes=VMEM_LIMIT),
        name="in_projection",
    )(x, norm_g, shift, scale, *([w_bf16] * len(dtypes)))


def _scan_constants():
    c = SCAN_CHUNK
    idx = np.arange(c)
    t, s = idx[:, None], idx[None, :]
    masks = [t == s]
    for lvl in range(SCAN_LEVELS):
        h = c >> (lvl + 1)
        masks.append(((t // (2 * h)) == (s // (2 * h))) & ((t % (2 * h)) >= h) & ((s % (2 * h)) < h))
    m_f = np.stack([m.astype(np.float32) for m in masks])
    m_b = m_f[:, ::-1, ::-1].copy()
    tri_f = (s <= t).astype(np.float32)
    tri_b = (s >= t).astype(np.float32)
    return np.concatenate([tri_f] * 2, axis=1), np.concatenate([tri_b] * 2, axis=1), m_f, m_b


def _dot_nt(a, b):
    return lax.dot_general(a, b, (((1,), (1,)), ((), ())), preferred_element_type=jnp.float32)


def _dot_tn(a, b):
    return lax.dot_general(a, b, (((0,), (0,)), ((), ())), preferred_element_type=jnp.float32)


def _interleave(*generators):
    pending = list(generators)
    while pending:
        for gen in list(pending):
            try:
                next(gen)
            except StopIteration:
                pending.remove(gen)


def _mix_rows(q, k, half, reverse):
    parts = []
    for lo in range(0, SCAN_CHUNK, 2 * half):
        first, second = (q, k) if reverse else (k, q)
        parts += [first[lo:lo + half], second[lo + half:lo + 2 * half]]
    return jnp.concatenate(parts, axis=0)


def _level_decay(g_cum, f, half, reverse):
    c = SCAN_CHUNK
    if half >= SUBLANES:
        parts = []
        for lo in range(0, c, 2 * half):
            mid = lo + half
            if reverse:
                parts += [g_cum[lo:mid] - g_cum[mid:mid + 1], g_cum[mid:mid + 1] - g_cum[mid:mid + half]]
            else:
                parts += [g_cum[mid - 1:mid] - g_cum[lo:mid], g_cum[mid:mid + half] - g_cum[mid - 1:mid]]
        return jnp.exp2(jnp.concatenate(parts, axis=0))
    if half == SUBLANES // 2:
        g3 = g_cum.reshape(c // SUBLANES, SUBLANES, LANES)
        r = half if reverse else half - 1
        later = lax.broadcasted_iota(jnp.int32, g3.shape, 1) >= half
        sign = jnp.where(later != reverse, 1.0, -1.0)
        return jnp.exp2((g3 - g3[:, r:r + 1, :]) * sign).reshape(c, LANES)
    f3 = f.reshape(c // SUBLANES, SUBLANES, LANES)
    row = lax.broadcasted_iota(jnp.int32, f3.shape, 1)
    if half == 1:
        on_query_side = (row % 2 == 0) if reverse else (row % 2 == 1)
        return jnp.where(on_query_side, f3, 1.0).reshape(c, LANES)
    prev = pltpu.roll(f3, 1, 1)
    nxt = pltpu.roll(f3, SUBLANES - 1, 1)
    m4 = row % 4
    if reverse:
        d = jnp.where(m4 == 0, f3 * nxt, jnp.where(m4 == 1, f3, jnp.where(m4 == 2, 1.0, prev)))
    else:
        d = jnp.where(m4 == 0, nxt, jnp.where(m4 == 1, 1.0, jnp.where(m4 == 2, f3, f3 * prev)))
    return d.reshape(c, LANES)


def _chunk_local(q, k, v, f, g_cum, m_ref, reverse, result):
    c = SCAN_CHUNK
    bf = jnp.bfloat16
    end_row = 0 if reverse else c - 1

    qb, kb = q.astype(bf), k.astype(bf)
    piece = 2 * SUBLANES
    q_fine = qb * _level_decay(g_cum, f, 1, reverse).astype(bf)
    p = _dot_nt(jnp.concatenate([qb, q_fine], axis=0), kb)
    a = [m_ref[0, r:r + piece] * p[r:r + piece] + m_ref[SCAN_LEVELS, r:r + piece] * p[c + r:c + r + piece]
         for r in range(0, c, piece)]
    yield
    for half in (SUBLANES, 2) + tuple(h for h in (c >> (lvl + 1) for lvl in range(SCAN_LEVELS))
                                      if h not in (SUBLANES, 2, 1)):
        d = _level_decay(g_cum, f, half, reverse).astype(bf)
        rows = list(range(0, c, piece))
        if half >= piece:
            x = _mix_rows(qb, kb, half, reverse) * d
            firsts = [lo + (0 if reverse else half) for lo in range(0, c, 2 * half)]
            rows = [r for s in firsts for r in range(s, s + half, piece)]
            p = _dot_nt(jnp.concatenate([x[s:s + half] for s in firsts], axis=0), x)
        elif half == SUBLANES:
            x = (_mix_rows(q, k, half, reverse)).astype(bf) * d
            p = _dot_nt(x, x)
        else:
            p = _dot_nt(qb * d, kb * d)
        m_level = SCAN_LEVELS - half.bit_length() + 1
        for i, r in enumerate(rows):
            a[r // piece] = a[r // piece] + m_ref[m_level, r:r + piece] * p[i * piece:(i + 1) * piece]
        yield
    d_read = jnp.exp2(g_cum)
    d_state = jnp.exp2(g_cum[end_row:end_row + 1] - g_cum)
    vb = v.astype(bf)
    o_intra = jnp.dot(jnp.concatenate(a, axis=0).astype(bf), vb, preferred_element_type=jnp.float32)
    yield
    kv = _dot_tn(vb, kb * d_state.astype(bf))
    result.extend([o_intra, kv, qb * d_read.astype(bf), d_read[end_row:end_row + 1]])
    yield


def _chunk_pair(q_r, z_r, v_r, starts, lb, w_ref, m_ref, reverse, results):
    c = SCAN_CHUNK
    bf = jnp.bfloat16
    fs, g3s = [], []
    for s in starts:
        f = lb + (1.0 - lb) * jax.nn.sigmoid(z_r[0, pl.ds(s, c), :])
        g = jnp.log2(f)
        fs.append(f)
        g_hi = g.astype(bf)
        g_lo = (g - g_hi.astype(jnp.float32)).astype(bf)
        g3s.append(jnp.concatenate([g_hi, g_lo], axis=0))
        yield
    g_cum = jnp.dot(w_ref[...], jnp.concatenate(g3s, axis=1), preferred_element_type=jnp.float32)
    yield
    chains = [_chunk_local(q_r[0, pl.ds(s, c), :], 1.0 - fs[j], v_r[0, pl.ds(s, c), :], fs[j],
                           g_cum[:, j * LANES:(j + 1) * LANES], m_ref, reverse, results[j])
              for j, s in enumerate(starts)]
    while chains:
        for chain in list(chains):
            try:
                next(chain)
            except StopIteration:
                chains.remove(chain)
        yield


def _direction_chunks(q_r, z_r, v_r, starts, lb, w_ref, m_ref, reverse):
    results = [[] for _ in starts]
    gens = [_chunk_pair(q_r, z_r, v_r, starts[j:j + 2], lb, w_ref, m_ref, reverse, results[j:j + 2])
            for j in range(0, len(starts), 2)]
    return gens, results


def _direction_states(chunks, stored):
    c = SCAN_CHUNK
    st = jnp.zeros((HEAD_DIM, HEAD_DIM), jnp.float32)
    for s, result, o_r in chunks:
        while not result:
            yield
        o_intra, kv, q_read, d_end = result
        if o_r is not None:
            o_r[pl.ds(s, c), :] = o_intra + jnp.dot(q_read, st.T.astype(jnp.bfloat16),
                                                    preferred_element_type=jnp.float32)
            stored[0] += 1
        st = st * d_end + kv
        yield


def _normalise_and_gate(of_ref, ob_ref, ga_ref, gn, o_ref, lo):
    rows = pl.ds(lo, SCAN_OUT_ROWS)
    o = of_ref[rows, :] + ob_ref[rows, :]
    o = o * lax.rsqrt(jnp.mean(o * o, axis=-1, keepdims=True) + EPS) * gn
    gate = _silu(ga_ref[0, rows, :].astype(jnp.float32))
    o_ref[0, rows, :] = (o * gate).astype(o_ref.dtype)


def _output_rows(of_ref, ob_ref, ga_ref, gn, o_ref, n_chunks, stored_f, stored_b):
    per = SCAN_OUT_ROWS // SCAN_CHUNK

    def ready_after(b):
        return per * (b + 1), n_chunks - per * b

    for b in sorted(range(n_chunks // per), key=lambda b: max(ready_after(b))):
        need_f, need_b = ready_after(b)
        while stored_f[0] < need_f or stored_b[0] < need_b:
            yield
        _normalise_and_gate(of_ref, ob_ref, ga_ref, gn, o_ref, b * SCAN_OUT_ROWS)
        yield


def _delayed(generator, rounds):
    for _ in range(rounds):
        yield
    yield from generator


def _scan_kernel(q_ref, zf_ref, zb_ref, v_ref, ga_ref, qc_ref, zfc_ref, zbc_ref, vc_ref,
                 lbl_ref, gn_ref, wf_ref, wb_ref, mf_ref, mb_ref, o_ref, of_ref, ob_ref):
    c = SCAN_CHUNK
    l0, l1 = lbl_ref[0], lbl_ref[1]
    mx = jnp.maximum(l0, l1)
    e0, e1 = jnp.exp(l0 - mx), jnp.exp(l1 - mx)
    lb = e0 / (e0 + e1)
    lb_f, lb_b = lb[0:1], lb[1:2]

    gens_f, gens_b, chunks_f, chunks_b = [], [], [], []
    for q_r, zf_r, zb_r, v_r, of_r, ob_r in ((qc_ref, zfc_ref, zbc_ref, vc_ref, None, None),
                                             (q_ref, zf_ref, zb_ref, v_ref, of_ref, ob_ref)):
        t = q_r.shape[1]
        starts_f = list(range(0, t, c))
        starts_b = starts_f[::-1]
        g_f, res_f = _direction_chunks(q_r, zf_r, v_r, starts_f, lb_f, wf_ref, mf_ref, False)
        g_b, res_b = _direction_chunks(q_r, zb_r, v_r, starts_b, lb_b, wb_ref, mb_ref, True)
        gens_f += g_f
        gens_b += g_b
        chunks_f += [(s, r, of_r) for s, r in zip(starts_f, res_f)]
        chunks_b += [(s, r, ob_r) for s, r in zip(starts_b, res_b)]
    staggered = [_delayed(g, SCAN_STAGGER * j) for j, pair in enumerate(zip(gens_f, gens_b)) for g in pair]
    stored_f, stored_b = [0], [0]
    _interleave(*staggered,
                _direction_states(chunks_f, stored_f),
                _direction_states(chunks_b, stored_b),
                _output_rows(of_ref, ob_ref, ga_ref, gn_ref[...], o_ref, q_ref.shape[1] // c, stored_f, stored_b))


def _hgrn_scan(p_lat, aux_lat, p_ctx, lb_logits, hgrn_norm_g):
    bsz, t_lat, _ = p_lat.shape
    t_ctx = p_ctx.shape[1]
    w_f3, w_b3, m_f, m_b = _scan_constants()
    wf = jnp.asarray(w_f3, jnp.bfloat16)
    wb = jnp.asarray(w_b3, jnp.bfloat16)
    mf = jnp.asarray(m_f, jnp.float32)
    mb = jnp.asarray(m_b, jnp.float32)

    def col(t, base):
        return pl.BlockSpec((1, t, LANES), lambda b, h, base=base: (b, 0, base + h))

    def whole(a):
        return pl.BlockSpec(a.shape, lambda b, h, nd=a.ndim: (0,) * nd)

    return pl.pallas_call(
        _scan_kernel,
        grid=(bsz, HGRN_HEADS),
        in_specs=[col(t_lat, COL_Q), col(t_lat, COL_ZF), col(t_lat, COL_ZB), col(t_lat, COL_V),
                  col(t_lat, COL_GA),
                  col(t_ctx, COL_Q), col(t_ctx, COL_ZF), col(t_ctx, COL_ZB), col(t_ctx, COL_V),
                  pl.BlockSpec((2, 2, LANES), lambda b, h: (0, 0, h)),
                  pl.BlockSpec((1, LANES), lambda b, h: (0, h)),
                  whole(wf), whole(wb), whole(mf), whole(mb)],
        out_specs=pl.BlockSpec((1, t_lat, LANES), lambda b, h: (b, 0, h)),
        out_shape=jax.ShapeDtypeStruct((bsz, t_lat, D_HGRN), jnp.bfloat16),
        scratch_shapes=[pltpu.VMEM((t_lat, HEAD_DIM), jnp.float32),
                        pltpu.VMEM((t_lat, HEAD_DIM), jnp.float32)],
        compiler_params=pltpu.CompilerParams(dimension_semantics=("arbitrary", "arbitrary"),
                                             vmem_limit_bytes=VMEM_LIMIT),
        name="hgrn_scan",
    )(p_lat, p_lat, p_lat, p_lat, aux_lat, p_ctx, p_ctx, p_ctx, p_ctx,
      lb_logits, hgrn_norm_g, wf, wb, mf, mb)


CONV_PADW = GRID_W + 32


def _conv_fill(u_ref, ug_ref, pad_ref, along_rows):
    t = u_ref.shape[1]
    n_rows = t // GRID_W

    @pl.when(along_rows)
    def _along_rows():
        padw = CONV_PADW

        def fill(r, carry):
            src = pl.multiple_of(r * GRID_W, GRID_W)
            dst = pl.multiple_of(r * padw, 32)
            glu = (u_ref[0, pl.ds(src, GRID_W), :].astype(jnp.float32)
                   * jax.nn.sigmoid(ug_ref[0, pl.ds(src, GRID_W), :].astype(jnp.float32)))
            pad_ref[pl.ds(dst, 16), :] = jnp.zeros((16, LANES), jnp.float32)
            pad_ref[pl.ds(dst + 16, GRID_W), :] = glu
            pad_ref[pl.ds(dst + 16 + GRID_W, 16), :] = jnp.zeros((16, LANES), jnp.float32)
            return carry

        lax.fori_loop(0, n_rows, fill, 0, unroll=4)

    @pl.when(jnp.logical_not(along_rows))
    def _along_cols():
        halo = CONV_HALF * GRID_W
        pad_ref[pl.ds(0, halo), :] = jnp.zeros((halo, LANES), jnp.float32)
        pad_ref[pl.ds(halo + t, halo), :] = jnp.zeros((halo, LANES), jnp.float32)

        def fill(r, carry):
            src = pl.multiple_of(r * GRID_W, GRID_W)
            glu = (u_ref[0, pl.ds(src, GRID_W), :].astype(jnp.float32)
                   * jax.nn.sigmoid(ug_ref[0, pl.ds(src, GRID_W), :].astype(jnp.float32)))
            pad_ref[pl.ds(halo + src, GRID_W), :] = glu
            return carry

        lax.fori_loop(0, n_rows, fill, 0, unroll=4)


def _conv_kernel(u_ref, ug_ref, w_ref, b_ref, o_ref, pad_ref):
    n_rows = u_ref.shape[1] // GRID_W
    along_rows = pl.program_id(1) < (D_CONV // 2) // LANES
    _conv_fill(u_ref, ug_ref, pad_ref, along_rows)
    bias = b_ref[...]

    def taps(first_tap, stride):
        def conv(r, carry):
            dst = pl.multiple_of(r * GRID_W, GRID_W)
            base = first_tap(r)
            acc = jnp.zeros((GRID_W, LANES), jnp.float32)
            for k in range(CONV_WIDTH):
                acc = acc + w_ref[k:k + 1, :] * pad_ref[pl.ds(base + k * stride, GRID_W), :]
            o_ref[0, pl.ds(dst, GRID_W), :] = (acc + bias).astype(o_ref.dtype)
            return carry

        lax.fori_loop(0, n_rows, conv, 0, unroll=8)

    @pl.when(along_rows)
    def _():
        taps(lambda r: r * CONV_PADW + (16 - CONV_HALF), 1)

    @pl.when(jnp.logical_not(along_rows))
    def _():
        taps(lambda r: pl.multiple_of(r * GRID_W, GRID_W), GRID_W)


def _axial_conv(aux_lat, conv_w, conv_b):
    bsz, t, _ = aux_lat.shape
    n_rows = t // GRID_W
    pad_rows = max(n_rows * CONV_PADW, t + 2 * CONV_HALF * GRID_W)
    w_pad = jnp.zeros((32, D_CONV), jnp.float32).at[:CONV_WIDTH].set(conv_w)
    return pl.pallas_call(
        _conv_kernel,
        grid=(bsz, D_CONV // LANES),
        in_specs=[pl.BlockSpec((1, t, LANES), lambda b, g: (b, 0, COL_U + g)),
                  pl.BlockSpec((1, t, LANES), lambda b, g: (b, 0, COL_UG + g)),
                  pl.BlockSpec((32, LANES), lambda b, g: (0, g)),
                  pl.BlockSpec((1, LANES), lambda b, g: (0, g))],
        out_specs=pl.BlockSpec((1, t, LANES), lambda b, g: (b, 0, g)),
        out_shape=jax.ShapeDtypeStruct((bsz, t, D_CONV), jnp.bfloat16),
        scratch_shapes=[pltpu.VMEM((pad_rows, LANES), jnp.float32)],
        compiler_params=pltpu.CompilerParams(dimension_semantics=("arbitrary", "arbitrary"),
                                             vmem_limit_bytes=VMEM_LIMIT),
        name="axial_conv",
    )(aux_lat, aux_lat, w_pad, conv_b)


def _out_kernel(x_ref, ba_ref, y_ref, gb_ref, gt_ref, lng_ref, lnb_ref, wa_ref, wb_ref, fg_ref, o_ref):
    y = y_ref[0].astype(jnp.float32)
    mu = jnp.mean(y, axis=-1, keepdims=True)
    yc = y - mu
    var = jnp.mean(yc * yc, axis=-1, keepdims=True)
    yn = yc * lax.rsqrt(var + EPS) * lng_ref[...] + lnb_ref[...]
    branch_b = _silu(yn) * _silu(gb_ref[0].astype(jnp.float32))
    mix = jnp.dot(ba_ref[0], wa_ref[...], preferred_element_type=jnp.float32)
    mix = mix + jnp.dot(branch_b.astype(jnp.bfloat16), wb_ref[...], preferred_element_type=jnp.float32)
    h = x_ref[0] + gt_ref[0] * mix
    o_ref[0] = h * lax.rsqrt(jnp.mean(h * h, axis=-1, keepdims=True) + EPS) * fg_ref[...]


def _output(x, branch_a, y_conv, aux_lat, gate, ln_g, ln_b, w_out_bf16, final_g):
    bsz, t, _ = x.shape
    rows = OUT_ROWS
    gb_block = COL_GB * LANES // D_CONV
    return pl.pallas_call(
        _out_kernel,
        grid=(bsz, t // rows),
        in_specs=[pl.BlockSpec((1, rows, D_MODEL), lambda b, i: (b, i, 0)),
                  pl.BlockSpec((1, rows, D_HGRN), lambda b, i: (b, i, 0)),
                  pl.BlockSpec((1, rows, D_CONV), lambda b, i: (b, i, 0)),
                  pl.BlockSpec((1, rows, D_CONV), lambda b, i: (b, i, gb_block)),
                  pl.BlockSpec((1, 1, D_MODEL), lambda b, i: (b, 0, 0)),
                  pl.BlockSpec((1, D_CONV), lambda b, i: (0, 0)),
                  pl.BlockSpec((1, D_CONV), lambda b, i: (0, 0)),
                  pl.BlockSpec((D_HGRN, D_MODEL), lambda b, i: (0, 0)),
                  pl.BlockSpec((D_CONV, D_MODEL), lambda b, i: (1, 0)),
                  pl.BlockSpec((1, D_MODEL), lambda b, i: (0, 0))],
        out_specs=pl.BlockSpec((1, rows, D_MODEL), lambda b, i: (b, i, 0)),
        out_shape=jax.ShapeDtypeStruct((bsz, t, D_MODEL), jnp.float32),
        compiler_params=pltpu.CompilerParams(dimension_semantics=("arbitrary", "arbitrary"),
                                             vmem_limit_bytes=VMEM_LIMIT),
        name="out_projection",
    )(x, branch_a, y_conv, aux_lat, gate, ln_g, ln_b, w_out_bf16, w_out_bf16, final_g)


def kernel(x, c, ctx, c_ctx, norm_g, w_mod, b_mod, w_in, lb_logits, hgrn_norm_g, conv_w, conv_b,
           conv_ln_g, conv_ln_b, w_out, final_norm_g):
    bsz, seq_len, _ = x.shape
    assert norm_g.shape[0] == 1, "single-layer block"
    assert seq_len % GRID_W == 0 and seq_len % SCAN_OUT_ROWS == 0 and seq_len % PROJ_ROWS == 0
    assert ctx.shape[1] % (2 * SCAN_CHUNK) == 0, "chunks are handled in pairs"

    pad = (-(bsz + 1)) % SUBLANES
    cc = jnp.concatenate([c, c_ctx[None, :], jnp.zeros((pad, D_MODEL), c.dtype)], axis=0)
    mod = _modulation(cc, w_mod[0], b_mod)
    shift, scale, gate = (mod[:, i * D_MODEL:(i + 1) * D_MODEL] for i in range(3))
    shift_lat, scale_lat, gate_lat = (m[:bsz, None, :] for m in (shift, scale, gate))
    shift_ctx, scale_ctx = (m[bsz][None, None, :] for m in (shift, scale))

    w_in_bf16 = w_in[0].astype(jnp.bfloat16)
    assert w_in_bf16.shape[1] == 2 * D_SCAN_IN
    p_lat, aux_lat = _projection(x, norm_g, shift_lat, scale_lat, w_in_bf16, (jnp.float32, jnp.bfloat16))
    p_ctx, = _projection(ctx.reshape(1, -1, D_MODEL), norm_g, shift_ctx, scale_ctx, w_in_bf16, (jnp.float32,))
    p_ctx = p_ctx.reshape(bsz, ctx.shape[1], D_SCAN_IN)

    branch_a = _hgrn_scan(p_lat, aux_lat, p_ctx, lb_logits, hgrn_norm_g)
    y_conv = _axial_conv(aux_lat, conv_w[0], conv_b)
    return _output(x, branch_a, y_conv, aux_lat, gate_lat, conv_ln_g, conv_ln_b,
                   w_out[0].astype(jnp.bfloat16), final_norm_g[None, :])
```

```python
import numpy as np
import jax
import jax.numpy as jnp
from jax import lax
from jax.experimental import pallas as pl
from jax.experimental.pallas import tpu as pltpu

D_MODEL = 1024
GRID_W = 64
D_HGRN = 512
HGRN_HEADS = 4
HEAD_DIM = D_HGRN // HGRN_HEADS
D_CONV = 512
CONV_WIDTH = 31
CONV_HALF = CONV_WIDTH // 2
EPS = 1e-6

LANES = 128
SUBLANES = 8
SCAN_CHUNK = 64
SCAN_LEVELS = SCAN_CHUNK.bit_length() - 1
SCAN_OUT_ROWS = 64
SCAN_STAGGER = 1
PROJ_ROWS = 1024
OUT_ROWS = 1024
VMEM_LIMIT = 56 * 1024 * 1024

D_SCAN_IN = 4 * D_HGRN
COL_Q, COL_ZF, COL_ZB, COL_V = (i * HGRN_HEADS for i in range(4))
COL_GA, COL_U, COL_UG, COL_GB = (i * (D_HGRN // LANES) for i in range(4))


def _silu(x):
    return x * jax.nn.sigmoid(x)


def _mod_kernel(c_ref, w_ref, b_ref, o_ref):
    a = _silu(c_ref[...])
    o_ref[...] = jnp.dot(a, w_ref[...], preferred_element_type=jnp.float32,
                         precision=lax.Precision.HIGHEST) + b_ref[...]


def _modulation(cc, w_mod, b_mod):
    rows = cc.shape[0]
    n = w_mod.shape[1]
    return pl.pallas_call(
        _mod_kernel,
        grid=(n // D_MODEL,),
        in_specs=[pl.BlockSpec((rows, D_MODEL), lambda j: (0, 0)),
                  pl.BlockSpec((D_MODEL, D_MODEL), lambda j: (0, j)),
                  pl.BlockSpec((1, D_MODEL), lambda j: (0, j))],
        out_specs=pl.BlockSpec((rows, D_MODEL), lambda j: (0, j)),
        out_shape=jax.ShapeDtypeStruct((rows, n), jnp.float32),
        compiler_params=pltpu.CompilerParams(dimension_semantics=("arbitrary",),
                                             vmem_limit_bytes=VMEM_LIMIT),
        name="modulation",
    )(cc, w_mod, b_mod)


def _proj_kernel(x_ref, g_ref, sh_ref, sc_ref, *refs):
    x = x_ref[0]
    y = x * lax.rsqrt(jnp.mean(x * x, axis=-1, keepdims=True) + EPS) * g_ref[...]
    a = (y * (1.0 + sc_ref[0]) + sh_ref[0]).astype(jnp.bfloat16)
    n = len(refs) // 2
    for w_ref, o_ref in zip(refs[:n], refs[n:]):
        o_ref[0] = jnp.dot(a, w_ref[...], preferred_element_type=jnp.float32).astype(o_ref.dtype)


def _projection(x, norm_g, shift, scale, w_bf16, dtypes):
    bsz, t, _ = x.shape
    rows = min(PROJ_ROWS, t)
    width = D_SCAN_IN
    return pl.pallas_call(
        _proj_kernel,
        grid=(bsz, t // rows),
        in_specs=[pl.BlockSpec((1, rows, D_MODEL), lambda b, i: (b, i, 0)),
                  pl.BlockSpec((1, D_MODEL), lambda b, i: (0, 0)),
                  pl.BlockSpec((1, 1, D_MODEL), lambda b, i: (b, 0, 0)),
                  pl.BlockSpec((1, 1, D_MODEL), lambda b, i: (b, 0, 0))]
        + [pl.BlockSpec((D_MODEL, width), lambda b, i, j=j: (0, j)) for j in range(len(dtypes))],
        out_specs=[pl.BlockSpec((1, rows, width), lambda b, i: (b, i, 0)) for _ in dtypes],
        out_shape=[jax.ShapeDtypeStruct((bsz, t, width), dt) for dt in dtypes],
        compiler_params=pltpu.CompilerParams(dimension_semantics=("arbitrary", "arbitrary"),
                                             vmem_limit_bytes=VMEM_LIMIT),
        name="in_projection",
    )(x, norm_g, shift, scale, *([w_bf16] * len(dtypes)))


def _scan_constants():
    c = SCAN_CHUNK
    idx = np.arange(c)
    t, s = idx[:, None], idx[None, :]
    masks = [t == s]
    for lvl in range(SCAN_LEVELS):
        h = c >> (lvl + 1)
        masks.append(((t // (2 * h)) == (s // (2 * h))) & ((t % (2 * h)) >= h) & ((s % (2 * h)) < h))
    m_f = np.stack([m.astype(np.float32) for m in masks])
    m_b = m_f[:, ::-1, ::-1].copy()
    tri_f = (s <= t).astype(np.float32)
    tri_b = (s >= t).astype(np.float32)
    return np.concatenate([tri_f] * 2, axis=1), np.concatenate([tri_b] * 2, axis=1), m_f, m_b


def _dot_nt(a, b):
    return lax.dot_general(a, b, (((1,), (1,)), ((), ())), preferred_element_type=jnp.float32)


def _dot_tn(a, b):
    return lax.dot_general(a, b, (((0,), (0,)), ((), ())), preferred_element_type=jnp.float32)


def _interleave(*generators):
    pending = list(generators)
    while pending:
        for gen in list(pending):
            try:
                next(gen)
            except StopIteration:
                pending.remove(gen)


def _mix_rows(q, k, half, reverse):
    parts = []
    for lo in range(0, SCAN_CHUNK, 2 * half):
        first, second = (q, k) if reverse else (k, q)
        parts += [first[lo:lo + half], second[lo + half:lo + 2 * half]]
    return jnp.concatenate(parts, axis=0)


def _level_decay(g_cum, f, half, reverse):
    c = SCAN_CHUNK
    if half >= SUBLANES:
        parts = []
        for lo in range(0, c, 2 * half):
            mid = lo + half
            if reverse:
                parts += [g_cum[lo:mid] - g_cum[mid:mid + 1], g_cum[mid:mid + 1] - g_cum[mid:mid + half]]
            else:
                parts += [g_cum[mid - 1:mid] - g_cum[lo:mid], g_cum[mid:mid + half] - g_cum[mid - 1:mid]]
        return jnp.exp2(jnp.concatenate(parts, axis=0))
    if half == SUBLANES // 2:
        g3 = g_cum.reshape(c // SUBLANES, SUBLANES, LANES)
        r = half if reverse else half - 1
        later = lax.broadcasted_iota(jnp.int32, g3.shape, 1) >= half
        sign = jnp.where(later != reverse, 1.0, -1.0)
        return jnp.exp2((g3 - g3[:, r:r + 1, :]) * sign).reshape(c, LANES)
    f3 = f.reshape(c // SUBLANES, SUBLANES, LANES)
    row = lax.broadcasted_iota(jnp.int32, f3.shape, 1)
    if half == 1:
        on_query_side = (row % 2 == 0) if reverse else (row % 2 == 1)
        return jnp.where(on_query_side, f3, 1.0).reshape(c, LANES)
    prev = pltpu.roll(f3, 1, 1)
    nxt = pltpu.roll(f3, SUBLANES - 1, 1)
    m4 = row % 4
    if reverse:
        d = jnp.where(m4 == 0, f3 * nxt, jnp.where(m4 == 1, f3, jnp.where(m4 == 2, 1.0, prev)))
    else:
        d = jnp.where(m4 == 0, nxt, jnp.where(m4 == 1, 1.0, jnp.where(m4 == 2, f3, f3 * prev)))
    return d.reshape(c, LANES)


def _chunk_local(q, k, v, f, g_cum, m_ref, reverse, result):
    c = SCAN_CHUNK
    bf = jnp.bfloat16
    end_row = 0 if reverse else c - 1

    qb, kb = q.astype(bf), k.astype(bf)
    piece = 2 * SUBLANES
    q_fine = qb * _level_decay(g_cum, f, 1, reverse).astype(bf)
    p = _dot_nt(jnp.concatenate([qb, q_fine], axis=0), kb)
    a = [m_ref[0, r:r + piece] * p[r:r + piece] + m_ref[SCAN_LEVELS, r:r + piece] * p[c + r:c + r + piece]
         for r in range(0, c, piece)]
    yield
    for half in (SUBLANES, 2) + tuple(h for h in (c >> (lvl + 1) for lvl in range(SCAN_LEVELS))
                                      if h not in (SUBLANES, 2, 1)):
        d = _level_decay(g_cum, f, half, reverse).astype(bf)
        rows = list(range(0, c, piece))
        if half >= piece:
            x = _mix_rows(qb, kb, half, reverse) * d
            firsts = [lo + (0 if reverse else half) for lo in range(0, c, 2 * half)]
            rows = [r for s in firsts for r in range(s, s + half, piece)]
            p = _dot_nt(jnp.concatenate([x[s:s + half] for s in firsts], axis=0), x)
        elif half == SUBLANES:
            x = (_mix_rows(q, k, half, reverse)).astype(bf) * d
            p = _dot_nt(x, x)
        else:
            p = _dot_nt(qb * d, kb * d)
        m_level = SCAN_LEVELS - half.bit_length() + 1
        for i, r in enumerate(rows):
            a[r // piece] = a[r // piece] + m_ref[m_level, r:r + piece] * p[i * piece:(i + 1) * piece]
        yield
    d_read = jnp.exp2(g_cum)
    d_state = jnp.exp2(g_cum[end_row:end_row + 1] - g_cum)
    vb = v.astype(bf)
    o_intra = jnp.dot(jnp.concatenate(a, axis=0).astype(bf), vb, preferred_element_type=jnp.float32)
    yield
    kv = _dot_tn(vb, kb * d_state.astype(bf))
    result.extend([o_intra, kv, qb * d_read.astype(bf), d_read[end_row:end_row + 1]])
    yield


def _chunk_pair(q_r, z_r, v_r, starts, lb, w_ref, m_ref, reverse, results):
    c = SCAN_CHUNK
    bf = jnp.bfloat16
    fs, g3s = [], []
    for s in starts:
        f = lb + (1.0 - lb) * jax.nn.sigmoid(z_r[0, pl.ds(s, c), :])
        g = jnp.log2(f)
        fs.append(f)
        g_hi = g.astype(bf)
        g_lo = (g - g_hi.astype(jnp.float32)).astype(bf)
        g3s.append(jnp.concatenate([g_hi, g_lo], axis=0))
        yield
    g_cum = jnp.dot(w_ref[...], jnp.concatenate(g3s, axis=1), preferred_element_type=jnp.float32)
    yield
    chains = [_chunk_local(q_r[0, pl.ds(s, c), :], 1.0 - fs[j], v_r[0, pl.ds(s, c), :], fs[j],
                           g_cum[:, j * LANES:(j + 1) * LANES], m_ref, reverse, results[j])
              for j, s in enumerate(starts)]
    while chains:
        for chain in list(chains):
            try:
                next(chain)
            except StopIteration:
                chains.remove(chain)
        yield


def _direction_chunks(q_r, z_r, v_r, starts, lb, w_ref, m_ref, reverse):
    results = [[] for _ in starts]
    gens = [_chunk_pair(q_r, z_r, v_r, starts[j:j + 2], lb, w_ref, m_ref, reverse, results[j:j + 2])
            for j in range(0, len(starts), 2)]
    return gens, results


def _direction_states(chunks, stored):
    c = SCAN_CHUNK
    st = jnp.zeros((HEAD_DIM, HEAD_DIM), jnp.float32)
    for s, result, o_r in chunks:
        while not result:
            yield
        o_intra, kv, q_read, d_end = result
        if o_r is not None:
            o_r[pl.ds(s, c), :] = o_intra + jnp.dot(q_read, st.T.astype(jnp.bfloat16),
                                                    preferred_element_type=jnp.float32)
            stored[0] += 1
        st = st * d_end + kv
        yield


def _normalise_and_gate(of_ref, ob_ref, ga_ref, gn, o_ref, lo):
    rows = pl.ds(lo, SCAN_OUT_ROWS)
    o = of_ref[rows, :] + ob_ref[rows, :]
    o = o * lax.rsqrt(jnp.mean(o * o, axis=-1, keepdims=True) + EPS) * gn
    gate = _silu(ga_ref[0, rows, :].astype(jnp.float32))
    o_ref[0, rows, :] = (o * gate).astype(o_ref.dtype)


def _output_rows(of_ref, ob_ref, ga_ref, gn, o_ref, n_chunks, stored_f, stored_b):
    per = SCAN_OUT_ROWS // SCAN_CHUNK

    def ready_after(b):
        return per * (b + 1), n_chunks - per * b

    for b in sorted(range(n_chunks // per), key=lambda b: max(ready_after(b))):
        need_f, need_b = ready_after(b)
        while stored_f[0] < need_f or stored_b[0] < need_b:
            yield
        _normalise_and_gate(of_ref, ob_ref, ga_ref, gn, o_ref, b * SCAN_OUT_ROWS)
        yield


def _delayed(generator, rounds):
    for _ in range(rounds):
        yield
    yield from generator


def _scan_kernel(q_ref, zf_ref, zb_ref, v_ref, ga_ref, qc_ref, zfc_ref, zbc_ref, vc_ref,
                 lbl_ref, gn_ref, wf_ref, wb_ref, mf_ref, mb_ref, o_ref, of_ref, ob_ref):
    c = SCAN_CHUNK
    l0, l1 = lbl_ref[0], lbl_ref[1]
    mx = jnp.maximum(l0, l1)
    e0, e1 = jnp.exp(l0 - mx), jnp.exp(l1 - mx)
    lb = e0 / (e0 + e1)
    lb_f, lb_b = lb[0:1], lb[1:2]

    gens_f, gens_b, chunks_f, chunks_b = [], [], [], []
    for q_r, zf_r, zb_r, v_r, of_r, ob_r in ((qc_ref, zfc_ref, zbc_ref, vc_ref, None, None),
                                             (q_ref, zf_ref, zb_ref, v_ref, of_ref, ob_ref)):
        t = q_r.shape[1]
        starts_f = list(range(0, t, c))
        starts_b = starts_f[::-1]
        g_f, res_f = _direction_chunks(q_r, zf_r, v_r, starts_f, lb_f, wf_ref, mf_ref, False)
        g_b, res_b = _direction_chunks(q_r, zb_r, v_r, starts_b, lb_b, wb_ref, mb_ref, True)
        gens_f += g_f
        gens_b += g_b
        chunks_f += [(s, r, of_r) for s, r in zip(starts_f, res_f)]
        chunks_b += [(s, r, ob_r) for s, r in zip(starts_b, res_b)]
    staggered = [_delayed(g, SCAN_STAGGER * j) for j, pair in enumerate(zip(gens_f, gens_b)) for g in pair]
    stored_f, stored_b = [0], [0]
    _interleave(*staggered,
                _direction_states(chunks_f, stored_f),
                _direction_states(chunks_b, stored_b),
                _output_rows(of_ref, ob_ref, ga_ref, gn_ref[...], o_ref, q_ref.shape[1] // c, stored_f, stored_b))


def _hgrn_scan(p_lat, aux_lat, p_ctx, lb_logits, hgrn_norm_g):
    bsz, t_lat, _ = p_lat.shape
    t_ctx = p_ctx.shape[1]
    w_f3, w_b3, m_f, m_b = _scan_constants()
    wf = jnp.asarray(w_f3, jnp.bfloat16)
    wb = jnp.asarray(w_b3, jnp.bfloat16)
    mf = jnp.asarray(m_f, jnp.float32)
    mb = jnp.asarray(m_b, jnp.float32)

    def col(t, base):
        return pl.BlockSpec((1, t, LANES), lambda b, h, base=base: (b, 0, base + h))

    def whole(a):
        return pl.BlockSpec(a.shape, lambda b, h, nd=a.ndim: (0,) * nd)

    return pl.pallas_call(
        _scan_kernel,
        grid=(bsz, HGRN_HEADS),
        in_specs=[col(t_lat, COL_Q), col(t_lat, COL_ZF), col(t_lat, COL_ZB), col(t_lat, COL_V),
                  col(t_lat, COL_GA),
                  col(t_ctx, COL_Q), col(t_ctx, COL_ZF), col(t_ctx, COL_ZB), col(t_ctx, COL_V),
                  pl.BlockSpec((2, 2, LANES), lambda b, h: (0, 0, h)),
                  pl.BlockSpec((1, LANES), lambda b, h: (0, h)),
                  whole(wf), whole(wb), whole(mf), whole(mb)],
        out_specs=pl.BlockSpec((1, t_lat, LANES), lambda b, h: (b, 0, h)),
        out_shape=jax.ShapeDtypeStruct((bsz, t_lat, D_HGRN), jnp.bfloat16),
        scratch_shapes=[pltpu.VMEM((t_lat, HEAD_DIM), jnp.float32),
                        pltpu.VMEM((t_lat, HEAD_DIM), jnp.float32)],
        compiler_params=pltpu.CompilerParams(dimension_semantics=("arbitrary", "arbitrary"),
                                             vmem_limit_bytes=VMEM_LIMIT),
        name="hgrn_scan",
    )(p_lat, p_lat, p_lat, p_lat, aux_lat, p_ctx, p_ctx, p_ctx, p_ctx,
      lb_logits, hgrn_norm_g, wf, wb, mf, mb)


CONV_PADW = GRID_W + 32


def _conv_fill(u_ref, ug_ref, pad_ref, along_rows):
    t = u_ref.shape[1]
    n_rows = t // GRID_W

    @pl.when(along_rows)
    def _along_rows():
        padw = CONV_PADW

        def fill(r, carry):
            src = pl.multiple_of(r * GRID_W, GRID_W)
            dst = pl.multiple_of(r * padw, 32)
            glu = (u_ref[0, pl.ds(src, GRID_W), :].astype(jnp.float32)
                   * jax.nn.sigmoid(ug_ref[0, pl.ds(src, GRID_W), :].astype(jnp.float32)))
            pad_ref[pl.ds(dst, 16), :] = jnp.zeros((16, LANES), jnp.float32)
            pad_ref[pl.ds(dst + 16, GRID_W), :] = glu
            pad_ref[pl.ds(dst + 16 + GRID_W, 16), :] = jnp.zeros((16, LANES), jnp.float32)
            return carry

        lax.fori_loop(0, n_rows, fill, 0, unroll=4)

    @pl.when(jnp.logical_not(along_rows))
    def _along_cols():
        halo = CONV_HALF * GRID_W
        pad_ref[pl.ds(0, halo), :] = jnp.zeros((halo, LANES), jnp.float32)
        pad_ref[pl.ds(halo + t, halo), :] = jnp.zeros((halo, LANES), jnp.float32)

        def fill(r, carry):
            src = pl.multiple_of(r * GRID_W, GRID_W)
            glu = (u_ref[0, pl.ds(src, GRID_W), :].astype(jnp.float32)
                   * jax.nn.sigmoid(ug_ref[0, pl.ds(src, GRID_W), :].astype(jnp.float32)))
            pad_ref[pl.ds(halo + src, GRID_W), :] = glu
            return carry

        lax.fori_loop(0, n_rows, fill, 0, unroll=4)


def _conv_kernel(u_ref, ug_ref, w_ref, b_ref, o_ref, pad_ref):
    n_rows = u_ref.shape[1] // GRID_W
    along_rows = pl.program_id(1) < (D_CONV // 2) // LANES
    _conv_fill(u_ref, ug_ref, pad_ref, along_rows)
    bias = b_ref[...]

    def taps(first_tap, stride):
        def conv(r, carry):
            dst = pl.multiple_of(r * GRID_W, GRID_W)
            base = first_tap(r)
            acc = jnp.zeros((GRID_W, LANES), jnp.float32)
            for k in range(CONV_WIDTH):
                acc = acc + w_ref[k:k + 1, :] * pad_ref[pl.ds(base + k * stride, GRID_W), :]
            o_ref[0, pl.ds(dst, GRID_W), :] = (acc + bias).astype(o_ref.dtype)
            return carry

        lax.fori_loop(0, n_rows, conv, 0, unroll=8)

    @pl.when(along_rows)
    def _():
        taps(lambda r: r * CONV_PADW + (16 - CONV_HALF), 1)

    @pl.when(jnp.logical_not(along_rows))
    def _():
        taps(lambda r: pl.multiple_of(r * GRID_W, GRID_W), GRID_W)


def _axial_conv(aux_lat, conv_w, conv_b):
    bsz, t, _ = aux_lat.shape
    n_rows = t // GRID_W
    pad_rows = max(n_rows * CONV_PADW, t + 2 * CONV_HALF * GRID_W)
    w_pad = jnp.zeros((32, D_CONV), jnp.float32).at[:CONV_WIDTH].set(conv_w)
    return pl.pallas_call(
        _conv_kernel,
        grid=(bsz, D_CONV // LANES),
        in_specs=[pl.BlockSpec((1, t, LANES), lambda b, g: (b, 0, COL_U + g)),
                  pl.BlockSpec((1, t, LANES), lambda b, g: (b, 0, COL_UG + g)),
                  pl.BlockSpec((32, LANES), lambda b, g: (0, g)),
                  pl.BlockSpec((1, LANES), lambda b, g: (0, g))],
        out_specs=pl.BlockSpec((1, t, LANES), lambda b, g: (b, 0, g)),
        out_shape=jax.ShapeDtypeStruct((bsz, t, D_CONV), jnp.bfloat16),
        scratch_shapes=[pltpu.VMEM((pad_rows, LANES), jnp.float32)],
        compiler_params=pltpu.CompilerParams(dimension_semantics=("arbitrary", "arbitrary"),
                                             vmem_limit_bytes=VMEM_LIMIT),
        name="axial_conv",
    )(aux_lat, aux_lat, w_pad, conv_b)


def _out_kernel(x_ref, ba_ref, y_ref, gb_ref, gt_ref, lng_ref, lnb_ref, wa_ref, wb_ref, fg_ref, o_ref):
    y = y_ref[0].astype(jnp.float32)
    mu = jnp.mean(y, axis=-1, keepdims=True)
    yc = y - mu
    var = jnp.mean(yc * yc, axis=-1, keepdims=True)
    yn = yc * lax.rsqrt(var + EPS) * lng_ref[...] + lnb_ref[...]
    branch_b = _silu(yn) * _silu(gb_ref[0].astype(jnp.float32))
    mix = jnp.dot(ba_ref[0], wa_ref[...], preferred_element_type=jnp.float32)
    mix = mix + jnp.dot(branch_b.astype(jnp.bfloat16), wb_ref[...], preferred_element_type=jnp.float32)
    h = x_ref[0] + gt_ref[0] * mix
    o_ref[0] = h * lax.rsqrt(jnp.mean(h * h, axis=-1, keepdims=True) + EPS) * fg_ref[...]


def _output(x, branch_a, y_conv, aux_lat, gate, ln_g, ln_b, w_out_bf16, final_g):
    bsz, t, _ = x.shape
    rows = OUT_ROWS
    gb_block = COL_GB * LANES // D_CONV
    return pl.pallas_call(
        _out_kernel,
        grid=(bsz, t // rows),
        in_specs=[pl.BlockSpec((1, rows, D_MODEL), lambda b, i: (b, i, 0)),
                  pl.BlockSpec((1, rows, D_HGRN), lambda b, i: (b, i, 0)),
                  pl.BlockSpec((1, rows, D_CONV), lambda b, i: (b, i, 0)),
                  pl.BlockSpec((1, rows, D_CONV), lambda b, i: (b, i, gb_block)),
                  pl.BlockSpec((1, 1, D_MODEL), lambda b, i: (b, 0, 0)),
                  pl.BlockSpec((1, D_CONV), lambda b, i: (0, 0)),
                  pl.BlockSpec((1, D_CONV), lambda b, i: (0, 0)),
                  pl.BlockSpec((D_HGRN, D_MODEL), lambda b, i: (0, 0)),
                  pl.BlockSpec((D_CONV, D_MODEL), lambda b, i: (1, 0)),
                  pl.BlockSpec((1, D_MODEL), lambda b, i: (0, 0))],
        out_specs=pl.BlockSpec((1, rows, D_MODEL), lambda b, i: (b, i, 0)),
        out_shape=jax.ShapeDtypeStruct((bsz, t, D_MODEL), jnp.float32),
        compiler_params=pltpu.CompilerParams(dimension_semantics=("arbitrary", "arbitrary"),
                                             vmem_limit_bytes=VMEM_LIMIT),
        name="out_projection",
    )(x, branch_a, y_conv, aux_lat, gate, ln_g, ln_b, w_out_bf16, w_out_bf16, final_g)


def kernel(x, c, ctx, c_ctx, norm_g, w_mod, b_mod, w_in, lb_logits, hgrn_norm_g, conv_w, conv_b,
           conv_ln_g, conv_ln_b, w_out, final_norm_g):
    bsz, seq_len, _ = x.shape
    assert norm_g.shape[0] == 1, "single-layer block"
    assert seq_len % GRID_W == 0 and seq_len % SCAN_OUT_ROWS == 0 and seq_len % PROJ_ROWS == 0
    assert ctx.shape[1] % (2 * SCAN_CHUNK) == 0, "chunks are handled in pairs"

    pad = (-(bsz + 1)) % SUBLANES
    cc = jnp.concatenate([c, c_ctx[None, :], jnp.zeros((pad, D_MODEL), c.dtype)], axis=0)
    mod = _modulation(cc, w_mod[0], b_mod)
    shift, scale, gate = (mod[:, i * D_MODEL:(i + 1) * D_MODEL] for i in range(3))
    shift_lat, scale_lat, gate_lat = (m[:bsz, None, :] for m in (shift, scale, gate))
    shift_ctx, scale_ctx = (m[bsz][None, None, :] for m in (shift, scale))

    w_in_bf16 = w_in[0].astype(jnp.bfloat16)
    assert w_in_bf16.shape[1] == 2 * D_SCAN_IN
    p_lat, aux_lat = _projection(x, norm_g, shift_lat, scale_lat, w_in_bf16, (jnp.float32, jnp.bfloat16))
    p_ctx, = _projection(ctx.reshape(1, -1, D_MODEL), norm_g, shift_ctx, scale_ctx, w_in_bf16, (jnp.float32,))
    p_ctx = p_ctx.reshape(bsz, ctx.shape[1], D_SCAN_IN)

    branch_a = _hgrn_scan(p_lat, aux_lat, p_ctx, lb_logits, hgrn_norm_g)
    y_conv = _axial_conv(aux_lat, conv_w[0], conv_b)
    return _output(x, branch_a, y_conv, aux_lat, gate_lat, conv_ln_g, conv_ln_b,
                   w_out[0].astype(jnp.bfloat16), final_norm_g[None, :])
```

```python
import numpy as np
import jax
import jax.numpy as jnp
from jax import lax
from jax.experimental import pallas as pl
from jax.experimental.pallas import tpu as pltpu

D_MODEL = 1024
GRID_W = 64
D_HGRN = 512
HGRN_HEADS = 4
HEAD_DIM = D_HGRN // HGRN_HEADS
D_CONV = 512
CONV_WIDTH = 31
CONV_HALF = CONV_WIDTH // 2
EPS = 1e-6

LANES = 128
SUBLANES = 8
SCAN_CHUNK = 64
SCAN_LEVELS = SCAN_CHUNK.bit_length() - 1
SCAN_OUT_ROWS = 128
SCAN_STAGGER = 1
PROJ_ROWS = 1024
OUT_ROWS = 1024
VMEM_LIMIT = 56 * 1024 * 1024

D_SCAN_IN = 4 * D_HGRN
COL_Q, COL_ZF, COL_ZB, COL_V = (i * HGRN_HEADS for i in range(4))
COL_GA, COL_U, COL_UG, COL_GB = (i * (D_HGRN // LANES) for i in range(4))


def _silu(x):
    return x * jax.nn.sigmoid(x)


def _mod_kernel(c_ref, w_ref, b_ref, o_ref):
    a = _silu(c_ref[...])
    o_ref[...] = jnp.dot(a, w_ref[...], preferred_element_type=jnp.float32,
                         precision=lax.Precision.HIGHEST) + b_ref[...]


def _modulation(cc, w_mod, b_mod):
    rows = cc.shape[0]
    n = w_mod.shape[1]
    return pl.pallas_call(
        _mod_kernel,
        grid=(n // D_MODEL,),
        in_specs=[pl.BlockSpec((rows, D_MODEL), lambda j: (0, 0)),
                  pl.BlockSpec((D_MODEL, D_MODEL), lambda j: (0, j)),
                  pl.BlockSpec((1, D_MODEL), lambda j: (0, j))],
        out_specs=pl.BlockSpec((rows, D_MODEL), lambda j: (0, j)),
        out_shape=jax.ShapeDtypeStruct((rows, n), jnp.float32),
        compiler_params=pltpu.CompilerParams(dimension_semantics=("arbitrary",),
                                             vmem_limit_bytes=VMEM_LIMIT),
        name="modulation",
    )(cc, w_mod, b_mod)


def _proj_kernel(x_ref, g_ref, sh_ref, sc_ref, *refs):
    x = x_ref[0]
    y = x * lax.rsqrt(jnp.mean(x * x, axis=-1, keepdims=True) + EPS) * g_ref[...]
    a = (y * (1.0 + sc_ref[0]) + sh_ref[0]).astype(jnp.bfloat16)
    n = len(refs) // 2
    for w_ref, o_ref in zip(refs[:n], refs[n:]):
        o_ref[0] = jnp.dot(a, w_ref[...], preferred_element_type=jnp.float32).astype(o_ref.dtype)


def _projection(x, norm_g, shift, scale, w_bf16, dtypes):
    bsz, t, _ = x.shape
    rows = min(PROJ_ROWS, t)
    width = D_SCAN_IN
    return pl.pallas_call(
        _proj_kernel,
        grid=(bsz, t // rows),
        in_specs=[pl.BlockSpec((1, rows, D_MODEL), lambda b, i: (b, i, 0)),
                  pl.BlockSpec((1, D_MODEL), lambda b, i: (0, 0)),
                  pl.BlockSpec((1, 1, D_MODEL), lambda b, i: (b, 0, 0)),
                  pl.BlockSpec((1, 1, D_MODEL), lambda b, i: (b, 0, 0))]
        + [pl.BlockSpec((D_MODEL, width), lambda b, i, j=j: (0, j)) for j in range(len(dtypes))],
        out_specs=[pl.BlockSpec((1, rows, width), lambda b, i: (b, i, 0)) for _ in dtypes],
        out_shape=[jax.ShapeDtypeStruct((bsz, t, width), dt) for dt in dtypes],
        compiler_params=pltpu.CompilerParams(dimension_semantics=("arbitrary", "arbitrary"),
                                             vmem_limit_bytes=VMEM_LIMIT),
        name="in_projection",
    )(x, norm_g, shift, scale, *([w_bf16] * len(dtypes)))


def _scan_constants():
    c = SCAN_CHUNK
    idx = np.arange(c)
    t, s = idx[:, None], idx[None, :]
    masks = [t == s]
    for lvl in range(SCAN_LEVELS):
        h = c >> (lvl + 1)
        masks.append(((t // (2 * h)) == (s // (2 * h))) & ((t % (2 * h)) >= h) & ((s % (2 * h)) < h))
    m_f = np.stack([m.astype(np.float32) for m in masks])
    m_b = m_f[:, ::-1, ::-1].copy()
    tri_f = (s <= t).astype(np.float32)
    tri_b = (s >= t).astype(np.float32)
    return np.concatenate([tri_f] * 2, axis=1), np.concatenate([tri_b] * 2, axis=1), m_f, m_b


def _dot_nt(a, b):
    return lax.dot_general(a, b, (((1,), (1,)), ((), ())), preferred_element_type=jnp.float32)


def _dot_tn(a, b):
    return lax.dot_general(a, b, (((0,), (0,)), ((), ())), preferred_element_type=jnp.float32)


def _interleave(*generators):
    pending = list(generators)
    while pending:
        for gen in list(pending):
            try:
                next(gen)
            except StopIteration:
                pending.remove(gen)


def _mix_rows(q, k, half, reverse):
    parts = []
    for lo in range(0, SCAN_CHUNK, 2 * half):
        first, second = (q, k) if reverse else (k, q)
        parts += [first[lo:lo + half], second[lo + half:lo + 2 * half]]
    return jnp.concatenate(parts, axis=0)


def _level_decay(g_cum, f, half, reverse):
    c = SCAN_CHUNK
    if half >= SUBLANES:
        parts = []
        for lo in range(0, c, 2 * half):
            mid = lo + half
            if reverse:
                parts += [g_cum[lo:mid] - g_cum[mid:mid + 1], g_cum[mid:mid + 1] - g_cum[mid:mid + half]]
            else:
                parts += [g_cum[mid - 1:mid] - g_cum[lo:mid], g_cum[mid:mid + half] - g_cum[mid - 1:mid]]
        return jnp.exp2(jnp.concatenate(parts, axis=0))
    if half == SUBLANES // 2:
        g3 = g_cum.reshape(c // SUBLANES, SUBLANES, LANES)
        r = half if reverse else half - 1
        later = lax.broadcasted_iota(jnp.int32, g3.shape, 1) >= half
        sign = jnp.where(later != reverse, 1.0, -1.0)
        return jnp.exp2((g3 - g3[:, r:r + 1, :]) * sign).reshape(c, LANES)
    f3 = f.reshape(c // SUBLANES, SUBLANES, LANES)
    row = lax.broadcasted_iota(jnp.int32, f3.shape, 1)
    if half == 1:
        on_query_side = (row % 2 == 0) if reverse else (row % 2 == 1)
        return jnp.where(on_query_side, f3, 1.0).reshape(c, LANES)
    prev = pltpu.roll(f3, 1, 1)
    nxt = pltpu.roll(f3, SUBLANES - 1, 1)
    m4 = row % 4
    if reverse:
        d = jnp.where(m4 == 0, f3 * nxt, jnp.where(m4 == 1, f3, jnp.where(m4 == 2, 1.0, prev)))
    else:
        d = jnp.where(m4 == 0, nxt, jnp.where(m4 == 1, 1.0, jnp.where(m4 == 2, f3, f3 * prev)))
    return d.reshape(c, LANES)


def _chunk_local(q, k, v, f, g_cum, m_ref, reverse, result):
    c = SCAN_CHUNK
    bf = jnp.bfloat16
    end_row = 0 if reverse else c - 1

    qb, kb = q.astype(bf), k.astype(bf)
    piece = 2 * SUBLANES
    q_fine = qb * _level_decay(g_cum, f, 1, reverse).astype(bf)
    p = _dot_nt(jnp.concatenate([qb, q_fine], axis=0), kb)
    a = [m_ref[0, r:r + piece] * p[r:r + piece] + m_ref[SCAN_LEVELS, r:r + piece] * p[c + r:c + r + piece]
         for r in range(0, c, piece)]
    yield
    for half in (SUBLANES, 2) + tuple(h for h in (c >> (lvl + 1) for lvl in range(SCAN_LEVELS))
                                      if h not in (SUBLANES, 2, 1)):
        d = _level_decay(g_cum, f, half, reverse).astype(bf)
        rows = list(range(0, c, piece))
        if half >= piece:
            x = _mix_rows(qb, kb, half, reverse) * d
            firsts = [lo + (0 if reverse else half) for lo in range(0, c, 2 * half)]
            rows = [r for s in firsts for r in range(s, s + half, piece)]
            p = _dot_nt(jnp.concatenate([x[s:s + half] for s in firsts], axis=0), x)
        elif half == SUBLANES:
            x = (_mix_rows(q, k, half, reverse)).astype(bf) * d
            p = _dot_nt(x, x)
        else:
            p = _dot_nt(qb * d, kb * d)
        m_level = SCAN_LEVELS - half.bit_length() + 1
        for i, r in enumerate(rows):
            a[r // piece] = a[r // piece] + m_ref[m_level, r:r + piece] * p[i * piece:(i + 1) * piece]
        yield
    d_read = jnp.exp2(g_cum)
    d_state = jnp.exp2(g_cum[end_row:end_row + 1] - g_cum)
    vb = v.astype(bf)
    o_intra = jnp.dot(jnp.concatenate(a, axis=0).astype(bf), vb, preferred_element_type=jnp.float32)
    yield
    kv = _dot_tn(vb, kb * d_state.astype(bf))
    result.extend([o_intra, kv, qb * d_read.astype(bf), d_read[end_row:end_row + 1]])
    yield


def _chunk_pair(q_r, z_r, v_r, starts, lb, w_ref, m_ref, reverse, results):
    c = SCAN_CHUNK
    bf = jnp.bfloat16
    fs, g3s = [], []
    for s in starts:
        f = lb + (1.0 - lb) * jax.nn.sigmoid(z_r[0, pl.ds(s, c), :])
        g = jnp.log2(f)
        fs.append(f)
        g_hi = g.astype(bf)
        g_lo = (g - g_hi.astype(jnp.float32)).astype(bf)
        g3s.append(jnp.concatenate([g_hi, g_lo], axis=0))
        yield
    g_cum = jnp.dot(w_ref[...], jnp.concatenate(g3s, axis=1), preferred_element_type=jnp.float32)
    yield
    chains = [_chunk_local(q_r[0, pl.ds(s, c), :], 1.0 - fs[j], v_r[0, pl.ds(s, c), :], fs[j],
                           g_cum[:, j * LANES:(j + 1) * LANES], m_ref, reverse, results[j])
              for j, s in enumerate(starts)]
    while chains:
        for chain in list(chains):
            try:
                next(chain)
            except StopIteration:
                chains.remove(chain)
        yield


def _direction_chunks(q_r, z_r, v_r, starts, lb, w_ref, m_ref, reverse):
    results = [[] for _ in starts]
    gens = [_chunk_pair(q_r, z_r, v_r, starts[j:j + 2], lb, w_ref, m_ref, reverse, results[j:j + 2])
            for j in range(0, len(starts), 2)]
    return gens, results


def _direction_states(chunks, stored):
    c = SCAN_CHUNK
    st = jnp.zeros((HEAD_DIM, HEAD_DIM), jnp.float32)
    for s, result, o_r in chunks:
        while not result:
            yield
        o_intra, kv, q_read, d_end = result
        if o_r is not None:
            o_r[pl.ds(s, c), :] = o_intra + jnp.dot(q_read, st.T.astype(jnp.bfloat16),
                                                    preferred_element_type=jnp.float32)
            stored[0] += 1
        st = st * d_end + kv
        yield


def _normalise_and_gate(of_ref, ob_ref, ga_ref, gn, o_ref, lo):
    rows = pl.ds(lo, SCAN_OUT_ROWS)
    o = of_ref[rows, :] + ob_ref[rows, :]
    o = o * lax.rsqrt(jnp.mean(o * o, axis=-1, keepdims=True) + EPS) * gn
    gate = _silu(ga_ref[0, rows, :].astype(jnp.float32))
    o_ref[0, rows, :] = (o * gate).astype(o_ref.dtype)


def _output_rows(of_ref, ob_ref, ga_ref, gn, o_ref, n_chunks, stored_f, stored_b):
    per = SCAN_OUT_ROWS // SCAN_CHUNK

    def ready_after(b):
        return per * (b + 1), n_chunks - per * b

    for b in sorted(range(n_chunks // per), key=lambda b: max(ready_after(b))):
        need_f, need_b = ready_after(b)
        while stored_f[0] < need_f or stored_b[0] < need_b:
            yield
        _normalise_and_gate(of_ref, ob_ref, ga_ref, gn, o_ref, b * SCAN_OUT_ROWS)
        yield


def _delayed(generator, rounds):
    for _ in range(rounds):
        yield
    yield from generator


def _scan_kernel(q_ref, zf_ref, zb_ref, v_ref, ga_ref, qc_ref, zfc_ref, zbc_ref, vc_ref,
                 lbl_ref, gn_ref, wf_ref, wb_ref, mf_ref, mb_ref, o_ref, of_ref, ob_ref):
    c = SCAN_CHUNK
    l0, l1 = lbl_ref[0], lbl_ref[1]
    mx = jnp.maximum(l0, l1)
    e0, e1 = jnp.exp(l0 - mx), jnp.exp(l1 - mx)
    lb = e0 / (e0 + e1)
    lb_f, lb_b = lb[0:1], lb[1:2]

    gens_f, gens_b, chunks_f, chunks_b = [], [], [], []
    for q_r, zf_r, zb_r, v_r, of_r, ob_r in ((qc_ref, zfc_ref, zbc_ref, vc_ref, None, None),
                                             (q_ref, zf_ref, zb_ref, v_ref, of_ref, ob_ref)):
        t = q_r.shape[1]
        starts_f = list(range(0, t, c))
        starts_b = starts_f[::-1]
        g_f, res_f = _direction_chunks(q_r, zf_r, v_r, starts_f, lb_f, wf_ref, mf_ref, False)
        g_b, res_b = _direction_chunks(q_r, zb_r, v_r, starts_b, lb_b, wb_ref, mb_ref, True)
        gens_f += g_f
        gens_b += g_b
        chunks_f += [(s, r, of_r) for s, r in zip(starts_f, res_f)]
        chunks_b += [(s, r, ob_r) for s, r in zip(starts_b, res_b)]
    staggered = [_delayed(g, SCAN_STAGGER * j) for j, pair in enumerate(zip(gens_f, gens_b)) for g in pair]
    stored_f, stored_b = [0], [0]
    _interleave(*staggered,
                _direction_states(chunks_f, stored_f),
                _direction_states(chunks_b, stored_b),
                _output_rows(of_ref, ob_ref, ga_ref, gn_ref[...], o_ref, q_ref.shape[1] // c, stored_f, stored_b))


def _hgrn_scan(p_lat, aux_lat, p_ctx, lb_logits, hgrn_norm_g):
    bsz, t_lat, _ = p_lat.shape
    t_ctx = p_ctx.shape[1]
    w_f3, w_b3, m_f, m_b = _scan_constants()
    wf = jnp.asarray(w_f3, jnp.bfloat16)
    wb = jnp.asarray(w_b3, jnp.bfloat16)
    mf = jnp.asarray(m_f, jnp.float32)
    mb = jnp.asarray(m_b, jnp.float32)

    def col(t, base):
        return pl.BlockSpec((1, t, LANES), lambda b, h, base=base: (b, 0, base + h))

    def whole(a):
        return pl.BlockSpec(a.shape, lambda b, h, nd=a.ndim: (0,) * nd)

    return pl.pallas_call(
        _scan_kernel,
        grid=(bsz, HGRN_HEADS),
        in_specs=[col(t_lat, COL_Q), col(t_lat, COL_ZF), col(t_lat, COL_ZB), col(t_lat, COL_V),
                  col(t_lat, COL_GA),
                  col(t_ctx, COL_Q), col(t_ctx, COL_ZF), col(t_ctx, COL_ZB), col(t_ctx, COL_V),
                  pl.BlockSpec((2, 2, LANES), lambda b, h: (0, 0, h)),
                  pl.BlockSpec((1, LANES), lambda b, h: (0, h)),
                  whole(wf), whole(wb), whole(mf), whole(mb)],
        out_specs=pl.BlockSpec((1, t_lat, LANES), lambda b, h: (b, 0, h)),
        out_shape=jax.ShapeDtypeStruct((bsz, t_lat, D_HGRN), jnp.bfloat16),
        scratch_shapes=[pltpu.VMEM((t_lat, HEAD_DIM), jnp.float32),
                        pltpu.VMEM((t_lat, HEAD_DIM), jnp.float32)],
        compiler_params=pltpu.CompilerParams(dimension_semantics=("arbitrary", "arbitrary"),
                                             vmem_limit_bytes=VMEM_LIMIT),
        name="hgrn_scan",
    )(p_lat, p_lat, p_lat, p_lat, aux_lat, p_ctx, p_ctx, p_ctx, p_ctx,
      lb_logits, hgrn_norm_g, wf, wb, mf, mb)


CONV_PADW = GRID_W + 32


def _conv_fill(u_ref, ug_ref, pad_ref, along_rows):
    t = u_ref.shape[1]
    n_rows = t // GRID_W

    @pl.when(along_rows)
    def _along_rows():
        padw = CONV_PADW

        def fill(r, carry):
            src = pl.multiple_of(r * GRID_W, GRID_W)
            dst = pl.multiple_of(r * padw, 32)
            glu = (u_ref[0, pl.ds(src, GRID_W), :].astype(jnp.float32)
                   * jax.nn.sigmoid(ug_ref[0, pl.ds(src, GRID_W), :].astype(jnp.float32)))
            pad_ref[pl.ds(dst, 16), :] = jnp.zeros((16, LANES), jnp.float32)
            pad_ref[pl.ds(dst + 16, GRID_W), :] = glu
            pad_ref[pl.ds(dst + 16 + GRID_W, 16), :] = jnp.zeros((16, LANES), jnp.float32)
            return carry

        lax.fori_loop(0, n_rows, fill, 0, unroll=8)

    @pl.when(jnp.logical_not(along_rows))
    def _along_cols():
        halo = CONV_HALF * GRID_W
        pad_ref[pl.ds(0, halo), :] = jnp.zeros((halo, LANES), jnp.float32)
        pad_ref[pl.ds(halo + t, halo), :] = jnp.zeros((halo, LANES), jnp.float32)

        def fill(r, carry):
            src = pl.multiple_of(r * GRID_W, GRID_W)
            glu = (u_ref[0, pl.ds(src, GRID_W), :].astype(jnp.float32)
                   * jax.nn.sigmoid(ug_ref[0, pl.ds(src, GRID_W), :].astype(jnp.float32)))
            pad_ref[pl.ds(halo + src, GRID_W), :] = glu
            return carry

        lax.fori_loop(0, n_rows, fill, 0, unroll=8)


def _conv_kernel(u_ref, ug_ref, w_ref, b_ref, o_ref, pad_ref):
    n_rows = u_ref.shape[1] // GRID_W
    along_rows = pl.program_id(1) < (D_CONV // 2) // LANES
    _conv_fill(u_ref, ug_ref, pad_ref, along_rows)
    bias = b_ref[...]

    def taps(first_tap, stride):
        def conv(r, carry):
            dst = pl.multiple_of(r * GRID_W, GRID_W)
            base = first_tap(r)
            acc = jnp.zeros((GRID_W, LANES), jnp.float32)
            for k in range(CONV_WIDTH):
                acc = acc + w_ref[k:k + 1, :] * pad_ref[pl.ds(base + k * stride, GRID_W), :]
            o_ref[0, pl.ds(dst, GRID_W), :] = (acc + bias).astype(o_ref.dtype)
            return carry

        lax.fori_loop(0, n_rows, conv, 0, unroll=32)

    @pl.when(along_rows)
    def _():
        taps(lambda r: r * CONV_PADW + (16 - CONV_HALF), 1)

    @pl.when(jnp.logical_not(along_rows))
    def _():
        taps(lambda r: pl.multiple_of(r * GRID_W, GRID_W), GRID_W)


def _axial_conv(aux_lat, conv_w, conv_b):
    bsz, t, _ = aux_lat.shape
    n_rows = t // GRID_W
    pad_rows = max(n_rows * CONV_PADW, t + 2 * CONV_HALF * GRID_W)
    w_pad = jnp.zeros((32, D_CONV), jnp.float32).at[:CONV_WIDTH].set(conv_w)
    return pl.pallas_call(
        _conv_kernel,
        grid=(bsz, D_CONV // LANES),
        in_specs=[pl.BlockSpec((1, t, LANES), lambda b, g: (b, 0, COL_U + g)),
                  pl.BlockSpec((1, t, LANES), lambda b, g: (b, 0, COL_UG + g)),
                  pl.BlockSpec((32, LANES), lambda b, g: (0, g)),
                  pl.BlockSpec((1, LANES), lambda b, g: (0, g))],
        out_specs=pl.BlockSpec((1, t, LANES), lambda b, g: (b, 0, g)),
        out_shape=jax.ShapeDtypeStruct((bsz, t, D_CONV), jnp.bfloat16),
        scratch_shapes=[pltpu.VMEM((pad_rows, LANES), jnp.float32)],
        compiler_params=pltpu.CompilerParams(dimension_semantics=("arbitrary", "arbitrary"),
                                             vmem_limit_bytes=VMEM_LIMIT),
        name="axial_conv",
    )(aux_lat, aux_lat, w_pad, conv_b)


def _out_kernel(x_ref, ba_ref, y_ref, gb_ref, gt_ref, lng_ref, lnb_ref, wa_ref, wb_ref, fg_ref, o_ref):
    y = y_ref[0].astype(jnp.float32)
    mu = jnp.mean(y, axis=-1, keepdims=True)
    yc = y - mu
    var = jnp.mean(yc * yc, axis=-1, keepdims=True)
    yn = yc * lax.rsqrt(var + EPS) * lng_ref[...] + lnb_ref[...]
    branch_b = _silu(yn) * _silu(gb_ref[0].astype(jnp.float32))
    mix = jnp.dot(ba_ref[0], wa_ref[...], preferred_element_type=jnp.float32)
    mix = mix + jnp.dot(branch_b.astype(jnp.bfloat16), wb_ref[...], preferred_element_type=jnp.float32)
    h = x_ref[0] + gt_ref[0] * mix
    o_ref[0] = h * lax.rsqrt(jnp.mean(h * h, axis=-1, keepdims=True) + EPS) * fg_ref[...]


def _output(x, branch_a, y_conv, aux_lat, gate, ln_g, ln_b, w_out_bf16, final_g):
    bsz, t, _ = x.shape
    rows = OUT_ROWS
    gb_block = COL_GB * LANES // D_CONV
    return pl.pallas_call(
        _out_kernel,
        grid=(bsz, t // rows),
        in_specs=[pl.BlockSpec((1, rows, D_MODEL), lambda b, i: (b, i, 0)),
                  pl.BlockSpec((1, rows, D_HGRN), lambda b, i: (b, i, 0)),
                  pl.BlockSpec((1, rows, D_CONV), lambda b, i: (b, i, 0)),
                  pl.BlockSpec((1, rows, D_CONV), lambda b, i: (b, i, gb_block)),
                  pl.BlockSpec((1, 1, D_MODEL), lambda b, i: (b, 0, 0)),
                  pl.BlockSpec((1, D_CONV), lambda b, i: (0, 0)),
                  pl.BlockSpec((1, D_CONV), lambda b, i: (0, 0)),
                  pl.BlockSpec((D_HGRN, D_MODEL), lambda b, i: (0, 0)),
                  pl.BlockSpec((D_CONV, D_MODEL), lambda b, i: (1, 0)),
                  pl.BlockSpec((1, D_MODEL), lambda b, i: (0, 0))],
        out_specs=pl.BlockSpec((1, rows, D_MODEL), lambda b, i: (b, i, 0)),
        out_shape=jax.ShapeDtypeStruct((bsz, t, D_MODEL), jnp.float32),
        compiler_params=pltpu.CompilerParams(dimension_semantics=("arbitrary", "arbitrary"),
                                             vmem_limit_bytes=VMEM_LIMIT),
        name="out_projection",
    )(x, branch_a, y_conv, aux_lat, gate, ln_g, ln_b, w_out_bf16, w_out_bf16, final_g)


def kernel(x, c, ctx, c_ctx, norm_g, w_mod, b_mod, w_in, lb_logits, hgrn_norm_g, conv_w, conv_b,
           conv_ln_g, conv_ln_b, w_out, final_norm_g):
    bsz, seq_len, _ = x.shape
    assert norm_g.shape[0] == 1, "single-layer block"
    assert seq_len % GRID_W == 0 and seq_len % SCAN_OUT_ROWS == 0 and seq_len % PROJ_ROWS == 0
    assert ctx.shape[1] % (2 * SCAN_CHUNK) == 0, "chunks are handled in pairs"

    pad = (-(bsz + 1)) % SUBLANES
    cc = jnp.concatenate([c, c_ctx[None, :], jnp.zeros((pad, D_MODEL), c.dtype)], axis=0)
    mod = _modulation(cc, w_mod[0], b_mod)
    shift, scale, gate = (mod[:, i * D_MODEL:(i + 1) * D_MODEL] for i in range(3))
    shift_lat, scale_lat, gate_lat = (m[:bsz, None, :] for m in (shift, scale, gate))
    shift_ctx, scale_ctx = (m[bsz][None, None, :] for m in (shift, scale))

    w_in_bf16 = w_in[0].astype(jnp.bfloat16)
    assert w_in_bf16.shape[1] == 2 * D_SCAN_IN
    p_lat, aux_lat = _projection(x, norm_g, shift_lat, scale_lat, w_in_bf16, (jnp.float32, jnp.bfloat16))
    p_ctx, = _projection(ctx.reshape(1, -1, D_MODEL), norm_g, shift_ctx, scale_ctx, w_in_bf16, (jnp.float32,))
    p_ctx = p_ctx.reshape(bsz, ctx.shape[1], D_SCAN_IN)

    branch_a = _hgrn_scan(p_lat, aux_lat, p_ctx, lb_logits, hgrn_norm_g)
    y_conv = _axial_conv(aux_lat, conv_w[0], conv_b)
    return _output(x, branch_a, y_conv, aux_lat, gate_lat, conv_ln_g, conv_ln_b,
                   w_out[0].astype(jnp.bfloat16), final_norm_g[None, :])
```

```python
import numpy as np
import jax
import jax.numpy as jnp
from jax import lax
from jax.experimental import pallas as pl
from jax.experimental.pallas import tpu as pltpu

D_MODEL = 1024
GRID_W = 64
D_HGRN = 512
HGRN_HEADS = 4
HEAD_DIM = D_HGRN // HGRN_HEADS
D_CONV = 512
CONV_WIDTH = 31
CONV_HALF = CONV_WIDTH // 2
EPS = 1e-6

LANES = 128
SUBLANES = 8
SCAN_CHUNK = 64
SCAN_LEVELS = SCAN_CHUNK.bit_length() - 1
SCAN_OUT_ROWS = 128
SCAN_STAGGER = 1
PROJ_ROWS = 1024
OUT_ROWS = 1024
VMEM_LIMIT = 56 * 1024 * 1024

D_SCAN_IN = 4 * D_HGRN
COL_Q, COL_ZF, COL_ZB, COL_V = (i * HGRN_HEADS for i in range(4))
COL_GA, COL_U, COL_UG, COL_GB = (i * (D_HGRN // LANES) for i in range(4))


def _silu(x):
    return x * jax.nn.sigmoid(x)


def _mod_kernel(c_ref, w_ref, b_ref, o_ref):
    a = _silu(c_ref[...])
    o_ref[...] = jnp.dot(a, w_ref[...], preferred_element_type=jnp.float32,
                         precision=lax.Precision.HIGHEST) + b_ref[...]


def _modulation(cc, w_mod, b_mod):
    rows = cc.shape[0]
    n = w_mod.shape[1]
    return pl.pallas_call(
        _mod_kernel,
        grid=(n // D_MODEL,),
        in_specs=[pl.BlockSpec((rows, D_MODEL), lambda j: (0, 0)),
                  pl.BlockSpec((D_MODEL, D_MODEL), lambda j: (0, j)),
                  pl.BlockSpec((1, D_MODEL), lambda j: (0, j))],
        out_specs=pl.BlockSpec((rows, D_MODEL), lambda j: (0, j)),
        out_shape=jax.ShapeDtypeStruct((rows, n), jnp.float32),
        compiler_params=pltpu.CompilerParams(dimension_semantics=("arbitrary",),
                                             vmem_limit_bytes=VMEM_LIMIT),
        name="modulation",
    )(cc, w_mod, b_mod)


def _proj_kernel(x_ref, g_ref, sh_ref, sc_ref, *refs):
    x = x_ref[0]
    y = x * lax.rsqrt(jnp.mean(x * x, axis=-1, keepdims=True) + EPS) * g_ref[...]
    a = (y * (1.0 + sc_ref[0]) + sh_ref[0]).astype(jnp.bfloat16)
    n = len(refs) // 2
    for w_ref, o_ref in zip(refs[:n], refs[n:]):
        o_ref[0] = jnp.dot(a, w_ref[...], preferred_element_type=jnp.float32).astype(o_ref.dtype)


def _projection(x, norm_g, shift, scale, w_bf16, dtypes):
    bsz, t, _ = x.shape
    rows = min(PROJ_ROWS, t)
    width = D_SCAN_IN
    return pl.pallas_call(
        _proj_kernel,
        grid=(bsz, t // rows),
        in_specs=[pl.BlockSpec((1, rows, D_MODEL), lambda b, i: (b, i, 0)),
                  pl.BlockSpec((1, D_MODEL), lambda b, i: (0, 0)),
                  pl.BlockSpec((1, 1, D_MODEL), lambda b, i: (b, 0, 0)),
                  pl.BlockSpec((1, 1, D_MODEL), lambda b, i: (b, 0, 0))]
        + [pl.BlockSpec((D_MODEL, width), lambda b, i, j=j: (0, j)) for j in range(len(dtypes))],
        out_specs=[pl.BlockSpec((1, rows, width), lambda b, i: (b, i, 0)) for _ in dtypes],
        out_shape=[jax.ShapeDtypeStruct((bsz, t, width), dt) for dt in dtypes],
        compiler_params=pltpu.CompilerParams(dimension_semantics=("arbitrary", "arbitrary"),
                                             vmem_limit_bytes=VMEM_LIMIT),
        name="in_projection",
    )(x, norm_g, shift, scale, *([w_bf16] * len(dtypes)))


def _scan_constants():
    c = SCAN_CHUNK
    idx = np.arange(c)
    t, s = idx[:, None], idx[None, :]
    masks = [t == s]
    for lvl in range(SCAN_LEVELS):
        h = c >> (lvl + 1)
        masks.append(((t // (2 * h)) == (s // (2 * h))) & ((t % (2 * h)) >= h) & ((s % (2 * h)) < h))
    m_f = np.stack([m.astype(np.float32) for m in masks])
    m_b = m_f[:, ::-1, ::-1].copy()
    tri_f = (s <= t).astype(np.float32)
    tri_b = (s >= t).astype(np.float32)
    return np.concatenate([tri_f] * 2, axis=1), np.concatenate([tri_b] * 2, axis=1), m_f, m_b


def _dot_nt(a, b):
    return lax.dot_general(a, b, (((1,), (1,)), ((), ())), preferred_element_type=jnp.float32)


def _dot_tn(a, b):
    return lax.dot_general(a, b, (((0,), (0,)), ((), ())), preferred_element_type=jnp.float32)


def _interleave(*generators):
    pending = list(generators)
    while pending:
        for gen in list(pending):
            try:
                next(gen)
            except StopIteration:
                pending.remove(gen)


def _mix_rows(q, k, half, reverse):
    parts = []
    for lo in range(0, SCAN_CHUNK, 2 * half):
        first, second = (q, k) if reverse else (k, q)
        parts += [first[lo:lo + half], second[lo + half:lo + 2 * half]]
    return jnp.concatenate(parts, axis=0)


def _level_decay(g_cum, f, half, reverse):
    c = SCAN_CHUNK
    if half >= SUBLANES:
        parts = []
        for lo in range(0, c, 2 * half):
            mid = lo + half
            if reverse:
                parts += [g_cum[lo:mid] - g_cum[mid:mid + 1], g_cum[mid:mid + 1] - g_cum[mid:mid + half]]
            else:
                parts += [g_cum[mid - 1:mid] - g_cum[lo:mid], g_cum[mid:mid + half] - g_cum[mid - 1:mid]]
        return jnp.exp2(jnp.concatenate(parts, axis=0))
    if half == SUBLANES // 2:
        g3 = g_cum.reshape(c // SUBLANES, SUBLANES, LANES)
        r = half if reverse else half - 1
        later = lax.broadcasted_iota(jnp.int32, g3.shape, 1) >= half
        sign = jnp.where(later != reverse, 1.0, -1.0)
        return jnp.exp2((g3 - g3[:, r:r + 1, :]) * sign).reshape(c, LANES)
    f3 = f.reshape(c // SUBLANES, SUBLANES, LANES)
    row = lax.broadcasted_iota(jnp.int32, f3.shape, 1)
    if half == 1:
        on_query_side = (row % 2 == 0) if reverse else (row % 2 == 1)
        return jnp.where(on_query_side, f3, 1.0).reshape(c, LANES)
    prev = pltpu.roll(f3, 1, 1)
    nxt = pltpu.roll(f3, SUBLANES - 1, 1)
    m4 = row % 4
    if reverse:
        d = jnp.where(m4 == 0, f3 * nxt, jnp.where(m4 == 1, f3, jnp.where(m4 == 2, 1.0, prev)))
    else:
        d = jnp.where(m4 == 0, nxt, jnp.where(m4 == 1, 1.0, jnp.where(m4 == 2, f3, f3 * prev)))
    return d.reshape(c, LANES)


def _chunk_local(q, k, v, f, g_cum, m_ref, reverse, result):
    c = SCAN_CHUNK
    bf = jnp.bfloat16
    end_row = 0 if reverse else c - 1

    qb, kb = q.astype(bf), k.astype(bf)
    piece = 2 * SUBLANES
    q_fine = qb * _level_decay(g_cum, f, 1, reverse).astype(bf)
    p = _dot_nt(jnp.concatenate([qb, q_fine], axis=0), kb)
    a = [m_ref[0, r:r + piece] * p[r:r + piece] + m_ref[SCAN_LEVELS, r:r + piece] * p[c + r:c + r + piece]
         for r in range(0, c, piece)]
    yield
    for half in (SUBLANES, 2) + tuple(h for h in (c >> (lvl + 1) for lvl in range(SCAN_LEVELS))
                                      if h not in (SUBLANES, 2, 1)):
        d = _level_decay(g_cum, f, half, reverse).astype(bf)
        rows = list(range(0, c, piece))
        if half >= piece:
            x = _mix_rows(qb, kb, half, reverse) * d
            firsts = [lo + (0 if reverse else half) for lo in range(0, c, 2 * half)]
            rows = [r for s in firsts for r in range(s, s + half, piece)]
            p = _dot_nt(jnp.concatenate([x[s:s + half] for s in firsts], axis=0), x)
        elif half == SUBLANES:
            x = (_mix_rows(q, k, half, reverse)).astype(bf) * d
            p = _dot_nt(x, x)
        else:
            p = _dot_nt(qb * d, kb * d)
        m_level = SCAN_LEVELS - half.bit_length() + 1
        for i, r in enumerate(rows):
            a[r // piece] = a[r // piece] + m_ref[m_level, r:r + piece] * p[i * piece:(i + 1) * piece]
        yield
    d_read = jnp.exp2(g_cum)
    d_state = jnp.exp2(g_cum[end_row:end_row + 1] - g_cum)
    vb = v.astype(bf)
    o_intra = jnp.dot(jnp.concatenate(a, axis=0).astype(bf), vb, preferred_element_type=jnp.float32)
    yield
    kv = _dot_tn(vb, kb * d_state.astype(bf))
    result.extend([o_intra, kv, qb * d_read.astype(bf), d_read[end_row:end_row + 1]])
    yield


def _chunk_pair(q_r, z_r, v_r, starts, lb, w_ref, m_ref, reverse, results):
    c = SCAN_CHUNK
    bf = jnp.bfloat16
    fs, g3s = [], []
    for s in starts:
        f = lb + (1.0 - lb) * jax.nn.sigmoid(z_r[0, pl.ds(s, c), :])
        g = jnp.log2(f)
        fs.append(f)
        g_hi = g.astype(bf)
        g_lo = (g - g_hi.astype(jnp.float32)).astype(bf)
        g3s.append(jnp.concatenate([g_hi, g_lo], axis=0))
        yield
    g_cum = jnp.dot(w_ref[...], jnp.concatenate(g3s, axis=1), preferred_element_type=jnp.float32)
    yield
    chains = [_chunk_local(q_r[0, pl.ds(s, c), :], 1.0 - fs[j], v_r[0, pl.ds(s, c), :], fs[j],
                           g_cum[:, j * LANES:(j + 1) * LANES], m_ref, reverse, results[j])
              for j, s in enumerate(starts)]
    while chains:
        for chain in list(chains):
            try:
                next(chain)
            except StopIteration:
                chains.remove(chain)
        yield


def _direction_chunks(q_r, z_r, v_r, starts, lb, w_ref, m_ref, reverse):
    results = [[] for _ in starts]
    gens = [_chunk_pair(q_r, z_r, v_r, starts[j:j + 2], lb, w_ref, m_ref, reverse, results[j:j + 2])
            for j in range(0, len(starts), 2)]
    return gens, results


def _direction_states(chunks, stored):
    c = SCAN_CHUNK
    st = jnp.zeros((HEAD_DIM, HEAD_DIM), jnp.float32)
    for s, result, o_r in chunks:
        while not result:
            yield
        o_intra, kv, q_read, d_end = result
        if o_r is not None:
            o_r[pl.ds(s, c), :] = o_intra + jnp.dot(q_read, st.T.astype(jnp.bfloat16),
                                                    preferred_element_type=jnp.float32)
            stored[0] += 1
        st = st * d_end + kv
        yield


def _normalise_and_gate(of_ref, ob_ref, ga_ref, gn, o_ref, lo):
    rows = pl.ds(lo, SCAN_OUT_ROWS)
    o = of_ref[rows, :] + ob_ref[rows, :]
    o = o * lax.rsqrt(jnp.mean(o * o, axis=-1, keepdims=True) + EPS) * gn
    gate = _silu(ga_ref[0, rows, :].astype(jnp.float32))
    o_ref[0, rows, :] = (o * gate).astype(o_ref.dtype)


def _output_rows(of_ref, ob_ref, ga_ref, gn, o_ref, n_chunks, stored_f, stored_b):
    per = SCAN_OUT_ROWS // SCAN_CHUNK

    def ready_after(b):
        return per * (b + 1), n_chunks - per * b

    for b in sorted(range(n_chunks // per), key=lambda b: max(ready_after(b))):
        need_f, need_b = ready_after(b)
        while stored_f[0] < need_f or stored_b[0] < need_b:
            yield
        _normalise_and_gate(of_ref, ob_ref, ga_ref, gn, o_ref, b * SCAN_OUT_ROWS)
        yield


def _delayed(generator, rounds):
    for _ in range(rounds):
        yield
    yield from generator


def _scan_kernel(q_ref, zf_ref, zb_ref, v_ref, ga_ref, qc_ref, zfc_ref, zbc_ref, vc_ref,
                 lbl_ref, gn_ref, wf_ref, wb_ref, mf_ref, mb_ref, o_ref, of_ref, ob_ref):
    c = SCAN_CHUNK
    l0, l1 = lbl_ref[0], lbl_ref[1]
    mx = jnp.maximum(l0, l1)
    e0, e1 = jnp.exp(l0 - mx), jnp.exp(l1 - mx)
    lb = e0 / (e0 + e1)
    lb_f, lb_b = lb[0:1], lb[1:2]

    gens_f, gens_b, chunks_f, chunks_b = [], [], [], []
    for q_r, zf_r, zb_r, v_r, of_r, ob_r in ((qc_ref, zfc_ref, zbc_ref, vc_ref, None, None),
                                             (q_ref, zf_ref, zb_ref, v_ref, of_ref, ob_ref)):
        t = q_r.shape[1]
        starts_f = list(range(0, t, c))
        starts_b = starts_f[::-1]
        g_f, res_f = _direction_chunks(q_r, zf_r, v_r, starts_f, lb_f, wf_ref, mf_ref, False)
        g_b, res_b = _direction_chunks(q_r, zb_r, v_r, starts_b, lb_b, wb_ref, mb_ref, True)
        gens_f += g_f
        gens_b += g_b
        chunks_f += [(s, r, of_r) for s, r in zip(starts_f, res_f)]
        chunks_b += [(s, r, ob_r) for s, r in zip(starts_b, res_b)]
    staggered = [_delayed(g, SCAN_STAGGER * j) for j, pair in enumerate(zip(gens_f, gens_b)) for g in pair]
    stored_f, stored_b = [0], [0]
    _interleave(*staggered,
                _direction_states(chunks_f, stored_f),
                _direction_states(chunks_b, stored_b),
                _output_rows(of_ref, ob_ref, ga_ref, gn_ref[...], o_ref, q_ref.shape[1] // c, stored_f, stored_b))


def _hgrn_scan(p_lat, aux_lat, p_ctx, lb_logits, hgrn_norm_g):
    bsz, t_lat, _ = p_lat.shape
    t_ctx = p_ctx.shape[1]
    w_f3, w_b3, m_f, m_b = _scan_constants()
    wf = jnp.asarray(w_f3, jnp.bfloat16)
    wb = jnp.asarray(w_b3, jnp.bfloat16)
    mf = jnp.asarray(m_f, jnp.float32)
    mb = jnp.asarray(m_b, jnp.float32)

    def col(t, base):
        return pl.BlockSpec((1, t, LANES), lambda b, h, base=base: (b, 0, base + h))

    def whole(a):
        return pl.BlockSpec(a.shape, lambda b, h, nd=a.ndim: (0,) * nd)

    return pl.pallas_call(
        _scan_kernel,
        grid=(bsz, HGRN_HEADS),
        in_specs=[col(t_lat, COL_Q), col(t_lat, COL_ZF), col(t_lat, COL_ZB), col(t_lat, COL_V),
                  col(t_lat, COL_GA),
                  col(t_ctx, COL_Q), col(t_ctx, COL_ZF), col(t_ctx, COL_ZB), col(t_ctx, COL_V),
                  pl.BlockSpec((2, 2, LANES), lambda b, h: (0, 0, h)),
                  pl.BlockSpec((1, LANES), lambda b, h: (0, h)),
                  whole(wf), whole(wb), whole(mf), whole(mb)],
        out_specs=pl.BlockSpec((1, t_lat, LANES), lambda b, h: (b, 0, h)),
        out_shape=jax.ShapeDtypeStruct((bsz, t_lat, D_HGRN), jnp.bfloat16),
        scratch_shapes=[pltpu.VMEM((t_lat, HEAD_DIM), jnp.float32),
                        pltpu.VMEM((t_lat, HEAD_DIM), jnp.float32)],
        compiler_params=pltpu.CompilerParams(dimension_semantics=("arbitrary", "arbitrary"),
                                             vmem_limit_bytes=VMEM_LIMIT),
        name="hgrn_scan",
    )(p_lat, p_lat, p_lat, p_lat, aux_lat, p_ctx, p_ctx, p_ctx, p_ctx,
      lb_logits, hgrn_norm_g, wf, wb, mf, mb)


CONV_PADW = GRID_W + 32


def _conv_fill(u_ref, ug_ref, pad_ref, along_rows):
    t = u_ref.shape[1]
    n_rows = t // GRID_W

    @pl.when(along_rows)
    def _along_rows():
        padw = CONV_PADW

        def fill(r, carry):
            src = pl.multiple_of(r * GRID_W, GRID_W)
            dst = pl.multiple_of(r * padw, 32)
            glu = (u_ref[0, pl.ds(src, GRID_W), :].astype(jnp.float32)
                   * jax.nn.sigmoid(ug_ref[0, pl.ds(src, GRID_W), :].astype(jnp.float32)))
            pad_ref[pl.ds(dst, 16), :] = jnp.zeros((16, LANES), jnp.float32)
            pad_ref[pl.ds(dst + 16, GRID_W), :] = glu
            pad_ref[pl.ds(dst + 16 + GRID_W, 16), :] = jnp.zeros((16, LANES), jnp.float32)
            return carry

        lax.fori_loop(0, n_rows, fill, 0, unroll=16)

    @pl.when(jnp.logical_not(along_rows))
    def _along_cols():
        halo = CONV_HALF * GRID_W
        pad_ref[pl.ds(0, halo), :] = jnp.zeros((halo, LANES), jnp.float32)
        pad_ref[pl.ds(halo + t, halo), :] = jnp.zeros((halo, LANES), jnp.float32)

        def fill(r, carry):
            src = pl.multiple_of(r * GRID_W, GRID_W)
            glu = (u_ref[0, pl.ds(src, GRID_W), :].astype(jnp.float32)
                   * jax.nn.sigmoid(ug_ref[0, pl.ds(src, GRID_W), :].astype(jnp.float32)))
            pad_ref[pl.ds(halo + src, GRID_W), :] = glu
            return carry

        lax.fori_loop(0, n_rows, fill, 0, unroll=16)


def _conv_kernel(u_ref, ug_ref, w_ref, b_ref, o_ref, pad_ref):
    n_rows = u_ref.shape[1] // GRID_W
    along_rows = pl.program_id(1) < (D_CONV // 2) // LANES
    _conv_fill(u_ref, ug_ref, pad_ref, along_rows)
    bias = b_ref[...]

    def taps(first_tap, stride):
        def conv(r, carry):
            dst = pl.multiple_of(r * GRID_W, GRID_W)
            base = first_tap(r)
            acc = jnp.zeros((GRID_W, LANES), jnp.float32)
            for k in range(CONV_WIDTH):
                acc = acc + w_ref[k:k + 1, :] * pad_ref[pl.ds(base + k * stride, GRID_W), :]
            o_ref[0, pl.ds(dst, GRID_W), :] = (acc + bias).astype(o_ref.dtype)
            return carry

        lax.fori_loop(0, n_rows, conv, 0, unroll=32)

    @pl.when(along_rows)
    def _():
        taps(lambda r: r * CONV_PADW + (16 - CONV_HALF), 1)

    @pl.when(jnp.logical_not(along_rows))
    def _():
        taps(lambda r: pl.multiple_of(r * GRID_W, GRID_W), GRID_W)


def _axial_conv(aux_lat, conv_w, conv_b):
    bsz, t, _ = aux_lat.shape
    n_rows = t // GRID_W
    pad_rows = max(n_rows * CONV_PADW, t + 2 * CONV_HALF * GRID_W)
    w_pad = jnp.zeros((32, D_CONV), jnp.float32).at[:CONV_WIDTH].set(conv_w)
    return pl.pallas_call(
        _conv_kernel,
        grid=(bsz, D_CONV // LANES),
        in_specs=[pl.BlockSpec((1, t, LANES), lambda b, g: (b, 0, COL_U + g)),
                  pl.BlockSpec((1, t, LANES), lambda b, g: (b, 0, COL_UG + g)),
                  pl.BlockSpec((32, LANES), lambda b, g: (0, g)),
                  pl.BlockSpec((1, LANES), lambda b, g: (0, g))],
        out_specs=pl.BlockSpec((1, t, LANES), lambda b, g: (b, 0, g)),
        out_shape=jax.ShapeDtypeStruct((bsz, t, D_CONV), jnp.bfloat16),
        scratch_shapes=[pltpu.VMEM((pad_rows, LANES), jnp.float32)],
        compiler_params=pltpu.CompilerParams(dimension_semantics=("arbitrary", "arbitrary"),
                                             vmem_limit_bytes=VMEM_LIMIT),
        name="axial_conv",
    )(aux_lat, aux_lat, w_pad, conv_b)


def _out_kernel(x_ref, ba_ref, y_ref, gb_ref, gt_ref, lng_ref, lnb_ref, wa_ref, wb_ref, fg_ref, o_ref):
    y = y_ref[0].astype(jnp.float32)
    mu = jnp.mean(y, axis=-1, keepdims=True)
    yc = y - mu
    var = jnp.mean(yc * yc, axis=-1, keepdims=True)
    yn = yc * lax.rsqrt(var + EPS) * lng_ref[...] + lnb_ref[...]
    branch_b = _silu(yn) * _silu(gb_ref[0].astype(jnp.float32))
    mix = jnp.dot(ba_ref[0], wa_ref[...], preferred_element_type=jnp.float32)
    mix = mix + jnp.dot(branch_b.astype(jnp.bfloat16), wb_ref[...], preferred_element_type=jnp.float32)
    h = x_ref[0] + gt_ref[0] * mix
    o_ref[0] = h * lax.rsqrt(jnp.mean(h * h, axis=-1, keepdims=True) + EPS) * fg_ref[...]


def _output(x, branch_a, y_conv, aux_lat, gate, ln_g, ln_b, w_out_bf16, final_g):
    bsz, t, _ = x.shape
    rows = OUT_ROWS
    gb_block = COL_GB * LANES // D_CONV
    return pl.pallas_call(
        _out_kernel,
        grid=(bsz, t // rows),
        in_specs=[pl.BlockSpec((1, rows, D_MODEL), lambda b, i: (b, i, 0)),
                  pl.BlockSpec((1, rows, D_HGRN), lambda b, i: (b, i, 0)),
                  pl.BlockSpec((1, rows, D_CONV), lambda b, i: (b, i, 0)),
                  pl.BlockSpec((1, rows, D_CONV), lambda b, i: (b, i, gb_block)),
                  pl.BlockSpec((1, 1, D_MODEL), lambda b, i: (b, 0, 0)),
                  pl.BlockSpec((1, D_CONV), lambda b, i: (0, 0)),
                  pl.BlockSpec((1, D_CONV), lambda b, i: (0, 0)),
                  pl.BlockSpec((D_HGRN, D_MODEL), lambda b, i: (0, 0)),
                  pl.BlockSpec((D_CONV, D_MODEL), lambda b, i: (1, 0)),
                  pl.BlockSpec((1, D_MODEL), lambda b, i: (0, 0))],
        out_specs=pl.BlockSpec((1, rows, D_MODEL), lambda b, i: (b, i, 0)),
        out_shape=jax.ShapeDtypeStruct((bsz, t, D_MODEL), jnp.float32),
        compiler_params=pltpu.CompilerParams(dimension_semantics=("arbitrary", "arbitrary"),
                                             vmem_limit_bytes=VMEM_LIMIT),
        name="out_projection",
    )(x, branch_a, y_conv, aux_lat, gate, ln_g, ln_b, w_out_bf16, w_out_bf16, final_g)


def kernel(x, c, ctx, c_ctx, norm_g, w_mod, b_mod, w_in, lb_logits, hgrn_norm_g, conv_w, conv_b,
           conv_ln_g, conv_ln_b, w_out, final_norm_g):
    bsz, seq_len, _ = x.shape
    assert norm_g.shape[0] == 1, "single-layer block"
    assert seq_len % GRID_W == 0 and seq_len % SCAN_OUT_ROWS == 0 and seq_len % PROJ_ROWS == 0
    assert ctx.shape[1] % (2 * SCAN_CHUNK) == 0, "chunks are handled in pairs"

    pad = (-(bsz + 1)) % SUBLANES
    cc = jnp.concatenate([c, c_ctx[None, :], jnp.zeros((pad, D_MODEL), c.dtype)], axis=0)
    mod = _modulation(cc, w_mod[0], b_mod)
    shift, scale, gate = (mod[:, i * D_MODEL:(i + 1) * D_MODEL] for i in range(3))
    shift_lat, scale_lat, gate_lat = (m[:bsz, None, :] for m in (shift, scale, gate))
    shift_ctx, scale_ctx = (m[bsz][None, None, :] for m in (shift, scale))

    w_in_bf16 = w_in[0].astype(jnp.bfloat16)
    assert w_in_bf16.shape[1] == 2 * D_SCAN_IN
    p_lat, aux_lat = _projection(x, norm_g, shift_lat, scale_lat, w_in_bf16, (jnp.float32, jnp.bfloat16))
    p_ctx, = _projection(ctx.reshape(1, -1, D_MODEL), norm_g, shift_ctx, scale_ctx, w_in_bf16, (jnp.float32,))
    p_ctx = p_ctx.reshape(bsz, ctx.shape[1], D_SCAN_IN)

    branch_a = _hgrn_scan(p_lat, aux_lat, p_ctx, lb_logits, hgrn_norm_g)
    y_conv = _axial_conv(aux_lat, conv_w[0], conv_b)
    return _output(x, branch_a, y_conv, aux_lat, gate_lat, conv_ln_g, conv_ln_b,
                   w_out[0].astype(jnp.bfloat16), final_norm_g[None, :])
```
